```python
import jax, jax.numpy as jnp
from jax import lax
import numpy as np

D_MODEL = 1024
BATCH = 4
SEQ = 4096
DEPTH = 1

N_HEADS_ATTN = 8
HEAD_DIM = 64
ROPE_DIM = HEAD_DIM // 4
ROPE_THETA = 500000.0
N_HEADS_IDX = 8
IDX_DIM = 64
TOPK_MAX = 256
Q_BLOCK = 128
N_POOL_GROUPS = 4
POOL_GROUP_DIM = 64
POOL_WINDOWS = (2, 4, 8, 16)
N_MEM = 256
N_HEADS_MEM = 4
N_BRANCHES = 3
N_EXPERTS = 32
TOP_K = 4
D_EXPERT = D_MODEL
SWIGLU_ALPHA = 1.702
SWIGLU_LIMIT = 7.0
MOE_BLOCK = 128
EPS = 1e-6

ATTN_W = N_HEADS_ATTN * HEAD_DIM
POOL_W = N_POOL_GROUPS * POOL_GROUP_DIM
MEM_W = N_HEADS_MEM * HEAD_DIM
IDXQ_W = N_HEADS_IDX * IDX_DIM
IN_SPLITS = (ATTN_W, ATTN_W, ATTN_W, IDXQ_W, IDX_DIM, N_HEADS_IDX, POOL_W, MEM_W)
IN_W = sum(IN_SPLITS)

kernel_name = "hybrid_dsa_pool_memory_moe_block"


def rmsnorm(x, g):
    xf = x.astype(jnp.float32)
    y = xf * lax.rsqrt(jnp.mean(xf * xf, axis=-1, keepdims=True) + EPS)
    return (y * g.astype(jnp.float32)).astype(x.dtype)


def partial_rope(x, pos):
    half = ROPE_DIM // 2
    inv = jnp.power(jnp.float32(ROPE_THETA), -jnp.arange(half, dtype=jnp.float32) / half)
    ang = pos.astype(jnp.float32)[:, None] * inv[None, :]
    cos = jnp.cos(ang)[None, :, None, :]
    sin = jnp.sin(ang)[None, :, None, :]
    xr = x[..., :ROPE_DIM].astype(jnp.float32)
    x1, x2 = xr[..., :half], xr[..., half:]
    rot = jnp.concatenate([x1 * cos - x2 * sin, x2 * cos + x1 * sin], axis=-1).astype(x.dtype)
    return jnp.concatenate([rot, x[..., ROPE_DIM:]], axis=-1)


def dsa_attention(q, k, v, iq, ik, iw):
    B, S = q.shape[0], q.shape[1]
    n_top = min(TOPK_MAX, S // 4)
    nb = S // Q_BLOCK
    key_pos = jnp.arange(S, dtype=jnp.int32)
    gather = jax.vmap(lambda a, i: a[i])
    ikf = ik.astype(jnp.float32)

    def to_blocks(a):
        return jnp.moveaxis(a.reshape((B, nb, Q_BLOCK) + a.shape[2:]), 1, 0)

    def block(args):
        qb, iqb, iwb, start = args
        q_pos = start + jnp.arange(Q_BLOCK, dtype=jnp.int32)
        causal = key_pos[None, :] <= q_pos[:, None]
        sc = jnp.einsum('bqhd,bsd->bqhs', iqb.astype(jnp.float32), ikf) * (IDX_DIM ** -0.5)
        sc = jnp.einsum('bqhs,bqh->bqs', jax.nn.relu(sc), iwb.astype(jnp.float32))
        sc = jnp.where(causal[None], sc, -jnp.inf)
        _, sel = lax.top_k(sc, n_top)
        k_sel = gather(k, sel)
        v_sel = gather(v, sel)
        logits = jnp.einsum('bqhd,bqkhd->bhqk', qb, k_sel).astype(jnp.float32) * (HEAD_DIM ** -0.5)
        valid = sel <= q_pos[None, :, None]
        logits = jnp.where(valid[:, None], logits, -jnp.inf)
        p = jax.nn.softmax(logits, axis=-1).astype(v.dtype)
        return jnp.einsum('bhqk,bqkhd->bqhd', p, v_sel)

    starts = jnp.arange(nb, dtype=jnp.int32) * Q_BLOCK
    out = lax.map(block, (to_blocks(q), to_blocks(iq), to_blocks(iw), starts))
    return jnp.moveaxis(out, 0, 1).reshape(B, S, -1)


def multiscale_pool(u, w_group, scale):
    B, S = u.shape[0], u.shape[1]
    ug = u.reshape(B, S, N_POOL_GROUPS, POOL_GROUP_DIM)
    cs = jnp.cumsum(ug.astype(jnp.float32), axis=1)
    cs = jnp.concatenate([jnp.zeros_like(cs[:, :1]), cs], axis=1)
    t = jnp.arange(S, dtype=jnp.int32)[:, None]
    win = jnp.array(POOL_WINDOWS, dtype=jnp.int32)[None, :]
    lo = jnp.maximum(t + 1 - win, 0)
    g_idx = jnp.arange(N_POOL_GROUPS, dtype=jnp.int32)[None, :]
    win_sum = cs[:, 1:] - cs[:, lo, g_idx]
    cnt = (t + 1 - lo).astype(jnp.float32)[None, :, :, None]
    pooled = (win_sum / cnt - ug.astype(jnp.float32)).astype(u.dtype)
    mixed = jnp.einsum('bsgc,gcd->bsgd', pooled, w_group)
    return mixed.reshape(B, S, POOL_W) * scale


def memory_attention(qm, mem_n, w_mem_kv):
    B, S = qm.shape[0], qm.shape[1]
    M = mem_n.shape[1]
    kv = mem_n @ w_mem_kv
    km = kv[..., :MEM_W].reshape(B, M, N_HEADS_MEM, HEAD_DIM)
    vm = kv[..., MEM_W:].reshape(B, M, N_HEADS_MEM, HEAD_DIM)
    q = qm.reshape(B, S, N_HEADS_MEM, HEAD_DIM)
    logits = jnp.einsum('bshd,bmhd->bhsm', q, km).astype(jnp.float32) * (HEAD_DIM ** -0.5)
    p = jax.nn.softmax(logits, axis=-1).astype(vm.dtype)
    return jnp.einsum('bhsm,bmhd->bshd', p, vm).reshape(B, S, MEM_W)


def moe(xn, w_router, b_router, w_up, b_up, w_down, b_down):
    B, S, D = xn.shape
    N = B * S
    A = N * TOP_K
    xt = xn.reshape(N, D)
    logits = (xt @ w_router + b_router).astype(jnp.float32)
    top_val, top_idx = lax.top_k(logits, TOP_K)
    gate = jax.nn.softmax(top_val, axis=-1)
    e_flat = top_idx.reshape(A).astype(jnp.int32)
    tok_flat = jnp.repeat(jnp.arange(N, dtype=jnp.int32), TOP_K)
    g_flat = gate.reshape(A)
    order = jnp.argsort(e_flat)
    e_sorted = e_flat[order]
    counts = jnp.bincount(e_flat, length=N_EXPERTS).astype(jnp.int32)
    padded = ((counts + MOE_BLOCK - 1) // MOE_BLOCK) * MOE_BLOCK
    start = jnp.cumsum(counts) - counts
    pend = jnp.cumsum(padded)
    pstart = pend - padded
    dest = pstart[e_sorted] + (jnp.arange(A, dtype=jnp.int32) - start[e_sorted])
    n_rows = A + N_EXPERTS * MOE_BLOCK
    n_blocks = n_rows // MOE_BLOCK
    row_tok = jnp.full((n_rows,), N, jnp.int32).at[dest].set(tok_flat[order])
    row_gate = jnp.zeros((n_rows,), jnp.float32).at[dest].set(g_flat[order])
    block_e = jnp.minimum(
        jnp.searchsorted(pend, jnp.arange(n_blocks, dtype=jnp.int32) * MOE_BLOCK, side='right'),
        N_EXPERTS - 1).astype(jnp.int32)
    x_pad = jnp.concatenate([xt, jnp.zeros((1, D), xt.dtype)], axis=0)

    def expert_block(args):
        rows, e = args
        hb = x_pad[rows] @ w_up[e] + b_up[e]
        glu = jnp.minimum(hb[:, :D_EXPERT], SWIGLU_LIMIT)
        lin = jnp.clip(hb[:, D_EXPERT:], -SWIGLU_LIMIT, SWIGLU_LIMIT)
        act = glu * jax.nn.sigmoid(SWIGLU_ALPHA * glu) * (lin + 1.0)
        return act @ w_down[e] + b_down[e]

    y_rows = lax.map(expert_block, (row_tok.reshape(n_blocks, MOE_BLOCK), block_e)).reshape(n_rows, D)
    y = jnp.zeros((N + 1, D), jnp.float32).at[row_tok].add(row_gate[:, None] * y_rows.astype(jnp.float32))
    return y[:N].reshape(B, S, D).astype(xn.dtype)


def setup_inputs(seed: int = 0) -> dict:
    key = jax.random.key(seed)
    ks = jax.random.split(key, 24)
    f32 = jnp.float32
    L = DEPTH

    def nrm(k, shape, fan_in):
        return jax.random.normal(k, shape, f32) * (fan_in ** -0.5)

    def gain(k, shape):
        return 1.0 + 0.05 * jax.random.normal(k, shape, f32)

    def bias(k, shape, s=0.02):
        return s * jax.random.normal(k, shape, f32)

    return {
        "x": jax.random.normal(ks[0], (BATCH, SEQ, D_MODEL), f32),
        "mem": jax.random.normal(ks[1], (BATCH, N_MEM, D_MODEL), f32),
        "norm_mix_g": gain(ks[2], (L, D_MODEL)),
        "w_in": nrm(ks[3], (L, D_MODEL, IN_W), D_MODEL),
        "w_gate": nrm(ks[4], (L, D_MODEL, N_BRANCHES * D_MODEL), D_MODEL),
        "b_gate": bias(ks[5], (L, N_BRANCHES * D_MODEL)),
        "w_proj_attn": nrm(ks[6], (L, ATTN_W, D_MODEL), ATTN_W),
        "w_proj_pool": nrm(ks[7], (L, POOL_W, D_MODEL), POOL_W),
        "w_proj_mem": nrm(ks[8], (L, MEM_W, D_MODEL), MEM_W),
        "pool_w_group": nrm(ks[9], (L, N_POOL_GROUPS, POOL_GROUP_DIM, POOL_GROUP_DIM), POOL_GROUP_DIM),
        "pool_scale": gain(ks[10], (L, POOL_W)),
        "mem_norm_g": gain(ks[11], (L, D_MODEL)),
        "w_mem_kv": nrm(ks[12], (L, D_MODEL, 2 * MEM_W), D_MODEL),
        "w_out": nrm(ks[13], (L, D_MODEL, D_MODEL), D_MODEL),
        "norm_ffn_g": gain(ks[14], (L, D_MODEL)),
        "w_router": nrm(ks[15], (L, D_MODEL, N_EXPERTS), D_MODEL),
        "b_router": bias(ks[16], (L, N_EXPERTS), 0.01),
        "w_up": nrm(ks[17], (L, N_EXPERTS, D_MODEL, 2 * D_EXPERT), D_MODEL),
        "b_up": bias(ks[18], (L, N_EXPERTS, 2 * D_EXPERT)),
        "w_down": nrm(ks[19], (L, N_EXPERTS, D_EXPERT, D_MODEL), D_EXPERT),
        "b_down": bias(ks[20], (L, N_EXPERTS, D_MODEL)),
        "norm_final_g": gain(ks[21], (D_MODEL,)),
    }


def reference(x, mem, norm_mix_g, w_in, w_gate, b_gate, w_proj_attn, w_proj_pool, w_proj_mem,
              pool_w_group, pool_scale, mem_norm_g, w_mem_kv, w_out, norm_ffn_g,
              w_router, b_router, w_up, b_up, w_down, b_down, norm_final_g):
    B, S, D = x.shape
    pos = jnp.arange(S, dtype=jnp.int32)
    split_points = tuple(int(p) for p in np.cumsum(IN_SPLITS)[:-1])
    h = x
    for l in range(DEPTH):
        xn = rmsnorm(h, norm_mix_g[l])
        proj = xn @ w_in[l]
        q, k, v, iq, ik, iw, u, qm = jnp.split(proj, split_points, axis=-1)
        q = partial_rope(q.reshape(B, S, N_HEADS_ATTN, HEAD_DIM), pos)
        k = partial_rope(k.reshape(B, S, N_HEADS_ATTN, HEAD_DIM), pos)
        v = v.reshape(B, S, N_HEADS_ATTN, HEAD_DIM)
        iq = partial_rope(iq.reshape(B, S, N_HEADS_IDX, IDX_DIM), pos)
        ik = partial_rope(ik.reshape(B, S, 1, IDX_DIM), pos)[:, :, 0]
        iw = iw * (N_HEADS_IDX ** -0.5)
        y_attn = dsa_attention(q, k, v, iq, ik, iw)
        y_pool = multiscale_pool(u, pool_w_group[l], pool_scale[l])
        y_mem = memory_attention(qm, rmsnorm(mem, mem_norm_g[l]), w_mem_kv[l])
        gates = jax.nn.sigmoid(xn @ w_gate[l] + b_gate[l]).reshape(B, S, N_BRANCHES, D)
        merged = (gates[:, :, 0] * (y_attn @ w_proj_attn[l])
                  + gates[:, :, 1] * (y_pool @ w_proj_pool[l])
                  + gates[:, :, 2] * (y_mem @ w_proj_mem[l]))
        h = h + merged @ w_out[l]
        h = h + moe(rmsnorm(h, norm_ffn_g[l]), w_router[l], b_router[l], w_up[l], b_up[l],
                    w_down[l], b_down[l])
    return rmsnorm(h, norm_final_g)
```

```python
import functools

import jax
import jax.numpy as jnp
import numpy as np
from jax import lax
from jax.experimental import pallas as pl
from jax.experimental.pallas import tpu as pltpu

N_HEADS_ATTN = 8
HEAD_DIM = 64
ROPE_DIM = HEAD_DIM // 4
ROPE_THETA = 500000.0
N_HEADS_IDX = 8
IDX_DIM = 64
TOPK_MAX = 256
N_POOL_GROUPS = 4
POOL_GROUP_DIM = 64
POOL_WINDOWS = (2, 4, 8, 16)
POOL_HALO = 16
N_HEADS_MEM = 4
N_BRANCHES = 3
N_EXPERTS = 32
TOP_K = 4
SWIGLU_ALPHA = 1.702
SWIGLU_LIMIT = 7.0
EPS = 1e-6

ATTN_W = N_HEADS_ATTN * HEAD_DIM
POOL_W = N_POOL_GROUPS * POOL_GROUP_DIM
MEM_W = N_HEADS_MEM * HEAD_DIM
IDXQ_W = N_HEADS_IDX * IDX_DIM

LANES = 128
VMEM_LIMIT = 56 * 1024 * 1024

IN_TILE = 512
Q_BLOCK = 128
KEY_CHUNK = 512
MERGE_TILE = 256
ROW_BLOCK = 256
MOVE_TILE = 256

NEG_BIG = -1e30
INT_MIN = -2 ** 31


def _dot(a, b):
    return jnp.dot(a, b, preferred_element_type=jnp.float32)


def _dot_nt(a, b):
    return lax.dot_general(a, b, (((1,), (1,)), ((), ())), preferred_element_type=jnp.float32)


def _rmsnorm(x, g):
    return x * lax.rsqrt(jnp.mean(x * x, axis=-1, keepdims=True) + EPS) * g


def _rope128(x, c, a, b):
    return x * c + pltpu.roll(x, LANES - ROPE_DIM // 2, 1) * a + pltpu.roll(x, ROPE_DIM // 2, 1) * b


def _in_proj_kernel(x_ref, g_ref, wa_ref, wb_ref, wc_ref, rc_ref, ra_ref, rb_ref,
                    q_ref, k_ref, v_ref, iq_ref, iklo_ref, ikhi_ref, iw_ref, u_ref, qm_ref):
    xn = _rmsnorm(x_ref[...], g_ref[...]).astype(jnp.bfloat16)
    rc, ra, rb = rc_ref[...], ra_ref[...], rb_ref[...]
    pa = _dot(xn, wa_ref[...])
    for seg, (ref, scale) in enumerate(((q_ref, HEAD_DIM ** -0.5), (k_ref, None),
                                        (v_ref, None), (iq_ref, IDX_DIM ** -0.5))):
        for c in range(ATTN_W // LANES):
            lo = seg * ATTN_W + c * LANES
            blk = pa[:, lo:lo + LANES]
            if ref is not v_ref:
                blk = _rope128(blk, rc, ra, rb)
            if scale is not None:
                blk = blk * scale
            ref[:, c * LANES:(c + 1) * LANES] = blk.astype(ref.dtype)
    pb = _dot(xn, wb_ref[...])
    lane = lax.broadcasted_iota(jnp.int32, pb.shape, 1)
    ik = jnp.where(lane < IDX_DIM, _rope128(pb, rc, ra, rb), 0.0)
    iklo_ref[...] = ik.astype(iklo_ref.dtype)
    ikhi_ref[...] = pltpu.roll(ik, IDX_DIM, 1).astype(ikhi_ref.dtype)
    iw_ref[...] = pb * (N_HEADS_IDX ** -0.5)
    pc = _dot(xn, wc_ref[...])
    u_ref[...] = pc[:, :POOL_W]
    qm_ref[...] = (pc[:, POOL_W:] * (HEAD_DIM ** -0.5)).astype(qm_ref.dtype)


def _in_proj(x2, g, wa, wb, wc, rc, ra, rb, seq):
    n, d = x2.shape
    t = IN_TILE
    tiles_per_seq = seq // t
    row = lambda i: (i, 0)
    full = lambda i: (0, 0)
    pos = lambda i: (i % tiles_per_seq, 0)
    bf = jnp.bfloat16
    out_shape = (
        jax.ShapeDtypeStruct((n, ATTN_W), bf), jax.ShapeDtypeStruct((n, ATTN_W), bf),
        jax.ShapeDtypeStruct((n, ATTN_W), bf), jax.ShapeDtypeStruct((n, IDXQ_W), bf),
        jax.ShapeDtypeStruct((n, LANES), bf), jax.ShapeDtypeStruct((n, LANES), bf),
        jax.ShapeDtypeStruct((n, LANES), jnp.float32),
        jax.ShapeDtypeStruct((n, POOL_W), jnp.float32), jax.ShapeDtypeStruct((n, MEM_W), bf))
    return pl.pallas_call(
        _in_proj_kernel,
        grid=(n // t,),
        in_specs=[pl.BlockSpec((t, d), row), pl.BlockSpec((1, d), full),
                  pl.BlockSpec(wa.shape, full), pl.BlockSpec(wb.shape, full),
                  pl.BlockSpec(wc.shape, full),
                  pl.BlockSpec((t, LANES), pos), pl.BlockSpec((t, LANES), pos),
                  pl.BlockSpec((t, LANES), pos)],
        out_specs=tuple(pl.BlockSpec((t, s.shape[1]), row) for s in out_shape),
        out_shape=out_shape,
        compiler_params=pltpu.CompilerParams(dimension_semantics=("arbitrary",),
                                             vmem_limit_bytes=VMEM_LIMIT),
        name="in_proj",
    )(x2, g, wa, wb, wc, rc, ra, rb)


def _mem_kv_kernel(mem_ref, g_ref, w_ref, km_ref, vm_ref):
    mn = _rmsnorm(mem_ref[...], g_ref[...]).astype(jnp.bfloat16)
    kv = _dot(mn, w_ref[...])
    km, vm = kv[:, :MEM_W], kv[:, MEM_W:]
    lane = lax.broadcasted_iota(jnp.int32, km.shape, 1)
    for h in range(N_HEADS_MEM):
        in_head = (lane >= h * HEAD_DIM) & (lane < (h + 1) * HEAD_DIM)
        km_ref[h] = jnp.where(in_head, km, 0.0).astype(km_ref.dtype)
        vm_ref[h] = jnp.where(in_head, vm, 0.0).astype(vm_ref.dtype)


def _mem_kv(mem, g, w):
    b, m, d = mem.shape
    out = jax.ShapeDtypeStruct((b, N_HEADS_MEM, m, MEM_W), jnp.bfloat16)
    return pl.pallas_call(
        _mem_kv_kernel,
        grid=(b,),
        in_specs=[pl.BlockSpec((None, m, d), lambda i: (i, 0, 0)),
                  pl.BlockSpec((1, d), lambda i: (0, 0)),
                  pl.BlockSpec(w.shape, lambda i: (0, 0))],
        out_specs=(pl.BlockSpec((None, N_HEADS_MEM, m, MEM_W), lambda i: (i, 0, 0, 0)),
                   pl.BlockSpec((None, N_HEADS_MEM, m, MEM_W), lambda i: (i, 0, 0, 0))),
        out_shape=(out, out),
        compiler_params=pltpu.CompilerParams(dimension_semantics=("arbitrary",),
                                             vmem_limit_bytes=VMEM_LIMIT),
        name="mem_kv",
    )(mem, g, w)


def _dsa_kernel(iq_ref, iw_ref, q_ref, iklo_ref, ikhi_ref, k_ref, v_ref, o_ref,
                key_ref, qh_ref, m_ref, l_ref, acc_ref, *, n_top, idx_bits):
    qb = pl.program_id(1)
    n_chunks = (qb * Q_BLOCK + Q_BLOCK + KEY_CHUNK - 1) // KEY_CHUNK
    q_pos = qb * Q_BLOCK + lax.broadcasted_iota(jnp.int32, (Q_BLOCK, 1), 0)
    lane_k = lax.broadcasted_iota(jnp.int32, (1, KEY_CHUNK), 1)

    iw = iw_ref[...]

    def score_chunk(c, carry):
        off = pl.multiple_of(c * KEY_CHUNK, KEY_CHUNK)
        klo = iklo_ref[pl.ds(off, KEY_CHUNK), :]
        khi = ikhi_ref[pl.ds(off, KEY_CHUNK), :]
        acc = jnp.zeros((Q_BLOCK, KEY_CHUNK), jnp.float32)
        for j in range(N_HEADS_IDX // 2):
            pair = iq_ref[:, j * LANES:(j + 1) * LANES]
            w_even = iw[:, IDX_DIM + 2 * j:IDX_DIM + 2 * j + 1]
            w_odd = iw[:, IDX_DIM + 2 * j + 1:IDX_DIM + 2 * j + 2]
            acc = acc + w_even * jnp.maximum(_dot_nt(pair, klo), 0.0)
            acc = acc + w_odd * jnp.maximum(_dot_nt(pair, khi), 0.0)
        acc = jnp.where(off + lane_k <= q_pos, acc, -jnp.inf)
        bits = lax.bitcast_convert_type(acc, jnp.int32)
        key_ref[:, pl.ds(off, KEY_CHUNK)] = bits ^ ((bits >> 31) & 0x7FFFFFFF)
        return carry

    lax.fori_loop(0, n_chunks, score_chunk, 0)

    def count(pred):
        def body(c, cnt):
            off = pl.multiple_of(c * KEY_CHUNK, KEY_CHUNK)
            hit = jnp.where(pred(key_ref[:, pl.ds(off, KEY_CHUNK)], off + lane_k), 1.0, 0.0)
            for s in range(KEY_CHUNK // LANES):
                cnt = cnt + hit[:, s * LANES:(s + 1) * LANES]
            return cnt
        cnt = lax.fori_loop(0, n_chunks, body, jnp.zeros((Q_BLOCK, LANES), jnp.float32))
        return jnp.sum(cnt, axis=1, keepdims=True)

    k_f = jnp.float32(n_top)
    zero = jnp.zeros((Q_BLOCK, 1), jnp.int32)
    thr = jnp.where(count(lambda kk, pos: kk >= zero) >= k_f, zero, INT_MIN)

    def thr_bit(i, cur):
        cand = cur + jnp.left_shift(jnp.int32(1), 30 - i)
        return jnp.where(count(lambda kk, pos: kk >= cand) >= k_f, cand, cur)

    thr = lax.fori_loop(0, 31, thr_bit, thr)
    need = k_f - count(lambda kk, pos: kk > thr)

    def tie_bit(i, cur):
        cand = cur + jnp.left_shift(jnp.int32(1), idx_bits - 1 - i)
        below = count(lambda kk, pos: (kk == thr) & (pos < cand))
        return jnp.where(below < need, cand, cur)

    tie_pos = lax.fori_loop(0, idx_bits, tie_bit, zero)

    lane_q = lax.broadcasted_iota(jnp.int32, (Q_BLOCK, LANES), 1)
    for j in range(N_HEADS_ATTN // 2):
        pair = q_ref[:, j * LANES:(j + 1) * LANES]
        qh_ref[2 * j] = jnp.where(lane_q < HEAD_DIM, pair, jnp.zeros_like(pair))
        qh_ref[2 * j + 1] = jnp.where(lane_q >= HEAD_DIM, pair, jnp.zeros_like(pair))
    m_ref[...] = jnp.full(m_ref.shape, NEG_BIG, jnp.float32)
    l_ref[...] = jnp.zeros(l_ref.shape, jnp.float32)
    acc_ref[...] = jnp.zeros(acc_ref.shape, jnp.float32)

    def attn_chunk(c, carry):
        off = pl.multiple_of(c * KEY_CHUNK, KEY_CHUNK)
        kk = key_ref[:, pl.ds(off, KEY_CHUNK)]
        pos = off + lane_k
        sel = ((kk > thr) | ((kk == thr) & (pos <= tie_pos))) & (pos <= q_pos)
        bias = jnp.where(sel, 0.0, NEG_BIG)
        for j in range(N_HEADS_ATTN // 2):
            kp = k_ref[pl.ds(off, KEY_CHUNK), j * LANES:(j + 1) * LANES]
            vp = v_ref[pl.ds(off, KEY_CHUNK), j * LANES:(j + 1) * LANES]
            for h in (2 * j, 2 * j + 1):
                s = _dot_nt(qh_ref[h], kp) + bias
                m_old = m_ref[h]
                m_new = jnp.maximum(m_old, jnp.max(s, axis=1, keepdims=True))
                alpha = jnp.exp(m_old - m_new)
                p = jnp.exp(s - m_new[:, :1])
                l_ref[h] = alpha * l_ref[h] + jnp.sum(p, axis=1, keepdims=True)
                acc_ref[h] = alpha * acc_ref[h] + _dot(p.astype(vp.dtype), vp)
                m_ref[h] = m_new
        return carry

    lax.fori_loop(0, n_chunks, attn_chunk, 0)
    for j in range(N_HEADS_ATTN // 2):
        even = acc_ref[2 * j] / l_ref[2 * j]
        odd = acc_ref[2 * j + 1] / l_ref[2 * j + 1]
        o_ref[:, j * LANES:(j + 1) * LANES] = jnp.where(lane_q < HEAD_DIM, even, odd).astype(o_ref.dtype)


def _dsa(iq, iw, q, iklo, ikhi, k, v):
    b, s, _ = q.shape
    n_top = min(TOPK_MAX, s // 4)
    idx_bits = max(1, int(np.ceil(np.log2(s))))
    qblk = lambda w: pl.BlockSpec((None, Q_BLOCK, w), lambda bi, qi: (bi, qi, 0))
    keys = lambda w: pl.BlockSpec((None, s, w), lambda bi, qi: (bi, 0, 0))
    return pl.pallas_call(
        functools.partial(_dsa_kernel, n_top=n_top, idx_bits=idx_bits),
        grid=(b, s // Q_BLOCK),
        in_specs=[qblk(IDXQ_W), qblk(LANES), qblk(ATTN_W), keys(LANES), keys(LANES),
                  keys(ATTN_W), keys(ATTN_W)],
        out_specs=qblk(ATTN_W),
        out_shape=jax.ShapeDtypeStruct((b, s, ATTN_W), jnp.bfloat16),
        scratch_shapes=[pltpu.VMEM((Q_BLOCK, s), jnp.int32),
                        pltpu.VMEM((N_HEADS_ATTN, Q_BLOCK, LANES), jnp.bfloat16),
                        pltpu.VMEM((N_HEADS_ATTN, Q_BLOCK, LANES), jnp.float32),
                        pltpu.VMEM((N_HEADS_ATTN, Q_BLOCK, LANES), jnp.float32),
                        pltpu.VMEM((N_HEADS_ATTN, Q_BLOCK, LANES), jnp.float32)],
        compiler_params=pltpu.CompilerParams(dimension_semantics=("arbitrary", "arbitrary"),
                                             vmem_limit_bytes=VMEM_LIMIT),
        name="dsa",
    )(iq, iw, q, iklo, ikhi, k, v)


def _merge_kernel(x_ref, ya_ref, u_ref, uprev_ref, qm_ref, km_ref, vm_ref,
                  gmix_ref, wg_ref, bg_ref, wpa_ref, wpp_ref, wpm_ref, wbd_ref, psc_ref,
                  wo_ref, gffn_ref, wr_ref, br_ref,
                  h_ref, xn2_ref, eg_ref, ei_ref, cnt_ref, carry_ref, *, tiles_per_seq):
    i = pl.program_id(0)
    t = x_ref.shape[0]
    d = x_ref.shape[1]
    bf = jnp.bfloat16
    x = x_ref[...]
    xn = _rmsnorm(x, gmix_ref[...]).astype(bf)

    tile_in_seq = i % tiles_per_seq
    u = u_ref[...]
    halo = jnp.where(tile_in_seq == 0, 0.0, uprev_ref[...])
    ext = jnp.concatenate([halo, u], axis=0)
    lane_p = lax.broadcasted_iota(jnp.int32, (t, POOL_W), 1)
    pos1 = (tile_in_seq * t + lax.broadcasted_iota(jnp.int32, (t, 1), 0) + 1).astype(jnp.float32)
    pooled = None
    run, width = ext, 1
    for g, w in enumerate(POOL_WINDOWS):
        while width < w:
            run = run[width:] + run[:-width]
            width *= 2
        start = POOL_HALO + 1 - w
        mean = run[start:start + t] / jnp.minimum(pos1, float(w))
        pooled = mean if pooled is None else jnp.where(lane_p >= g * POOL_GROUP_DIM, mean, pooled)
    pooled = pooled - u
    mixed = _dot(pooled.astype(bf), wbd_ref[...]) * psc_ref[...]
    y_pool = _dot(mixed.astype(bf), wpp_ref[...])

    qm = qm_ref[...]
    probs = []
    for h in range(N_HEADS_MEM):
        s = _dot_nt(qm, km_ref[h])
        p = jnp.exp(s - jnp.max(s, axis=1, keepdims=True))
        probs.append((p / jnp.sum(p, axis=1, keepdims=True)).astype(bf))
    y_mem = _dot(probs[0], vm_ref[0])
    for h in range(1, N_HEADS_MEM):
        y_mem = y_mem + _dot(probs[h], vm_ref[h])
    y_mem = _dot(y_mem.astype(bf), wpm_ref[...])

    y_attn = _dot(ya_ref[...], wpa_ref[...])

    def gate(br):
        z = _dot(xn, wg_ref[:, br * d:(br + 1) * d]) + bg_ref[:, br * d:(br + 1) * d]
        return 1.0 / (1.0 + jnp.exp(-z))

    merged = gate(0) * y_attn + gate(1) * y_pool + gate(2) * y_mem
    h = x + _dot(merged.astype(bf), wo_ref[...])
    h_ref[...] = h
    xn2 = _rmsnorm(h, gffn_ref[...])
    xn2_ref[...] = xn2

    logits = jnp.dot(xn2, wr_ref[...], precision=lax.Precision.HIGHEST,
                     preferred_element_type=jnp.float32) + br_ref[...]
    lane_e = lax.broadcasted_iota(jnp.int32, logits.shape, 1).astype(jnp.float32)
    work = logits
    vals, onehots, ids = [], [], []
    for _ in range(TOP_K):
        mx = jnp.max(work, axis=1, keepdims=True)
        idx = jnp.min(jnp.where(work == mx, lane_e, float(LANES)), axis=1, keepdims=True)
        oh = lane_e == idx
        vals.append(mx)
        ids.append(idx)
        onehots.append(oh)
        work = jnp.where(oh, -jnp.inf, work)
    exps = [jnp.exp(v - vals[0]) for v in vals]
    denom = exps[0] + exps[1] + exps[2] + exps[3]

    @pl.when(i == 0)
    def _():
        carry_ref[...] = jnp.zeros(carry_ref.shape, jnp.float32)

    member = jnp.where(onehots[0] | onehots[1] | onehots[2] | onehots[3], 1.0, 0.0)
    r_io = lax.broadcasted_iota(jnp.int32, (t, t), 0)
    c_io = lax.broadcasted_iota(jnp.int32, (t, t), 1)
    earlier = jnp.where(c_io < r_io, 1.0, 0.0).astype(bf)
    before = _dot(earlier, member.astype(bf)) + carry_ref[...]
    carry_ref[...] = carry_ref[...] + jnp.sum(member, axis=0, keepdims=True)
    cnt_ref[...] = carry_ref[...]

    lane8 = lax.broadcasted_iota(jnp.int32, (t, 2 * TOP_K), 1)
    eg = jnp.zeros((t, 2 * TOP_K), jnp.float32)
    ei = jnp.zeros((t, 2 * TOP_K), jnp.float32)
    for j in range(TOP_K):
        rank = jnp.sum(jnp.where(onehots[j], before, 0.0), axis=1, keepdims=True)
        eg = jnp.where(lane8 == j, exps[j] / denom, eg)
        ei = jnp.where(lane8 == j, ids[j], ei)
        ei = jnp.where(lane8 == TOP_K + j, rank, ei)
    eg_ref[...] = eg
    ei_ref[...] = ei.astype(jnp.int32)


def _merge(x2, ya, u, qm, km, vm, gmix, wg, bg, wpa, wpp, wpm, wbd, psc, wo, gffn, wr, br, seq):
    n, d = x2.shape
    t = MERGE_TILE
    tiles_per_seq = seq // t
    m = km.shape[2]
    row = lambda i: (i, 0)
    full = lambda i: (0, 0)
    per_batch = lambda i: (i // tiles_per_seq, 0, 0, 0)
    halo_blocks = t // POOL_HALO
    prev = lambda i: (jnp.maximum(i * halo_blocks - 1, 0), 0)
    consts = (gmix, wg, bg, wpa, wpp, wpm, wbd, psc, wo, gffn, wr, br)
    out_shape = (jax.ShapeDtypeStruct((n, d), jnp.float32), jax.ShapeDtypeStruct((n, d), jnp.float32),
                 jax.ShapeDtypeStruct((n, 2 * TOP_K), jnp.float32),
                 jax.ShapeDtypeStruct((n, 2 * TOP_K), jnp.int32),
                 jax.ShapeDtypeStruct((1, LANES), jnp.float32))
    return pl.pallas_call(
        functools.partial(_merge_kernel, tiles_per_seq=tiles_per_seq),
        grid=(n // t,),
        in_specs=[pl.BlockSpec((t, d), row), pl.BlockSpec((t, ATTN_W), row),
                  pl.BlockSpec((t, POOL_W), row), pl.BlockSpec((POOL_HALO, POOL_W), prev),
                  pl.BlockSpec((t, MEM_W), row),
                  pl.BlockSpec((None, N_HEADS_MEM, m, MEM_W), per_batch),
                  pl.BlockSpec((None, N_HEADS_MEM, m, MEM_W), per_batch)]
                 + [pl.BlockSpec(c.shape, full) for c in consts],
        out_specs=(pl.BlockSpec((t, d), row), pl.BlockSpec((t, d), row),
                   pl.BlockSpec((t, 2 * TOP_K), row), pl.BlockSpec((t, 2 * TOP_K), row),
                   pl.BlockSpec((1, LANES), full)),
        out_shape=out_shape,
        scratch_shapes=[pltpu.VMEM((1, LANES), jnp.float32)],
        compiler_params=pltpu.CompilerParams(dimension_semantics=("arbitrary",),
                                             vmem_limit_bytes=VMEM_LIMIT),
        name="merge",
    )(x2, ya, u, u, qm, km, vm, *consts)


def _row_copy(src_ref, src_row, dst_ref, dst_row, sem):
    return pltpu.make_async_copy(src_ref.at[pl.ds(src_row, 1)], dst_ref.at[pl.ds(dst_row, 1)], sem)


def _dispatch_kernel(pstart_ref, e_ref, r_ref, x_ref, init_ref, xs_ref, sem):
    del init_ref
    t = x_ref.shape[0]

    def issue(tok, carry):
        for j in range(TOP_K):
            a = tok * TOP_K + j
            _row_copy(x_ref, tok, xs_ref, pstart_ref[e_ref[a]] + r_ref[a], sem).start()
        return carry

    lax.fori_loop(0, t, issue, 0)

    def drain(tok, carry):
        for j in range(TOP_K):
            _row_copy(x_ref, 0, xs_ref, 0, sem).wait()
        return carry

    lax.fori_loop(0, t, drain, 0)


def _dispatch(pstart, e_flat, r_flat, xn2, n_rows):
    n, d = xn2.shape
    t = MOVE_TILE
    smem = lambda: pl.BlockSpec((t * TOP_K,), lambda i, ps: (i,), memory_space=pltpu.SMEM)
    return pl.pallas_call(
        _dispatch_kernel,
        grid_spec=pltpu.PrefetchScalarGridSpec(
            num_scalar_prefetch=1,
            grid=(n // t,),
            in_specs=[smem(), smem(), pl.BlockSpec((t, d), lambda i, ps: (i, 0)),
                      pl.BlockSpec(memory_space=pl.ANY)],
            out_specs=pl.BlockSpec(memory_space=pl.ANY),
            scratch_shapes=[pltpu.SemaphoreType.DMA(())]),
        out_shape=jax.ShapeDtypeStruct((n_rows, d), jnp.float32),
        input_output_aliases={4: 0},
        compiler_params=pltpu.CompilerParams(dimension_semantics=("arbitrary",),
                                             vmem_limit_bytes=VMEM_LIMIT),
        name="dispatch",
    )(pstart, e_flat, r_flat, xn2, jnp.zeros((n_rows, d), jnp.float32))


def _experts_kernel(be_ref, nused_ref, xs_ref, wu_ref, bu_ref, wd_ref, bd_ref, ys_ref):
    i = pl.program_id(0)
    f = wd_ref.shape[0]

    @pl.when(i < nused_ref[0])
    def _():
        hb = _dot(xs_ref[...].astype(jnp.bfloat16), wu_ref[...]) + bu_ref[...]
        glu = jnp.minimum(hb[:, :f], SWIGLU_LIMIT)
        lin = jnp.clip(hb[:, f:], -SWIGLU_LIMIT, SWIGLU_LIMIT)
        act = glu * (1.0 / (1.0 + jnp.exp(-SWIGLU_ALPHA * glu))) * (lin + 1.0)
        ys_ref[...] = _dot(act.astype(jnp.bfloat16), wd_ref[...]) + bd_ref[...]

    @pl.when(i >= nused_ref[0])
    def _():
        ys_ref[...] = jnp.zeros(ys_ref.shape, ys_ref.dtype)


def _experts(block_e, nused, xs, wu, bu, wd, bd):
    n_rows, d = xs.shape
    e, _, f2 = wu.shape
    f = wd.shape[1]
    rows = lambda i, be, nu: (i, 0)
    per_e = lambda i, be, nu: (be[i], 0, 0)
    return pl.pallas_call(
        _experts_kernel,
        grid_spec=pltpu.PrefetchScalarGridSpec(
            num_scalar_prefetch=2,
            grid=(n_rows // ROW_BLOCK,),
            in_specs=[pl.BlockSpec((ROW_BLOCK, d), rows),
                      pl.BlockSpec((None, d, f2), per_e), pl.BlockSpec((None, 1, f2), per_e),
                      pl.BlockSpec((None, f, d), per_e), pl.BlockSpec((None, 1, d), per_e)],
            out_specs=pl.BlockSpec((ROW_BLOCK, d), rows)),
        out_shape=jax.ShapeDtypeStruct((n_rows, d), jnp.float32),
        compiler_params=pltpu.CompilerParams(dimension_semantics=("arbitrary",),
                                             vmem_limit_bytes=VMEM_LIMIT),
        name="experts",
    )(block_e, nused, xs, wu, bu, wd, bd)


def _combine_kernel(pstart_ref, e_ref, r_ref, h_ref, eg_ref, g_ref, ys_ref, o_ref, buf_ref, sem):
    t = h_ref.shape[0]

    def issue(tok, carry):
        for j in range(TOP_K):
            a = tok * TOP_K + j
            _row_copy(ys_ref, pstart_ref[e_ref[a]] + r_ref[a], buf_ref.at[j], tok, sem).start()
        return carry

    lax.fori_loop(0, t, issue, 0)

    def drain(tok, carry):
        for j in range(TOP_K):
            _row_copy(ys_ref, 0, buf_ref.at[j], 0, sem).wait()
        return carry

    lax.fori_loop(0, t, drain, 0)
    eg = eg_ref[...]
    h = h_ref[...]
    for j in range(TOP_K):
        h = h + eg[:, j:j + 1] * buf_ref[j]
    o_ref[...] = _rmsnorm(h, g_ref[...])


def _combine(pstart, e_flat, r_flat, h, eg, g, ys):
    n, d = h.shape
    t = MOVE_TILE
    smem = lambda: pl.BlockSpec((t * TOP_K,), lambda i, ps: (i,), memory_space=pltpu.SMEM)
    return pl.pallas_call(
        _combine_kernel,
        grid_spec=pltpu.PrefetchScalarGridSpec(
            num_scalar_prefetch=1,
            grid=(n // t,),
            in_specs=[smem(), smem(), pl.BlockSpec((t, d), lambda i, ps: (i, 0)),
                      pl.BlockSpec((t, 2 * TOP_K), lambda i, ps: (i, 0)),
                      pl.BlockSpec((1, d), lambda i, ps: (0, 0)),
                      pl.BlockSpec(memory_space=pl.ANY)],
            out_specs=pl.BlockSpec((t, d), lambda i, ps: (i, 0)),
            scratch_shapes=[pltpu.VMEM((TOP_K, t, d), jnp.float32), pltpu.SemaphoreType.DMA(())]),
        out_shape=jax.ShapeDtypeStruct((n, d), jnp.float32),
        compiler_params=pltpu.CompilerParams(dimension_semantics=("arbitrary",),
                                             vmem_limit_bytes=VMEM_LIMIT),
        name="combine",
    )(pstart, e_flat, r_flat, h, eg, g, ys)


def _rope_tables(seq):
    half = ROPE_DIM // 2
    inv = jnp.power(jnp.float32(ROPE_THETA), -jnp.arange(half, dtype=jnp.float32) / half)
    ang = jnp.arange(seq, dtype=jnp.float32)[:, None] * inv[None, :]
    cos, sin = jnp.cos(ang), jnp.sin(ang)
    pad = HEAD_DIM - ROPE_DIM
    one, zero = jnp.ones((seq, pad), jnp.float32), jnp.zeros((seq, pad), jnp.float32)
    zh = jnp.zeros((seq, half), jnp.float32)
    reps = LANES // HEAD_DIM
    c = jnp.tile(jnp.concatenate([cos, cos, one], axis=1), (1, reps))
    a = jnp.tile(jnp.concatenate([-sin, zh, zero], axis=1), (1, reps))
    b = jnp.tile(jnp.concatenate([zh, sin, zero], axis=1), (1, reps))
    return c, a, b


def _layer(h, mem, p, tables):
    b, s, d = h.shape
    n = b * s
    bf = jnp.bfloat16
    x2 = h.reshape(n, d)
    w_in = p["w_in"]
    o_ik = 3 * ATTN_W + IDXQ_W
    o_u = o_ik + IDX_DIM + N_HEADS_IDX
    wa = w_in[:, :o_ik].astype(bf)
    wb = jnp.pad(w_in[:, o_ik:o_u], ((0, 0), (0, LANES - (o_u - o_ik)))).astype(bf)
    wc = w_in[:, o_u:].astype(bf)
    q, k, v, iq, iklo, ikhi, iw, u, qm = _in_proj(x2, p["norm_mix_g"][None], wa, wb, wc, *tables, s)
    km, vm = _mem_kv(mem, p["mem_norm_g"][None], p["w_mem_kv"].astype(bf))
    b3 = lambda a: a.reshape(b, s, a.shape[-1])
    ya = _dsa(b3(iq), b3(iw), b3(q), b3(iklo), b3(ikhi), b3(k), b3(v)).reshape(n, ATTN_W)

    wbd = jnp.zeros((POOL_W, POOL_W), jnp.float32)
    for g in range(N_POOL_GROUPS):
        lo = g * POOL_GROUP_DIM
        wbd = wbd.at[lo:lo + POOL_GROUP_DIM, lo:lo + POOL_GROUP_DIM].set(p["pool_w_group"][g])
    wr = jnp.pad(p["w_router"], ((0, 0), (0, LANES - N_EXPERTS)))
    br = jnp.pad(p["b_router"], (0, LANES - N_EXPERTS), constant_values=NEG_BIG)[None]
    hmid, xn2, eg, ei, counts = _merge(
        x2, ya, u, qm, km, vm, p["norm_mix_g"][None], p["w_gate"].astype(bf), p["b_gate"][None],
        p["w_proj_attn"].astype(bf), p["w_proj_pool"].astype(bf), p["w_proj_mem"].astype(bf),
        wbd.astype(bf), p["pool_scale"][None], p["w_out"].astype(bf), p["norm_ffn_g"][None], wr, br, s)

    cnt = counts[0, :N_EXPERTS].astype(jnp.int32)
    padded = ((cnt + ROW_BLOCK - 1) // ROW_BLOCK) * ROW_BLOCK
    pend = jnp.cumsum(padded)
    pstart = (pend - padded).astype(jnp.int32)
    n_rows = n * TOP_K + N_EXPERTS * ROW_BLOCK
    n_blocks = n_rows // ROW_BLOCK
    nused = (pend[-1] // ROW_BLOCK).astype(jnp.int32)
    blk = jnp.minimum(jnp.arange(n_blocks, dtype=jnp.int32), nused - 1) * ROW_BLOCK
    block_e = jnp.minimum(jnp.searchsorted(pend, blk, side="right"), N_EXPERTS - 1).astype(jnp.int32)
    e_flat = ei[:, :TOP_K].reshape(-1)
    r_flat = ei[:, TOP_K:].reshape(-1)

    xs = _dispatch(pstart, e_flat, r_flat, xn2, n_rows)
    ys = _experts(block_e, nused[None], xs, p["w_up"].astype(bf), p["b_up"][:, None, :],
                  p["w_down"].astype(bf), p["b_down"][:, None, :])
    return pstart, e_flat, r_flat, hmid, eg, ys


def kernel(x, mem, norm_mix_g, w_in, w_gate, b_gate, w_proj_attn, w_proj_pool, w_proj_mem,
           pool_w_group, pool_scale, mem_norm_g, w_mem_kv, w_out, norm_ffn_g,
           w_router, b_router, w_up, b_up, w_down, b_down, norm_final_g):
    b, s, d = x.shape
    assert s % IN_TILE == 0 and s % KEY_CHUNK == 0 and (b * s) % MOVE_TILE == 0
    stacked = dict(norm_mix_g=norm_mix_g, w_in=w_in, w_gate=w_gate, b_gate=b_gate,
                   w_proj_attn=w_proj_attn, w_proj_pool=w_proj_pool, w_proj_mem=w_proj_mem,
                   pool_w_group=pool_w_group, pool_scale=pool_scale, mem_norm_g=mem_norm_g,
                   w_mem_kv=w_mem_kv, w_out=w_out, norm_ffn_g=norm_ffn_g, w_router=w_router,
                   b_router=b_router, w_up=w_up, b_up=b_up, w_down=w_down, b_down=b_down)
    depth = w_in.shape[0]
    tables = _rope_tables(s)
    h = x
    for l in range(depth):
        p = {name: val[l] for name, val in stacked.items()}
        pstart, e_flat, r_flat, hmid, eg, ys = _layer(h, mem, p, tables)
        last = l == depth - 1
        g = norm_final_g[None] if last else None
        assert last, "the combine step applies the final norm; deeper stacks need a norm-free combine"
        h = _combine(pstart, e_flat, r_flat, hmid, eg, g, ys).reshape(b, s, d)
    return h
```

```python
import functools

import jax
import jax.numpy as jnp
import numpy as np
from jax import lax
from jax.experimental import pallas as pl
from jax.experimental.pallas import tpu as pltpu

N_HEADS_ATTN = 8
HEAD_DIM = 64
ROPE_DIM = HEAD_DIM // 4
ROPE_THETA = 500000.0
N_HEADS_IDX = 8
IDX_DIM = 64
TOPK_MAX = 256
N_POOL_GROUPS = 4
POOL_GROUP_DIM = 64
POOL_WINDOWS = (2, 4, 8, 16)
POOL_HALO = 16
N_HEADS_MEM = 4
N_BRANCHES = 3
N_EXPERTS = 32
TOP_K = 4
SWIGLU_ALPHA = 1.702
SWIGLU_LIMIT = 7.0
EPS = 1e-6

ATTN_W = N_HEADS_ATTN * HEAD_DIM
POOL_W = N_POOL_GROUPS * POOL_GROUP_DIM
MEM_W = N_HEADS_MEM * HEAD_DIM
IDXQ_W = N_HEADS_IDX * IDX_DIM

LANES = 128
SUBLANES = 8
VMEM_LIMIT = 56 * 1024 * 1024

IN_TILE = 512
Q_BLOCK = 128
KEY_CHUNK = 256
COUNT_CHUNK = 512
COUNT_ROWS = 32
MERGE_TILE = 256
ROW_BLOCK = 256
MOVE_TILE = 256

NEG_BIG = -1e30
INT_MIN = -2 ** 31


def _dot(a, b):
    return jnp.dot(a, b, preferred_element_type=jnp.float32)


def _dot_nt(a, b):
    return lax.dot_general(a, b, (((1,), (1,)), ((), ())), preferred_element_type=jnp.float32)


def _rmsnorm(x, g):
    return x * lax.rsqrt(jnp.mean(x * x, axis=-1, keepdims=True) + EPS) * g


def _rope128(x, c, a, b):
    return x * c + pltpu.roll(x, LANES - ROPE_DIM // 2, 1) * a + pltpu.roll(x, ROPE_DIM // 2, 1) * b


def _in_proj_kernel(x_ref, g_ref, wa_ref, wb_ref, wc_ref, rc_ref, ra_ref, rb_ref,
                    q_ref, k_ref, vt_ref, iq_ref, iklo_ref, ikhi_ref, iw_ref, u_ref, qm_ref):
    xn = _rmsnorm(x_ref[...], g_ref[...]).astype(jnp.bfloat16)
    rc, ra, rb = rc_ref[...], ra_ref[...], rb_ref[...]
    pa = _dot(xn, wa_ref[...])
    for seg, (ref, scale) in enumerate(((q_ref, HEAD_DIM ** -0.5), (k_ref, None),
                                        (None, None), (iq_ref, IDX_DIM ** -0.5))):
        for c in range(ATTN_W // LANES):
            lo = seg * ATTN_W + c * LANES
            blk = pa[:, lo:lo + LANES]
            if ref is None:
                vt_ref[c * LANES:(c + 1) * LANES, :] = blk.T.astype(vt_ref.dtype)
                continue
            blk = _rope128(blk, rc, ra, rb)
            if scale is not None:
                blk = blk * scale
            ref[:, c * LANES:(c + 1) * LANES] = blk.astype(ref.dtype)
    pb = _dot(xn, wb_ref[...])
    lane = lax.broadcasted_iota(jnp.int32, pb.shape, 1)
    ik = jnp.where(lane < IDX_DIM, _rope128(pb, rc, ra, rb), 0.0)
    iklo_ref[...] = ik.astype(iklo_ref.dtype)
    ikhi_ref[...] = pltpu.roll(ik, IDX_DIM, 1).astype(ikhi_ref.dtype)
    iw_ref[...] = pb * (N_HEADS_IDX ** -0.5)
    pc = _dot(xn, wc_ref[...])
    u_ref[...] = pc[:, :POOL_W]
    qm_ref[...] = (pc[:, POOL_W:] * (HEAD_DIM ** -0.5)).astype(qm_ref.dtype)


def _in_proj(x2, g, wa, wb, wc, rc, ra, rb, batch, seq):
    n, d = x2.shape
    t = IN_TILE
    tiles_per_seq = seq // t
    row = lambda i: (i, 0)
    full = lambda i: (0, 0)
    pos = lambda i: (i % tiles_per_seq, 0)
    bf = jnp.bfloat16
    flat = lambda w, dt: (jax.ShapeDtypeStruct((n, w), dt), pl.BlockSpec((t, w), row))
    outs = (flat(ATTN_W, bf), flat(ATTN_W, bf),
            (jax.ShapeDtypeStruct((batch, ATTN_W, seq), bf),
             pl.BlockSpec((None, ATTN_W, t), lambda i: (i // tiles_per_seq, 0, i % tiles_per_seq))),
            flat(IDXQ_W, bf), flat(LANES, bf), flat(LANES, bf), flat(LANES, jnp.float32),
            flat(POOL_W, jnp.float32), flat(MEM_W, bf))
    return pl.pallas_call(
        _in_proj_kernel,
        grid=(n // t,),
        in_specs=[pl.BlockSpec((t, d), row), pl.BlockSpec((1, d), full),
                  pl.BlockSpec(wa.shape, full), pl.BlockSpec(wb.shape, full),
                  pl.BlockSpec(wc.shape, full),
                  pl.BlockSpec((t, LANES), pos), pl.BlockSpec((t, LANES), pos),
                  pl.BlockSpec((t, LANES), pos)],
        out_specs=tuple(o[1] for o in outs),
        out_shape=tuple(o[0] for o in outs),
        compiler_params=pltpu.CompilerParams(dimension_semantics=("arbitrary",),
                                             vmem_limit_bytes=VMEM_LIMIT),
        name="in_proj",
    )(x2, g, wa, wb, wc, rc, ra, rb)


def _mem_kv_kernel(mem_ref, g_ref, w_ref, km_ref, vm_ref):
    mn = _rmsnorm(mem_ref[...], g_ref[...]).astype(jnp.bfloat16)
    kv = _dot(mn, w_ref[...])
    km, vm = kv[:, :MEM_W], kv[:, MEM_W:]
    lane = lax.broadcasted_iota(jnp.int32, km.shape, 1)
    for h in range(N_HEADS_MEM):
        in_head = (lane >= h * HEAD_DIM) & (lane < (h + 1) * HEAD_DIM)
        km_ref[h] = jnp.where(in_head, km, 0.0).astype(km_ref.dtype)
        vm_ref[h] = jnp.where(in_head, vm, 0.0).astype(vm_ref.dtype)


def _mem_kv(mem, g, w):
    b, m, d = mem.shape
    out = jax.ShapeDtypeStruct((b, N_HEADS_MEM, m, MEM_W), jnp.bfloat16)
    return pl.pallas_call(
        _mem_kv_kernel,
        grid=(b,),
        in_specs=[pl.BlockSpec((None, m, d), lambda i: (i, 0, 0)),
                  pl.BlockSpec((1, d), lambda i: (0, 0)),
                  pl.BlockSpec(w.shape, lambda i: (0, 0))],
        out_specs=(pl.BlockSpec((None, N_HEADS_MEM, m, MEM_W), lambda i: (i, 0, 0, 0)),
                   pl.BlockSpec((None, N_HEADS_MEM, m, MEM_W), lambda i: (i, 0, 0, 0))),
        out_shape=(out, out),
        compiler_params=pltpu.CompilerParams(dimension_semantics=("arbitrary",),
                                             vmem_limit_bytes=VMEM_LIMIT),
        name="mem_kv",
    )(mem, g, w)


def _dsa_kernel(iq_ref, iw_ref, q_ref, iklo_ref, ikhi_ref, k_ref, vt_ref, o_ref,
                key_ref, iqt_ref, qt_ref, bias_ref, s_ref, acc_ref, *, n_top, idx_bits):
    qb = pl.program_id(1)
    n_chunks = (qb * Q_BLOCK + Q_BLOCK + KEY_CHUNK - 1) // KEY_CHUNK
    n_count_chunks = (qb * Q_BLOCK + Q_BLOCK + COUNT_CHUNK - 1) // COUNT_CHUNK
    q_pos = qb * Q_BLOCK + lax.broadcasted_iota(jnp.int32, (1, Q_BLOCK), 1)
    row_k = lax.broadcasted_iota(jnp.int32, (KEY_CHUNK, 1), 0)
    row_c = lax.broadcasted_iota(jnp.int32, (COUNT_CHUNK, 1), 0)
    row_d = lax.broadcasted_iota(jnp.int32, (LANES, Q_BLOCK), 0)
    bf = jnp.bfloat16

    def chunk_off(c):
        return pl.multiple_of(c * KEY_CHUNK, KEY_CHUNK)

    wt = iw_ref[...].T
    for j in range(N_HEADS_IDX // 2):
        iqt_ref[j // 2, :, (j % 2) * Q_BLOCK:(j % 2 + 1) * Q_BLOCK] = (
            iq_ref[:, j * LANES:(j + 1) * LANES].astype(jnp.float32).T.astype(bf))
    for j in range(N_HEADS_ATTN // 2):
        pair_t = q_ref[:, j * LANES:(j + 1) * LANES].astype(jnp.float32).T
        qt_ref[j, :, :Q_BLOCK] = jnp.where(row_d < HEAD_DIM, pair_t, 0.0).astype(bf)
        qt_ref[j, :, Q_BLOCK:] = jnp.where(row_d >= HEAD_DIM, pair_t, 0.0).astype(bf)

    def score_chunk(c, carry):
        off = chunk_off(c)
        acc = jnp.zeros((KEY_CHUNK, Q_BLOCK), jnp.float32)
        for parity, ik_ref in enumerate((iklo_ref, ikhi_ref)):
            ik = ik_ref[pl.ds(off, KEY_CHUNK), :]
            for g in range(N_HEADS_IDX // 4):
                dots = _dot(ik, iqt_ref[g])
                for side in range(2):
                    head = 2 * (2 * g + side) + parity
                    w = wt[IDX_DIM + head:IDX_DIM + head + 1, :]
                    acc = acc + w * jnp.maximum(dots[:, side * Q_BLOCK:(side + 1) * Q_BLOCK], 0.0)
        acc = jnp.where(off + row_k <= q_pos, acc, -jnp.inf)
        bits = lax.bitcast_convert_type(acc, jnp.int32)
        key_ref[pl.ds(off, KEY_CHUNK), :] = bits ^ ((bits >> 31) & 0x7FFFFFFF)
        return carry

    lax.fori_loop(0, n_count_chunks * (COUNT_CHUNK // KEY_CHUNK), score_chunk, 0)

    def count(pred):
        def body(c, cnt):
            off = pl.multiple_of(c * COUNT_CHUNK, COUNT_CHUNK)
            hit = jnp.where(pred(key_ref[pl.ds(off, COUNT_CHUNK), :], off + row_c), 1.0, 0.0)
            return cnt + jnp.sum(hit.reshape(COUNT_CHUNK // COUNT_ROWS, COUNT_ROWS, Q_BLOCK), axis=0)
        cnt = lax.fori_loop(0, n_count_chunks, body, jnp.zeros((COUNT_ROWS, Q_BLOCK), jnp.float32))
        return jnp.sum(cnt, axis=0, keepdims=True)

    k_f = jnp.float32(n_top)
    zero = jnp.zeros((1, Q_BLOCK), jnp.int32)
    thr = jnp.where(count(lambda kk, pos: kk >= zero) >= k_f, zero, INT_MIN)

    def thr_bit(i, cur):
        cand = cur + jnp.left_shift(jnp.int32(1), 30 - i)
        return jnp.where(count(lambda kk, pos: kk >= cand) >= k_f, cand, cur)

    thr = lax.fori_loop(0, 31, thr_bit, thr)
    need = k_f - count(lambda kk, pos: kk > thr)
    n_ties = count(lambda kk, pos: kk == thr)

    def tie_search():
        def tie_bit(i, cur):
            cand = cur + jnp.left_shift(jnp.int32(1), idx_bits - 1 - i)
            below = count(lambda kk, pos: (kk == thr) & (pos < cand))
            return jnp.where(below < need, cand, cur)
        return lax.fori_loop(0, idx_bits, tie_bit, zero)

    has_surplus_ties = jnp.max(jnp.where(n_ties > need, 1.0, 0.0)) > 0.0
    tie_pos = lax.cond(has_surplus_ties, tie_search, lambda: zero + (2 ** idx_bits - 1))

    acc_ref[...] = jnp.zeros(acc_ref.shape, jnp.float32)

    n_pairs = N_HEADS_ATTN // 2

    def attn_chunk(c, carry):
        ms, ls = carry
        off = chunk_off(c)
        kk = key_ref[pl.ds(off, KEY_CHUNK), :]
        pos = off + row_k
        sel = ((kk > thr) | ((kk == thr) & (pos <= tie_pos))) & (pos <= q_pos)
        bias = jnp.where(sel, 0.0, NEG_BIG)
        bias_ref[...] = jnp.concatenate([bias, bias], axis=1)
        for j in range(n_pairs):
            kp = k_ref[pl.ds(off, KEY_CHUNK), j * LANES:(j + 1) * LANES]
            s_ref[j] = _dot(kp, qt_ref[j]) + bias_ref[...]
        new_ms, new_ls = [], []
        for j in range(n_pairs):
            s = s_ref[j]
            m_new = jnp.maximum(ms[j], jnp.max(s, axis=0, keepdims=True))
            alpha = jnp.exp(ms[j] - m_new)
            p = jnp.exp(s - m_new)
            new_ls.append(alpha * ls[j] + jnp.sum(p, axis=0, keepdims=True))
            new_ms.append(m_new)
            vt = vt_ref[j * LANES:(j + 1) * LANES, pl.ds(off, KEY_CHUNK)]
            pv = _dot(vt, p.astype(bf))
            lo, mid, hi = j * LANES, j * LANES + HEAD_DIM, (j + 1) * LANES
            acc_ref[lo:mid, :] = alpha[:, :Q_BLOCK] * acc_ref[lo:mid, :] + pv[:HEAD_DIM, :Q_BLOCK]
            acc_ref[mid:hi, :] = alpha[:, Q_BLOCK:] * acc_ref[mid:hi, :] + pv[HEAD_DIM:, Q_BLOCK:]
        return tuple(new_ms), tuple(new_ls)

    init = (tuple(jnp.full((1, 2 * Q_BLOCK), NEG_BIG, jnp.float32) for _ in range(n_pairs)),
            tuple(jnp.zeros((1, 2 * Q_BLOCK), jnp.float32) for _ in range(n_pairs)))
    _, ls = lax.fori_loop(0, n_chunks, attn_chunk, init)
    for j in range(n_pairs):
        lo, mid, hi = j * LANES, j * LANES + HEAD_DIM, (j + 1) * LANES
        even = acc_ref[lo:mid, :] / ls[j][:, :Q_BLOCK]
        odd = acc_ref[mid:hi, :] / ls[j][:, Q_BLOCK:]
        o_ref[:, lo:hi] = jnp.concatenate([even, odd], axis=0).T.astype(o_ref.dtype)


def _dsa(iq, iw, q, iklo, ikhi, k, vt):
    b, s, _ = q.shape
    n_top = min(TOPK_MAX, s // 4)
    idx_bits = max(1, int(np.ceil(np.log2(s))))
    qblk = lambda w: pl.BlockSpec((None, Q_BLOCK, w), lambda bi, qi: (bi, qi, 0))
    keys = lambda w: pl.BlockSpec((None, s, w), lambda bi, qi: (bi, 0, 0))
    return pl.pallas_call(
        functools.partial(_dsa_kernel, n_top=n_top, idx_bits=idx_bits),
        grid=(b, s // Q_BLOCK),
        in_specs=[qblk(IDXQ_W), qblk(LANES), qblk(ATTN_W), keys(LANES), keys(LANES),
                  keys(ATTN_W), pl.BlockSpec((None, ATTN_W, s), lambda bi, qi: (bi, 0, 0))],
        out_specs=qblk(ATTN_W),
        out_shape=jax.ShapeDtypeStruct((b, s, ATTN_W), jnp.bfloat16),
        scratch_shapes=[pltpu.VMEM((s, Q_BLOCK), jnp.int32),
                        pltpu.VMEM((N_HEADS_IDX // 4, LANES, 2 * Q_BLOCK), jnp.bfloat16),
                        pltpu.VMEM((N_HEADS_ATTN // 2, LANES, 2 * Q_BLOCK), jnp.bfloat16),
                        pltpu.VMEM((KEY_CHUNK, 2 * Q_BLOCK), jnp.float32),
                        pltpu.VMEM((N_HEADS_ATTN // 2, KEY_CHUNK, 2 * Q_BLOCK), jnp.float32),
                        pltpu.VMEM((ATTN_W, Q_BLOCK), jnp.float32)],
        compiler_params=pltpu.CompilerParams(dimension_semantics=("arbitrary", "arbitrary"),
                                             vmem_limit_bytes=VMEM_LIMIT),
        name="dsa",
    )(iq, iw, q, iklo, ikhi, k, vt)


def _merge_kernel(x_ref, ya_ref, u_ref, uprev_ref, qm_ref, km_ref, vm_ref,
                  gmix_ref, wg_ref, bg_ref, wpa_ref, wpp_ref, wpm_ref, wbd_ref, psc_ref,
                  wo_ref, gffn_ref, wr_ref, br_ref,
                  h_ref, xn2_ref, eg_ref, ei_ref, cnt_ref, carry_ref, *, tiles_per_seq):
    i = pl.program_id(0)
    t = x_ref.shape[0]
    d = x_ref.shape[1]
    bf = jnp.bfloat16
    x = x_ref[...]
    xn = _rmsnorm(x, gmix_ref[...]).astype(bf)

    tile_in_seq = i % tiles_per_seq
    u = u_ref[...]
    halo = jnp.where(tile_in_seq == 0, 0.0, uprev_ref[...])
    ext = jnp.concatenate([halo, u], axis=0)
    lane_p = lax.broadcasted_iota(jnp.int32, (t, POOL_W), 1)
    pos1 = (tile_in_seq * t + lax.broadcasted_iota(jnp.int32, (t, 1), 0) + 1).astype(jnp.float32)
    pooled = None
    run, width = ext, 1
    for g, w in enumerate(POOL_WINDOWS):
        while width < w:
            run = run[width:] + run[:-width]
            width *= 2
        start = POOL_HALO + 1 - w
        mean = run[start:start + t] / jnp.minimum(pos1, float(w))
        pooled = mean if pooled is None else jnp.where(lane_p >= g * POOL_GROUP_DIM, mean, pooled)
    pooled = pooled - u
    mixed = _dot(pooled.astype(bf), wbd_ref[...]) * psc_ref[...]
    y_pool = _dot(mixed.astype(bf), wpp_ref[...])

    qm = qm_ref[...]
    probs = []
    for h in range(N_HEADS_MEM):
        s = _dot_nt(qm, km_ref[h])
        p = jnp.exp(s - jnp.max(s, axis=1, keepdims=True))
        probs.append((p / jnp.sum(p, axis=1, keepdims=True)).astype(bf))
    y_mem = _dot(probs[0], vm_ref[0])
    for h in range(1, N_HEADS_MEM):
        y_mem = y_mem + _dot(probs[h], vm_ref[h])
    y_mem = _dot(y_mem.astype(bf), wpm_ref[...])

    y_attn = _dot(ya_ref[...], wpa_ref[...])

    def gate(br):
        z = _dot(xn, wg_ref[:, br * d:(br + 1) * d]) + bg_ref[:, br * d:(br + 1) * d]
        return 1.0 / (1.0 + jnp.exp(-z))

    merged = gate(0) * y_attn + gate(1) * y_pool + gate(2) * y_mem
    h = x + _dot(merged.astype(bf), wo_ref[...])
    h_ref[...] = h
    xn2 = _rmsnorm(h, gffn_ref[...])
    xn2_ref[...] = xn2

    logits = jnp.dot(xn2, wr_ref[...], precision=lax.Precision.HIGHEST,
                     preferred_element_type=jnp.float32) + br_ref[...]
    lane_e = lax.broadcasted_iota(jnp.int32, logits.shape, 1).astype(jnp.float32)
    work = logits
    vals, onehots, ids = [], [], []
    for _ in range(TOP_K):
        mx = jnp.max(work, axis=1, keepdims=True)
        idx = jnp.min(jnp.where(work == mx, lane_e, float(LANES)), axis=1, keepdims=True)
        oh = lane_e == idx
        vals.append(mx)
        ids.append(idx)
        onehots.append(oh)
        work = jnp.where(oh, -jnp.inf, work)
    exps = [jnp.exp(v - vals[0]) for v in vals]
    denom = exps[0] + exps[1] + exps[2] + exps[3]

    @pl.when(i == 0)
    def _():
        carry_ref[...] = jnp.zeros(carry_ref.shape, jnp.float32)

    member = jnp.where(onehots[0] | onehots[1] | onehots[2] | onehots[3], 1.0, 0.0)
    r_io = lax.broadcasted_iota(jnp.int32, (t, t), 0)
    c_io = lax.broadcasted_iota(jnp.int32, (t, t), 1)
    earlier = jnp.where(c_io < r_io, 1.0, 0.0).astype(bf)
    before = _dot(earlier, member.astype(bf)) + carry_ref[...]
    carry_ref[...] = carry_ref[...] + jnp.sum(member, axis=0, keepdims=True)
    cnt_ref[...] = carry_ref[...]

    lane8 = lax.broadcasted_iota(jnp.int32, (t, 2 * TOP_K), 1)
    eg = jnp.zeros((t, 2 * TOP_K), jnp.float32)
    ei = jnp.zeros((t, 2 * TOP_K), jnp.float32)
    for j in range(TOP_K):
        rank = jnp.sum(jnp.where(onehots[j], before, 0.0), axis=1, keepdims=True)
        eg = jnp.where(lane8 == j, exps[j] / denom, eg)
        ei = jnp.where(lane8 == j, ids[j], ei)
        ei = jnp.where(lane8 == TOP_K + j, rank, ei)
    eg_ref[...] = eg
    ei_ref[...] = ei.astype(jnp.int32)


def _merge(x2, ya, u, qm, km, vm, gmix, wg, bg, wpa, wpp, wpm, wbd, psc, wo, gffn, wr, br, seq):
    n, d = x2.shape
    t = MERGE_TILE
    tiles_per_seq = seq // t
    m = km.shape[2]
    row = lambda i: (i, 0)
    full = lambda i: (0, 0)
    per_batch = lambda i: (i // tiles_per_seq, 0, 0, 0)
    halo_blocks = t // POOL_HALO
    prev = lambda i: (jnp.maximum(i * halo_blocks - 1, 0), 0)
    consts = (gmix, wg, bg, wpa, wpp, wpm, wbd, psc, wo, gffn, wr, br)
    out_shape = (jax.ShapeDtypeStruct((n, d), jnp.float32), jax.ShapeDtypeStruct((n, d), jnp.float32),
                 jax.ShapeDtypeStruct((n, 2 * TOP_K), jnp.float32),
                 jax.ShapeDtypeStruct((n, 2 * TOP_K), jnp.int32),
                 jax.ShapeDtypeStruct((1, LANES), jnp.float32))
    return pl.pallas_call(
        functools.partial(_merge_kernel, tiles_per_seq=tiles_per_seq),
        grid=(n // t,),
        in_specs=[pl.BlockSpec((t, d), row), pl.BlockSpec((t, ATTN_W), row),
                  pl.BlockSpec((t, POOL_W), row), pl.BlockSpec((POOL_HALO, POOL_W), prev),
                  pl.BlockSpec((t, MEM_W), row),
                  pl.BlockSpec((None, N_HEADS_MEM, m, MEM_W), per_batch),
                  pl.BlockSpec((None, N_HEADS_MEM, m, MEM_W), per_batch)]
                 + [pl.BlockSpec(c.shape, full) for c in consts],
        out_specs=(pl.BlockSpec((t, d), row), pl.BlockSpec((t, d), row),
                   pl.BlockSpec((t, 2 * TOP_K), row), pl.BlockSpec((t, 2 * TOP_K), row),
                   pl.BlockSpec((1, LANES), full)),
        out_shape=out_shape,
        scratch_shapes=[pltpu.VMEM((1, LANES), jnp.float32)],
        compiler_params=pltpu.CompilerParams(dimension_semantics=("arbitrary",),
                                             vmem_limit_bytes=VMEM_LIMIT),
        name="merge",
    )(x2, ya, u, u, qm, km, vm, *consts)


def _row_copy(src_ref, src_row, dst_ref, dst_row, sem):
    return pltpu.make_async_copy(src_ref.at[pl.ds(src_row, 1)], dst_ref.at[pl.ds(dst_row, 1)], sem)


def _dispatch_kernel(pstart_ref, e_ref, r_ref, x_ref, init_ref, xs_ref, sem):
    del init_ref
    t = x_ref.shape[0]

    def issue(tok, carry):
        for j in range(TOP_K):
            a = tok * TOP_K + j
            _row_copy(x_ref, tok, xs_ref, pstart_ref[e_ref[a]] + r_ref[a], sem).start()
        return carry

    lax.fori_loop(0, t, issue, 0)

    def drain(tok, carry):
        for j in range(TOP_K):
            _row_copy(x_ref, 0, xs_ref, 0, sem).wait()
        return carry

    lax.fori_loop(0, t, drain, 0)


def _dispatch(pstart, e_flat, r_flat, xn2, n_rows):
    n, d = xn2.shape
    t = MOVE_TILE
    smem = lambda: pl.BlockSpec((t * TOP_K,), lambda i, ps: (i,), memory_space=pltpu.SMEM)
    return pl.pallas_call(
        _dispatch_kernel,
        grid_spec=pltpu.PrefetchScalarGridSpec(
            num_scalar_prefetch=1,
            grid=(n // t,),
            in_specs=[smem(), smem(), pl.BlockSpec((t, d), lambda i, ps: (i, 0)),
                      pl.BlockSpec(memory_space=pl.ANY)],
            out_specs=pl.BlockSpec(memory_space=pl.ANY),
            scratch_shapes=[pltpu.SemaphoreType.DMA(())]),
        out_shape=jax.ShapeDtypeStruct((n_rows, d), jnp.float32),
        input_output_aliases={4: 0},
        compiler_params=pltpu.CompilerParams(dimension_semantics=("arbitrary",),
                                             vmem_limit_bytes=VMEM_LIMIT),
        name="dispatch",
    )(pstart, e_flat, r_flat, xn2, jnp.zeros((n_rows, d), jnp.float32))


def _experts_kernel(be_ref, nused_ref, xs_ref, wu_ref, bu_ref, wd_ref, bd_ref, ys_ref):
    i = pl.program_id(0)
    f = wd_ref.shape[0]

    @pl.when(i < nused_ref[0])
    def _():
        hb = _dot(xs_ref[...].astype(jnp.bfloat16), wu_ref[...]) + bu_ref[...]
        glu = jnp.minimum(hb[:, :f], SWIGLU_LIMIT)
        lin = jnp.clip(hb[:, f:], -SWIGLU_LIMIT, SWIGLU_LIMIT)
        act = glu * (1.0 / (1.0 + jnp.exp(-SWIGLU_ALPHA * glu))) * (lin + 1.0)
        ys_ref[...] = _dot(act.astype(jnp.bfloat16), wd_ref[...]) + bd_ref[...]

    @pl.when(i >= nused_ref[0])
    def _():
        ys_ref[...] = jnp.zeros(ys_ref.shape, ys_ref.dtype)


def _experts(block_e, nused, xs, wu, bu, wd, bd):
    n_rows, d = xs.shape
    e, _, f2 = wu.shape
    f = wd.shape[1]
    rows = lambda i, be, nu: (i, 0)
    per_e = lambda i, be, nu: (be[i], 0, 0)
    return pl.pallas_call(
        _experts_kernel,
        grid_spec=pltpu.PrefetchScalarGridSpec(
            num_scalar_prefetch=2,
            grid=(n_rows // ROW_BLOCK,),
            in_specs=[pl.BlockSpec((ROW_BLOCK, d), rows),
                      pl.BlockSpec((None, d, f2), per_e), pl.BlockSpec((None, 1, f2), per_e),
                      pl.BlockSpec((None, f, d), per_e), pl.BlockSpec((None, 1, d), per_e)],
            out_specs=pl.BlockSpec((ROW_BLOCK, d), rows)),
        out_shape=jax.ShapeDtypeStruct((n_rows, d), jnp.float32),
        compiler_params=pltpu.CompilerParams(dimension_semantics=("arbitrary",),
                                             vmem_limit_bytes=VMEM_LIMIT),
        name="experts",
    )(block_e, nused, xs, wu, bu, wd, bd)


def _combine_kernel(pstart_ref, e_ref, r_ref, h_ref, eg_ref, g_ref, ys_ref, o_ref, buf_ref, sem,
                    *, final_norm):
    t = h_ref.shape[0]

    def issue(tok, carry):
        for j in range(TOP_K):
            a = tok * TOP_K + j
            _row_copy(ys_ref, pstart_ref[e_ref[a]] + r_ref[a], buf_ref.at[j], tok, sem).start()
        return carry

    lax.fori_loop(0, t, issue, 0)

    def drain(tok, carry):
        for j in range(TOP_K):
            _row_copy(ys_ref, 0, buf_ref.at[j], 0, sem).wait()
        return carry

    lax.fori_loop(0, t, drain, 0)
    eg = eg_ref[...]
    h = h_ref[...]
    for j in range(TOP_K):
        h = h + eg[:, j:j + 1] * buf_ref[j]
    o_ref[...] = _rmsnorm(h, g_ref[...]) if final_norm else h


def _combine(pstart, e_flat, r_flat, h, eg, g, ys, final_norm):
    n, d = h.shape
    t = MOVE_TILE
    smem = lambda: pl.BlockSpec((t * TOP_K,), lambda i, ps: (i,), memory_space=pltpu.SMEM)
    return pl.pallas_call(
        functools.partial(_combine_kernel, final_norm=final_norm),
        grid_spec=pltpu.PrefetchScalarGridSpec(
            num_scalar_prefetch=1,
            grid=(n // t,),
            in_specs=[smem(), smem(), pl.BlockSpec((t, d), lambda i, ps: (i, 0)),
                      pl.BlockSpec((t, 2 * TOP_K), lambda i, ps: (i, 0)),
                      pl.BlockSpec((1, d), lambda i, ps: (0, 0)),
                      pl.BlockSpec(memory_space=pl.ANY)],
            out_specs=pl.BlockSpec((t, d), lambda i, ps: (i, 0)),
            scratch_shapes=[pltpu.VMEM((TOP_K, t, d), jnp.float32), pltpu.SemaphoreType.DMA(())]),
        out_shape=jax.ShapeDtypeStruct((n, d), jnp.float32),
        compiler_params=pltpu.CompilerParams(dimension_semantics=("arbitrary",),
                                             vmem_limit_bytes=VMEM_LIMIT),
        name="combine",
    )(pstart, e_flat, r_flat, h, eg, g, ys)


def _rope_tables(seq):
    half = ROPE_DIM // 2
    inv = jnp.power(jnp.float32(ROPE_THETA), -jnp.arange(half, dtype=jnp.float32) / half)
    ang = jnp.arange(seq, dtype=jnp.float32)[:, None] * inv[None, :]
    cos, sin = jnp.cos(ang), jnp.sin(ang)
    pad = HEAD_DIM - ROPE_DIM
    one, zero = jnp.ones((seq, pad), jnp.float32), jnp.zeros((seq, pad), jnp.float32)
    zh = jnp.zeros((seq, half), jnp.float32)
    reps = LANES // HEAD_DIM
    c = jnp.tile(jnp.concatenate([cos, cos, one], axis=1), (1, reps))
    a = jnp.tile(jnp.concatenate([-sin, zh, zero], axis=1), (1, reps))
    b = jnp.tile(jnp.concatenate([zh, sin, zero], axis=1), (1, reps))
    return c, a, b


def _layer(h, mem, p, tables, final_g):
    b, s, d = h.shape
    n = b * s
    bf = jnp.bfloat16
    x2 = h.reshape(n, d)
    w_in = p["w_in"]
    o_ik = 3 * ATTN_W + IDXQ_W
    o_u = o_ik + IDX_DIM + N_HEADS_IDX
    wa = w_in[:, :o_ik].astype(bf)
    wb = jnp.pad(w_in[:, o_ik:o_u], ((0, 0), (0, LANES - (o_u - o_ik)))).astype(bf)
    wc = w_in[:, o_u:].astype(bf)
    q, k, vt, iq, iklo, ikhi, iw, u, qm = _in_proj(x2, p["norm_mix_g"][None], wa, wb, wc, *tables, b, s)
    km, vm = _mem_kv(mem, p["mem_norm_g"][None], p["w_mem_kv"].astype(bf))
    b3 = lambda a: a.reshape(b, s, a.shape[-1])
    ya = _dsa(b3(iq), b3(iw), b3(q), b3(iklo), b3(ikhi), b3(k), vt).reshape(n, ATTN_W)

    wbd = jnp.zeros((POOL_W, POOL_W), jnp.float32)
    for g in range(N_POOL_GROUPS):
        lo = g * POOL_GROUP_DIM
        wbd = wbd.at[lo:lo + POOL_GROUP_DIM, lo:lo + POOL_GROUP_DIM].set(p["pool_w_group"][g])
    wr = jnp.pad(p["w_router"], ((0, 0), (0, LANES - N_EXPERTS)))
    br = jnp.pad(p["b_router"], (0, LANES - N_EXPERTS), constant_values=NEG_BIG)[None]
    hmid, xn2, eg, ei, counts = _merge(
        x2, ya, u, qm, km, vm, p["norm_mix_g"][None], p["w_gate"].astype(bf), p["b_gate"][None],
        p["w_proj_attn"].astype(bf), p["w_proj_pool"].astype(bf), p["w_proj_mem"].astype(bf),
        wbd.astype(bf), p["pool_scale"][None], p["w_out"].astype(bf), p["norm_ffn_g"][None], wr, br, s)

    cnt = counts[0, :N_EXPERTS].astype(jnp.int32)
    padded = ((cnt + ROW_BLOCK - 1) // ROW_BLOCK) * ROW_BLOCK
    pend = jnp.cumsum(padded)
    pstart = (pend - padded).astype(jnp.int32)
    n_rows = n * TOP_K + N_EXPERTS * ROW_BLOCK
    n_blocks = n_rows // ROW_BLOCK
    nused = (pend[-1] // ROW_BLOCK).astype(jnp.int32)
    blk = jnp.minimum(jnp.arange(n_blocks, dtype=jnp.int32), nused - 1) * ROW_BLOCK
    block_e = jnp.sum(blk[:, None] >= pend[None, :], axis=1).astype(jnp.int32)
    block_e = jnp.minimum(block_e, N_EXPERTS - 1)
    e_flat = ei[:, :TOP_K].reshape(-1)
    r_flat = ei[:, TOP_K:].reshape(-1)

    xs = _dispatch(pstart, e_flat, r_flat, xn2, n_rows)
    ys = _experts(block_e, nused[None], xs, p["w_up"].astype(bf), p["b_up"][:, None, :],
                  p["w_down"].astype(bf), p["b_down"][:, None, :])
    g = p["norm_ffn_g"][None] if final_g is None else final_g[None]
    return _combine(pstart, e_flat, r_flat, hmid, eg, g, ys, final_g is not None).reshape(b, s, d)


def kernel(x, mem, norm_mix_g, w_in, w_gate, b_gate, w_proj_attn, w_proj_pool, w_proj_mem,
           pool_w_group, pool_scale, mem_norm_g, w_mem_kv, w_out, norm_ffn_g,
           w_router, b_router, w_up, b_up, w_down, b_down, norm_final_g):
    b, s, d = x.shape
    assert s % IN_TILE == 0 and s % KEY_CHUNK == 0 and (b * s) % MOVE_TILE == 0
    stacked = dict(norm_mix_g=norm_mix_g, w_in=w_in, w_gate=w_gate, b_gate=b_gate,
                   w_proj_attn=w_proj_attn, w_proj_pool=w_proj_pool, w_proj_mem=w_proj_mem,
                   pool_w_group=pool_w_group, pool_scale=pool_scale, mem_norm_g=mem_norm_g,
                   w_mem_kv=w_mem_kv, w_out=w_out, norm_ffn_g=norm_ffn_g, w_router=w_router,
                   b_router=b_router, w_up=w_up, b_up=b_up, w_down=w_down, b_down=b_down)
    depth = w_in.shape[0]
    tables = _rope_tables(s)
    h = x
    for l in range(depth):
        p = {name: val[l] for name, val in stacked.items()}
        h = _layer(h, mem, p, tables, norm_final_g if l == depth - 1 else None)
    return h
```

```python
import functools

import jax
import jax.numpy as jnp
import numpy as np
from jax import lax
from jax.experimental import pallas as pl
from jax.experimental.pallas import tpu as pltpu

N_HEADS_ATTN = 8
HEAD_DIM = 64
ROPE_DIM = HEAD_DIM // 4
ROPE_THETA = 500000.0
N_HEADS_IDX = 8
IDX_DIM = 64
TOPK_MAX = 256
N_POOL_GROUPS = 4
POOL_GROUP_DIM = 64
POOL_WINDOWS = (2, 4, 8, 16)
POOL_HALO = 16
N_HEADS_MEM = 4
N_BRANCHES = 3
N_EXPERTS = 32
TOP_K = 4
SWIGLU_ALPHA = 1.702
SWIGLU_LIMIT = 7.0
EPS = 1e-6

ATTN_W = N_HEADS_ATTN * HEAD_DIM
POOL_W = N_POOL_GROUPS * POOL_GROUP_DIM
MEM_W = N_HEADS_MEM * HEAD_DIM
IDXQ_W = N_HEADS_IDX * IDX_DIM

LANES = 128
SUBLANES = 8
VMEM_LIMIT = 56 * 1024 * 1024

IN_TILE = 512
Q_BLOCK = 128
KEY_CHUNK = 256
COUNT_CHUNK = 512
COUNT_ROWS = 32
MERGE_TILE = 256
ROW_BLOCK = 256
MOVE_TILE = 256

VT_ROWS = LANES + 16
VT_ALL = (N_HEADS_ATTN // 2) * VT_ROWS
LOG2_E = 1.4426950408889634

NEG_BIG = -1e30
INT_MIN = -2 ** 31


def _dot(a, b):
    return jnp.dot(a, b, preferred_element_type=jnp.float32)


def _dot_nt(a, b):
    return lax.dot_general(a, b, (((1,), (1,)), ((), ())), preferred_element_type=jnp.float32)


def _rmsnorm(x, g):
    return x * lax.rsqrt(jnp.mean(x * x, axis=-1, keepdims=True) + EPS) * g


def _rope128(x, c, a, b):
    return x * c + pltpu.roll(x, LANES - ROPE_DIM // 2, 1) * a + pltpu.roll(x, ROPE_DIM // 2, 1) * b


def _in_proj_kernel(x_ref, g_ref, wa_ref, wb_ref, wc_ref, rc_ref, ra_ref, rb_ref,
                    q_ref, k_ref, vt_ref, iq_ref, iklo_ref, ikhi_ref, iw_ref, u_ref, qm_ref):
    xn = _rmsnorm(x_ref[...], g_ref[...]).astype(jnp.bfloat16)
    rc, ra, rb = rc_ref[...], ra_ref[...], rb_ref[...]
    pa = _dot(xn, wa_ref[...])
    for seg, (ref, scale) in enumerate(((q_ref, LOG2_E * HEAD_DIM ** -0.5), (k_ref, None),
                                        (None, None), (iq_ref, IDX_DIM ** -0.5))):
        for c in range(ATTN_W // LANES):
            lo = seg * ATTN_W + c * LANES
            blk = pa[:, lo:lo + LANES]
            if ref is None:
                vt_ref[c * VT_ROWS:c * VT_ROWS + LANES, :] = blk.T.astype(vt_ref.dtype)
                vt_ref[c * VT_ROWS + LANES:(c + 1) * VT_ROWS, :] = jnp.ones(
                    (VT_ROWS - LANES, blk.shape[0]), vt_ref.dtype)
                continue
            blk = _rope128(blk, rc, ra, rb)
            if scale is not None:
                blk = blk * scale
            ref[:, c * LANES:(c + 1) * LANES] = blk.astype(ref.dtype)
    pb = _dot(xn, wb_ref[...])
    lane = lax.broadcasted_iota(jnp.int32, pb.shape, 1)
    ik = jnp.where(lane < IDX_DIM, _rope128(pb, rc, ra, rb), 0.0)
    iklo_ref[...] = ik.astype(iklo_ref.dtype)
    ikhi_ref[...] = pltpu.roll(ik, IDX_DIM, 1).astype(ikhi_ref.dtype)
    iw_ref[...] = pb * (N_HEADS_IDX ** -0.5)
    pc = _dot(xn, wc_ref[...])
    u_ref[...] = pc[:, :POOL_W]
    qm_ref[...] = (pc[:, POOL_W:] * (HEAD_DIM ** -0.5)).astype(qm_ref.dtype)


def _in_proj(x2, g, wa, wb, wc, rc, ra, rb, batch, seq):
    n, d = x2.shape
    t = IN_TILE
    tiles_per_seq = seq // t
    row = lambda i: (i, 0)
    full = lambda i: (0, 0)
    pos = lambda i: (i % tiles_per_seq, 0)
    bf = jnp.bfloat16
    flat = lambda w, dt: (jax.ShapeDtypeStruct((n, w), dt), pl.BlockSpec((t, w), row))
    outs = (flat(ATTN_W, bf), flat(ATTN_W, bf),
            (jax.ShapeDtypeStruct((batch, VT_ALL, seq), bf),
             pl.BlockSpec((None, VT_ALL, t), lambda i: (i // tiles_per_seq, 0, i % tiles_per_seq))),
            flat(IDXQ_W, bf), flat(LANES, bf), flat(LANES, bf), flat(LANES, jnp.float32),
            flat(POOL_W, jnp.float32), flat(MEM_W, bf))
    return pl.pallas_call(
        _in_proj_kernel,
        grid=(n // t,),
        in_specs=[pl.BlockSpec((t, d), row), pl.BlockSpec((1, d), full),
                  pl.BlockSpec(wa.shape, full), pl.BlockSpec(wb.shape, full),
                  pl.BlockSpec(wc.shape, full),
                  pl.BlockSpec((t, LANES), pos), pl.BlockSpec((t, LANES), pos),
                  pl.BlockSpec((t, LANES), pos)],
        out_specs=tuple(o[1] for o in outs),
        out_shape=tuple(o[0] for o in outs),
        compiler_params=pltpu.CompilerParams(dimension_semantics=("arbitrary",),
                                             vmem_limit_bytes=VMEM_LIMIT),
        name="in_proj",
    )(x2, g, wa, wb, wc, rc, ra, rb)


def _mem_kv_kernel(mem_ref, g_ref, w_ref, km_ref, vm_ref):
    mn = _rmsnorm(mem_ref[...], g_ref[...]).astype(jnp.bfloat16)
    kv = _dot(mn, w_ref[...])
    km, vm = kv[:, :MEM_W], kv[:, MEM_W:]
    lane = lax.broadcasted_iota(jnp.int32, km.shape, 1)
    for h in range(N_HEADS_MEM):
        in_head = (lane >= h * HEAD_DIM) & (lane < (h + 1) * HEAD_DIM)
        km_ref[h] = jnp.where(in_head, km, 0.0).astype(km_ref.dtype)
        vm_ref[h] = jnp.where(in_head, vm, 0.0).astype(vm_ref.dtype)


def _mem_kv(mem, g, w):
    b, m, d = mem.shape
    out = jax.ShapeDtypeStruct((b, N_HEADS_MEM, m, MEM_W), jnp.bfloat16)
    return pl.pallas_call(
        _mem_kv_kernel,
        grid=(b,),
        in_specs=[pl.BlockSpec((None, m, d), lambda i: (i, 0, 0)),
                  pl.BlockSpec((1, d), lambda i: (0, 0)),
                  pl.BlockSpec(w.shape, lambda i: (0, 0))],
        out_specs=(pl.BlockSpec((None, N_HEADS_MEM, m, MEM_W), lambda i: (i, 0, 0, 0)),
                   pl.BlockSpec((None, N_HEADS_MEM, m, MEM_W), lambda i: (i, 0, 0, 0))),
        out_shape=(out, out),
        compiler_params=pltpu.CompilerParams(dimension_semantics=("arbitrary",),
                                             vmem_limit_bytes=VMEM_LIMIT),
        name="mem_kv",
    )(mem, g, w)


def _dsa_kernel(iq_ref, iw_ref, q_ref, iklo_ref, ikhi_ref, k_ref, vt_ref, o_ref,
                key_ref, iqt_ref, qt_ref, bias_ref, s_ref, acc_ref, *, n_top, idx_bits):
    qb = pl.program_id(1)
    n_chunks = (qb * Q_BLOCK + Q_BLOCK + KEY_CHUNK - 1) // KEY_CHUNK
    n_count_chunks = (qb * Q_BLOCK + Q_BLOCK + COUNT_CHUNK - 1) // COUNT_CHUNK
    q_pos = qb * Q_BLOCK + lax.broadcasted_iota(jnp.int32, (1, Q_BLOCK), 1)
    row_k = lax.broadcasted_iota(jnp.int32, (KEY_CHUNK, 1), 0)
    row_c = lax.broadcasted_iota(jnp.int32, (COUNT_CHUNK, 1), 0)
    row_d = lax.broadcasted_iota(jnp.int32, (LANES, Q_BLOCK), 0)
    bf = jnp.bfloat16

    def chunk_off(c):
        return pl.multiple_of(c * KEY_CHUNK, KEY_CHUNK)

    wt = iw_ref[...].T
    for j in range(N_HEADS_IDX // 2):
        iqt_ref[j // 2, :, (j % 2) * Q_BLOCK:(j % 2 + 1) * Q_BLOCK] = (
            iq_ref[:, j * LANES:(j + 1) * LANES].astype(jnp.float32).T.astype(bf))
    for j in range(N_HEADS_ATTN // 2):
        pair_t = q_ref[:, j * LANES:(j + 1) * LANES].astype(jnp.float32).T
        qt_ref[j, :, :Q_BLOCK] = jnp.where(row_d < HEAD_DIM, pair_t, 0.0).astype(bf)
        qt_ref[j, :, Q_BLOCK:] = jnp.where(row_d >= HEAD_DIM, pair_t, 0.0).astype(bf)

    def score_chunk(c, carry):
        off = pl.multiple_of(c * COUNT_CHUNK, COUNT_CHUNK)
        acc = jnp.zeros((COUNT_CHUNK, Q_BLOCK), jnp.float32)
        for parity, ik_ref in enumerate((iklo_ref, ikhi_ref)):
            ik = ik_ref[pl.ds(off, COUNT_CHUNK), :]
            for g in range(N_HEADS_IDX // 4):
                dots = _dot(ik, iqt_ref[g])
                for side in range(2):
                    head = 2 * (2 * g + side) + parity
                    w = wt[IDX_DIM + head:IDX_DIM + head + 1, :]
                    acc = acc + w * jnp.maximum(dots[:, side * Q_BLOCK:(side + 1) * Q_BLOCK], 0.0)
        acc = jnp.where(off + row_c <= q_pos, acc, -jnp.inf)
        bits = lax.bitcast_convert_type(acc, jnp.int32)
        key_ref[pl.ds(off, COUNT_CHUNK), :] = bits ^ ((bits >> 31) & 0x7FFFFFFF)
        return carry

    lax.fori_loop(0, n_count_chunks, score_chunk, 0)

    def count(pred):
        def body(c, cnt):
            off = pl.multiple_of(c * COUNT_CHUNK, COUNT_CHUNK)
            hit = jnp.where(pred(key_ref[pl.ds(off, COUNT_CHUNK), :], off + row_c), 1.0, 0.0)
            return cnt + jnp.sum(hit.reshape(COUNT_CHUNK // COUNT_ROWS, COUNT_ROWS, Q_BLOCK), axis=0)
        cnt = lax.fori_loop(0, n_count_chunks, body, jnp.zeros((COUNT_ROWS, Q_BLOCK), jnp.float32))
        return jnp.sum(cnt, axis=0, keepdims=True)

    k_f = jnp.float32(n_top)
    zero = jnp.zeros((1, Q_BLOCK), jnp.int32)
    thr = jnp.where(count(lambda kk, pos: kk >= zero) >= k_f, zero, INT_MIN)

    def thr_bit(i, cur):
        cand = cur + jnp.left_shift(jnp.int32(1), 30 - i)
        return jnp.where(count(lambda kk, pos: kk >= cand) >= k_f, cand, cur)

    thr = lax.fori_loop(0, 31, thr_bit, thr)
    need = k_f - count(lambda kk, pos: kk > thr)
    n_ties = count(lambda kk, pos: kk == thr)

    def tie_search():
        def tie_bit(i, cur):
            cand = cur + jnp.left_shift(jnp.int32(1), idx_bits - 1 - i)
            below = count(lambda kk, pos: (kk == thr) & (pos < cand))
            return jnp.where(below < need, cand, cur)
        return lax.fori_loop(0, idx_bits, tie_bit, zero)

    has_surplus_ties = jnp.max(jnp.where(n_ties > need, 1.0, 0.0)) > 0.0
    tie_pos = lax.cond(has_surplus_ties, tie_search, lambda: zero + (2 ** idx_bits - 1))

    acc_ref[...] = jnp.zeros(acc_ref.shape, jnp.float32)

    n_pairs = N_HEADS_ATTN // 2

    def attn_chunk(c, carry):
        ms, ls = carry
        off = chunk_off(c)
        kk = key_ref[pl.ds(off, KEY_CHUNK), :]
        pos = off + row_k
        sel = ((kk > thr) | ((kk == thr) & (pos <= tie_pos))) & (pos <= q_pos)
        bias = jnp.where(sel, 0.0, NEG_BIG)
        bias_ref[...] = jnp.concatenate([bias, bias], axis=1)
        for j in range(n_pairs):
            kp = k_ref[pl.ds(off, KEY_CHUNK), j * LANES:(j + 1) * LANES]
            s_ref[j] = _dot(kp, qt_ref[j]) + bias_ref[...]
        new_ms, new_ls = [], []
        for j in range(n_pairs):
            s = s_ref[j]
            m_new = jnp.maximum(ms[j], jnp.max(s, axis=0, keepdims=True))
            alpha = jnp.exp2(ms[j] - m_new)
            p = jnp.exp2(s - m_new).astype(bf)
            vt = vt_ref[j * VT_ROWS:(j + 1) * VT_ROWS, pl.ds(off, KEY_CHUNK)]
            pv = _dot(vt, p)
            new_ls.append(alpha * ls[j] + pv[LANES:LANES + 1, :])
            new_ms.append(m_new)
            lo, mid, hi = j * LANES, j * LANES + HEAD_DIM, (j + 1) * LANES
            acc_ref[lo:mid, :] = alpha[:, :Q_BLOCK] * acc_ref[lo:mid, :] + pv[:HEAD_DIM, :Q_BLOCK]
            acc_ref[mid:hi, :] = alpha[:, Q_BLOCK:] * acc_ref[mid:hi, :] + pv[HEAD_DIM:LANES, Q_BLOCK:]
        return tuple(new_ms), tuple(new_ls)

    init = (tuple(jnp.full((1, 2 * Q_BLOCK), NEG_BIG, jnp.float32) for _ in range(n_pairs)),
            tuple(jnp.zeros((1, 2 * Q_BLOCK), jnp.float32) for _ in range(n_pairs)))
    _, ls = lax.fori_loop(0, n_chunks, attn_chunk, init)
    for j in range(n_pairs):
        lo, mid, hi = j * LANES, j * LANES + HEAD_DIM, (j + 1) * LANES
        even = acc_ref[lo:mid, :] / ls[j][:, :Q_BLOCK]
        odd = acc_ref[mid:hi, :] / ls[j][:, Q_BLOCK:]
        o_ref[:, lo:hi] = jnp.concatenate([even, odd], axis=0).T.astype(o_ref.dtype)


def _dsa(iq, iw, q, iklo, ikhi, k, vt):
    b, s, _ = q.shape
    n_top = min(TOPK_MAX, s // 4)
    idx_bits = max(1, int(np.ceil(np.log2(s))))
    qblk = lambda w: pl.BlockSpec((None, Q_BLOCK, w), lambda bi, qi: (bi, qi, 0))
    keys = lambda w: pl.BlockSpec((None, s, w), lambda bi, qi: (bi, 0, 0))
    return pl.pallas_call(
        functools.partial(_dsa_kernel, n_top=n_top, idx_bits=idx_bits),
        grid=(b, s // Q_BLOCK),
        in_specs=[qblk(IDXQ_W), qblk(LANES), qblk(ATTN_W), keys(LANES), keys(LANES),
                  keys(ATTN_W), pl.BlockSpec((None, VT_ALL, s), lambda bi, qi: (bi, 0, 0))],
        out_specs=qblk(ATTN_W),
        out_shape=jax.ShapeDtypeStruct((b, s, ATTN_W), jnp.bfloat16),
        scratch_shapes=[pltpu.VMEM((s, Q_BLOCK), jnp.int32),
                        pltpu.VMEM((N_HEADS_IDX // 4, LANES, 2 * Q_BLOCK), jnp.bfloat16),
                        pltpu.VMEM((N_HEADS_ATTN // 2, LANES, 2 * Q_BLOCK), jnp.bfloat16),
                        pltpu.VMEM((KEY_CHUNK, 2 * Q_BLOCK), jnp.float32),
                        pltpu.VMEM((N_HEADS_ATTN // 2, KEY_CHUNK, 2 * Q_BLOCK), jnp.float32),
                        pltpu.VMEM((ATTN_W, Q_BLOCK), jnp.float32)],
        compiler_params=pltpu.CompilerParams(dimension_semantics=("arbitrary", "arbitrary"),
                                             vmem_limit_bytes=VMEM_LIMIT),
        name="dsa",
    )(iq, iw, q, iklo, ikhi, k, vt)


def _merge_kernel(x_ref, ya_ref, u_ref, uprev_ref, qm_ref, km_ref, vm_ref,
                  gmix_ref, wg_ref, bg_ref, wpa_ref, wpp_ref, wpm_ref, wbd_ref, psc_ref,
                  wo_ref, gffn_ref, wr_ref, br_ref,
                  h_ref, xn2_ref, eg_ref, ei_ref, cnt_ref, carry_ref, *, tiles_per_seq):
    i = pl.program_id(0)
    t = x_ref.shape[0]
    d = x_ref.shape[1]
    bf = jnp.bfloat16
    x = x_ref[...]
    xn = _rmsnorm(x, gmix_ref[...]).astype(bf)

    tile_in_seq = i % tiles_per_seq
    u = u_ref[...]
    halo = jnp.where(tile_in_seq == 0, 0.0, uprev_ref[...])
    ext = jnp.concatenate([halo, u], axis=0)
    lane_p = lax.broadcasted_iota(jnp.int32, (t, POOL_W), 1)
    pos1 = (tile_in_seq * t + lax.broadcasted_iota(jnp.int32, (t, 1), 0) + 1).astype(jnp.float32)
    pooled = None
    run, width = ext, 1
    for g, w in enumerate(POOL_WINDOWS):
        while width < w:
            run = run[width:] + run[:-width]
            width *= 2
        start = POOL_HALO + 1 - w
        mean = run[start:start + t] / jnp.minimum(pos1, float(w))
        pooled = mean if pooled is None else jnp.where(lane_p >= g * POOL_GROUP_DIM, mean, pooled)
    pooled = pooled - u
    mixed = _dot(pooled.astype(bf), wbd_ref[...]) * psc_ref[...]
    y_pool = _dot(mixed.astype(bf), wpp_ref[...])

    qm = qm_ref[...]
    probs = []
    for h in range(N_HEADS_MEM):
        s = _dot_nt(qm, km_ref[h])
        p = jnp.exp(s - jnp.max(s, axis=1, keepdims=True))
        probs.append((p / jnp.sum(p, axis=1, keepdims=True)).astype(bf))
    y_mem = _dot(probs[0], vm_ref[0])
    for h in range(1, N_HEADS_MEM):
        y_mem = y_mem + _dot(probs[h], vm_ref[h])
    y_mem = _dot(y_mem.astype(bf), wpm_ref[...])

    y_attn = _dot(ya_ref[...], wpa_ref[...])

    def gate(br):
        z = _dot(xn, wg_ref[:, br * d:(br + 1) * d]) + bg_ref[:, br * d:(br + 1) * d]
        return 1.0 / (1.0 + jnp.exp(-z))

    merged = gate(0) * y_attn + gate(1) * y_pool + gate(2) * y_mem
    h = x + _dot(merged.astype(bf), wo_ref[...])
    h_ref[...] = h
    xn2 = _rmsnorm(h, gffn_ref[...])
    xn2_ref[...] = xn2

    logits = jnp.dot(xn2, wr_ref[...], precision=lax.Precision.HIGHEST,
                     preferred_element_type=jnp.float32) + br_ref[...]
    lane_e = lax.broadcasted_iota(jnp.int32, logits.shape, 1).astype(jnp.float32)
    work = logits
    vals, onehots, ids = [], [], []
    for _ in range(TOP_K):
        mx = jnp.max(work, axis=1, keepdims=True)
        idx = jnp.min(jnp.where(work == mx, lane_e, float(LANES)), axis=1, keepdims=True)
        oh = lane_e == idx
        vals.append(mx)
        ids.append(idx)
        onehots.append(oh)
        work = jnp.where(oh, -jnp.inf, work)
    exps = [jnp.exp(v - vals[0]) for v in vals]
    denom = exps[0] + exps[1] + exps[2] + exps[3]

    @pl.when(i == 0)
    def _():
        carry_ref[...] = jnp.zeros(carry_ref.shape, jnp.float32)

    member = jnp.where(onehots[0] | onehots[1] | onehots[2] | onehots[3], 1.0, 0.0)
    r_io = lax.broadcasted_iota(jnp.int32, (t, t), 0)
    c_io = lax.broadcasted_iota(jnp.int32, (t, t), 1)
    earlier = jnp.where(c_io < r_io, 1.0, 0.0).astype(bf)
    before = _dot(earlier, member.astype(bf)) + carry_ref[...]
    carry_ref[...] = carry_ref[...] + jnp.sum(member, axis=0, keepdims=True)
    cnt_ref[...] = carry_ref[...]

    lane8 = lax.broadcasted_iota(jnp.int32, (t, 2 * TOP_K), 1)
    eg = jnp.zeros((t, 2 * TOP_K), jnp.float32)
    ei = jnp.zeros((t, 2 * TOP_K), jnp.float32)
    for j in range(TOP_K):
        rank = jnp.sum(jnp.where(onehots[j], before, 0.0), axis=1, keepdims=True)
        eg = jnp.where(lane8 == j, exps[j] / denom, eg)
        ei = jnp.where(lane8 == j, ids[j], ei)
        ei = jnp.where(lane8 == TOP_K + j, rank, ei)
    eg_ref[...] = eg
    ei_ref[...] = ei.astype(jnp.int32)


def _merge(x2, ya, u, qm, km, vm, gmix, wg, bg, wpa, wpp, wpm, wbd, psc, wo, gffn, wr, br, seq):
    n, d = x2.shape
    t = MERGE_TILE
    tiles_per_seq = seq // t
    m = km.shape[2]
    row = lambda i: (i, 0)
    full = lambda i: (0, 0)
    per_batch = lambda i: (i // tiles_per_seq, 0, 0, 0)
    halo_blocks = t // POOL_HALO
    prev = lambda i: (jnp.maximum(i * halo_blocks - 1, 0), 0)
    consts = (gmix, wg, bg, wpa, wpp, wpm, wbd, psc, wo, gffn, wr, br)
    out_shape = (jax.ShapeDtypeStruct((n, d), jnp.float32), jax.ShapeDtypeStruct((n, d), jnp.float32),
                 jax.ShapeDtypeStruct((n, 2 * TOP_K), jnp.float32),
                 jax.ShapeDtypeStruct((n, 2 * TOP_K), jnp.int32),
                 jax.ShapeDtypeStruct((1, LANES), jnp.float32))
    return pl.pallas_call(
        functools.partial(_merge_kernel, tiles_per_seq=tiles_per_seq),
        grid=(n // t,),
        in_specs=[pl.BlockSpec((t, d), row), pl.BlockSpec((t, ATTN_W), row),
                  pl.BlockSpec((t, POOL_W), row), pl.BlockSpec((POOL_HALO, POOL_W), prev),
                  pl.BlockSpec((t, MEM_W), row),
                  pl.BlockSpec((None, N_HEADS_MEM, m, MEM_W), per_batch),
                  pl.BlockSpec((None, N_HEADS_MEM, m, MEM_W), per_batch)]
                 + [pl.BlockSpec(c.shape, full) for c in consts],
        out_specs=(pl.BlockSpec((t, d), row), pl.BlockSpec((t, d), row),
                   pl.BlockSpec((t, 2 * TOP_K), row), pl.BlockSpec((t, 2 * TOP_K), row),
                   pl.BlockSpec((1, LANES), full)),
        out_shape=out_shape,
        scratch_shapes=[pltpu.VMEM((1, LANES), jnp.float32)],
        compiler_params=pltpu.CompilerParams(dimension_semantics=("arbitrary",),
                                             vmem_limit_bytes=VMEM_LIMIT),
        name="merge",
    )(x2, ya, u, u, qm, km, vm, *consts)


def _row_copy(src_ref, src_row, dst_ref, dst_row, sem):
    return pltpu.make_async_copy(src_ref.at[pl.ds(src_row, 1)], dst_ref.at[pl.ds(dst_row, 1)], sem)


def _dispatch_kernel(pstart_ref, cnt_ref, e_ref, r_ref, x_ref, xs_ref, zero_ref, sem):
    t = x_ref.shape[0]

    @pl.when(pl.program_id(0) == 0)
    def _():
        zero_ref[...] = jnp.zeros(zero_ref.shape, zero_ref.dtype)

        def for_each_pad_row(act):
            def per_expert(e, carry):
                n_pad = (ROW_BLOCK - cnt_ref[e] % ROW_BLOCK) % ROW_BLOCK
                first = pstart_ref[e] + cnt_ref[e]

                def per_row(r, c):
                    act(_row_copy(zero_ref, 0, xs_ref, first + r, sem))
                    return c
                return lax.fori_loop(0, n_pad, per_row, carry)
            lax.fori_loop(0, N_EXPERTS, per_expert, 0)

        def for_each_unused_block(act):
            n_blocks = xs_ref.shape[0] // ROW_BLOCK
            first = (pstart_ref[N_EXPERTS - 1] + cnt_ref[N_EXPERTS - 1] + ROW_BLOCK - 1) // ROW_BLOCK

            def per_block(b, c):
                rows = pl.ds(pl.multiple_of(b * ROW_BLOCK, ROW_BLOCK), ROW_BLOCK)
                act(pltpu.make_async_copy(zero_ref, xs_ref.at[rows], sem))
                return c
            lax.fori_loop(first, n_blocks, per_block, 0)

        for act in (lambda cp: cp.start(), lambda cp: cp.wait()):
            for_each_pad_row(act)
            for_each_unused_block(act)

    def issue(tok, carry):
        for j in range(TOP_K):
            a = tok * TOP_K + j
            _row_copy(x_ref, tok, xs_ref, pstart_ref[e_ref[a]] + r_ref[a], sem).start()
        return carry

    lax.fori_loop(0, t, issue, 0)

    def drain(tok, carry):
        for j in range(TOP_K):
            _row_copy(x_ref, 0, xs_ref, 0, sem).wait()
        return carry

    lax.fori_loop(0, t, drain, 0)


def _dispatch(pstart, cnt, e_flat, r_flat, xn2, n_rows):
    n, d = xn2.shape
    t = MOVE_TILE
    smem = lambda: pl.BlockSpec((t * TOP_K,), lambda i, ps, ct: (i,), memory_space=pltpu.SMEM)
    return pl.pallas_call(
        _dispatch_kernel,
        grid_spec=pltpu.PrefetchScalarGridSpec(
            num_scalar_prefetch=2,
            grid=(n // t,),
            in_specs=[smem(), smem(), pl.BlockSpec((t, d), lambda i, ps, ct: (i, 0))],
            out_specs=pl.BlockSpec(memory_space=pl.ANY),
            scratch_shapes=[pltpu.VMEM((ROW_BLOCK, d), jnp.float32),
                            pltpu.SemaphoreType.DMA(())]),
        out_shape=jax.ShapeDtypeStruct((n_rows, d), jnp.float32),
        compiler_params=pltpu.CompilerParams(dimension_semantics=("arbitrary",),
                                             vmem_limit_bytes=VMEM_LIMIT),
        name="dispatch",
    )(pstart, cnt, e_flat, r_flat, xn2)


def _experts_kernel(be_ref, nused_ref, xs_ref, wu_ref, bu_ref, wd_ref, bd_ref, ys_ref,
                    wu16_ref, wd16_ref):
    i = pl.program_id(0)
    f = wd_ref.shape[0]

    @pl.when((i == 0) | (be_ref[i] != be_ref[jnp.maximum(i - 1, 0)]))
    def _():
        wu16_ref[...] = wu_ref[...].astype(wu16_ref.dtype)
        wd16_ref[...] = wd_ref[...].astype(wd16_ref.dtype)

    @pl.when(i < nused_ref[0])
    def _():
        hb = _dot(xs_ref[...].astype(jnp.bfloat16), wu16_ref[...]) + bu_ref[...]
        glu = jnp.minimum(hb[:, :f], SWIGLU_LIMIT)
        lin = jnp.clip(hb[:, f:], -SWIGLU_LIMIT, SWIGLU_LIMIT)
        act = glu * (1.0 / (1.0 + jnp.exp(-SWIGLU_ALPHA * glu))) * (lin + 1.0)
        ys_ref[...] = _dot(act.astype(jnp.bfloat16), wd16_ref[...]) + bd_ref[...]

    @pl.when(i >= nused_ref[0])
    def _():
        ys_ref[...] = jnp.zeros(ys_ref.shape, ys_ref.dtype)


def _experts(block_e, nused, xs, wu, bu, wd, bd):
    n_rows, d = xs.shape
    e, _, f2 = wu.shape
    f = wd.shape[1]
    rows = lambda i, be, nu: (i, 0)
    used_rows = lambda i, be, nu: (jnp.minimum(i, nu[0] - 1), 0)
    per_e = lambda i, be, nu: (be[i], 0, 0)
    return pl.pallas_call(
        _experts_kernel,
        grid_spec=pltpu.PrefetchScalarGridSpec(
            num_scalar_prefetch=2,
            grid=(n_rows // ROW_BLOCK,),
            in_specs=[pl.BlockSpec((ROW_BLOCK, d), used_rows),
                      pl.BlockSpec((None, d, f2), per_e), pl.BlockSpec((None, 1, f2), per_e),
                      pl.BlockSpec((None, f, d), per_e), pl.BlockSpec((None, 1, d), per_e)],
            out_specs=pl.BlockSpec((ROW_BLOCK, d), rows),
            scratch_shapes=[pltpu.VMEM((d, f2), jnp.bfloat16), pltpu.VMEM((f, d), jnp.bfloat16)]),
        out_shape=jax.ShapeDtypeStruct((n_rows, d), jnp.float32),
        compiler_params=pltpu.CompilerParams(dimension_semantics=("arbitrary",),
                                             vmem_limit_bytes=VMEM_LIMIT),
        name="experts",
    )(block_e, nused, xs, wu, bu, wd, bd)


def _combine_kernel(pstart_ref, e_ref, r_ref, h_ref, eg_ref, g_ref, ys_ref, o_ref, buf_ref, sem,
                    *, final_norm):
    t = h_ref.shape[0]

    def issue(tok, carry):
        for j in range(TOP_K):
            a = tok * TOP_K + j
            _row_copy(ys_ref, pstart_ref[e_ref[a]] + r_ref[a], buf_ref.at[j], tok, sem).start()
        return carry

    lax.fori_loop(0, t, issue, 0)

    def drain(tok, carry):
        for j in range(TOP_K):
            _row_copy(ys_ref, 0, buf_ref.at[j], 0, sem).wait()
        return carry

    lax.fori_loop(0, t, drain, 0)
    eg = eg_ref[...]
    h = h_ref[...]
    for j in range(TOP_K):
        h = h + eg[:, j:j + 1] * buf_ref[j]
    o_ref[...] = _rmsnorm(h, g_ref[...]) if final_norm else h


def _combine(pstart, e_flat, r_flat, h, eg, g, ys, final_norm):
    n, d = h.shape
    t = MOVE_TILE
    smem = lambda: pl.BlockSpec((t * TOP_K,), lambda i, ps: (i,), memory_space=pltpu.SMEM)
    return pl.pallas_call(
        functools.partial(_combine_kernel, final_norm=final_norm),
        grid_spec=pltpu.PrefetchScalarGridSpec(
            num_scalar_prefetch=1,
            grid=(n // t,),
            in_specs=[smem(), smem(), pl.BlockSpec((t, d), lambda i, ps: (i, 0)),
                      pl.BlockSpec((t, 2 * TOP_K), lambda i, ps: (i, 0)),
                      pl.BlockSpec((1, d), lambda i, ps: (0, 0)),
                      pl.BlockSpec(memory_space=pl.ANY)],
            out_specs=pl.BlockSpec((t, d), lambda i, ps: (i, 0)),
            scratch_shapes=[pltpu.VMEM((TOP_K, t, d), jnp.float32), pltpu.SemaphoreType.DMA(())]),
        out_shape=jax.ShapeDtypeStruct((n, d), jnp.float32),
        compiler_params=pltpu.CompilerParams(dimension_semantics=("arbitrary",),
                                             vmem_limit_bytes=VMEM_LIMIT),
        name="combine",
    )(pstart, e_flat, r_flat, h, eg, g, ys)


def _rope_tables(seq):
    half = ROPE_DIM // 2
    inv = jnp.power(jnp.float32(ROPE_THETA), -jnp.arange(half, dtype=jnp.float32) / half)
    ang = jnp.arange(seq, dtype=jnp.float32)[:, None] * inv[None, :]
    cos, sin = jnp.cos(ang), jnp.sin(ang)
    pad = HEAD_DIM - ROPE_DIM
    one, zero = jnp.ones((seq, pad), jnp.float32), jnp.zeros((seq, pad), jnp.float32)
    zh = jnp.zeros((seq, half), jnp.float32)
    reps = LANES // HEAD_DIM
    c = jnp.tile(jnp.concatenate([cos, cos, one], axis=1), (1, reps))
    a = jnp.tile(jnp.concatenate([-sin, zh, zero], axis=1), (1, reps))
    b = jnp.tile(jnp.concatenate([zh, sin, zero], axis=1), (1, reps))
    return c, a, b


def _layer(h, mem, p, tables, final_g):
    b, s, d = h.shape
    n = b * s
    bf = jnp.bfloat16
    x2 = h.reshape(n, d)
    w_in = p["w_in"]
    o_ik = 3 * ATTN_W + IDXQ_W
    o_u = o_ik + IDX_DIM + N_HEADS_IDX
    wa = w_in[:, :o_ik].astype(bf)
    wb = jnp.pad(w_in[:, o_ik:o_u], ((0, 0), (0, LANES - (o_u - o_ik)))).astype(bf)
    wc = w_in[:, o_u:].astype(bf)
    q, k, vt, iq, iklo, ikhi, iw, u, qm = _in_proj(x2, p["norm_mix_g"][None], wa, wb, wc, *tables, b, s)
    km, vm = _mem_kv(mem, p["mem_norm_g"][None], p["w_mem_kv"].astype(bf))
    b3 = lambda a: a.reshape(b, s, a.shape[-1])
    ya = _dsa(b3(iq), b3(iw), b3(q), b3(iklo), b3(ikhi), b3(k), vt).reshape(n, ATTN_W)

    wbd = jnp.zeros((POOL_W, POOL_W), jnp.float32)
    for g in range(N_POOL_GROUPS):
        lo = g * POOL_GROUP_DIM
        wbd = wbd.at[lo:lo + POOL_GROUP_DIM, lo:lo + POOL_GROUP_DIM].set(p["pool_w_group"][g])
    wr = jnp.pad(p["w_router"], ((0, 0), (0, LANES - N_EXPERTS)))
    br = jnp.pad(p["b_router"], (0, LANES - N_EXPERTS), constant_values=NEG_BIG)[None]
    hmid, xn2, eg, ei, counts = _merge(
        x2, ya, u, qm, km, vm, p["norm_mix_g"][None], p["w_gate"].astype(bf), p["b_gate"][None],
        p["w_proj_attn"].astype(bf), p["w_proj_pool"].astype(bf), p["w_proj_mem"].astype(bf),
        wbd.astype(bf), p["pool_scale"][None], p["w_out"].astype(bf), p["norm_ffn_g"][None], wr, br, s)

    cnt = counts[0, :N_EXPERTS].astype(jnp.int32)
    padded = ((cnt + ROW_BLOCK - 1) // ROW_BLOCK) * ROW_BLOCK
    pend = jnp.cumsum(padded)
    pstart = (pend - padded).astype(jnp.int32)
    n_rows = n * TOP_K + N_EXPERTS * ROW_BLOCK
    n_blocks = n_rows // ROW_BLOCK
    nused = (pend[-1] // ROW_BLOCK).astype(jnp.int32)
    blk = jnp.minimum(jnp.arange(n_blocks, dtype=jnp.int32), nused - 1) * ROW_BLOCK
    block_e = jnp.sum(blk[:, None] >= pend[None, :], axis=1).astype(jnp.int32)
    block_e = jnp.minimum(block_e, N_EXPERTS - 1)
    e_flat = ei[:, :TOP_K].reshape(-1)
    r_flat = ei[:, TOP_K:].reshape(-1)

    xs = _dispatch(pstart, cnt, e_flat, r_flat, xn2, n_rows)
    ys = _experts(block_e, nused[None], xs, p["w_up"], p["b_up"][:, None, :],
                  p["w_down"], p["b_down"][:, None, :])
    g = p["norm_ffn_g"][None] if final_g is None else final_g[None]
    return _combine(pstart, e_flat, r_flat, hmid, eg, g, ys, final_g is not None).reshape(b, s, d)


def kernel(x, mem, norm_mix_g, w_in, w_gate, b_gate, w_proj_attn, w_proj_pool, w_proj_mem,
           pool_w_group, pool_scale, mem_norm_g, w_mem_kv, w_out, norm_ffn_g,
           w_router, b_router, w_up, b_up, w_down, b_down, norm_final_g):
    b, s, d = x.shape
    assert s % IN_TILE == 0 and s % KEY_CHUNK == 0 and (b * s) % MOVE_TILE == 0
    stacked = dict(norm_mix_g=norm_mix_g, w_in=w_in, w_gate=w_gate, b_gate=b_gate,
                   w_proj_attn=w_proj_attn, w_proj_pool=w_proj_pool, w_proj_mem=w_proj_mem,
                   pool_w_group=pool_w_group, pool_scale=pool_scale, mem_norm_g=mem_norm_g,
                   w_mem_kv=w_mem_kv, w_out=w_out, norm_ffn_g=norm_ffn_g, w_router=w_router,
                   b_router=b_router, w_up=w_up, b_up=b_up, w_down=w_down, b_down=b_down)
    depth = w_in.shape[0]
    tables = _rope_tables(s)
    h = x
    for l in range(depth):
        p = {name: val[l] for name, val in stacked.items()}
        h = _layer(h, mem, p, tables, norm_final_g if l == depth - 1 else None)
    return h
```

```python
import functools

import jax
import jax.numpy as jnp
import numpy as np
from jax import lax
from jax.experimental import pallas as pl
from jax.experimental.pallas import tpu as pltpu

N_HEADS_ATTN = 8
HEAD_DIM = 64
ROPE_DIM = HEAD_DIM // 4
ROPE_THETA = 500000.0
N_HEADS_IDX = 8
IDX_DIM = 64
TOPK_MAX = 256
N_POOL_GROUPS = 4
POOL_GROUP_DIM = 64
POOL_WINDOWS = (2, 4, 8, 16)
POOL_HALO = 16
N_HEADS_MEM = 4
N_BRANCHES = 3
N_EXPERTS = 32
TOP_K = 4
SWIGLU_ALPHA = 1.702
SWIGLU_LIMIT = 7.0
EPS = 1e-6

ATTN_W = N_HEADS_ATTN * HEAD_DIM
POOL_W = N_POOL_GROUPS * POOL_GROUP_DIM
MEM_W = N_HEADS_MEM * HEAD_DIM
IDXQ_W = N_HEADS_IDX * IDX_DIM

LANES = 128
SUBLANES = 8
VMEM_LIMIT = 56 * 1024 * 1024

IN_TILE = 512
Q_BLOCK = 128
KEY_CHUNK = 256
COUNT_CHUNK = 512
COUNT_ROWS = 32
MERGE_TILE = 256
ROW_BLOCK = 256
MOVE_TILE = 256

VT_ROWS = LANES + 16
VT_ALL = (N_HEADS_ATTN // 2) * VT_ROWS
LOG2_E = 1.4426950408889634

NEG_BIG = -1e30
INT_MIN = -2 ** 31


def _dot(a, b):
    return jnp.dot(a, b, preferred_element_type=jnp.float32)


def _dot_nt(a, b):
    return lax.dot_general(a, b, (((1,), (1,)), ((), ())), preferred_element_type=jnp.float32)


def _rmsnorm(x, g):
    return x * lax.rsqrt(jnp.mean(x * x, axis=-1, keepdims=True) + EPS) * g


def _rope128(x, c, a, b):
    return x * c + pltpu.roll(x, LANES - ROPE_DIM // 2, 1) * a + pltpu.roll(x, ROPE_DIM // 2, 1) * b


def _in_proj_kernel(x_ref, g_ref, wa_ref, wb_ref, wc_ref, rc_ref, ra_ref, rb_ref,
                    q_ref, k_ref, vt_ref, iq_ref, iklo_ref, ikhi_ref, iw_ref, u_ref, qm_ref):
    xn = _rmsnorm(x_ref[...], g_ref[...]).astype(jnp.bfloat16)
    rc, ra, rb = rc_ref[...], ra_ref[...], rb_ref[...]
    pa = _dot(xn, wa_ref[...])
    for seg, (ref, scale) in enumerate(((q_ref, LOG2_E * HEAD_DIM ** -0.5), (k_ref, None),
                                        (None, None), (iq_ref, IDX_DIM ** -0.5))):
        for c in range(ATTN_W // LANES):
            lo = seg * ATTN_W + c * LANES
            blk = pa[:, lo:lo + LANES]
            if ref is None:
                vt_ref[c * VT_ROWS:c * VT_ROWS + LANES, :] = blk.T.astype(vt_ref.dtype)
                vt_ref[c * VT_ROWS + LANES:(c + 1) * VT_ROWS, :] = jnp.ones(
                    (VT_ROWS - LANES, blk.shape[0]), vt_ref.dtype)
                continue
            blk = _rope128(blk, rc, ra, rb)
            if scale is not None:
                blk = blk * scale
            ref[:, c * LANES:(c + 1) * LANES] = blk.astype(ref.dtype)
    pb = _dot(xn, wb_ref[...])
    lane = lax.broadcasted_iota(jnp.int32, pb.shape, 1)
    ik = jnp.where(lane < IDX_DIM, _rope128(pb, rc, ra, rb), 0.0)
    iklo_ref[...] = ik.astype(iklo_ref.dtype)
    ikhi_ref[...] = pltpu.roll(ik, IDX_DIM, 1).astype(ikhi_ref.dtype)
    iw_ref[...] = pb * (N_HEADS_IDX ** -0.5)
    pc = _dot(xn, wc_ref[...])
    u_ref[...] = pc[:, :POOL_W]
    qm_ref[...] = (pc[:, POOL_W:] * (HEAD_DIM ** -0.5)).astype(qm_ref.dtype)


def _in_proj(x2, g, wa, wb, wc, rc, ra, rb, batch, seq):
    n, d = x2.shape
    t = IN_TILE
    tiles_per_seq = seq // t
    row = lambda i: (i, 0)
    full = lambda i: (0, 0)
    pos = lambda i: (i % tiles_per_seq, 0)
    bf = jnp.bfloat16
    flat = lambda w, dt: (jax.ShapeDtypeStruct((n, w), dt), pl.BlockSpec((t, w), row))
    outs = (flat(ATTN_W, bf), flat(ATTN_W, bf),
            (jax.ShapeDtypeStruct((batch, VT_ALL, seq), bf),
             pl.BlockSpec((None, VT_ALL, t), lambda i: (i // tiles_per_seq, 0, i % tiles_per_seq))),
            flat(IDXQ_W, bf), flat(LANES, bf), flat(LANES, bf), flat(LANES, jnp.float32),
            flat(POOL_W, jnp.float32), flat(MEM_W, bf))
    return pl.pallas_call(
        _in_proj_kernel,
        grid=(n // t,),
        in_specs=[pl.BlockSpec((t, d), row), pl.BlockSpec((1, d), full),
                  pl.BlockSpec(wa.shape, full), pl.BlockSpec(wb.shape, full),
                  pl.BlockSpec(wc.shape, full),
                  pl.BlockSpec((t, LANES), pos), pl.BlockSpec((t, LANES), pos),
                  pl.BlockSpec((t, LANES), pos)],
        out_specs=tuple(o[1] for o in outs),
        out_shape=tuple(o[0] for o in outs),
        compiler_params=pltpu.CompilerParams(dimension_semantics=("arbitrary",),
                                             vmem_limit_bytes=VMEM_LIMIT),
        name="in_proj",
    )(x2, g, wa, wb, wc, rc, ra, rb)


def _mem_kv_kernel(mem_ref, g_ref, w_ref, km_ref, vm_ref):
    mn = _rmsnorm(mem_ref[...], g_ref[...]).astype(jnp.bfloat16)
    kv = _dot(mn, w_ref[...])
    km, vm = kv[:, :MEM_W], kv[:, MEM_W:]
    lane = lax.broadcasted_iota(jnp.int32, km.shape, 1)
    for h in range(N_HEADS_MEM):
        in_head = (lane >= h * HEAD_DIM) & (lane < (h + 1) * HEAD_DIM)
        km_ref[h] = jnp.where(in_head, km, 0.0).astype(km_ref.dtype)
        vm_ref[h] = jnp.where(in_head, vm, 0.0).astype(vm_ref.dtype)


def _mem_kv(mem, g, w):
    b, m, d = mem.shape
    out = jax.ShapeDtypeStruct((b, N_HEADS_MEM, m, MEM_W), jnp.bfloat16)
    return pl.pallas_call(
        _mem_kv_kernel,
        grid=(b,),
        in_specs=[pl.BlockSpec((None, m, d), lambda i: (i, 0, 0)),
                  pl.BlockSpec((1, d), lambda i: (0, 0)),
                  pl.BlockSpec(w.shape, lambda i: (0, 0))],
        out_specs=(pl.BlockSpec((None, N_HEADS_MEM, m, MEM_W), lambda i: (i, 0, 0, 0)),
                   pl.BlockSpec((None, N_HEADS_MEM, m, MEM_W), lambda i: (i, 0, 0, 0))),
        out_shape=(out, out),
        compiler_params=pltpu.CompilerParams(dimension_semantics=("arbitrary",),
                                             vmem_limit_bytes=VMEM_LIMIT),
        name="mem_kv",
    )(mem, g, w)


def _dsa_kernel(iq_ref, iw_ref, q_ref, iklo_ref, ikhi_ref, k_ref, vt_ref, o_ref,
                key_ref, iqt_ref, qt_ref, bias_ref, s_ref, acc_ref, *, n_top, idx_bits):
    qb = pl.program_id(1)
    n_chunks = (qb * Q_BLOCK + Q_BLOCK + KEY_CHUNK - 1) // KEY_CHUNK
    n_count_chunks = (qb * Q_BLOCK + Q_BLOCK + COUNT_CHUNK - 1) // COUNT_CHUNK
    q_pos = qb * Q_BLOCK + lax.broadcasted_iota(jnp.int32, (1, Q_BLOCK), 1)
    row_k = lax.broadcasted_iota(jnp.int32, (KEY_CHUNK, 1), 0)
    row_c = lax.broadcasted_iota(jnp.int32, (COUNT_CHUNK, 1), 0)
    row_d = lax.broadcasted_iota(jnp.int32, (LANES, Q_BLOCK), 0)
    bf = jnp.bfloat16

    def chunk_off(c):
        return pl.multiple_of(c * KEY_CHUNK, KEY_CHUNK)

    wt = iw_ref[...].T
    for j in range(N_HEADS_IDX // 2):
        iqt_ref[j // 2, :, (j % 2) * Q_BLOCK:(j % 2 + 1) * Q_BLOCK] = (
            iq_ref[:, j * LANES:(j + 1) * LANES].astype(jnp.float32).T.astype(bf))
    for j in range(N_HEADS_ATTN // 2):
        pair_t = q_ref[:, j * LANES:(j + 1) * LANES].astype(jnp.float32).T
        qt_ref[j, :, :Q_BLOCK] = jnp.where(row_d < HEAD_DIM, pair_t, 0.0).astype(bf)
        qt_ref[j, :, Q_BLOCK:] = jnp.where(row_d >= HEAD_DIM, pair_t, 0.0).astype(bf)

    def score_chunk(c, carry):
        off = pl.multiple_of(c * COUNT_CHUNK, COUNT_CHUNK)
        acc = jnp.zeros((COUNT_CHUNK, Q_BLOCK), jnp.float32)
        for parity, ik_ref in enumerate((iklo_ref, ikhi_ref)):
            ik = ik_ref[pl.ds(off, COUNT_CHUNK), :]
            for g in range(N_HEADS_IDX // 4):
                dots = _dot(ik, iqt_ref[g])
                for side in range(2):
                    head = 2 * (2 * g + side) + parity
                    w = wt[IDX_DIM + head:IDX_DIM + head + 1, :]
                    acc = acc + w * jnp.maximum(dots[:, side * Q_BLOCK:(side + 1) * Q_BLOCK], 0.0)
        acc = jnp.where(off + row_c <= q_pos, acc, -jnp.inf)
        bits = lax.bitcast_convert_type(acc, jnp.int32)
        key_ref[pl.ds(off, COUNT_CHUNK), :] = bits ^ ((bits >> 31) & 0x7FFFFFFF)
        return carry

    lax.fori_loop(0, n_count_chunks, score_chunk, 0)

    def count(pred):
        def body(c, cnt):
            off = pl.multiple_of(c * COUNT_CHUNK, COUNT_CHUNK)
            hit = jnp.where(pred(key_ref[pl.ds(off, COUNT_CHUNK), :], off + row_c), 1.0, 0.0)
            return cnt + jnp.sum(hit.reshape(COUNT_CHUNK // COUNT_ROWS, COUNT_ROWS, Q_BLOCK), axis=0)
        cnt = lax.fori_loop(0, n_count_chunks, body, jnp.zeros((COUNT_ROWS, Q_BLOCK), jnp.float32))
        return jnp.sum(cnt, axis=0, keepdims=True)

    k_f = jnp.float32(n_top)
    zero = jnp.zeros((1, Q_BLOCK), jnp.int32)
    thr = jnp.where(count(lambda kk, pos: kk >= zero) >= k_f, zero, INT_MIN)

    def thr_bit(i, cur):
        cand = cur + jnp.left_shift(jnp.int32(1), 30 - i)
        return jnp.where(count(lambda kk, pos: kk >= cand) >= k_f, cand, cur)

    thr = lax.fori_loop(0, 31, thr_bit, thr)
    need = k_f - count(lambda kk, pos: kk > thr)
    n_ties = count(lambda kk, pos: kk == thr)

    def tie_search():
        def tie_bit(i, cur):
            cand = cur + jnp.left_shift(jnp.int32(1), idx_bits - 1 - i)
            below = count(lambda kk, pos: (kk == thr) & (pos < cand))
            return jnp.where(below < need, cand, cur)
        return lax.fori_loop(0, idx_bits, tie_bit, zero)

    has_surplus_ties = jnp.max(jnp.where(n_ties > need, 1.0, 0.0)) > 0.0
    tie_pos = lax.cond(has_surplus_ties, tie_search, lambda: zero + (2 ** idx_bits - 1))

    acc_ref[...] = jnp.zeros(acc_ref.shape, jnp.float32)

    n_pairs = N_HEADS_ATTN // 2

    def masked_logits(c, slot):
        off = chunk_off(jnp.minimum(c, n_chunks - 1))
        kk = key_ref[pl.ds(off, KEY_CHUNK), :]
        pos = c * KEY_CHUNK + row_k
        sel = ((kk > thr) | ((kk == thr) & (pos <= tie_pos))) & (pos <= q_pos)
        bias = jnp.where(sel, 0.0, NEG_BIG)
        bias_ref[slot] = jnp.concatenate([bias, bias], axis=1)
        for j in range(n_pairs):
            kp = k_ref[pl.ds(off, KEY_CHUNK), j * LANES:(j + 1) * LANES]
            s_ref[slot, j] = _dot(kp, qt_ref[j]) + bias_ref[slot]

    def softmax_pv(c, slot, ms, ls):
        off = chunk_off(jnp.minimum(c, n_chunks - 1))
        new_ms, new_ls = [], []
        for j in range(n_pairs):
            s = s_ref[slot, j]
            m_new = jnp.maximum(ms[j], jnp.max(s, axis=0, keepdims=True))
            alpha = jnp.exp2(ms[j] - m_new)
            p = jnp.exp2(s - m_new).astype(bf)
            vt = vt_ref[j * VT_ROWS:(j + 1) * VT_ROWS, pl.ds(off, KEY_CHUNK)]
            pv = _dot(vt, p)
            new_ls.append(alpha * ls[j] + pv[LANES:LANES + 1, :])
            new_ms.append(m_new)
            lo, mid, hi = j * LANES, j * LANES + HEAD_DIM, (j + 1) * LANES
            acc_ref[lo:mid, :] = alpha[:, :Q_BLOCK] * acc_ref[lo:mid, :] + pv[:HEAD_DIM, :Q_BLOCK]
            acc_ref[mid:hi, :] = alpha[:, Q_BLOCK:] * acc_ref[mid:hi, :] + pv[HEAD_DIM:LANES, Q_BLOCK:]
        return tuple(new_ms), tuple(new_ls)

    masked_logits(0, 0)

    def attn_two_chunks(i, carry):
        ms, ls = carry
        masked_logits(2 * i + 1, 1)
        ms, ls = softmax_pv(2 * i, 0, ms, ls)
        masked_logits(2 * i + 2, 0)
        return softmax_pv(2 * i + 1, 1, ms, ls)

    init = (tuple(jnp.full((1, 2 * Q_BLOCK), NEG_BIG, jnp.float32) for _ in range(n_pairs)),
            tuple(jnp.zeros((1, 2 * Q_BLOCK), jnp.float32) for _ in range(n_pairs)))
    _, ls = lax.fori_loop(0, (n_chunks + 1) // 2, attn_two_chunks, init)
    for j in range(n_pairs):
        lo, mid, hi = j * LANES, j * LANES + HEAD_DIM, (j + 1) * LANES
        even = acc_ref[lo:mid, :] / ls[j][:, :Q_BLOCK]
        odd = acc_ref[mid:hi, :] / ls[j][:, Q_BLOCK:]
        o_ref[:, lo:hi] = jnp.concatenate([even, odd], axis=0).T.astype(o_ref.dtype)


def _dsa(iq, iw, q, iklo, ikhi, k, vt):
    b, s, _ = q.shape
    n_top = min(TOPK_MAX, s // 4)
    idx_bits = max(1, int(np.ceil(np.log2(s))))
    qblk = lambda w: pl.BlockSpec((None, Q_BLOCK, w), lambda bi, qi: (bi, qi, 0))
    keys = lambda w: pl.BlockSpec((None, s, w), lambda bi, qi: (bi, 0, 0))
    return pl.pallas_call(
        functools.partial(_dsa_kernel, n_top=n_top, idx_bits=idx_bits),
        grid=(b, s // Q_BLOCK),
        in_specs=[qblk(IDXQ_W), qblk(LANES), qblk(ATTN_W), keys(LANES), keys(LANES),
                  keys(ATTN_W), pl.BlockSpec((None, VT_ALL, s), lambda bi, qi: (bi, 0, 0))],
        out_specs=qblk(ATTN_W),
        out_shape=jax.ShapeDtypeStruct((b, s, ATTN_W), jnp.bfloat16),
        scratch_shapes=[pltpu.VMEM((s, Q_BLOCK), jnp.int32),
                        pltpu.VMEM((N_HEADS_IDX // 4, LANES, 2 * Q_BLOCK), jnp.bfloat16),
                        pltpu.VMEM((N_HEADS_ATTN // 2, LANES, 2 * Q_BLOCK), jnp.bfloat16),
                        pltpu.VMEM((2, KEY_CHUNK, 2 * Q_BLOCK), jnp.float32),
                        pltpu.VMEM((2, N_HEADS_ATTN // 2, KEY_CHUNK, 2 * Q_BLOCK), jnp.float32),
                        pltpu.VMEM((ATTN_W, Q_BLOCK), jnp.float32)],
        compiler_params=pltpu.CompilerParams(dimension_semantics=("arbitrary", "arbitrary"),
                                             vmem_limit_bytes=VMEM_LIMIT),
        name="dsa",
    )(iq, iw, q, iklo, ikhi, k, vt)


def _merge_kernel(x_ref, ya_ref, u_ref, uprev_ref, qm_ref, km_ref, vm_ref,
                  gmix_ref, wg_ref, bg_ref, wpa_ref, wpp_ref, wpm_ref, wbd_ref, psc_ref,
                  wo_ref, gffn_ref, wrh_ref, wrl_ref, br_ref,
                  h_ref, xn2_ref, eg_ref, ei_ref, cnt_ref, carry_ref, *, tiles_per_seq):
    i = pl.program_id(0)
    t = x_ref.shape[0]
    d = x_ref.shape[1]
    bf = jnp.bfloat16
    x = x_ref[...]
    xn = _rmsnorm(x, gmix_ref[...]).astype(bf)

    tile_in_seq = i % tiles_per_seq
    u = u_ref[...]
    halo = jnp.where(tile_in_seq == 0, 0.0, uprev_ref[...])
    ext = jnp.concatenate([halo, u], axis=0)
    lane_p = lax.broadcasted_iota(jnp.int32, (t, POOL_W), 1)
    pos1 = (tile_in_seq * t + lax.broadcasted_iota(jnp.int32, (t, 1), 0) + 1).astype(jnp.float32)
    pooled = None
    run, width = ext, 1
    for g, w in enumerate(POOL_WINDOWS):
        while width < w:
            run = run[width:] + run[:-width]
            width *= 2
        start = POOL_HALO + 1 - w
        mean = run[start:start + t] / jnp.minimum(pos1, float(w))
        pooled = mean if pooled is None else jnp.where(lane_p >= g * POOL_GROUP_DIM, mean, pooled)
    pooled = pooled - u
    mixed = _dot(pooled.astype(bf), wbd_ref[...]) * psc_ref[...]
    y_pool = _dot(mixed.astype(bf), wpp_ref[...])

    qm = qm_ref[...]
    probs = []
    for h in range(N_HEADS_MEM):
        s = _dot_nt(qm, km_ref[h])
        p = jnp.exp(s - jnp.max(s, axis=1, keepdims=True))
        probs.append((p / jnp.sum(p, axis=1, keepdims=True)).astype(bf))
    y_mem = _dot(probs[0], vm_ref[0])
    for h in range(1, N_HEADS_MEM):
        y_mem = y_mem + _dot(probs[h], vm_ref[h])
    y_mem = _dot(y_mem.astype(bf), wpm_ref[...])

    y_attn = _dot(ya_ref[...], wpa_ref[...])

    def gate(br):
        z = _dot(xn, wg_ref[:, br * d:(br + 1) * d]) + bg_ref[:, br * d:(br + 1) * d]
        return 1.0 / (1.0 + jnp.exp(-z))

    merged = gate(0) * y_attn + gate(1) * y_pool + gate(2) * y_mem
    h = x + _dot(merged.astype(bf), wo_ref[...])
    h_ref[...] = h
    xn2 = _rmsnorm(h, gffn_ref[...])
    xn2_ref[...] = xn2

    x_hi = xn2.astype(bf)
    x_lo = (xn2 - x_hi.astype(jnp.float32)).astype(bf)
    logits = (_dot(x_hi, wrh_ref[...]) + (_dot(x_lo, wrh_ref[...]) + _dot(x_hi, wrl_ref[...]))
              + br_ref[...])
    lane_e = lax.broadcasted_iota(jnp.int32, logits.shape, 1).astype(jnp.float32)
    work = logits
    vals, onehots, ids = [], [], []
    for _ in range(TOP_K):
        mx = jnp.max(work, axis=1, keepdims=True)
        idx = jnp.min(jnp.where(work == mx, lane_e, float(LANES)), axis=1, keepdims=True)
        oh = lane_e == idx
        vals.append(mx)
        ids.append(idx)
        onehots.append(oh)
        work = jnp.where(oh, -jnp.inf, work)
    exps = [jnp.exp(v - vals[0]) for v in vals]
    denom = exps[0] + exps[1] + exps[2] + exps[3]

    @pl.when(i == 0)
    def _():
        carry_ref[...] = jnp.zeros(carry_ref.shape, jnp.float32)

    member = jnp.where(onehots[0] | onehots[1] | onehots[2] | onehots[3], 1.0, 0.0)
    r_io = lax.broadcasted_iota(jnp.int32, (t, t), 0)
    c_io = lax.broadcasted_iota(jnp.int32, (t, t), 1)
    earlier = jnp.where(c_io < r_io, 1.0, 0.0).astype(bf)
    before = _dot(earlier, member.astype(bf)) + carry_ref[...]
    carry_ref[...] = carry_ref[...] + jnp.sum(member, axis=0, keepdims=True)
    cnt_ref[...] = carry_ref[...]

    lane8 = lax.broadcasted_iota(jnp.int32, (t, 2 * TOP_K), 1)
    eg = jnp.zeros((t, 2 * TOP_K), jnp.float32)
    ei = jnp.zeros((t, 2 * TOP_K), jnp.float32)
    for j in range(TOP_K):
        rank = jnp.sum(jnp.where(onehots[j], before, 0.0), axis=1, keepdims=True)
        eg = jnp.where(lane8 == j, exps[j] / denom, eg)
        ei = jnp.where(lane8 == j, ids[j], ei)
        ei = jnp.where(lane8 == TOP_K + j, rank, ei)
    eg_ref[...] = eg
    ei_ref[...] = ei.astype(jnp.int32)


def _merge(x2, ya, u, qm, km, vm, gmix, wg, bg, wpa, wpp, wpm, wbd, psc, wo, gffn, wrh, wrl, br, seq):
    n, d = x2.shape
    t = MERGE_TILE
    tiles_per_seq = seq // t
    m = km.shape[2]
    row = lambda i: (i, 0)
    full = lambda i: (0, 0)
    per_batch = lambda i: (i // tiles_per_seq, 0, 0, 0)
    halo_blocks = t // POOL_HALO
    prev = lambda i: (jnp.maximum(i * halo_blocks - 1, 0), 0)
    consts = (gmix, wg, bg, wpa, wpp, wpm, wbd, psc, wo, gffn, wrh, wrl, br)
    out_shape = (jax.ShapeDtypeStruct((n, d), jnp.float32), jax.ShapeDtypeStruct((n, d), jnp.float32),
                 jax.ShapeDtypeStruct((n, 2 * TOP_K), jnp.float32),
                 jax.ShapeDtypeStruct((n, 2 * TOP_K), jnp.int32),
                 jax.ShapeDtypeStruct((1, LANES), jnp.float32))
    return pl.pallas_call(
        functools.partial(_merge_kernel, tiles_per_seq=tiles_per_seq),
        grid=(n // t,),
        in_specs=[pl.BlockSpec((t, d), row), pl.BlockSpec((t, ATTN_W), row),
                  pl.BlockSpec((t, POOL_W), row), pl.BlockSpec((POOL_HALO, POOL_W), prev),
                  pl.BlockSpec((t, MEM_W), row),
                  pl.BlockSpec((None, N_HEADS_MEM, m, MEM_W), per_batch),
                  pl.BlockSpec((None, N_HEADS_MEM, m, MEM_W), per_batch)]
                 + [pl.BlockSpec(c.shape, full) for c in consts],
        out_specs=(pl.BlockSpec((t, d), row), pl.BlockSpec((t, d), row),
                   pl.BlockSpec((t, 2 * TOP_K), row), pl.BlockSpec((t, 2 * TOP_K), row),
                   pl.BlockSpec((1, LANES), full)),
        out_shape=out_shape,
        scratch_shapes=[pltpu.VMEM((1, LANES), jnp.float32)],
        compiler_params=pltpu.CompilerParams(dimension_semantics=("arbitrary",),
                                             vmem_limit_bytes=VMEM_LIMIT),
        name="merge",
    )(x2, ya, u, u, qm, km, vm, *consts)


def _row_copy(src_ref, src_row, dst_ref, dst_row, sem):
    return pltpu.make_async_copy(src_ref.at[pl.ds(src_row, 1)], dst_ref.at[pl.ds(dst_row, 1)], sem)


def _dispatch_kernel(pstart_ref, cnt_ref, e_ref, r_ref, x_ref, xs_ref, zero_ref, sem):
    t = x_ref.shape[0]

    @pl.when(pl.program_id(0) == 0)
    def _():
        zero_ref[...] = jnp.zeros(zero_ref.shape, zero_ref.dtype)

        def for_each_pad_row(act):
            def per_expert(e, carry):
                n_pad = (ROW_BLOCK - cnt_ref[e] % ROW_BLOCK) % ROW_BLOCK
                first = pstart_ref[e] + cnt_ref[e]

                def per_row(r, c):
                    act(_row_copy(zero_ref, 0, xs_ref, first + r, sem))
                    return c
                return lax.fori_loop(0, n_pad, per_row, carry)
            lax.fori_loop(0, N_EXPERTS, per_expert, 0)

        def for_each_unused_block(act):
            n_blocks = xs_ref.shape[0] // ROW_BLOCK
            first = (pstart_ref[N_EXPERTS - 1] + cnt_ref[N_EXPERTS - 1] + ROW_BLOCK - 1) // ROW_BLOCK

            def per_block(b, c):
                rows = pl.ds(pl.multiple_of(b * ROW_BLOCK, ROW_BLOCK), ROW_BLOCK)
                act(pltpu.make_async_copy(zero_ref, xs_ref.at[rows], sem))
                return c
            lax.fori_loop(first, n_blocks, per_block, 0)

        for act in (lambda cp: cp.start(), lambda cp: cp.wait()):
            for_each_pad_row(act)
            for_each_unused_block(act)

    def issue(tok, carry):
        for j in range(TOP_K):
            a = tok * TOP_K + j
            _row_copy(x_ref, tok, xs_ref, pstart_ref[e_ref[a]] + r_ref[a], sem).start()
        return carry

    lax.fori_loop(0, t, issue, 0)

    def drain(tok, carry):
        for j in range(TOP_K):
            _row_copy(x_ref, 0, xs_ref, 0, sem).wait()
        return carry

    lax.fori_loop(0, t, drain, 0)


def _dispatch(pstart, cnt, e_flat, r_flat, xn2, n_rows):
    n, d = xn2.shape
    t = MOVE_TILE
    smem = lambda: pl.BlockSpec((t * TOP_K,), lambda i, ps, ct: (i,), memory_space=pltpu.SMEM)
    return pl.pallas_call(
        _dispatch_kernel,
        grid_spec=pltpu.PrefetchScalarGridSpec(
            num_scalar_prefetch=2,
            grid=(n // t,),
            in_specs=[smem(), smem(), pl.BlockSpec((t, d), lambda i, ps, ct: (i, 0))],
            out_specs=pl.BlockSpec(memory_space=pl.ANY),
            scratch_shapes=[pltpu.VMEM((ROW_BLOCK, d), jnp.float32),
                            pltpu.SemaphoreType.DMA(())]),
        out_shape=jax.ShapeDtypeStruct((n_rows, d), jnp.float32),
        compiler_params=pltpu.CompilerParams(dimension_semantics=("arbitrary",),
                                             vmem_limit_bytes=VMEM_LIMIT),
        name="dispatch",
    )(pstart, cnt, e_flat, r_flat, xn2)


def _experts_kernel(be_ref, nused_ref, xs_ref, wu_ref, bu_ref, wd_ref, bd_ref, ys_ref,
                    wu16_ref, wd16_ref):
    i = pl.program_id(0)
    f = wd_ref.shape[0]

    @pl.when((i == 0) | (be_ref[i] != be_ref[jnp.maximum(i - 1, 0)]))
    def _():
        wu16_ref[...] = wu_ref[...].astype(wu16_ref.dtype)
        wd16_ref[...] = wd_ref[...].astype(wd16_ref.dtype)

    @pl.when(i < nused_ref[0])
    def _():
        hb = _dot(xs_ref[...].astype(jnp.bfloat16), wu16_ref[...]) + bu_ref[...]
        glu = jnp.minimum(hb[:, :f], SWIGLU_LIMIT)
        lin = jnp.clip(hb[:, f:], -SWIGLU_LIMIT, SWIGLU_LIMIT)
        act = glu * (1.0 / (1.0 + jnp.exp(-SWIGLU_ALPHA * glu))) * (lin + 1.0)
        ys_ref[...] = _dot(act.astype(jnp.bfloat16), wd16_ref[...]) + bd_ref[...]

    @pl.when(i >= nused_ref[0])
    def _():
        ys_ref[...] = jnp.zeros(ys_ref.shape, ys_ref.dtype)


def _experts(block_e, nused, xs, wu, bu, wd, bd):
    n_rows, d = xs.shape
    e, _, f2 = wu.shape
    f = wd.shape[1]
    rows = lambda i, be, nu: (i, 0)
    used_rows = lambda i, be, nu: (jnp.minimum(i, nu[0] - 1), 0)
    per_e = lambda i, be, nu: (be[i], 0, 0)
    return pl.pallas_call(
        _experts_kernel,
        grid_spec=pltpu.PrefetchScalarGridSpec(
            num_scalar_prefetch=2,
            grid=(n_rows // ROW_BLOCK,),
            in_specs=[pl.BlockSpec((ROW_BLOCK, d), used_rows),
                      pl.BlockSpec((None, d, f2), per_e), pl.BlockSpec((None, 1, f2), per_e),
                      pl.BlockSpec((None, f, d), per_e), pl.BlockSpec((None, 1, d), per_e)],
            out_specs=pl.BlockSpec((ROW_BLOCK, d), rows),
            scratch_shapes=[pltpu.VMEM((d, f2), jnp.bfloat16), pltpu.VMEM((f, d), jnp.bfloat16)]),
        out_shape=jax.ShapeDtypeStruct((n_rows, d), jnp.float32),
        compiler_params=pltpu.CompilerParams(dimension_semantics=("arbitrary",),
                                             vmem_limit_bytes=VMEM_LIMIT),
        name="experts",
    )(block_e, nused, xs, wu, bu, wd, bd)


def _combine_kernel(pstart_ref, e_ref, r_ref, h_ref, eg_ref, g_ref, ys_ref, o_ref, buf_ref, sem,
                    *, final_norm):
    t = h_ref.shape[0]

    def issue(tok, carry):
        for j in range(TOP_K):
            a = tok * TOP_K + j
            _row_copy(ys_ref, pstart_ref[e_ref[a]] + r_ref[a], buf_ref.at[j], tok, sem).start()
        return carry

    lax.fori_loop(0, t, issue, 0)

    def drain(tok, carry):
        for j in range(TOP_K):
            _row_copy(ys_ref, 0, buf_ref.at[j], 0, sem).wait()
        return carry

    lax.fori_loop(0, t, drain, 0)
    eg = eg_ref[...]
    h = h_ref[...]
    for j in range(TOP_K):
        h = h + eg[:, j:j + 1] * buf_ref[j]
    o_ref[...] = _rmsnorm(h, g_ref[...]) if final_norm else h


def _combine(pstart, e_flat, r_flat, h, eg, g, ys, final_norm):
    n, d = h.shape
    t = MOVE_TILE
    smem = lambda: pl.BlockSpec((t * TOP_K,), lambda i, ps: (i,), memory_space=pltpu.SMEM)
    return pl.pallas_call(
        functools.partial(_combine_kernel, final_norm=final_norm),
        grid_spec=pltpu.PrefetchScalarGridSpec(
            num_scalar_prefetch=1,
            grid=(n // t,),
            in_specs=[smem(), smem(), pl.BlockSpec((t, d), lambda i, ps: (i, 0)),
                      pl.BlockSpec((t, 2 * TOP_K), lambda i, ps: (i, 0)),
                      pl.BlockSpec((1, d), lambda i, ps: (0, 0)),
                      pl.BlockSpec(memory_space=pl.ANY)],
            out_specs=pl.BlockSpec((t, d), lambda i, ps: (i, 0)),
            scratch_shapes=[pltpu.VMEM((TOP_K, t, d), jnp.float32), pltpu.SemaphoreType.DMA(())]),
        out_shape=jax.ShapeDtypeStruct((n, d), jnp.float32),
        compiler_params=pltpu.CompilerParams(dimension_semantics=("arbitrary",),
                                             vmem_limit_bytes=VMEM_LIMIT),
        name="combine",
    )(pstart, e_flat, r_flat, h, eg, g, ys)


def _rope_tables(seq):
    half = ROPE_DIM // 2
    inv = jnp.power(jnp.float32(ROPE_THETA), -jnp.arange(half, dtype=jnp.float32) / half)
    ang = jnp.arange(seq, dtype=jnp.float32)[:, None] * inv[None, :]
    cos, sin = jnp.cos(ang), jnp.sin(ang)
    pad = HEAD_DIM - ROPE_DIM
    one, zero = jnp.ones((seq, pad), jnp.float32), jnp.zeros((seq, pad), jnp.float32)
    zh = jnp.zeros((seq, half), jnp.float32)
    reps = LANES // HEAD_DIM
    c = jnp.tile(jnp.concatenate([cos, cos, one], axis=1), (1, reps))
    a = jnp.tile(jnp.concatenate([-sin, zh, zero], axis=1), (1, reps))
    b = jnp.tile(jnp.concatenate([zh, sin, zero], axis=1), (1, reps))
    return c, a, b


def _layer(h, mem, p, tables, final_g):
    b, s, d = h.shape
    n = b * s
    bf = jnp.bfloat16
    x2 = h.reshape(n, d)
    w_in = p["w_in"]
    o_ik = 3 * ATTN_W + IDXQ_W
    o_u = o_ik + IDX_DIM + N_HEADS_IDX
    wa = w_in[:, :o_ik].astype(bf)
    wb = jnp.pad(w_in[:, o_ik:o_u], ((0, 0), (0, LANES - (o_u - o_ik)))).astype(bf)
    wc = w_in[:, o_u:].astype(bf)
    q, k, vt, iq, iklo, ikhi, iw, u, qm = _in_proj(x2, p["norm_mix_g"][None], wa, wb, wc, *tables, b, s)
    km, vm = _mem_kv(mem, p["mem_norm_g"][None], p["w_mem_kv"].astype(bf))
    b3 = lambda a: a.reshape(b, s, a.shape[-1])
    ya = _dsa(b3(iq), b3(iw), b3(q), b3(iklo), b3(ikhi), b3(k), vt).reshape(n, ATTN_W)

    wbd = jnp.zeros((POOL_W, POOL_W), jnp.float32)
    for g in range(N_POOL_GROUPS):
        lo = g * POOL_GROUP_DIM
        wbd = wbd.at[lo:lo + POOL_GROUP_DIM, lo:lo + POOL_GROUP_DIM].set(p["pool_w_group"][g])
    wr = jnp.pad(p["w_router"], ((0, 0), (0, LANES - N_EXPERTS)))
    wr_hi = wr.astype(bf)
    br = jnp.pad(p["b_router"], (0, LANES - N_EXPERTS), constant_values=NEG_BIG)[None]
    hmid, xn2, eg, ei, counts = _merge(
        x2, ya, u, qm, km, vm, p["norm_mix_g"][None], p["w_gate"].astype(bf), p["b_gate"][None],
        p["w_proj_attn"].astype(bf), p["w_proj_pool"].astype(bf), p["w_proj_mem"].astype(bf),
        wbd.astype(bf), p["pool_scale"][None], p["w_out"].astype(bf), p["norm_ffn_g"][None],
        wr_hi, (wr - wr_hi.astype(jnp.float32)).astype(bf), br, s)

    cnt = counts[0, :N_EXPERTS].astype(jnp.int32)
    padded = ((cnt + ROW_BLOCK - 1) // ROW_BLOCK) * ROW_BLOCK
    pend = jnp.cumsum(padded)
    pstart = (pend - padded).astype(jnp.int32)
    n_rows = n * TOP_K + N_EXPERTS * ROW_BLOCK
    n_blocks = n_rows // ROW_BLOCK
    nused = (pend[-1] // ROW_BLOCK).astype(jnp.int32)
    blk = jnp.minimum(jnp.arange(n_blocks, dtype=jnp.int32), nused - 1) * ROW_BLOCK
    block_e = jnp.sum(blk[:, None] >= pend[None, :], axis=1).astype(jnp.int32)
    block_e = jnp.minimum(block_e, N_EXPERTS - 1)
    e_flat = ei[:, :TOP_K].reshape(-1)
    r_flat = ei[:, TOP_K:].reshape(-1)

    xs = _dispatch(pstart, cnt, e_flat, r_flat, xn2, n_rows)
    ys = _experts(block_e, nused[None], xs, p["w_up"], p["b_up"][:, None, :],
                  p["w_down"], p["b_down"][:, None, :])
    g = p["norm_ffn_g"][None] if final_g is None else final_g[None]
    return _combine(pstart, e_flat, r_flat, hmid, eg, g, ys, final_g is not None).reshape(b, s, d)


def kernel(x, mem, norm_mix_g, w_in, w_gate, b_gate, w_proj_attn, w_proj_pool, w_proj_mem,
           pool_w_group, pool_scale, mem_norm_g, w_mem_kv, w_out, norm_ffn_g,
           w_router, b_router, w_up, b_up, w_down, b_down, norm_final_g):
    b, s, d = x.shape
    assert s % IN_TILE == 0 and s % KEY_CHUNK == 0 and (b * s) % MOVE_TILE == 0
    stacked = dict(norm_mix_g=norm_mix_g, w_in=w_in, w_gate=w_gate, b_gate=b_gate,
                   w_proj_attn=w_proj_attn, w_proj_pool=w_proj_pool, w_proj_mem=w_proj_mem,
                   pool_w_group=pool_w_group, pool_scale=pool_scale, mem_norm_g=mem_norm_g,
                   w_mem_kv=w_mem_kv, w_out=w_out, norm_ffn_g=norm_ffn_g, w_router=w_router,
                   b_router=b_router, w_up=w_up, b_up=b_up, w_down=w_down, b_down=b_down)
    depth = w_in.shape[0]
    tables = _rope_tables(s)
    h = x
    for l in range(depth):
        p = {name: val[l] for name, val in stacked.items()}
        h = _layer(h, mem, p, tables, norm_final_g if l == depth - 1 else None)
    return h
```

```python
import functools

import jax
import jax.numpy as jnp
import numpy as np
from jax import lax
from jax.experimental import pallas as pl
from jax.experimental.pallas import tpu as pltpu

N_HEADS_ATTN = 8
HEAD_DIM = 64
ROPE_DIM = HEAD_DIM // 4
ROPE_THETA = 500000.0
N_HEADS_IDX = 8
IDX_DIM = 64
TOPK_MAX = 256
N_POOL_GROUPS = 4
POOL_GROUP_DIM = 64
POOL_WINDOWS = (2, 4, 8, 16)
POOL_HALO = 16
N_HEADS_MEM = 4
N_BRANCHES = 3
N_EXPERTS = 32
TOP_K = 4
SWIGLU_ALPHA = 1.702
SWIGLU_LIMIT = 7.0
EPS = 1e-6

ATTN_W = N_HEADS_ATTN * HEAD_DIM
POOL_W = N_POOL_GROUPS * POOL_GROUP_DIM
MEM_W = N_HEADS_MEM * HEAD_DIM
IDXQ_W = N_HEADS_IDX * IDX_DIM

LANES = 128
SUBLANES = 8
VMEM_LIMIT = 56 * 1024 * 1024

IN_TILE = 512
Q_BLOCK = 128
KEY_CHUNK = 256
COUNT_CHUNK = 512
COUNT_ROWS = 32
MERGE_TILE = 256
ROW_BLOCK = 256
MOVE_TILE = 256

VT_ROWS = LANES + 16
VT_ALL = (N_HEADS_ATTN // 2) * VT_ROWS
LOG2_E = 1.4426950408889634

NEG_BIG = -1e30
INT_MIN = -2 ** 31


def _dot(a, b):
    return jnp.dot(a, b, preferred_element_type=jnp.float32)


def _dot_nt(a, b):
    return lax.dot_general(a, b, (((1,), (1,)), ((), ())), preferred_element_type=jnp.float32)


def _rmsnorm(x, g):
    return x * lax.rsqrt(jnp.mean(x * x, axis=-1, keepdims=True) + EPS) * g


def _store_row_tiles(ref, x):
    rows, width = x.shape
    for j in range(width // LANES):
        ref[pl.ds(j, rows, stride=width // LANES), :] = x[:, j * LANES:(j + 1) * LANES]


def _load_row_tiles(ref, rows, width):
    pieces = width // LANES
    return jnp.concatenate([ref[pl.ds(j, rows, stride=pieces), :] for j in range(pieces)], axis=1)


def _row_tile(ref, row):
    return ref.at[pl.ds(pl.multiple_of(row * SUBLANES, SUBLANES), SUBLANES)]


def _rope128(x, c, a, b):
    return x * c + pltpu.roll(x, LANES - ROPE_DIM // 2, 1) * a + pltpu.roll(x, ROPE_DIM // 2, 1) * b


def _in_proj_kernel(x_ref, g_ref, wa_ref, wb_ref, wc_ref, rc_ref, ra_ref, rb_ref,
                    q_ref, k_ref, vt_ref, iq_ref, iklo_ref, ikhi_ref, iw_ref, u_ref, qm_ref):
    xn = _rmsnorm(x_ref[...], g_ref[...]).astype(jnp.bfloat16)
    rc, ra, rb = rc_ref[...], ra_ref[...], rb_ref[...]
    pa = _dot(xn, wa_ref[...])
    for seg, (ref, scale) in enumerate(((q_ref, LOG2_E * HEAD_DIM ** -0.5), (k_ref, None),
                                        (None, None), (iq_ref, IDX_DIM ** -0.5))):
        for c in range(ATTN_W // LANES):
            lo = seg * ATTN_W + c * LANES
            blk = pa[:, lo:lo + LANES]
            if ref is None:
                vt_ref[c * VT_ROWS:c * VT_ROWS + LANES, :] = blk.T.astype(vt_ref.dtype)
                vt_ref[c * VT_ROWS + LANES:(c + 1) * VT_ROWS, :] = jnp.ones(
                    (VT_ROWS - LANES, blk.shape[0]), vt_ref.dtype)
                continue
            blk = _rope128(blk, rc, ra, rb)
            if scale is not None:
                blk = blk * scale
            ref[:, c * LANES:(c + 1) * LANES] = blk.astype(ref.dtype)
    pb = _dot(xn, wb_ref[...])
    lane = lax.broadcasted_iota(jnp.int32, pb.shape, 1)
    ik = jnp.where(lane < IDX_DIM, _rope128(pb, rc, ra, rb), 0.0)
    iklo_ref[...] = ik.astype(iklo_ref.dtype)
    ikhi_ref[...] = pltpu.roll(ik, IDX_DIM, 1).astype(ikhi_ref.dtype)
    iw_ref[...] = pb * (N_HEADS_IDX ** -0.5)
    pc = _dot(xn, wc_ref[...])
    u_ref[...] = pc[:, :POOL_W]
    qm_ref[...] = (pc[:, POOL_W:] * (HEAD_DIM ** -0.5)).astype(qm_ref.dtype)


def _in_proj(x2, g, wa, wb, wc, rc, ra, rb, batch, seq):
    n, d = x2.shape
    t = IN_TILE
    tiles_per_seq = seq // t
    row = lambda i: (i, 0)
    full = lambda i: (0, 0)
    pos = lambda i: (i % tiles_per_seq, 0)
    bf = jnp.bfloat16
    flat = lambda w, dt: (jax.ShapeDtypeStruct((n, w), dt), pl.BlockSpec((t, w), row))
    outs = (flat(ATTN_W, bf), flat(ATTN_W, bf),
            (jax.ShapeDtypeStruct((batch, VT_ALL, seq), bf),
             pl.BlockSpec((None, VT_ALL, t), lambda i: (i // tiles_per_seq, 0, i % tiles_per_seq))),
            flat(IDXQ_W, bf), flat(LANES, bf), flat(LANES, bf), flat(LANES, jnp.float32),
            flat(POOL_W, jnp.float32), flat(MEM_W, bf))
    return pl.pallas_call(
        _in_proj_kernel,
        grid=(n // t,),
        in_specs=[pl.BlockSpec((t, d), row), pl.BlockSpec((1, d), full),
                  pl.BlockSpec(wa.shape, full), pl.BlockSpec(wb.shape, full),
                  pl.BlockSpec(wc.shape, full),
                  pl.BlockSpec((t, LANES), pos), pl.BlockSpec((t, LANES), pos),
                  pl.BlockSpec((t, LANES), pos)],
        out_specs=tuple(o[1] for o in outs),
        out_shape=tuple(o[0] for o in outs),
        compiler_params=pltpu.CompilerParams(dimension_semantics=("arbitrary",),
                                             vmem_limit_bytes=VMEM_LIMIT),
        name="in_proj",
    )(x2, g, wa, wb, wc, rc, ra, rb)


def _mem_kv_kernel(mem_ref, g_ref, w_ref, km_ref, vm_ref):
    mn = _rmsnorm(mem_ref[...], g_ref[...]).astype(jnp.bfloat16)
    kv = _dot(mn, w_ref[...])
    km, vm = kv[:, :MEM_W], kv[:, MEM_W:]
    lane = lax.broadcasted_iota(jnp.int32, km.shape, 1)
    for h in range(N_HEADS_MEM):
        in_head = (lane >= h * HEAD_DIM) & (lane < (h + 1) * HEAD_DIM)
        km_ref[h] = jnp.where(in_head, km, 0.0).astype(km_ref.dtype)
        vm_ref[h] = jnp.where(in_head, vm, 0.0).astype(vm_ref.dtype)


def _mem_kv(mem, g, w):
    b, m, d = mem.shape
    out = jax.ShapeDtypeStruct((b, N_HEADS_MEM, m, MEM_W), jnp.bfloat16)
    return pl.pallas_call(
        _mem_kv_kernel,
        grid=(b,),
        in_specs=[pl.BlockSpec((None, m, d), lambda i: (i, 0, 0)),
                  pl.BlockSpec((1, d), lambda i: (0, 0)),
                  pl.BlockSpec(w.shape, lambda i: (0, 0))],
        out_specs=(pl.BlockSpec((None, N_HEADS_MEM, m, MEM_W), lambda i: (i, 0, 0, 0)),
                   pl.BlockSpec((None, N_HEADS_MEM, m, MEM_W), lambda i: (i, 0, 0, 0))),
        out_shape=(out, out),
        compiler_params=pltpu.CompilerParams(dimension_semantics=("arbitrary",),
                                             vmem_limit_bytes=VMEM_LIMIT),
        name="mem_kv",
    )(mem, g, w)


def _dsa_kernel(iq_ref, iw_ref, q_ref, iklo_ref, ikhi_ref, k_ref, vt_ref, o_ref,
                key_ref, iqt_ref, qt_ref, bias_ref, s_ref, acc_ref, *, n_top, idx_bits):
    qb = pl.program_id(1)
    n_chunks = (qb * Q_BLOCK + Q_BLOCK + KEY_CHUNK - 1) // KEY_CHUNK
    n_count_chunks = (qb * Q_BLOCK + Q_BLOCK + COUNT_CHUNK - 1) // COUNT_CHUNK
    q_pos = qb * Q_BLOCK + lax.broadcasted_iota(jnp.int32, (1, Q_BLOCK), 1)
    row_k = lax.broadcasted_iota(jnp.int32, (KEY_CHUNK, 1), 0)
    row_c = lax.broadcasted_iota(jnp.int32, (COUNT_CHUNK, 1), 0)
    row_d = lax.broadcasted_iota(jnp.int32, (LANES, Q_BLOCK), 0)
    bf = jnp.bfloat16

    def chunk_off(c):
        return pl.multiple_of(c * KEY_CHUNK, KEY_CHUNK)

    wt = iw_ref[...].T
    for j in range(N_HEADS_IDX // 2):
        iqt_ref[j // 2, :, (j % 2) * Q_BLOCK:(j % 2 + 1) * Q_BLOCK] = (
            iq_ref[:, j * LANES:(j + 1) * LANES].astype(jnp.float32).T.astype(bf))
    for j in range(N_HEADS_ATTN // 2):
        pair_t = q_ref[:, j * LANES:(j + 1) * LANES].astype(jnp.float32).T
        qt_ref[j, :, :Q_BLOCK] = jnp.where(row_d < HEAD_DIM, pair_t, 0.0).astype(bf)
        qt_ref[j, :, Q_BLOCK:] = jnp.where(row_d >= HEAD_DIM, pair_t, 0.0).astype(bf)

    def score_chunk(c, carry):
        off = pl.multiple_of(c * COUNT_CHUNK, COUNT_CHUNK)
        acc = jnp.zeros((COUNT_CHUNK, Q_BLOCK), jnp.float32)
        for parity, ik_ref in enumerate((iklo_ref, ikhi_ref)):
            ik = ik_ref[pl.ds(off, COUNT_CHUNK), :]
            for g in range(N_HEADS_IDX // 4):
                dots = _dot(ik, iqt_ref[g])
                for side in range(2):
                    head = 2 * (2 * g + side) + parity
                    w = wt[IDX_DIM + head:IDX_DIM + head + 1, :]
                    acc = acc + w * jnp.maximum(dots[:, side * Q_BLOCK:(side + 1) * Q_BLOCK], 0.0)
        acc = jnp.where(off + row_c <= q_pos, acc, -jnp.inf)
        bits = lax.bitcast_convert_type(acc, jnp.int32)
        key_ref[pl.ds(off, COUNT_CHUNK), :] = bits ^ ((bits >> 31) & 0x7FFFFFFF)
        return carry

    lax.fori_loop(0, n_count_chunks, score_chunk, 0)

    def count(pred):
        def body(c, cnt):
            off = pl.multiple_of(c * COUNT_CHUNK, COUNT_CHUNK)
            hit = jnp.where(pred(key_ref[pl.ds(off, COUNT_CHUNK), :], off + row_c), 1.0, 0.0)
            return cnt + jnp.sum(hit.reshape(COUNT_CHUNK // COUNT_ROWS, COUNT_ROWS, Q_BLOCK), axis=0)
        cnt = lax.fori_loop(0, n_count_chunks, body, jnp.zeros((COUNT_ROWS, Q_BLOCK), jnp.float32))
        return jnp.sum(cnt, axis=0, keepdims=True)

    k_f = jnp.float32(n_top)
    zero = jnp.zeros((1, Q_BLOCK), jnp.int32)
    thr = jnp.where(count(lambda kk, pos: kk >= zero) >= k_f, zero, INT_MIN)

    def thr_bit(i, cur):
        cand = cur + jnp.left_shift(jnp.int32(1), 30 - i)
        return jnp.where(count(lambda kk, pos: kk >= cand) >= k_f, cand, cur)

    thr = lax.fori_loop(0, 31, thr_bit, thr)
    need = k_f - count(lambda kk, pos: kk > thr)
    n_ties = count(lambda kk, pos: kk == thr)

    def tie_search():
        def tie_bit(i, cur):
            cand = cur + jnp.left_shift(jnp.int32(1), idx_bits - 1 - i)
            below = count(lambda kk, pos: (kk == thr) & (pos < cand))
            return jnp.where(below < need, cand, cur)
        return lax.fori_loop(0, idx_bits, tie_bit, zero)

    has_surplus_ties = jnp.max(jnp.where(n_ties > need, 1.0, 0.0)) > 0.0
    tie_pos = lax.cond(has_surplus_ties, tie_search, lambda: zero + (2 ** idx_bits - 1))

    acc_ref[...] = jnp.zeros(acc_ref.shape, jnp.float32)

    n_pairs = N_HEADS_ATTN // 2

    def masked_logits(c, slot):
        off = chunk_off(jnp.minimum(c, n_chunks - 1))
        kk = key_ref[pl.ds(off, KEY_CHUNK), :]
        pos = c * KEY_CHUNK + row_k
        sel = ((kk > thr) | ((kk == thr) & (pos <= tie_pos))) & (pos <= q_pos)
        bias = jnp.where(sel, 0.0, NEG_BIG)
        bias_ref[slot] = jnp.concatenate([bias, bias], axis=1)
        for j in range(n_pairs):
            kp = k_ref[pl.ds(off, KEY_CHUNK), j * LANES:(j + 1) * LANES]
            s_ref[slot, j] = _dot(kp, qt_ref[j]) + bias_ref[slot]

    def softmax_pv(c, slot, ms, ls):
        off = chunk_off(jnp.minimum(c, n_chunks - 1))
        new_ms, new_ls = [], []
        for j in range(n_pairs):
            s = s_ref[slot, j]
            m_new = jnp.maximum(ms[j], jnp.max(s, axis=0, keepdims=True))
            alpha = jnp.exp2(ms[j] - m_new)
            p = jnp.exp2(s - m_new).astype(bf)
            vt = vt_ref[j * VT_ROWS:(j + 1) * VT_ROWS, pl.ds(off, KEY_CHUNK)]
            pv = _dot(vt, p)
            new_ls.append(alpha * ls[j] + pv[LANES:LANES + 1, :])
            new_ms.append(m_new)
            lo, mid, hi = j * LANES, j * LANES + HEAD_DIM, (j + 1) * LANES
            acc_ref[lo:mid, :] = alpha[:, :Q_BLOCK] * acc_ref[lo:mid, :] + pv[:HEAD_DIM, :Q_BLOCK]
            acc_ref[mid:hi, :] = alpha[:, Q_BLOCK:] * acc_ref[mid:hi, :] + pv[HEAD_DIM:LANES, Q_BLOCK:]
        return tuple(new_ms), tuple(new_ls)

    masked_logits(0, 0)

    def attn_two_chunks(i, carry):
        ms, ls = carry
        masked_logits(2 * i + 1, 1)
        ms, ls = softmax_pv(2 * i, 0, ms, ls)
        masked_logits(2 * i + 2, 0)
        return softmax_pv(2 * i + 1, 1, ms, ls)

    init = (tuple(jnp.full((1, 2 * Q_BLOCK), NEG_BIG, jnp.float32) for _ in range(n_pairs)),
            tuple(jnp.zeros((1, 2 * Q_BLOCK), jnp.float32) for _ in range(n_pairs)))
    _, ls = lax.fori_loop(0, (n_chunks + 1) // 2, attn_two_chunks, init)
    for j in range(n_pairs):
        lo, mid, hi = j * LANES, j * LANES + HEAD_DIM, (j + 1) * LANES
        even = acc_ref[lo:mid, :] / ls[j][:, :Q_BLOCK]
        odd = acc_ref[mid:hi, :] / ls[j][:, Q_BLOCK:]
        o_ref[:, lo:hi] = jnp.concatenate([even, odd], axis=0).T.astype(o_ref.dtype)


def _dsa(iq, iw, q, iklo, ikhi, k, vt):
    b, s, _ = q.shape
    n_top = min(TOPK_MAX, s // 4)
    idx_bits = max(1, int(np.ceil(np.log2(s))))
    qblk = lambda w: pl.BlockSpec((None, Q_BLOCK, w), lambda bi, qi: (bi, qi, 0))
    keys = lambda w: pl.BlockSpec((None, s, w), lambda bi, qi: (bi, 0, 0))
    return pl.pallas_call(
        functools.partial(_dsa_kernel, n_top=n_top, idx_bits=idx_bits),
        grid=(b, s // Q_BLOCK),
        in_specs=[qblk(IDXQ_W), qblk(LANES), qblk(ATTN_W), keys(LANES), keys(LANES),
                  keys(ATTN_W), pl.BlockSpec((None, VT_ALL, s), lambda bi, qi: (bi, 0, 0))],
        out_specs=qblk(ATTN_W),
        out_shape=jax.ShapeDtypeStruct((b, s, ATTN_W), jnp.bfloat16),
        scratch_shapes=[pltpu.VMEM((s, Q_BLOCK), jnp.int32),
                        pltpu.VMEM((N_HEADS_IDX // 4, LANES, 2 * Q_BLOCK), jnp.bfloat16),
                        pltpu.VMEM((N_HEADS_ATTN // 2, LANES, 2 * Q_BLOCK), jnp.bfloat16),
                        pltpu.VMEM((2, KEY_CHUNK, 2 * Q_BLOCK), jnp.float32),
                        pltpu.VMEM((2, N_HEADS_ATTN // 2, KEY_CHUNK, 2 * Q_BLOCK), jnp.float32),
                        pltpu.VMEM((ATTN_W, Q_BLOCK), jnp.float32)],
        compiler_params=pltpu.CompilerParams(dimension_semantics=("arbitrary", "arbitrary"),
                                             vmem_limit_bytes=VMEM_LIMIT),
        name="dsa",
    )(iq, iw, q, iklo, ikhi, k, vt)


def _merge_kernel(x_ref, ya_ref, u_ref, uprev_ref, qm_ref, km_ref, vm_ref,
                  gmix_ref, wg_ref, bg_ref, wpa_ref, wpp_ref, wpm_ref, wbd_ref, psc_ref,
                  wo_ref, gffn_ref, wrh_ref, wrl_ref, br_ref,
                  h_ref, xn2_ref, eg_ref, ei_ref, cnt_ref, carry_ref, *, tiles_per_seq):
    i = pl.program_id(0)
    t = x_ref.shape[0]
    d = x_ref.shape[1]
    bf = jnp.bfloat16
    x = x_ref[...]
    xn = _rmsnorm(x, gmix_ref[...]).astype(bf)

    tile_in_seq = i % tiles_per_seq
    u = u_ref[...]
    halo = jnp.where(tile_in_seq == 0, 0.0, uprev_ref[...])
    ext = jnp.concatenate([halo, u], axis=0)
    lane_p = lax.broadcasted_iota(jnp.int32, (t, POOL_W), 1)
    pos1 = (tile_in_seq * t + lax.broadcasted_iota(jnp.int32, (t, 1), 0) + 1).astype(jnp.float32)
    pooled = None
    run, width = ext, 1
    for g, w in enumerate(POOL_WINDOWS):
        while width < w:
            run = run[width:] + run[:-width]
            width *= 2
        start = POOL_HALO + 1 - w
        mean = run[start:start + t] / jnp.minimum(pos1, float(w))
        pooled = mean if pooled is None else jnp.where(lane_p >= g * POOL_GROUP_DIM, mean, pooled)
    pooled = pooled - u
    mixed = _dot(pooled.astype(bf), wbd_ref[...]) * psc_ref[...]
    y_pool = _dot(mixed.astype(bf), wpp_ref[...])

    qm = qm_ref[...]
    probs = []
    for h in range(N_HEADS_MEM):
        s = _dot_nt(qm, km_ref[h])
        p = jnp.exp(s - jnp.max(s, axis=1, keepdims=True))
        probs.append((p / jnp.sum(p, axis=1, keepdims=True)).astype(bf))
    y_mem = _dot(probs[0], vm_ref[0])
    for h in range(1, N_HEADS_MEM):
        y_mem = y_mem + _dot(probs[h], vm_ref[h])
    y_mem = _dot(y_mem.astype(bf), wpm_ref[...])

    y_attn = _dot(ya_ref[...], wpa_ref[...])

    def gate(br):
        z = _dot(xn, wg_ref[:, br * d:(br + 1) * d]) + bg_ref[:, br * d:(br + 1) * d]
        return 1.0 / (1.0 + jnp.exp(-z))

    merged = gate(0) * y_attn + gate(1) * y_pool + gate(2) * y_mem
    h = x + _dot(merged.astype(bf), wo_ref[...])
    h_ref[...] = h
    xn2 = _rmsnorm(h, gffn_ref[...])
    _store_row_tiles(xn2_ref, xn2)

    x_hi = xn2.astype(bf)
    x_lo = (xn2 - x_hi.astype(jnp.float32)).astype(bf)
    logits = (_dot(x_hi, wrh_ref[...]) + (_dot(x_lo, wrh_ref[...]) + _dot(x_hi, wrl_ref[...]))
              + br_ref[...])
    lane_e = lax.broadcasted_iota(jnp.int32, logits.shape, 1).astype(jnp.float32)
    work = logits
    vals, onehots, ids = [], [], []
    for _ in range(TOP_K):
        mx = jnp.max(work, axis=1, keepdims=True)
        idx = jnp.min(jnp.where(work == mx, lane_e, float(LANES)), axis=1, keepdims=True)
        oh = lane_e == idx
        vals.append(mx)
        ids.append(idx)
        onehots.append(oh)
        work = jnp.where(oh, -jnp.inf, work)
    exps = [jnp.exp(v - vals[0]) for v in vals]
    denom = exps[0] + exps[1] + exps[2] + exps[3]

    @pl.when(i == 0)
    def _():
        carry_ref[...] = jnp.zeros(carry_ref.shape, jnp.float32)

    member = jnp.where(onehots[0] | onehots[1] | onehots[2] | onehots[3], 1.0, 0.0)
    r_io = lax.broadcasted_iota(jnp.int32, (t, t), 0)
    c_io = lax.broadcasted_iota(jnp.int32, (t, t), 1)
    earlier = jnp.where(c_io < r_io, 1.0, 0.0).astype(bf)
    before = _dot(earlier, member.astype(bf)) + carry_ref[...]
    carry_ref[...] = carry_ref[...] + jnp.sum(member, axis=0, keepdims=True)
    cnt_ref[...] = carry_ref[...]

    lane8 = lax.broadcasted_iota(jnp.int32, (t, 2 * TOP_K), 1)
    eg = jnp.zeros((t, 2 * TOP_K), jnp.float32)
    ei = jnp.zeros((t, 2 * TOP_K), jnp.float32)
    for j in range(TOP_K):
        rank = jnp.sum(jnp.where(onehots[j], before, 0.0), axis=1, keepdims=True)
        eg = jnp.where(lane8 == j, exps[j] / denom, eg)
        ei = jnp.where(lane8 == j, ids[j], ei)
        ei = jnp.where(lane8 == TOP_K + j, rank, ei)
    eg_ref[...] = eg
    ei_ref[...] = ei.astype(jnp.int32)


def _merge(x2, ya, u, qm, km, vm, gmix, wg, bg, wpa, wpp, wpm, wbd, psc, wo, gffn, wrh, wrl, br, seq):
    n, d = x2.shape
    t = MERGE_TILE
    tiles_per_seq = seq // t
    m = km.shape[2]
    row = lambda i: (i, 0)
    full = lambda i: (0, 0)
    per_batch = lambda i: (i // tiles_per_seq, 0, 0, 0)
    halo_blocks = t // POOL_HALO
    prev = lambda i: (jnp.maximum(i * halo_blocks - 1, 0), 0)
    consts = (gmix, wg, bg, wpa, wpp, wpm, wbd, psc, wo, gffn, wrh, wrl, br)
    pieces = d // LANES
    out_shape = (jax.ShapeDtypeStruct((n, d), jnp.float32),
                 jax.ShapeDtypeStruct((n * pieces, LANES), jnp.float32),
                 jax.ShapeDtypeStruct((n, 2 * TOP_K), jnp.float32),
                 jax.ShapeDtypeStruct((n, 2 * TOP_K), jnp.int32),
                 jax.ShapeDtypeStruct((1, LANES), jnp.float32))
    return pl.pallas_call(
        functools.partial(_merge_kernel, tiles_per_seq=tiles_per_seq),
        grid=(n // t,),
        in_specs=[pl.BlockSpec((t, d), row), pl.BlockSpec((t, ATTN_W), row),
                  pl.BlockSpec((t, POOL_W), row), pl.BlockSpec((POOL_HALO, POOL_W), prev),
                  pl.BlockSpec((t, MEM_W), row),
                  pl.BlockSpec((None, N_HEADS_MEM, m, MEM_W), per_batch),
                  pl.BlockSpec((None, N_HEADS_MEM, m, MEM_W), per_batch)]
                 + [pl.BlockSpec(c.shape, full) for c in consts],
        out_specs=(pl.BlockSpec((t, d), row), pl.BlockSpec((t * pieces, LANES), row),
                   pl.BlockSpec((t, 2 * TOP_K), row), pl.BlockSpec((t, 2 * TOP_K), row),
                   pl.BlockSpec((1, LANES), full)),
        out_shape=out_shape,
        scratch_shapes=[pltpu.VMEM((1, LANES), jnp.float32)],
        compiler_params=pltpu.CompilerParams(dimension_semantics=("arbitrary",),
                                             vmem_limit_bytes=VMEM_LIMIT),
        name="merge",
    )(x2, ya, u, u, qm, km, vm, *consts)


def _row_copy(src_ref, src_row, dst_ref, dst_row, sem):
    return pltpu.make_async_copy(_row_tile(src_ref, src_row), _row_tile(dst_ref, dst_row), sem)


def _dispatch_kernel(pstart_ref, cnt_ref, dest_ref, x_ref, xs_ref, zero_ref, sem):
    t = x_ref.shape[0] // SUBLANES
    block_tiles = ROW_BLOCK * SUBLANES

    @pl.when(pl.program_id(0) == 0)
    def _():
        zero_ref[...] = jnp.zeros(zero_ref.shape, zero_ref.dtype)

        def for_each_pad_row(act):
            def per_expert(e, carry):
                n_pad = (ROW_BLOCK - cnt_ref[e] % ROW_BLOCK) % ROW_BLOCK
                first = pstart_ref[e] + cnt_ref[e]

                def per_row(r, c):
                    act(_row_copy(zero_ref, 0, xs_ref, first + r, sem))
                    return c
                return lax.fori_loop(0, n_pad, per_row, carry)
            lax.fori_loop(0, N_EXPERTS, per_expert, 0)

        def for_each_unused_block(act):
            n_blocks = xs_ref.shape[0] // block_tiles
            first = (pstart_ref[N_EXPERTS - 1] + cnt_ref[N_EXPERTS - 1] + ROW_BLOCK - 1) // ROW_BLOCK

            def per_block(b, c):
                rows = pl.ds(pl.multiple_of(b * block_tiles, block_tiles), block_tiles)
                act(pltpu.make_async_copy(zero_ref, xs_ref.at[rows], sem))
                return c
            lax.fori_loop(first, n_blocks, per_block, 0)

        for act in (lambda cp: cp.start(), lambda cp: cp.wait()):
            for_each_pad_row(act)
            for_each_unused_block(act)

    def issue(tok, carry):
        for j in range(TOP_K):
            _row_copy(x_ref, tok, xs_ref, dest_ref[tok * TOP_K + j], sem).start()
        return carry

    lax.fori_loop(0, t, issue, 0)

    def drain(tok, carry):
        for j in range(TOP_K):
            _row_copy(x_ref, 0, xs_ref, 0, sem).wait()
        return carry

    lax.fori_loop(0, t, drain, 0)


def _dispatch(pstart, cnt, dest, xn2, n_rows):
    tiles, lanes = xn2.shape
    n = tiles // SUBLANES
    t = MOVE_TILE
    return pl.pallas_call(
        _dispatch_kernel,
        grid_spec=pltpu.PrefetchScalarGridSpec(
            num_scalar_prefetch=2,
            grid=(n // t,),
            in_specs=[pl.BlockSpec((t * TOP_K,), lambda i, ps, ct: (i,), memory_space=pltpu.SMEM),
                      pl.BlockSpec((t * SUBLANES, lanes), lambda i, ps, ct: (i, 0))],
            out_specs=pl.BlockSpec(memory_space=pl.ANY),
            scratch_shapes=[pltpu.VMEM((ROW_BLOCK * SUBLANES, lanes), jnp.float32),
                            pltpu.SemaphoreType.DMA(())]),
        out_shape=jax.ShapeDtypeStruct((n_rows * SUBLANES, lanes), jnp.float32),
        compiler_params=pltpu.CompilerParams(dimension_semantics=("arbitrary",),
                                             vmem_limit_bytes=VMEM_LIMIT),
        name="dispatch",
    )(pstart, cnt, dest, xn2)


def _experts_kernel(be_ref, nused_ref, xs_ref, wu_ref, bu_ref, wd_ref, bd_ref, ys_ref,
                    wu16_ref, wd16_ref):
    i = pl.program_id(0)
    d, f = wu_ref.shape[0], wd_ref.shape[0]

    @pl.when((i == 0) | (be_ref[i] != be_ref[jnp.maximum(i - 1, 0)]))
    def _():
        wu16_ref[...] = wu_ref[...].astype(wu16_ref.dtype)
        wd16_ref[...] = wd_ref[...].astype(wd16_ref.dtype)

    @pl.when(i < nused_ref[0])
    def _():
        x = _load_row_tiles(xs_ref, ROW_BLOCK, d)
        hb = _dot(x.astype(jnp.bfloat16), wu16_ref[...]) + bu_ref[...]
        glu = jnp.minimum(hb[:, :f], SWIGLU_LIMIT)
        lin = jnp.clip(hb[:, f:], -SWIGLU_LIMIT, SWIGLU_LIMIT)
        act = glu * (1.0 / (1.0 + jnp.exp(-SWIGLU_ALPHA * glu))) * (lin + 1.0)
        _store_row_tiles(ys_ref, _dot(act.astype(jnp.bfloat16), wd16_ref[...]) + bd_ref[...])

    @pl.when(i >= nused_ref[0])
    def _():
        ys_ref[...] = jnp.zeros(ys_ref.shape, ys_ref.dtype)


def _experts(block_e, nused, xs, wu, bu, wd, bd):
    tiles, lanes = xs.shape
    e, d, f2 = wu.shape
    f = wd.shape[1]
    block = (ROW_BLOCK * SUBLANES, lanes)
    rows = lambda i, be, nu: (i, 0)
    used_rows = lambda i, be, nu: (jnp.minimum(i, nu[0] - 1), 0)
    per_e = lambda i, be, nu: (be[i], 0, 0)
    return pl.pallas_call(
        _experts_kernel,
        grid_spec=pltpu.PrefetchScalarGridSpec(
            num_scalar_prefetch=2,
            grid=(tiles // block[0],),
            in_specs=[pl.BlockSpec(block, used_rows),
                      pl.BlockSpec((None, d, f2), per_e), pl.BlockSpec((None, 1, f2), per_e),
                      pl.BlockSpec((None, f, d), per_e), pl.BlockSpec((None, 1, d), per_e)],
            out_specs=pl.BlockSpec(block, rows),
            scratch_shapes=[pltpu.VMEM((d, f2), jnp.bfloat16), pltpu.VMEM((f, d), jnp.bfloat16)]),
        out_shape=jax.ShapeDtypeStruct(xs.shape, jnp.float32),
        compiler_params=pltpu.CompilerParams(dimension_semantics=("arbitrary",),
                                             vmem_limit_bytes=VMEM_LIMIT),
        name="experts",
    )(block_e, nused, xs, wu, bu, wd, bd)


def _combine_kernel(dest_ref, h_ref, eg_ref, g_ref, ys_ref, o_ref, buf_ref, sem, *, final_norm):
    t, d = h_ref.shape

    def issue(tok, carry):
        for j in range(TOP_K):
            _row_copy(ys_ref, dest_ref[tok * TOP_K + j], buf_ref.at[j], tok, sem).start()
        return carry

    lax.fori_loop(0, t, issue, 0)

    def drain(tok, carry):
        for j in range(TOP_K):
            _row_copy(ys_ref, 0, buf_ref.at[j], 0, sem).wait()
        return carry

    lax.fori_loop(0, t, drain, 0)
    eg = eg_ref[...]
    h = h_ref[...]
    for j in range(TOP_K):
        h = h + eg[:, j:j + 1] * _load_row_tiles(buf_ref.at[j], t, d)
    o_ref[...] = _rmsnorm(h, g_ref[...]) if final_norm else h


def _combine(dest, h, eg, g, ys, final_norm):
    n, d = h.shape
    t = MOVE_TILE
    return pl.pallas_call(
        functools.partial(_combine_kernel, final_norm=final_norm),
        grid=(n // t,),
        in_specs=[pl.BlockSpec((t * TOP_K,), lambda i: (i,), memory_space=pltpu.SMEM),
                  pl.BlockSpec((t, d), lambda i: (i, 0)),
                  pl.BlockSpec((t, 2 * TOP_K), lambda i: (i, 0)),
                  pl.BlockSpec((1, d), lambda i: (0, 0)),
                  pl.BlockSpec(memory_space=pl.ANY)],
        out_specs=pl.BlockSpec((t, d), lambda i: (i, 0)),
        scratch_shapes=[pltpu.VMEM((TOP_K, t * SUBLANES, ys.shape[1]), jnp.float32),
                        pltpu.SemaphoreType.DMA(())],
        out_shape=jax.ShapeDtypeStruct((n, d), jnp.float32),
        compiler_params=pltpu.CompilerParams(dimension_semantics=("arbitrary",),
                                             vmem_limit_bytes=VMEM_LIMIT),
        name="combine",
    )(dest, h, eg, g, ys)


def _rope_tables(seq):
    half = ROPE_DIM // 2
    inv = jnp.power(jnp.float32(ROPE_THETA), -jnp.arange(half, dtype=jnp.float32) / half)
    ang = jnp.arange(seq, dtype=jnp.float32)[:, None] * inv[None, :]
    cos, sin = jnp.cos(ang), jnp.sin(ang)
    pad = HEAD_DIM - ROPE_DIM
    one, zero = jnp.ones((seq, pad), jnp.float32), jnp.zeros((seq, pad), jnp.float32)
    zh = jnp.zeros((seq, half), jnp.float32)
    reps = LANES // HEAD_DIM
    c = jnp.tile(jnp.concatenate([cos, cos, one], axis=1), (1, reps))
    a = jnp.tile(jnp.concatenate([-sin, zh, zero], axis=1), (1, reps))
    b = jnp.tile(jnp.concatenate([zh, sin, zero], axis=1), (1, reps))
    return c, a, b


def _layer(h, mem, p, tables, final_g):
    b, s, d = h.shape
    n = b * s
    bf = jnp.bfloat16
    x2 = h.reshape(n, d)
    w_in = p["w_in"]
    o_ik = 3 * ATTN_W + IDXQ_W
    o_u = o_ik + IDX_DIM + N_HEADS_IDX
    wa = w_in[:, :o_ik].astype(bf)
    wb = jnp.pad(w_in[:, o_ik:o_u], ((0, 0), (0, LANES - (o_u - o_ik)))).astype(bf)
    wc = w_in[:, o_u:].astype(bf)
    q, k, vt, iq, iklo, ikhi, iw, u, qm = _in_proj(x2, p["norm_mix_g"][None], wa, wb, wc, *tables, b, s)
    km, vm = _mem_kv(mem, p["mem_norm_g"][None], p["w_mem_kv"].astype(bf))
    b3 = lambda a: a.reshape(b, s, a.shape[-1])
    ya = _dsa(b3(iq), b3(iw), b3(q), b3(iklo), b3(ikhi), b3(k), vt).reshape(n, ATTN_W)

    wbd = jnp.zeros((POOL_W, POOL_W), jnp.float32)
    for g in range(N_POOL_GROUPS):
        lo = g * POOL_GROUP_DIM
        wbd = wbd.at[lo:lo + POOL_GROUP_DIM, lo:lo + POOL_GROUP_DIM].set(p["pool_w_group"][g])
    wr = jnp.pad(p["w_router"], ((0, 0), (0, LANES - N_EXPERTS)))
    wr_hi = wr.astype(bf)
    br = jnp.pad(p["b_router"], (0, LANES - N_EXPERTS), constant_values=NEG_BIG)[None]
    hmid, xn2, eg, ei, counts = _merge(
        x2, ya, u, qm, km, vm, p["norm_mix_g"][None], p["w_gate"].astype(bf), p["b_gate"][None],
        p["w_proj_attn"].astype(bf), p["w_proj_pool"].astype(bf), p["w_proj_mem"].astype(bf),
        wbd.astype(bf), p["pool_scale"][None], p["w_out"].astype(bf), p["norm_ffn_g"][None],
        wr_hi, (wr - wr_hi.astype(jnp.float32)).astype(bf), br, s)

    cnt = counts[0, :N_EXPERTS].astype(jnp.int32)
    padded = ((cnt + ROW_BLOCK - 1) // ROW_BLOCK) * ROW_BLOCK
    pend = jnp.cumsum(padded)
    pstart = (pend - padded).astype(jnp.int32)
    n_rows = n * TOP_K + N_EXPERTS * ROW_BLOCK
    n_blocks = n_rows // ROW_BLOCK
    nused = (pend[-1] // ROW_BLOCK).astype(jnp.int32)
    blk = jnp.minimum(jnp.arange(n_blocks, dtype=jnp.int32), nused - 1) * ROW_BLOCK
    block_e = jnp.sum(blk[:, None] >= pend[None, :], axis=1).astype(jnp.int32)
    block_e = jnp.minimum(block_e, N_EXPERTS - 1)
    dest = (pstart[ei[:, :TOP_K]] + ei[:, TOP_K:]).reshape(-1)

    xs = _dispatch(pstart, cnt, dest, xn2, n_rows)
    ys = _experts(block_e, nused[None], xs, p["w_up"], p["b_up"][:, None, :],
                  p["w_down"], p["b_down"][:, None, :])
    g = p["norm_ffn_g"][None] if final_g is None else final_g[None]
    return _combine(dest, hmid, eg, g, ys, final_g is not None).reshape(b, s, d)


def kernel(x, mem, norm_mix_g, w_in, w_gate, b_gate, w_proj_attn, w_proj_pool, w_proj_mem,
           pool_w_group, pool_scale, mem_norm_g, w_mem_kv, w_out, norm_ffn_g,
           w_router, b_router, w_up, b_up, w_down, b_down, norm_final_g):
    b, s, d = x.shape
    assert s % IN_TILE == 0 and s % COUNT_CHUNK == 0 and (b * s) % MOVE_TILE == 0
    assert d == LANES * SUBLANES, "the row-tile layout maps one model row onto one (8, 128) tile"
    stacked = dict(norm_mix_g=norm_mix_g, w_in=w_in, w_gate=w_gate, b_gate=b_gate,
                   w_proj_attn=w_proj_attn, w_proj_pool=w_proj_pool, w_proj_mem=w_proj_mem,
                   pool_w_group=pool_w_group, pool_scale=pool_scale, mem_norm_g=mem_norm_g,
                   w_mem_kv=w_mem_kv, w_out=w_out, norm_ffn_g=norm_ffn_g, w_router=w_router,
                   b_router=b_router, w_up=w_up, b_up=b_up, w_down=w_down, b_down=b_down)
    depth = w_in.shape[0]
    tables = _rope_tables(s)
    h = x
    for l in range(depth):
        p = {name: val[l] for name, val in stacked.items()}
        h = _layer(h, mem, p, tables, norm_final_g if l == depth - 1 else None)
    return h
```

```python
import functools

import jax
import jax.numpy as jnp
import numpy as np
from jax import lax
from jax.experimental import pallas as pl
from jax.experimental.pallas import tpu as pltpu

N_HEADS_ATTN = 8
HEAD_DIM = 64
ROPE_DIM = HEAD_DIM // 4
ROPE_THETA = 500000.0
N_HEADS_IDX = 8
IDX_DIM = 64
TOPK_MAX = 256
N_POOL_GROUPS = 4
POOL_GROUP_DIM = 64
POOL_WINDOWS = (2, 4, 8, 16)
POOL_HALO = 16
N_HEADS_MEM = 4
N_BRANCHES = 3
N_EXPERTS = 32
TOP_K = 4
SWIGLU_ALPHA = 1.702
SWIGLU_LIMIT = 7.0
EPS = 1e-6

ATTN_W = N_HEADS_ATTN * HEAD_DIM
POOL_W = N_POOL_GROUPS * POOL_GROUP_DIM
MEM_W = N_HEADS_MEM * HEAD_DIM
IDXQ_W = N_HEADS_IDX * IDX_DIM

LANES = 128
SUBLANES = 8
VMEM_LIMIT = 56 * 1024 * 1024

IN_TILE = 512
Q_BLOCK = 128
KEY_CHUNK = 256
COUNT_CHUNK = 512
COUNT_ROWS = 32
MERGE_TILE = 256
ROW_BLOCK = 256
MOVE_TILE = 256
ISSUE_UNROLL = 4

VT_ROWS = LANES + 16
VT_ALL = (N_HEADS_ATTN // 2) * VT_ROWS
LOG2_E = 1.4426950408889634

NEG_BIG = -1e30
INT_MIN = -2 ** 31


def _dot(a, b):
    return jnp.dot(a, b, preferred_element_type=jnp.float32)


def _dot_nt(a, b):
    return lax.dot_general(a, b, (((1,), (1,)), ((), ())), preferred_element_type=jnp.float32)


def _rmsnorm(x, g):
    return x * lax.rsqrt(jnp.mean(x * x, axis=-1, keepdims=True) + EPS) * g


def _store_row_tiles(ref, x):
    rows, width = x.shape
    for j in range(width // LANES):
        ref[pl.ds(j, rows, stride=width // LANES), :] = x[:, j * LANES:(j + 1) * LANES]


def _load_row_tiles(ref, rows, width):
    pieces = width // LANES
    return jnp.concatenate([ref[pl.ds(j, rows, stride=pieces), :] for j in range(pieces)], axis=1)


def _row_tile(ref, row):
    return ref.at[pl.ds(pl.multiple_of(row * SUBLANES, SUBLANES), SUBLANES)]


def _rope128(x, c, a, b):
    return x * c + pltpu.roll(x, LANES - ROPE_DIM // 2, 1) * a + pltpu.roll(x, ROPE_DIM // 2, 1) * b


def _in_proj_kernel(x_ref, g_ref, wa_ref, wb_ref, wc_ref, rc_ref, ra_ref, rb_ref,
                    q_ref, k_ref, vt_ref, iq_ref, iklo_ref, ikhi_ref, iw_ref, u_ref, qm_ref):
    xn = _rmsnorm(x_ref[...], g_ref[...]).astype(jnp.bfloat16)
    rc, ra, rb = rc_ref[...], ra_ref[...], rb_ref[...]
    pa = _dot(xn, wa_ref[...])
    for seg, (ref, scale) in enumerate(((q_ref, LOG2_E * HEAD_DIM ** -0.5), (k_ref, None),
                                        (None, None), (iq_ref, IDX_DIM ** -0.5))):
        for c in range(ATTN_W // LANES):
            lo = seg * ATTN_W + c * LANES
            blk = pa[:, lo:lo + LANES]
            if ref is None:
                vt_ref[c * VT_ROWS:c * VT_ROWS + LANES, :] = blk.T.astype(vt_ref.dtype)
                vt_ref[c * VT_ROWS + LANES:(c + 1) * VT_ROWS, :] = jnp.ones(
                    (VT_ROWS - LANES, blk.shape[0]), vt_ref.dtype)
                continue
            blk = _rope128(blk, rc, ra, rb)
            if scale is not None:
                blk = blk * scale
            ref[:, c * LANES:(c + 1) * LANES] = blk.astype(ref.dtype)
    pb = _dot(xn, wb_ref[...])
    lane = lax.broadcasted_iota(jnp.int32, pb.shape, 1)
    ik = jnp.where(lane < IDX_DIM, _rope128(pb, rc, ra, rb), 0.0)
    iklo_ref[...] = ik.astype(iklo_ref.dtype)
    ikhi_ref[...] = pltpu.roll(ik, IDX_DIM, 1).astype(ikhi_ref.dtype)
    iw_ref[...] = pb * (N_HEADS_IDX ** -0.5)
    pc = _dot(xn, wc_ref[...])
    u_ref[...] = pc[:, :POOL_W]
    qm_ref[...] = (pc[:, POOL_W:] * (HEAD_DIM ** -0.5)).astype(qm_ref.dtype)


def _in_proj(x2, g, wa, wb, wc, rc, ra, rb, batch, seq):
    n, d = x2.shape
    t = IN_TILE
    tiles_per_seq = seq // t
    row = lambda i: (i, 0)
    full = lambda i: (0, 0)
    pos = lambda i: (i % tiles_per_seq, 0)
    bf = jnp.bfloat16
    flat = lambda w, dt: (jax.ShapeDtypeStruct((n, w), dt), pl.BlockSpec((t, w), row))
    outs = (flat(ATTN_W, bf), flat(ATTN_W, bf),
            (jax.ShapeDtypeStruct((batch, VT_ALL, seq), bf),
             pl.BlockSpec((None, VT_ALL, t), lambda i: (i // tiles_per_seq, 0, i % tiles_per_seq))),
            flat(IDXQ_W, bf), flat(LANES, bf), flat(LANES, bf), flat(LANES, jnp.float32),
            flat(POOL_W, jnp.float32), flat(MEM_W, bf))
    return pl.pallas_call(
        _in_proj_kernel,
        grid=(n // t,),
        in_specs=[pl.BlockSpec((t, d), row), pl.BlockSpec((1, d), full),
                  pl.BlockSpec(wa.shape, full), pl.BlockSpec(wb.shape, full),
                  pl.BlockSpec(wc.shape, full),
                  pl.BlockSpec((t, LANES), pos), pl.BlockSpec((t, LANES), pos),
                  pl.BlockSpec((t, LANES), pos)],
        out_specs=tuple(o[1] for o in outs),
        out_shape=tuple(o[0] for o in outs),
        compiler_params=pltpu.CompilerParams(dimension_semantics=("arbitrary",),
                                             vmem_limit_bytes=VMEM_LIMIT),
        name="in_proj",
    )(x2, g, wa, wb, wc, rc, ra, rb)


def _mem_kv_kernel(mem_ref, g_ref, w_ref, km_ref, vm_ref):
    mn = _rmsnorm(mem_ref[...], g_ref[...]).astype(jnp.bfloat16)
    kv = _dot(mn, w_ref[...])
    km, vm = kv[:, :MEM_W], kv[:, MEM_W:]
    lane = lax.broadcasted_iota(jnp.int32, km.shape, 1)
    for h in range(N_HEADS_MEM):
        in_head = (lane >= h * HEAD_DIM) & (lane < (h + 1) * HEAD_DIM)
        km_ref[h] = jnp.where(in_head, km, 0.0).astype(km_ref.dtype)
        vm_ref[h] = jnp.where(in_head, vm, 0.0).astype(vm_ref.dtype)


def _mem_kv(mem, g, w):
    b, m, d = mem.shape
    out = jax.ShapeDtypeStruct((b, N_HEADS_MEM, m, MEM_W), jnp.bfloat16)
    return pl.pallas_call(
        _mem_kv_kernel,
        grid=(b,),
        in_specs=[pl.BlockSpec((None, m, d), lambda i: (i, 0, 0)),
                  pl.BlockSpec((1, d), lambda i: (0, 0)),
                  pl.BlockSpec(w.shape, lambda i: (0, 0))],
        out_specs=(pl.BlockSpec((None, N_HEADS_MEM, m, MEM_W), lambda i: (i, 0, 0, 0)),
                   pl.BlockSpec((None, N_HEADS_MEM, m, MEM_W), lambda i: (i, 0, 0, 0))),
        out_shape=(out, out),
        compiler_params=pltpu.CompilerParams(dimension_semantics=("arbitrary",),
                                             vmem_limit_bytes=VMEM_LIMIT),
        name="mem_kv",
    )(mem, g, w)


def _dsa_kernel(iq_ref, iw_ref, q_ref, iklo_ref, ikhi_ref, k_ref, vt_ref, o_ref,
                key_ref, iqt_ref, qt_ref, bias_ref, s_ref, acc_ref, *, n_top, idx_bits):
    qb = pl.program_id(1)
    n_chunks = (qb * Q_BLOCK + Q_BLOCK + KEY_CHUNK - 1) // KEY_CHUNK
    n_count_chunks = (qb * Q_BLOCK + Q_BLOCK + COUNT_CHUNK - 1) // COUNT_CHUNK
    q_pos = qb * Q_BLOCK + lax.broadcasted_iota(jnp.int32, (1, Q_BLOCK), 1)
    row_k = lax.broadcasted_iota(jnp.int32, (KEY_CHUNK, 1), 0)
    row_c = lax.broadcasted_iota(jnp.int32, (COUNT_CHUNK, 1), 0)
    row_d = lax.broadcasted_iota(jnp.int32, (LANES, Q_BLOCK), 0)
    bf = jnp.bfloat16

    def chunk_off(c):
        return pl.multiple_of(c * KEY_CHUNK, KEY_CHUNK)

    wt = iw_ref[...].T
    for j in range(N_HEADS_IDX // 2):
        iqt_ref[j // 2, :, (j % 2) * Q_BLOCK:(j % 2 + 1) * Q_BLOCK] = (
            iq_ref[:, j * LANES:(j + 1) * LANES].astype(jnp.float32).T.astype(bf))
    for j in range(N_HEADS_ATTN // 2):
        pair_t = q_ref[:, j * LANES:(j + 1) * LANES].astype(jnp.float32).T
        qt_ref[j, :, :Q_BLOCK] = jnp.where(row_d < HEAD_DIM, pair_t, 0.0).astype(bf)
        qt_ref[j, :, Q_BLOCK:] = jnp.where(row_d >= HEAD_DIM, pair_t, 0.0).astype(bf)

    def score_chunk(c, carry):
        off = pl.multiple_of(c * COUNT_CHUNK, COUNT_CHUNK)
        acc = jnp.zeros((COUNT_CHUNK, Q_BLOCK), jnp.float32)
        for parity, ik_ref in enumerate((iklo_ref, ikhi_ref)):
            ik = ik_ref[pl.ds(off, COUNT_CHUNK), :]
            for g in range(N_HEADS_IDX // 4):
                dots = _dot(ik, iqt_ref[g])
                for side in range(2):
                    head = 2 * (2 * g + side) + parity
                    w = wt[IDX_DIM + head:IDX_DIM + head + 1, :]
                    acc = acc + w * jnp.maximum(dots[:, side * Q_BLOCK:(side + 1) * Q_BLOCK], 0.0)
        acc = jnp.where(off + row_c <= q_pos, acc, -jnp.inf)
        bits = lax.bitcast_convert_type(acc, jnp.int32)
        key_ref[pl.ds(off, COUNT_CHUNK), :] = bits ^ ((bits >> 31) & 0x7FFFFFFF)
        return carry

    lax.fori_loop(0, n_count_chunks, score_chunk, 0)

    def count(pred):
        def body(c, cnt):
            off = pl.multiple_of(c * COUNT_CHUNK, COUNT_CHUNK)
            hit = jnp.where(pred(key_ref[pl.ds(off, COUNT_CHUNK), :], off + row_c), 1.0, 0.0)
            return cnt + jnp.sum(hit.reshape(COUNT_CHUNK // COUNT_ROWS, COUNT_ROWS, Q_BLOCK), axis=0)
        cnt = lax.fori_loop(0, n_count_chunks, body, jnp.zeros((COUNT_ROWS, Q_BLOCK), jnp.float32))
        return jnp.sum(cnt, axis=0, keepdims=True)

    k_f = jnp.float32(n_top)
    zero = jnp.zeros((1, Q_BLOCK), jnp.int32)
    thr = jnp.where(count(lambda kk, pos: kk >= zero) >= k_f, zero, INT_MIN)

    def thr_bit(i, cur):
        cand = cur + jnp.left_shift(jnp.int32(1), 30 - i)
        return jnp.where(count(lambda kk, pos: kk >= cand) >= k_f, cand, cur)

    thr = lax.fori_loop(0, 31, thr_bit, thr)
    need = k_f - count(lambda kk, pos: kk > thr)
    n_ties = count(lambda kk, pos: kk == thr)

    def tie_search():
        def tie_bit(i, cur):
            cand = cur + jnp.left_shift(jnp.int32(1), idx_bits - 1 - i)
            below = count(lambda kk, pos: (kk == thr) & (pos < cand))
            return jnp.where(below < need, cand, cur)
        return lax.fori_loop(0, idx_bits, tie_bit, zero)

    has_surplus_ties = jnp.max(jnp.where(n_ties > need, 1.0, 0.0)) > 0.0
    tie_pos = lax.cond(has_surplus_ties, tie_search, lambda: zero + (2 ** idx_bits - 1))

    acc_ref[...] = jnp.zeros(acc_ref.shape, jnp.float32)

    n_pairs = N_HEADS_ATTN // 2

    def masked_logits(c, slot):
        off = chunk_off(jnp.minimum(c, n_chunks - 1))
        kk = key_ref[pl.ds(off, KEY_CHUNK), :]
        pos = c * KEY_CHUNK + row_k
        sel = ((kk > thr) | ((kk == thr) & (pos <= tie_pos))) & (pos <= q_pos)
        bias = jnp.where(sel, 0.0, NEG_BIG)
        bias_ref[slot] = jnp.concatenate([bias, bias], axis=1)
        for j in range(n_pairs):
            kp = k_ref[pl.ds(off, KEY_CHUNK), j * LANES:(j + 1) * LANES]
            s_ref[slot, j] = _dot(kp, qt_ref[j]) + bias_ref[slot]

    def softmax_pv(c, slot, ms, ls):
        off = chunk_off(jnp.minimum(c, n_chunks - 1))
        new_ms, new_ls = [], []
        for j in range(n_pairs):
            s = s_ref[slot, j]
            m_new = jnp.maximum(ms[j], jnp.max(s, axis=0, keepdims=True))
            alpha = jnp.exp2(ms[j] - m_new)
            p = jnp.exp2(s - m_new).astype(bf)
            vt = vt_ref[j * VT_ROWS:(j + 1) * VT_ROWS, pl.ds(off, KEY_CHUNK)]
            pv = _dot(vt, p)
            new_ls.append(alpha * ls[j] + pv[LANES:LANES + 1, :])
            new_ms.append(m_new)
            lo, mid, hi = j * LANES, j * LANES + HEAD_DIM, (j + 1) * LANES
            acc_ref[lo:mid, :] = alpha[:, :Q_BLOCK] * acc_ref[lo:mid, :] + pv[:HEAD_DIM, :Q_BLOCK]
            acc_ref[mid:hi, :] = alpha[:, Q_BLOCK:] * acc_ref[mid:hi, :] + pv[HEAD_DIM:LANES, Q_BLOCK:]
        return tuple(new_ms), tuple(new_ls)

    masked_logits(0, 0)

    def attn_two_chunks(i, carry):
        ms, ls = carry
        masked_logits(2 * i + 1, 1)
        ms, ls = softmax_pv(2 * i, 0, ms, ls)
        masked_logits(2 * i + 2, 0)
        return softmax_pv(2 * i + 1, 1, ms, ls)

    init = (tuple(jnp.full((1, 2 * Q_BLOCK), NEG_BIG, jnp.float32) for _ in range(n_pairs)),
            tuple(jnp.zeros((1, 2 * Q_BLOCK), jnp.float32) for _ in range(n_pairs)))
    _, ls = lax.fori_loop(0, (n_chunks + 1) // 2, attn_two_chunks, init)
    for j in range(n_pairs):
        lo, mid, hi = j * LANES, j * LANES + HEAD_DIM, (j + 1) * LANES
        even = acc_ref[lo:mid, :] / ls[j][:, :Q_BLOCK]
        odd = acc_ref[mid:hi, :] / ls[j][:, Q_BLOCK:]
        o_ref[:, lo:hi] = jnp.concatenate([even, odd], axis=0).T.astype(o_ref.dtype)


def _dsa(iq, iw, q, iklo, ikhi, k, vt):
    b, s, _ = q.shape
    n_top = min(TOPK_MAX, s // 4)
    idx_bits = max(1, int(np.ceil(np.log2(s))))
    qblk = lambda w: pl.BlockSpec((None, Q_BLOCK, w), lambda bi, qi: (bi, qi, 0))
    keys = lambda w: pl.BlockSpec((None, s, w), lambda bi, qi: (bi, 0, 0))
    return pl.pallas_call(
        functools.partial(_dsa_kernel, n_top=n_top, idx_bits=idx_bits),
        grid=(b, s // Q_BLOCK),
        in_specs=[qblk(IDXQ_W), qblk(LANES), qblk(ATTN_W), keys(LANES), keys(LANES),
                  keys(ATTN_W), pl.BlockSpec((None, VT_ALL, s), lambda bi, qi: (bi, 0, 0))],
        out_specs=qblk(ATTN_W),
        out_shape=jax.ShapeDtypeStruct((b, s, ATTN_W), jnp.bfloat16),
        scratch_shapes=[pltpu.VMEM((s, Q_BLOCK), jnp.int32),
                        pltpu.VMEM((N_HEADS_IDX // 4, LANES, 2 * Q_BLOCK), jnp.bfloat16),
                        pltpu.VMEM((N_HEADS_ATTN // 2, LANES, 2 * Q_BLOCK), jnp.bfloat16),
                        pltpu.VMEM((2, KEY_CHUNK, 2 * Q_BLOCK), jnp.float32),
                        pltpu.VMEM((2, N_HEADS_ATTN // 2, KEY_CHUNK, 2 * Q_BLOCK), jnp.float32),
                        pltpu.VMEM((ATTN_W, Q_BLOCK), jnp.float32)],
        compiler_params=pltpu.CompilerParams(dimension_semantics=("arbitrary", "arbitrary"),
                                             vmem_limit_bytes=VMEM_LIMIT),
        name="dsa",
    )(iq, iw, q, iklo, ikhi, k, vt)


def _merge_kernel(x_ref, ya_ref, u_ref, uprev_ref, qm_ref, km_ref, vm_ref,
                  gmix_ref, wg_ref, bg_ref, wpa_ref, wpp_ref, wpm_ref, wbd_ref, psc_ref,
                  wo_ref, gffn_ref, wrh_ref, wrl_ref, br_ref,
                  h_ref, xn2_ref, eg_ref, ei_ref, cnt_ref, carry_ref, *, tiles_per_seq):
    i = pl.program_id(0)
    t = x_ref.shape[0]
    d = x_ref.shape[1]
    bf = jnp.bfloat16
    x = x_ref[...]
    xn = _rmsnorm(x, gmix_ref[...]).astype(bf)

    tile_in_seq = i % tiles_per_seq
    u = u_ref[...]
    halo = jnp.where(tile_in_seq == 0, 0.0, uprev_ref[...])
    ext = jnp.concatenate([halo, u], axis=0)
    lane_p = lax.broadcasted_iota(jnp.int32, (t, POOL_W), 1)
    pos1 = (tile_in_seq * t + lax.broadcasted_iota(jnp.int32, (t, 1), 0) + 1).astype(jnp.float32)
    pooled = None
    run, width = ext, 1
    for g, w in enumerate(POOL_WINDOWS):
        while width < w:
            run = run[width:] + run[:-width]
            width *= 2
        start = POOL_HALO + 1 - w
        mean = run[start:start + t] / jnp.minimum(pos1, float(w))
        pooled = mean if pooled is None else jnp.where(lane_p >= g * POOL_GROUP_DIM, mean, pooled)
    pooled = pooled - u
    mixed = _dot(pooled.astype(bf), wbd_ref[...]) * psc_ref[...]
    y_pool = _dot(mixed.astype(bf), wpp_ref[...])

    qm = qm_ref[...]
    probs = []
    for h in range(N_HEADS_MEM):
        s = _dot_nt(qm, km_ref[h])
        p = jnp.exp(s - jnp.max(s, axis=1, keepdims=True))
        probs.append((p / jnp.sum(p, axis=1, keepdims=True)).astype(bf))
    y_mem = _dot(probs[0], vm_ref[0])
    for h in range(1, N_HEADS_MEM):
        y_mem = y_mem + _dot(probs[h], vm_ref[h])
    y_mem = _dot(y_mem.astype(bf), wpm_ref[...])

    y_attn = _dot(ya_ref[...], wpa_ref[...])

    def gate(br):
        z = _dot(xn, wg_ref[:, br * d:(br + 1) * d]) + bg_ref[:, br * d:(br + 1) * d]
        return 1.0 / (1.0 + jnp.exp(-z))

    merged = gate(0) * y_attn + gate(1) * y_pool + gate(2) * y_mem
    h = x + _dot(merged.astype(bf), wo_ref[...])
    h_ref[...] = h
    xn2 = _rmsnorm(h, gffn_ref[...])
    _store_row_tiles(xn2_ref, xn2)

    x_hi = xn2.astype(bf)
    x_lo = (xn2 - x_hi.astype(jnp.float32)).astype(bf)
    logits = (_dot(x_hi, wrh_ref[...]) + (_dot(x_lo, wrh_ref[...]) + _dot(x_hi, wrl_ref[...]))
              + br_ref[...])
    lane_e = lax.broadcasted_iota(jnp.int32, logits.shape, 1).astype(jnp.float32)
    work = logits
    vals, onehots, ids = [], [], []
    for _ in range(TOP_K):
        mx = jnp.max(work, axis=1, keepdims=True)
        idx = jnp.min(jnp.where(work == mx, lane_e, float(LANES)), axis=1, keepdims=True)
        oh = lane_e == idx
        vals.append(mx)
        ids.append(idx)
        onehots.append(oh)
        work = jnp.where(oh, -jnp.inf, work)
    exps = [jnp.exp(v - vals[0]) for v in vals]
    denom = exps[0] + exps[1] + exps[2] + exps[3]

    @pl.when(i == 0)
    def _():
        carry_ref[...] = jnp.zeros(carry_ref.shape, jnp.float32)

    member = jnp.where(onehots[0] | onehots[1] | onehots[2] | onehots[3], 1.0, 0.0)
    r_io = lax.broadcasted_iota(jnp.int32, (t, t), 0)
    c_io = lax.broadcasted_iota(jnp.int32, (t, t), 1)
    earlier = jnp.where(c_io < r_io, 1.0, 0.0).astype(bf)
    before = _dot(earlier, member.astype(bf)) + carry_ref[...]
    carry_ref[...] = carry_ref[...] + jnp.sum(member, axis=0, keepdims=True)
    cnt_ref[...] = carry_ref[...]

    lane8 = lax.broadcasted_iota(jnp.int32, (t, 2 * TOP_K), 1)
    eg = jnp.zeros((t, 2 * TOP_K), jnp.float32)
    ei = jnp.zeros((t, 2 * TOP_K), jnp.float32)
    for j in range(TOP_K):
        rank = jnp.sum(jnp.where(onehots[j], before, 0.0), axis=1, keepdims=True)
        eg = jnp.where(lane8 == j, exps[j] / denom, eg)
        ei = jnp.where(lane8 == j, ids[j], ei)
        ei = jnp.where(lane8 == TOP_K + j, rank, ei)
    eg_ref[...] = eg
    ei_ref[...] = ei.astype(jnp.int32)


def _merge(x2, ya, u, qm, km, vm, gmix, wg, bg, wpa, wpp, wpm, wbd, psc, wo, gffn, wrh, wrl, br, seq):
    n, d = x2.shape
    t = MERGE_TILE
    tiles_per_seq = seq // t
    m = km.shape[2]
    row = lambda i: (i, 0)
    full = lambda i: (0, 0)
    per_batch = lambda i: (i // tiles_per_seq, 0, 0, 0)
    halo_blocks = t // POOL_HALO
    prev = lambda i: (jnp.maximum(i * halo_blocks - 1, 0), 0)
    consts = (gmix, wg, bg, wpa, wpp, wpm, wbd, psc, wo, gffn, wrh, wrl, br)
    pieces = d // LANES
    out_shape = (jax.ShapeDtypeStruct((n, d), jnp.float32),
                 jax.ShapeDtypeStruct((n * pieces, LANES), jnp.float32),
                 jax.ShapeDtypeStruct((n, 2 * TOP_K), jnp.float32),
                 jax.ShapeDtypeStruct((n, 2 * TOP_K), jnp.int32),
                 jax.ShapeDtypeStruct((1, LANES), jnp.float32))
    return pl.pallas_call(
        functools.partial(_merge_kernel, tiles_per_seq=tiles_per_seq),
        grid=(n // t,),
        in_specs=[pl.BlockSpec((t, d), row), pl.BlockSpec((t, ATTN_W), row),
                  pl.BlockSpec((t, POOL_W), row), pl.BlockSpec((POOL_HALO, POOL_W), prev),
                  pl.BlockSpec((t, MEM_W), row),
                  pl.BlockSpec((None, N_HEADS_MEM, m, MEM_W), per_batch),
                  pl.BlockSpec((None, N_HEADS_MEM, m, MEM_W), per_batch)]
                 + [pl.BlockSpec(c.shape, full) for c in consts],
        out_specs=(pl.BlockSpec((t, d), row), pl.BlockSpec((t * pieces, LANES), row),
                   pl.BlockSpec((t, 2 * TOP_K), row), pl.BlockSpec((t, 2 * TOP_K), row),
                   pl.BlockSpec((1, LANES), full)),
        out_shape=out_shape,
        scratch_shapes=[pltpu.VMEM((1, LANES), jnp.float32)],
        compiler_params=pltpu.CompilerParams(dimension_semantics=("arbitrary",),
                                             vmem_limit_bytes=VMEM_LIMIT),
        name="merge",
    )(x2, ya, u, u, qm, km, vm, *consts)


def _row_copy(src_ref, src_row, dst_ref, dst_row, sem):
    return pltpu.make_async_copy(_row_tile(src_ref, src_row), _row_tile(dst_ref, dst_row), sem)


def _dispatch_kernel(pstart_ref, cnt_ref, dest_ref, x_ref, xs_ref, zero_ref, sem):
    t = x_ref.shape[0] // SUBLANES
    block_tiles = ROW_BLOCK * SUBLANES

    @pl.when(pl.program_id(0) == 0)
    def _():
        zero_ref[...] = jnp.zeros(zero_ref.shape, zero_ref.dtype)

        def for_each_pad_row(act):
            def per_expert(e, carry):
                n_pad = (ROW_BLOCK - cnt_ref[e] % ROW_BLOCK) % ROW_BLOCK
                first = pstart_ref[e] + cnt_ref[e]

                def per_row(r, c):
                    act(_row_copy(zero_ref, 0, xs_ref, first + r, sem))
                    return c
                return lax.fori_loop(0, n_pad, per_row, carry)
            lax.fori_loop(0, N_EXPERTS, per_expert, 0)

        def for_each_unused_block(act):
            n_blocks = xs_ref.shape[0] // block_tiles
            first = (pstart_ref[N_EXPERTS - 1] + cnt_ref[N_EXPERTS - 1] + ROW_BLOCK - 1) // ROW_BLOCK

            def per_block(b, c):
                rows = pl.ds(pl.multiple_of(b * block_tiles, block_tiles), block_tiles)
                act(pltpu.make_async_copy(zero_ref, xs_ref.at[rows], sem))
                return c
            lax.fori_loop(first, n_blocks, per_block, 0)

        for act in (lambda cp: cp.start(), lambda cp: cp.wait()):
            for_each_pad_row(act)
            for_each_unused_block(act)

    def issue(tok, carry):
        for j in range(TOP_K):
            _row_copy(x_ref, tok, xs_ref, dest_ref[tok * TOP_K + j], sem).start(priority=j % 2)
        return carry

    lax.fori_loop(0, t, issue, 0, unroll=ISSUE_UNROLL)
    for j in range(TOP_K):
        pltpu.make_async_copy(x_ref, xs_ref.at[pl.ds(0, x_ref.shape[0])], sem).wait()


def _dispatch(pstart, cnt, dest, xn2, n_rows):
    tiles, lanes = xn2.shape
    n = tiles // SUBLANES
    t = MOVE_TILE
    return pl.pallas_call(
        _dispatch_kernel,
        grid_spec=pltpu.PrefetchScalarGridSpec(
            num_scalar_prefetch=2,
            grid=(n // t,),
            in_specs=[pl.BlockSpec((t * TOP_K,), lambda i, ps, ct: (i,), memory_space=pltpu.SMEM),
                      pl.BlockSpec((t * SUBLANES, lanes), lambda i, ps, ct: (i, 0))],
            out_specs=pl.BlockSpec(memory_space=pl.ANY),
            scratch_shapes=[pltpu.VMEM((ROW_BLOCK * SUBLANES, lanes), jnp.float32),
                            pltpu.SemaphoreType.DMA(())]),
        out_shape=jax.ShapeDtypeStruct((n_rows * SUBLANES, lanes), jnp.float32),
        compiler_params=pltpu.CompilerParams(dimension_semantics=("arbitrary",),
                                             vmem_limit_bytes=VMEM_LIMIT),
        name="dispatch",
    )(pstart, cnt, dest, xn2)


def _experts_kernel(be_ref, nused_ref, xs_ref, wu_ref, bu_ref, wd_ref, bd_ref, ys_ref,
                    wu16_ref, wd16_ref):
    i = pl.program_id(0)
    d, f = wu_ref.shape[0], wd_ref.shape[0]

    @pl.when((i == 0) | (be_ref[i] != be_ref[jnp.maximum(i - 1, 0)]))
    def _():
        wu16_ref[...] = wu_ref[...].astype(wu16_ref.dtype)
        wd16_ref[...] = wd_ref[...].astype(wd16_ref.dtype)

    @pl.when(i < nused_ref[0])
    def _():
        x = _load_row_tiles(xs_ref, ROW_BLOCK, d)
        hb = _dot(x.astype(jnp.bfloat16), wu16_ref[...]) + bu_ref[...]
        glu = jnp.minimum(hb[:, :f], SWIGLU_LIMIT)
        lin = jnp.clip(hb[:, f:], -SWIGLU_LIMIT, SWIGLU_LIMIT)
        act = glu * (1.0 / (1.0 + jnp.exp(-SWIGLU_ALPHA * glu))) * (lin + 1.0)
        _store_row_tiles(ys_ref, _dot(act.astype(jnp.bfloat16), wd16_ref[...]) + bd_ref[...])

    @pl.when(i >= nused_ref[0])
    def _():
        ys_ref[...] = jnp.zeros(ys_ref.shape, ys_ref.dtype)


def _experts(block_e, nused, xs, wu, bu, wd, bd):
    tiles, lanes = xs.shape
    e, d, f2 = wu.shape
    f = wd.shape[1]
    block = (ROW_BLOCK * SUBLANES, lanes)
    rows = lambda i, be, nu: (i, 0)
    used_rows = lambda i, be, nu: (jnp.minimum(i, nu[0] - 1), 0)
    per_e = lambda i, be, nu: (be[i], 0, 0)
    return pl.pallas_call(
        _experts_kernel,
        grid_spec=pltpu.PrefetchScalarGridSpec(
            num_scalar_prefetch=2,
            grid=(tiles // block[0],),
            in_specs=[pl.BlockSpec(block, used_rows),
                      pl.BlockSpec((None, d, f2), per_e), pl.BlockSpec((None, 1, f2), per_e),
                      pl.BlockSpec((None, f, d), per_e), pl.BlockSpec((None, 1, d), per_e)],
            out_specs=pl.BlockSpec(block, rows),
            scratch_shapes=[pltpu.VMEM((d, f2), jnp.bfloat16), pltpu.VMEM((f, d), jnp.bfloat16)]),
        out_shape=jax.ShapeDtypeStruct(xs.shape, jnp.float32),
        compiler_params=pltpu.CompilerParams(dimension_semantics=("arbitrary",),
                                             vmem_limit_bytes=VMEM_LIMIT),
        name="experts",
    )(block_e, nused, xs, wu, bu, wd, bd)


def _combine_kernel(dest_ref, dest_next_ref, h_ref, eg_ref, g_ref, ys_ref, o_ref, buf_ref, sems,
                    *, final_norm):
    i = pl.program_id(0)
    t, d = h_ref.shape
    half = i % 2

    def gather_rows(idx_ref, into):
        def issue(tok, carry):
            for j in range(TOP_K):
                _row_copy(ys_ref, idx_ref[tok * TOP_K + j], buf_ref.at[into, j], tok,
                          sems.at[into]).start(priority=j % 2)
            return carry
        lax.fori_loop(0, t, issue, 0, unroll=ISSUE_UNROLL)

    @pl.when(i == 0)
    def _():
        gather_rows(dest_ref, 0)

    @pl.when(i + 1 < pl.num_programs(0))
    def _():
        gather_rows(dest_next_ref, 1 - half)

    for j in range(TOP_K):
        pltpu.make_async_copy(ys_ref.at[pl.ds(0, t * SUBLANES)], buf_ref.at[half, j],
                              sems.at[half]).wait()
    eg = eg_ref[...]
    h = h_ref[...]
    for j in range(TOP_K):
        h = h + eg[:, j:j + 1] * _load_row_tiles(buf_ref.at[half, j], t, d)
    o_ref[...] = _rmsnorm(h, g_ref[...]) if final_norm else h


def _combine(dest, h, eg, g, ys, final_norm):
    n, d = h.shape
    t = MOVE_TILE
    return pl.pallas_call(
        functools.partial(_combine_kernel, final_norm=final_norm),
        grid=(n // t,),
        in_specs=[pl.BlockSpec((t * TOP_K,), lambda i: (i,), memory_space=pltpu.SMEM),
                  pl.BlockSpec((t * TOP_K,), lambda i: (jnp.minimum(i + 1, n // t - 1),),
                               memory_space=pltpu.SMEM),
                  pl.BlockSpec((t, d), lambda i: (i, 0)),
                  pl.BlockSpec((t, 2 * TOP_K), lambda i: (i, 0)),
                  pl.BlockSpec((1, d), lambda i: (0, 0)),
                  pl.BlockSpec(memory_space=pl.ANY)],
        out_specs=pl.BlockSpec((t, d), lambda i: (i, 0)),
        scratch_shapes=[pltpu.VMEM((2, TOP_K, t * SUBLANES, ys.shape[1]), jnp.float32),
                        pltpu.SemaphoreType.DMA((2,))],
        out_shape=jax.ShapeDtypeStruct((n, d), jnp.float32),
        compiler_params=pltpu.CompilerParams(dimension_semantics=("arbitrary",),
                                             vmem_limit_bytes=VMEM_LIMIT),
        name="combine",
    )(dest, dest, h, eg, g, ys)


def _rope_tables(seq):
    half = ROPE_DIM // 2
    inv = jnp.power(jnp.float32(ROPE_THETA), -jnp.arange(half, dtype=jnp.float32) / half)
    ang = jnp.arange(seq, dtype=jnp.float32)[:, None] * inv[None, :]
    cos, sin = jnp.cos(ang), jnp.sin(ang)
    pad = HEAD_DIM - ROPE_DIM
    one, zero = jnp.ones((seq, pad), jnp.float32), jnp.zeros((seq, pad), jnp.float32)
    zh = jnp.zeros((seq, half), jnp.float32)
    reps = LANES // HEAD_DIM
    c = jnp.tile(jnp.concatenate([cos, cos, one], axis=1), (1, reps))
    a = jnp.tile(jnp.concatenate([-sin, zh, zero], axis=1), (1, reps))
    b = jnp.tile(jnp.concatenate([zh, sin, zero], axis=1), (1, reps))
    return c, a, b


def _layer(h, mem, p, tables, final_g):
    b, s, d = h.shape
    n = b * s
    bf = jnp.bfloat16
    x2 = h.reshape(n, d)
    w_in = p["w_in"]
    o_ik = 3 * ATTN_W + IDXQ_W
    o_u = o_ik + IDX_DIM + N_HEADS_IDX
    wa = w_in[:, :o_ik].astype(bf)
    wb = jnp.pad(w_in[:, o_ik:o_u], ((0, 0), (0, LANES - (o_u - o_ik)))).astype(bf)
    wc = w_in[:, o_u:].astype(bf)
    q, k, vt, iq, iklo, ikhi, iw, u, qm = _in_proj(x2, p["norm_mix_g"][None], wa, wb, wc, *tables, b, s)
    km, vm = _mem_kv(mem, p["mem_norm_g"][None], p["w_mem_kv"].astype(bf))
    b3 = lambda a: a.reshape(b, s, a.shape[-1])
    ya = _dsa(b3(iq), b3(iw), b3(q), b3(iklo), b3(ikhi), b3(k), vt).reshape(n, ATTN_W)

    wbd = jnp.zeros((POOL_W, POOL_W), jnp.float32)
    for g in range(N_POOL_GROUPS):
        lo = g * POOL_GROUP_DIM
        wbd = wbd.at[lo:lo + POOL_GROUP_DIM, lo:lo + POOL_GROUP_DIM].set(p["pool_w_group"][g])
    wr = jnp.pad(p["w_router"], ((0, 0), (0, LANES - N_EXPERTS)))
    wr_hi = wr.astype(bf)
    br = jnp.pad(p["b_router"], (0, LANES - N_EXPERTS), constant_values=NEG_BIG)[None]
    hmid, xn2, eg, ei, counts = _merge(
        x2, ya, u, qm, km, vm, p["norm_mix_g"][None], p["w_gate"].astype(bf), p["b_gate"][None],
        p["w_proj_attn"].astype(bf), p["w_proj_pool"].astype(bf), p["w_proj_mem"].astype(bf),
        wbd.astype(bf), p["pool_scale"][None], p["w_out"].astype(bf), p["norm_ffn_g"][None],
        wr_hi, (wr - wr_hi.astype(jnp.float32)).astype(bf), br, s)

    cnt = counts[0, :N_EXPERTS].astype(jnp.int32)
    padded = ((cnt + ROW_BLOCK - 1) // ROW_BLOCK) * ROW_BLOCK
    pend = jnp.cumsum(padded)
    pstart = (pend - padded).astype(jnp.int32)
    n_rows = n * TOP_K + N_EXPERTS * ROW_BLOCK
    n_blocks = n_rows // ROW_BLOCK
    nused = (pend[-1] // ROW_BLOCK).astype(jnp.int32)
    blk = jnp.minimum(jnp.arange(n_blocks, dtype=jnp.int32), nused - 1) * ROW_BLOCK
    block_e = jnp.sum(blk[:, None] >= pend[None, :], axis=1).astype(jnp.int32)
    block_e = jnp.minimum(block_e, N_EXPERTS - 1)
    chosen = ei[:, :TOP_K, None] == jnp.arange(N_EXPERTS, dtype=jnp.int32)
    dest = (jnp.sum(jnp.where(chosen, pstart, 0), axis=-1) + ei[:, TOP_K:]).reshape(-1)

    xs = _dispatch(pstart, cnt, dest, xn2, n_rows)
    ys = _experts(block_e, nused[None], xs, p["w_up"], p["b_up"][:, None, :],
                  p["w_down"], p["b_down"][:, None, :])
    g = p["norm_ffn_g"][None] if final_g is None else final_g[None]
    return _combine(dest, hmid, eg, g, ys, final_g is not None).reshape(b, s, d)


def kernel(x, mem, norm_mix_g, w_in, w_gate, b_gate, w_proj_attn, w_proj_pool, w_proj_mem,
           pool_w_group, pool_scale, mem_norm_g, w_mem_kv, w_out, norm_ffn_g,
           w_router, b_router, w_up, b_up, w_down, b_down, norm_final_g):
    b, s, d = x.shape
    assert s % IN_TILE == 0 and s % COUNT_CHUNK == 0 and (b * s) % MOVE_TILE == 0
    assert d == LANES * SUBLANES, "the row-tile layout maps one model row onto one (8, 128) tile"
    stacked = dict(norm_mix_g=norm_mix_g, w_in=w_in, w_gate=w_gate, b_gate=b_gate,
                   w_proj_attn=w_proj_attn, w_proj_pool=w_proj_pool, w_proj_mem=w_proj_mem,
                   pool_w_group=pool_w_group, pool_scale=pool_scale, mem_norm_g=mem_norm_g,
                   w_mem_kv=w_mem_kv, w_out=w_out, norm_ffn_g=norm_ffn_g, w_router=w_router,
                   b_router=b_router, w_up=w_up, b_up=b_up, w_down=w_down, b_down=b_down)
    depth = w_in.shape[0]
    tables = _rope_tables(s)
    h = x
    for l in range(depth):
        p = {name: val[l] for name, val in stacked.items()}
        h = _layer(h, mem, p, tables, norm_final_g if l == depth - 1 else None)
    return h
```

```python
import functools

import jax
import jax.numpy as jnp
import numpy as np
from jax import lax
from jax.experimental import pallas as pl
from jax.experimental.pallas import tpu as pltpu

N_HEADS_ATTN = 8
HEAD_DIM = 64
ROPE_DIM = HEAD_DIM // 4
ROPE_THETA = 500000.0
N_HEADS_IDX = 8
IDX_DIM = 64
TOPK_MAX = 256
N_POOL_GROUPS = 4
POOL_GROUP_DIM = 64
POOL_WINDOWS = (2, 4, 8, 16)
POOL_HALO = 16
N_HEADS_MEM = 4
N_BRANCHES = 3
N_EXPERTS = 32
TOP_K = 4
SWIGLU_ALPHA = 1.702
SWIGLU_LIMIT = 7.0
EPS = 1e-6

ATTN_W = N_HEADS_ATTN * HEAD_DIM
POOL_W = N_POOL_GROUPS * POOL_GROUP_DIM
MEM_W = N_HEADS_MEM * HEAD_DIM
IDXQ_W = N_HEADS_IDX * IDX_DIM

LANES = 128
SUBLANES = 8
VMEM_LIMIT = 56 * 1024 * 1024

IN_TILE = 512
Q_BLOCK = 128
KEY_CHUNK = 256
COUNT_CHUNK = 512
COUNT_ROWS = 32
MERGE_TILE = 256
ROW_BLOCK = 256
MOVE_TILE = 256
ISSUE_UNROLL = 4

VT_ROWS = LANES + 16
VT_ALL = (N_HEADS_ATTN // 2) * VT_ROWS
LOG2_E = 1.4426950408889634

NEG_BIG = -1e30
INT16_MIN = -2 ** 15
COUNT_ROWS16 = 64


def _dot(a, b):
    return jnp.dot(a, b, preferred_element_type=jnp.float32)


def _dot_nt(a, b):
    return lax.dot_general(a, b, (((1,), (1,)), ((), ())), preferred_element_type=jnp.float32)


def _rmsnorm(x, g):
    return x * lax.rsqrt(jnp.mean(x * x, axis=-1, keepdims=True) + EPS) * g


def _store_row_tiles(ref, x):
    rows, width = x.shape
    for j in range(width // LANES):
        ref[pl.ds(j, rows, stride=width // LANES), :] = x[:, j * LANES:(j + 1) * LANES]


def _load_row_tiles(ref, rows, width):
    pieces = width // LANES
    return jnp.concatenate([ref[pl.ds(j, rows, stride=pieces), :] for j in range(pieces)], axis=1)


def _row_tile(ref, row):
    return ref.at[pl.ds(pl.multiple_of(row * SUBLANES, SUBLANES), SUBLANES)]


def _rope128(x, c, a, b):
    return x * c + pltpu.roll(x, LANES - ROPE_DIM // 2, 1) * a + pltpu.roll(x, ROPE_DIM // 2, 1) * b


def _in_proj_kernel(x_ref, g_ref, wa_ref, wb_ref, wc_ref, rc_ref, ra_ref, rb_ref,
                    q_ref, k_ref, vt_ref, iq_ref, iklo_ref, ikhi_ref, iw_ref, u_ref, qm_ref):
    xn = _rmsnorm(x_ref[...], g_ref[...]).astype(jnp.bfloat16)
    rc, ra, rb = rc_ref[...], ra_ref[...], rb_ref[...]
    pa = _dot(xn, wa_ref[...])
    for seg, (ref, scale) in enumerate(((q_ref, LOG2_E * HEAD_DIM ** -0.5), (k_ref, None),
                                        (None, None), (iq_ref, IDX_DIM ** -0.5))):
        for c in range(ATTN_W // LANES):
            lo = seg * ATTN_W + c * LANES
            blk = pa[:, lo:lo + LANES]
            if ref is None:
                vt_ref[c * VT_ROWS:c * VT_ROWS + LANES, :] = blk.T.astype(vt_ref.dtype)
                vt_ref[c * VT_ROWS + LANES:(c + 1) * VT_ROWS, :] = jnp.ones(
                    (VT_ROWS - LANES, blk.shape[0]), vt_ref.dtype)
                continue
            blk = _rope128(blk, rc, ra, rb)
            if scale is not None:
                blk = blk * scale
            ref[:, c * LANES:(c + 1) * LANES] = blk.astype(ref.dtype)
    pb = _dot(xn, wb_ref[...])
    lane = lax.broadcasted_iota(jnp.int32, pb.shape, 1)
    ik = jnp.where(lane < IDX_DIM, _rope128(pb, rc, ra, rb), 0.0)
    iklo_ref[...] = ik.astype(iklo_ref.dtype)
    ikhi_ref[...] = pltpu.roll(ik, IDX_DIM, 1).astype(ikhi_ref.dtype)
    iw_ref[...] = pb * (N_HEADS_IDX ** -0.5)
    pc = _dot(xn, wc_ref[...])
    u_ref[...] = pc[:, :POOL_W]
    qm_ref[...] = (pc[:, POOL_W:] * (HEAD_DIM ** -0.5)).astype(qm_ref.dtype)


def _in_proj(x2, g, wa, wb, wc, rc, ra, rb, batch, seq):
    n, d = x2.shape
    t = IN_TILE
    tiles_per_seq = seq // t
    row = lambda i: (i, 0)
    full = lambda i: (0, 0)
    pos = lambda i: (i % tiles_per_seq, 0)
    bf = jnp.bfloat16
    flat = lambda w, dt: (jax.ShapeDtypeStruct((n, w), dt), pl.BlockSpec((t, w), row))
    outs = (flat(ATTN_W, bf), flat(ATTN_W, bf),
            (jax.ShapeDtypeStruct((batch, VT_ALL, seq), bf),
             pl.BlockSpec((None, VT_ALL, t), lambda i: (i // tiles_per_seq, 0, i % tiles_per_seq))),
            flat(IDXQ_W, bf), flat(LANES, bf), flat(LANES, bf), flat(LANES, jnp.float32),
            flat(POOL_W, jnp.float32), flat(MEM_W, bf))
    return pl.pallas_call(
        _in_proj_kernel,
        grid=(n // t,),
        in_specs=[pl.BlockSpec((t, d), row), pl.BlockSpec((1, d), full),
                  pl.BlockSpec(wa.shape, full), pl.BlockSpec(wb.shape, full),
                  pl.BlockSpec(wc.shape, full),
                  pl.BlockSpec((t, LANES), pos), pl.BlockSpec((t, LANES), pos),
                  pl.BlockSpec((t, LANES), pos)],
        out_specs=tuple(o[1] for o in outs),
        out_shape=tuple(o[0] for o in outs),
        compiler_params=pltpu.CompilerParams(dimension_semantics=("arbitrary",),
                                             vmem_limit_bytes=VMEM_LIMIT),
        name="in_proj",
    )(x2, g, wa, wb, wc, rc, ra, rb)


def _mem_kv_kernel(mem_ref, g_ref, w_ref, km_ref, vm_ref):
    mn = _rmsnorm(mem_ref[...], g_ref[...]).astype(jnp.bfloat16)
    kv = _dot(mn, w_ref[...])
    km, vm = kv[:, :MEM_W], kv[:, MEM_W:]
    lane = lax.broadcasted_iota(jnp.int32, km.shape, 1)
    for h in range(N_HEADS_MEM):
        in_head = (lane >= h * HEAD_DIM) & (lane < (h + 1) * HEAD_DIM)
        km_ref[h] = jnp.where(in_head, km, 0.0).astype(km_ref.dtype)
        vm_ref[h] = jnp.where(in_head, vm, 0.0).astype(vm_ref.dtype)


def _mem_kv(mem, g, w):
    b, m, d = mem.shape
    out = jax.ShapeDtypeStruct((b, N_HEADS_MEM, m, MEM_W), jnp.bfloat16)
    return pl.pallas_call(
        _mem_kv_kernel,
        grid=(b,),
        in_specs=[pl.BlockSpec((None, m, d), lambda i: (i, 0, 0)),
                  pl.BlockSpec((1, d), lambda i: (0, 0)),
                  pl.BlockSpec(w.shape, lambda i: (0, 0))],
        out_specs=(pl.BlockSpec((None, N_HEADS_MEM, m, MEM_W), lambda i: (i, 0, 0, 0)),
                   pl.BlockSpec((None, N_HEADS_MEM, m, MEM_W), lambda i: (i, 0, 0, 0))),
        out_shape=(out, out),
        compiler_params=pltpu.CompilerParams(dimension_semantics=("arbitrary",),
                                             vmem_limit_bytes=VMEM_LIMIT),
        name="mem_kv",
    )(mem, g, w)


def _dsa_kernel(iq_ref, iw_ref, q_ref, iklo_ref, ikhi_ref, k_ref, vt_ref, o_ref,
                key_ref, half_ref, iqt_ref, qt_ref, bias_ref, s_ref, acc_ref, *, n_top, idx_bits):
    qb = pl.program_id(1)
    n_chunks = (qb * Q_BLOCK + Q_BLOCK + KEY_CHUNK - 1) // KEY_CHUNK
    n_count_chunks = (qb * Q_BLOCK + Q_BLOCK + COUNT_CHUNK - 1) // COUNT_CHUNK
    q_pos = qb * Q_BLOCK + lax.broadcasted_iota(jnp.int32, (1, Q_BLOCK), 1)
    row_k = lax.broadcasted_iota(jnp.int32, (KEY_CHUNK, 1), 0)
    row_c = lax.broadcasted_iota(jnp.int32, (COUNT_CHUNK, 1), 0)
    row_d = lax.broadcasted_iota(jnp.int32, (LANES, Q_BLOCK), 0)
    bf = jnp.bfloat16

    def chunk_off(c):
        return pl.multiple_of(c * KEY_CHUNK, KEY_CHUNK)

    wt = iw_ref[...].T
    for j in range(N_HEADS_IDX // 2):
        iqt_ref[j // 2, :, (j % 2) * Q_BLOCK:(j % 2 + 1) * Q_BLOCK] = (
            iq_ref[:, j * LANES:(j + 1) * LANES].astype(jnp.float32).T.astype(bf))
    for j in range(N_HEADS_ATTN // 2):
        pair_t = q_ref[:, j * LANES:(j + 1) * LANES].astype(jnp.float32).T
        qt_ref[j, :, :Q_BLOCK] = jnp.where(row_d < HEAD_DIM, pair_t, 0.0).astype(bf)
        qt_ref[j, :, Q_BLOCK:] = jnp.where(row_d >= HEAD_DIM, pair_t, 0.0).astype(bf)

    def score_chunk(c, carry):
        off = pl.multiple_of(c * COUNT_CHUNK, COUNT_CHUNK)
        acc = jnp.zeros((COUNT_CHUNK, Q_BLOCK), jnp.float32)
        for parity, ik_ref in enumerate((iklo_ref, ikhi_ref)):
            ik = ik_ref[pl.ds(off, COUNT_CHUNK), :]
            for g in range(N_HEADS_IDX // 4):
                dots = _dot(ik, iqt_ref[g])
                for side in range(2):
                    head = 2 * (2 * g + side) + parity
                    w = wt[IDX_DIM + head:IDX_DIM + head + 1, :]
                    acc = acc + w * jnp.maximum(dots[:, side * Q_BLOCK:(side + 1) * Q_BLOCK], 0.0)
        acc = jnp.where(off + row_c <= q_pos, acc, -jnp.inf)
        bits = lax.bitcast_convert_type(acc, jnp.int32)
        key = bits ^ ((bits >> 31) & 0x7FFFFFFF)
        key_ref[pl.ds(off, COUNT_CHUNK), :] = key
        half_ref[pl.ds(off, COUNT_CHUNK), :] = (key >> 16).astype(jnp.int16)
        return carry

    lax.fori_loop(0, n_count_chunks, score_chunk, 0)

    def count16(pred):
        def body(c, cnt):
            off = pl.multiple_of(c * COUNT_CHUNK, COUNT_CHUNK)
            hit = jnp.where(pred(half_ref[pl.ds(off, COUNT_CHUNK), :]), jnp.int16(1), jnp.int16(0))
            hit = hit.reshape(COUNT_CHUNK // COUNT_ROWS16, COUNT_ROWS16, Q_BLOCK)
            for part in range(COUNT_CHUNK // COUNT_ROWS16):
                cnt = cnt + hit[part]
            return cnt
        cnt = lax.fori_loop(0, n_count_chunks, body, jnp.zeros((COUNT_ROWS16, Q_BLOCK), jnp.int16))
        return jnp.sum(cnt.astype(jnp.float32), axis=0, keepdims=True)

    def search16(k_q):
        zero = jnp.zeros((1, Q_BLOCK), jnp.int32)
        zero16 = zero.astype(jnp.int16)
        cur = jnp.where(count16(lambda v: v >= zero16) >= k_q, zero, INT16_MIN)

        def bit(i, cur):
            cand = cur + jnp.left_shift(jnp.int32(1), 14 - i)
            cand16 = cand.astype(jnp.int16)
            return jnp.where(count16(lambda v: v >= cand16) >= k_q, cand, cur)
        return lax.fori_loop(0, 15, bit, cur)

    k_f = jnp.float32(n_top)
    thr_hi = search16(k_f)
    thr_hi16 = thr_hi.astype(jnp.int16)
    k_lo = k_f - count16(lambda v: v > thr_hi16)

    def low_half_chunk(c, carry):
        off = pl.multiple_of(c * COUNT_CHUNK, COUNT_CHUNK)
        kk = key_ref[pl.ds(off, COUNT_CHUNK), :]
        low = jnp.where((kk >> 16) == thr_hi, (kk & 0xFFFF) + INT16_MIN, INT16_MIN)
        half_ref[pl.ds(off, COUNT_CHUNK), :] = low.astype(jnp.int16)
        return carry

    lax.fori_loop(0, n_count_chunks, low_half_chunk, 0)
    thr = jnp.left_shift(thr_hi, 16) | (search16(k_lo) - INT16_MIN)

    def count(pred):
        def body(c, cnt):
            off = pl.multiple_of(c * COUNT_CHUNK, COUNT_CHUNK)
            hit = jnp.where(pred(key_ref[pl.ds(off, COUNT_CHUNK), :], off + row_c), 1.0, 0.0)
            return cnt + jnp.sum(hit.reshape(COUNT_CHUNK // COUNT_ROWS, COUNT_ROWS, Q_BLOCK), axis=0)
        cnt = lax.fori_loop(0, n_count_chunks, body, jnp.zeros((COUNT_ROWS, Q_BLOCK), jnp.float32))
        return jnp.sum(cnt, axis=0, keepdims=True)

    zero = jnp.zeros((1, Q_BLOCK), jnp.int32)
    need = k_f - count(lambda kk, pos: kk > thr)
    n_ties = count(lambda kk, pos: kk == thr)

    def tie_search():
        def tie_bit(i, cur):
            cand = cur + jnp.left_shift(jnp.int32(1), idx_bits - 1 - i)
            below = count(lambda kk, pos: (kk == thr) & (pos < cand))
            return jnp.where(below < need, cand, cur)
        return lax.fori_loop(0, idx_bits, tie_bit, zero)

    has_surplus_ties = jnp.max(jnp.where(n_ties > need, 1.0, 0.0)) > 0.0
    tie_pos = lax.cond(has_surplus_ties, tie_search, lambda: zero + (2 ** idx_bits - 1))

    acc_ref[...] = jnp.zeros(acc_ref.shape, jnp.float32)

    n_pairs = N_HEADS_ATTN // 2

    def masked_logits(c, slot):
        off = chunk_off(jnp.minimum(c, n_chunks - 1))
        kk = key_ref[pl.ds(off, KEY_CHUNK), :]
        pos = c * KEY_CHUNK + row_k
        sel = ((kk > thr) | ((kk == thr) & (pos <= tie_pos))) & (pos <= q_pos)
        bias = jnp.where(sel, 0.0, NEG_BIG)
        bias_ref[slot] = jnp.concatenate([bias, bias], axis=1)
        for j in range(n_pairs):
            kp = k_ref[pl.ds(off, KEY_CHUNK), j * LANES:(j + 1) * LANES]
            s_ref[slot, j] = _dot(kp, qt_ref[j]) + bias_ref[slot]

    def softmax_pv(c, slot, ms, ls):
        off = chunk_off(jnp.minimum(c, n_chunks - 1))
        new_ms, new_ls = [], []
        for j in range(n_pairs):
            s = s_ref[slot, j]
            m_new = jnp.maximum(ms[j], jnp.max(s, axis=0, keepdims=True))
            alpha = jnp.exp2(ms[j] - m_new)
            p = jnp.exp2(s - m_new).astype(bf)
            vt = vt_ref[j * VT_ROWS:(j + 1) * VT_ROWS, pl.ds(off, KEY_CHUNK)]
            pv = _dot(vt, p)
            new_ls.append(alpha * ls[j] + pv[LANES:LANES + 1, :])
            new_ms.append(m_new)
            lo, mid, hi = j * LANES, j * LANES + HEAD_DIM, (j + 1) * LANES
            acc_ref[lo:mid, :] = alpha[:, :Q_BLOCK] * acc_ref[lo:mid, :] + pv[:HEAD_DIM, :Q_BLOCK]
            acc_ref[mid:hi, :] = alpha[:, Q_BLOCK:] * acc_ref[mid:hi, :] + pv[HEAD_DIM:LANES, Q_BLOCK:]
        return tuple(new_ms), tuple(new_ls)

    masked_logits(0, 0)

    def attn_two_chunks(i, carry):
        ms, ls = carry
        masked_logits(2 * i + 1, 1)
        ms, ls = softmax_pv(2 * i, 0, ms, ls)
        masked_logits(2 * i + 2, 0)
        return softmax_pv(2 * i + 1, 1, ms, ls)

    init = (tuple(jnp.full((1, 2 * Q_BLOCK), NEG_BIG, jnp.float32) for _ in range(n_pairs)),
            tuple(jnp.zeros((1, 2 * Q_BLOCK), jnp.float32) for _ in range(n_pairs)))
    _, ls = lax.fori_loop(0, (n_chunks + 1) // 2, attn_two_chunks, init)
    for j in range(n_pairs):
        lo, mid, hi = j * LANES, j * LANES + HEAD_DIM, (j + 1) * LANES
        even = acc_ref[lo:mid, :] / ls[j][:, :Q_BLOCK]
        odd = acc_ref[mid:hi, :] / ls[j][:, Q_BLOCK:]
        o_ref[:, lo:hi] = jnp.concatenate([even, odd], axis=0).T.astype(o_ref.dtype)


def _dsa(iq, iw, q, iklo, ikhi, k, vt):
    b, s, _ = q.shape
    n_top = min(TOPK_MAX, s // 4)
    idx_bits = max(1, int(np.ceil(np.log2(s))))
    qblk = lambda w: pl.BlockSpec((None, Q_BLOCK, w), lambda bi, qi: (bi, qi, 0))
    keys = lambda w: pl.BlockSpec((None, s, w), lambda bi, qi: (bi, 0, 0))
    return pl.pallas_call(
        functools.partial(_dsa_kernel, n_top=n_top, idx_bits=idx_bits),
        grid=(b, s // Q_BLOCK),
        in_specs=[qblk(IDXQ_W), qblk(LANES), qblk(ATTN_W), keys(LANES), keys(LANES),
                  keys(ATTN_W), pl.BlockSpec((None, VT_ALL, s), lambda bi, qi: (bi, 0, 0))],
        out_specs=qblk(ATTN_W),
        out_shape=jax.ShapeDtypeStruct((b, s, ATTN_W), jnp.bfloat16),
        scratch_shapes=[pltpu.VMEM((s, Q_BLOCK), jnp.int32),
                        pltpu.VMEM((s, Q_BLOCK), jnp.int16),
                        pltpu.VMEM((N_HEADS_IDX // 4, LANES, 2 * Q_BLOCK), jnp.bfloat16),
                        pltpu.VMEM((N_HEADS_ATTN // 2, LANES, 2 * Q_BLOCK), jnp.bfloat16),
                        pltpu.VMEM((2, KEY_CHUNK, 2 * Q_BLOCK), jnp.float32),
                        pltpu.VMEM((2, N_HEADS_ATTN // 2, KEY_CHUNK, 2 * Q_BLOCK), jnp.float32),
                        pltpu.VMEM((ATTN_W, Q_BLOCK), jnp.float32)],
        compiler_params=pltpu.CompilerParams(dimension_semantics=("arbitrary", "arbitrary"),
                                             vmem_limit_bytes=VMEM_LIMIT),
        name="dsa",
    )(iq, iw, q, iklo, ikhi, k, vt)


def _merge_kernel(x_ref, ya_ref, u_ref, uprev_ref, qm_ref, km_ref, vm_ref,
                  gmix_ref, wg_ref, bg_ref, wpa_ref, wpp_ref, wpm_ref, wbd_ref, psc_ref,
                  wo_ref, gffn_ref, wrh_ref, wrl_ref, br_ref,
                  h_ref, xn2_ref, eg_ref, ei_ref, cnt_ref, carry_ref, *, tiles_per_seq):
    i = pl.program_id(0)
    t = x_ref.shape[0]
    d = x_ref.shape[1]
    bf = jnp.bfloat16
    x = x_ref[...]
    xn = _rmsnorm(x, gmix_ref[...]).astype(bf)

    tile_in_seq = i % tiles_per_seq
    u = u_ref[...]
    halo = jnp.where(tile_in_seq == 0, 0.0, uprev_ref[...])
    ext = jnp.concatenate([halo, u], axis=0)
    lane_p = lax.broadcasted_iota(jnp.int32, (t, POOL_W), 1)
    pos1 = (tile_in_seq * t + lax.broadcasted_iota(jnp.int32, (t, 1), 0) + 1).astype(jnp.float32)
    pooled = None
    run, width = ext, 1
    for g, w in enumerate(POOL_WINDOWS):
        while width < w:
            run = run[width:] + run[:-width]
            width *= 2
        start = POOL_HALO + 1 - w
        mean = run[start:start + t] / jnp.minimum(pos1, float(w))
        pooled = mean if pooled is None else jnp.where(lane_p >= g * POOL_GROUP_DIM, mean, pooled)
    pooled = pooled - u
    mixed = _dot(pooled.astype(bf), wbd_ref[...]) * psc_ref[...]
    y_pool = _dot(mixed.astype(bf), wpp_ref[...])

    qm = qm_ref[...]
    probs = []
    for h in range(N_HEADS_MEM):
        s = _dot_nt(qm, km_ref[h])
        p = jnp.exp(s - jnp.max(s, axis=1, keepdims=True))
        probs.append((p / jnp.sum(p, axis=1, keepdims=True)).astype(bf))
    y_mem = _dot(probs[0], vm_ref[0])
    for h in range(1, N_HEADS_MEM):
        y_mem = y_mem + _dot(probs[h], vm_ref[h])
    y_mem = _dot(y_mem.astype(bf), wpm_ref[...])

    y_attn = _dot(ya_ref[...], wpa_ref[...])

    def gate(br):
        z = _dot(xn, wg_ref[:, br * d:(br + 1) * d]) + bg_ref[:, br * d:(br + 1) * d]
        return 1.0 / (1.0 + jnp.exp(-z))

    merged = gate(0) * y_attn + gate(1) * y_pool + gate(2) * y_mem
    h = x + _dot(merged.astype(bf), wo_ref[...])
    h_ref[...] = h
    xn2 = _rmsnorm(h, gffn_ref[...])
    _store_row_tiles(xn2_ref, xn2)

    x_hi = xn2.astype(bf)
    x_lo = (xn2 - x_hi.astype(jnp.float32)).astype(bf)
    logits = (_dot(x_hi, wrh_ref[...]) + (_dot(x_lo, wrh_ref[...]) + _dot(x_hi, wrl_ref[...]))
              + br_ref[...])
    lane_e = lax.broadcasted_iota(jnp.int32, logits.shape, 1).astype(jnp.float32)
    work = logits
    vals, onehots, ids = [], [], []
    for _ in range(TOP_K):
        mx = jnp.max(work, axis=1, keepdims=True)
        idx = jnp.min(jnp.where(work == mx, lane_e, float(LANES)), axis=1, keepdims=True)
        oh = lane_e == idx
        vals.append(mx)
        ids.append(idx)
        onehots.append(oh)
        work = jnp.where(oh, -jnp.inf, work)
    exps = [jnp.exp(v - vals[0]) for v in vals]
    denom = exps[0] + exps[1] + exps[2] + exps[3]

    @pl.when(i == 0)
    def _():
        carry_ref[...] = jnp.zeros(carry_ref.shape, jnp.float32)

    member = jnp.where(onehots[0] | onehots[1] | onehots[2] | onehots[3], 1.0, 0.0)
    r_io = lax.broadcasted_iota(jnp.int32, (t, t), 0)
    c_io = lax.broadcasted_iota(jnp.int32, (t, t), 1)
    earlier = jnp.where(c_io < r_io, 1.0, 0.0).astype(bf)
    before = _dot(earlier, member.astype(bf)) + carry_ref[...]
    carry_ref[...] = carry_ref[...] + jnp.sum(member, axis=0, keepdims=True)
    cnt_ref[...] = carry_ref[...]

    lane8 = lax.broadcasted_iota(jnp.int32, (t, 2 * TOP_K), 1)
    eg = jnp.zeros((t, 2 * TOP_K), jnp.float32)
    ei = jnp.zeros((t, 2 * TOP_K), jnp.float32)
    for j in range(TOP_K):
        rank = jnp.sum(jnp.where(onehots[j], before, 0.0), axis=1, keepdims=True)
        eg = jnp.where(lane8 == j, exps[j] / denom, eg)
        ei = jnp.where(lane8 == j, ids[j], ei)
        ei = jnp.where(lane8 == TOP_K + j, rank, ei)
    eg_ref[...] = eg
    ei_ref[...] = ei.astype(jnp.int32)


def _merge(x2, ya, u, qm, km, vm, gmix, wg, bg, wpa, wpp, wpm, wbd, psc, wo, gffn, wrh, wrl, br, seq):
    n, d = x2.shape
    t = MERGE_TILE
    tiles_per_seq = seq // t
    m = km.shape[2]
    row = lambda i: (i, 0)
    full = lambda i: (0, 0)
    per_batch = lambda i: (i // tiles_per_seq, 0, 0, 0)
    halo_blocks = t // POOL_HALO
    prev = lambda i: (jnp.maximum(i * halo_blocks - 1, 0), 0)
    consts = (gmix, wg, bg, wpa, wpp, wpm, wbd, psc, wo, gffn, wrh, wrl, br)
    pieces = d // LANES
    out_shape = (jax.ShapeDtypeStruct((n, d), jnp.float32),
                 jax.ShapeDtypeStruct((n * pieces, LANES), jnp.float32),
                 jax.ShapeDtypeStruct((n, 2 * TOP_K), jnp.float32),
                 jax.ShapeDtypeStruct((n, 2 * TOP_K), jnp.int32),
                 jax.ShapeDtypeStruct((1, LANES), jnp.float32))
    return pl.pallas_call(
        functools.partial(_merge_kernel, tiles_per_seq=tiles_per_seq),
        grid=(n // t,),
        in_specs=[pl.BlockSpec((t, d), row), pl.BlockSpec((t, ATTN_W), row),
                  pl.BlockSpec((t, POOL_W), row), pl.BlockSpec((POOL_HALO, POOL_W), prev),
                  pl.BlockSpec((t, MEM_W), row),
                  pl.BlockSpec((None, N_HEADS_MEM, m, MEM_W), per_batch),
                  pl.BlockSpec((None, N_HEADS_MEM, m, MEM_W), per_batch)]
                 + [pl.BlockSpec(c.shape, full) for c in consts],
        out_specs=(pl.BlockSpec((t, d), row), pl.BlockSpec((t * pieces, LANES), row),
                   pl.BlockSpec((t, 2 * TOP_K), row), pl.BlockSpec((t, 2 * TOP_K), row),
                   pl.BlockSpec((1, LANES), full)),
        out_shape=out_shape,
        scratch_shapes=[pltpu.VMEM((1, LANES), jnp.float32)],
        compiler_params=pltpu.CompilerParams(dimension_semantics=("arbitrary",),
                                             vmem_limit_bytes=VMEM_LIMIT),
        name="merge",
    )(x2, ya, u, u, qm, km, vm, *consts)


def _row_copy(src_ref, src_row, dst_ref, dst_row, sem):
    return pltpu.make_async_copy(_row_tile(src_ref, src_row), _row_tile(dst_ref, dst_row), sem)


def _dispatch_kernel(pstart_ref, cnt_ref, dest_ref, x_ref, xs_ref, zero_ref, sem):
    t = x_ref.shape[0] // SUBLANES
    block_tiles = ROW_BLOCK * SUBLANES

    @pl.when(pl.program_id(0) == 0)
    def _():
        zero_ref[...] = jnp.zeros(zero_ref.shape, zero_ref.dtype)

        def for_each_pad_row(act):
            def per_expert(e, carry):
                n_pad = (ROW_BLOCK - cnt_ref[e] % ROW_BLOCK) % ROW_BLOCK
                first = pstart_ref[e] + cnt_ref[e]

                def per_row(r, c):
                    act(_row_copy(zero_ref, 0, xs_ref, first + r, sem))
                    return c
                return lax.fori_loop(0, n_pad, per_row, carry)
            lax.fori_loop(0, N_EXPERTS, per_expert, 0)

        def for_each_unused_block(act):
            n_blocks = xs_ref.shape[0] // block_tiles
            first = (pstart_ref[N_EXPERTS - 1] + cnt_ref[N_EXPERTS - 1] + ROW_BLOCK - 1) // ROW_BLOCK

            def per_block(b, c):
                rows = pl.ds(pl.multiple_of(b * block_tiles, block_tiles), block_tiles)
                act(pltpu.make_async_copy(zero_ref, xs_ref.at[rows], sem))
                return c
            lax.fori_loop(first, n_blocks, per_block, 0)

        for act in (lambda cp: cp.start(), lambda cp: cp.wait()):
            for_each_pad_row(act)
            for_each_unused_block(act)

    def issue(tok, carry):
        for j in range(TOP_K):
            _row_copy(x_ref, tok, xs_ref, dest_ref[tok * TOP_K + j], sem).start(priority=j % 2)
        return carry

    lax.fori_loop(0, t, issue, 0, unroll=ISSUE_UNROLL)
    for j in range(TOP_K):
        pltpu.make_async_copy(x_ref, xs_ref.at[pl.ds(0, x_ref.shape[0])], sem).wait()


def _dispatch(pstart, cnt, dest, xn2, n_rows):
    tiles, lanes = xn2.shape
    n = tiles // SUBLANES
    t = MOVE_TILE
    return pl.pallas_call(
        _dispatch_kernel,
        grid_spec=pltpu.PrefetchScalarGridSpec(
            num_scalar_prefetch=2,
            grid=(n // t,),
            in_specs=[pl.BlockSpec((t * TOP_K,), lambda i, ps, ct: (i,), memory_space=pltpu.SMEM),
                      pl.BlockSpec((t * SUBLANES, lanes), lambda i, ps, ct: (i, 0))],
            out_specs=pl.BlockSpec(memory_space=pl.ANY),
            scratch_shapes=[pltpu.VMEM((ROW_BLOCK * SUBLANES, lanes), jnp.float32),
                            pltpu.SemaphoreType.DMA(())]),
        out_shape=jax.ShapeDtypeStruct((n_rows * SUBLANES, lanes), jnp.float32),
        compiler_params=pltpu.CompilerParams(dimension_semantics=("arbitrary",),
                                             vmem_limit_bytes=VMEM_LIMIT),
        name="dispatch",
    )(pstart, cnt, dest, xn2)


def _experts_kernel(be_ref, nused_ref, xs_ref, wu_ref, bu_ref, wd_ref, bd_ref, ys_ref,
                    wu16_ref, wd16_ref):
    i = pl.program_id(0)
    d, f = wu_ref.shape[0], wd_ref.shape[0]

    @pl.when((i == 0) | (be_ref[i] != be_ref[jnp.maximum(i - 1, 0)]))
    def _():
        wu16_ref[...] = wu_ref[...].astype(wu16_ref.dtype)
        wd16_ref[...] = wd_ref[...].astype(wd16_ref.dtype)

    @pl.when(i < nused_ref[0])
    def _():
        x = _load_row_tiles(xs_ref, ROW_BLOCK, d)
        hb = _dot(x.astype(jnp.bfloat16), wu16_ref[...]) + bu_ref[...]
        glu = jnp.minimum(hb[:, :f], SWIGLU_LIMIT)
        lin = jnp.clip(hb[:, f:], -SWIGLU_LIMIT, SWIGLU_LIMIT)
        act = glu * (1.0 / (1.0 + jnp.exp(-SWIGLU_ALPHA * glu))) * (lin + 1.0)
        _store_row_tiles(ys_ref, _dot(act.astype(jnp.bfloat16), wd16_ref[...]) + bd_ref[...])

    @pl.when(i >= nused_ref[0])
    def _():
        ys_ref[...] = jnp.zeros(ys_ref.shape, ys_ref.dtype)


def _experts(block_e, nused, xs, wu, bu, wd, bd):
    tiles, lanes = xs.shape
    e, d, f2 = wu.shape
    f = wd.shape[1]
    block = (ROW_BLOCK * SUBLANES, lanes)
    rows = lambda i, be, nu: (i, 0)
    used_rows = lambda i, be, nu: (jnp.minimum(i, nu[0] - 1), 0)
    per_e = lambda i, be, nu: (be[i], 0, 0)
    return pl.pallas_call(
        _experts_kernel,
        grid_spec=pltpu.PrefetchScalarGridSpec(
            num_scalar_prefetch=2,
            grid=(tiles // block[0],),
            in_specs=[pl.BlockSpec(block, used_rows),
                      pl.BlockSpec((None, d, f2), per_e), pl.BlockSpec((None, 1, f2), per_e),
                      pl.BlockSpec((None, f, d), per_e), pl.BlockSpec((None, 1, d), per_e)],
            out_specs=pl.BlockSpec(block, rows),
            scratch_shapes=[pltpu.VMEM((d, f2), jnp.bfloat16), pltpu.VMEM((f, d), jnp.bfloat16)]),
        out_shape=jax.ShapeDtypeStruct(xs.shape, jnp.float32),
        compiler_params=pltpu.CompilerParams(dimension_semantics=("arbitrary",),
                                             vmem_limit_bytes=VMEM_LIMIT),
        name="experts",
    )(block_e, nused, xs, wu, bu, wd, bd)


def _combine_kernel(dest_ref, dest_next_ref, h_ref, eg_ref, g_ref, ys_ref, o_ref, buf_ref, sems,
                    *, final_norm):
    i = pl.program_id(0)
    t, d = h_ref.shape
    half = i % 2

    def gather_rows(idx_ref, into):
        def issue(tok, carry):
            for j in range(TOP_K):
                _row_copy(ys_ref, idx_ref[tok * TOP_K + j], buf_ref.at[into, j], tok,
                          sems.at[into]).start(priority=j % 2)
            return carry
        lax.fori_loop(0, t, issue, 0, unroll=ISSUE_UNROLL)

    @pl.when(i == 0)
    def _():
        gather_rows(dest_ref, 0)

    @pl.when(i + 1 < pl.num_programs(0))
    def _():
        gather_rows(dest_next_ref, 1 - half)

    for j in range(TOP_K):
        pltpu.make_async_copy(ys_ref.at[pl.ds(0, t * SUBLANES)], buf_ref.at[half, j],
                              sems.at[half]).wait()
    eg = eg_ref[...]
    h = h_ref[...]
    for j in range(TOP_K):
        h = h + eg[:, j:j + 1] * _load_row_tiles(buf_ref.at[half, j], t, d)
    o_ref[...] = _rmsnorm(h, g_ref[...]) if final_norm else h


def _combine(dest, h, eg, g, ys, final_norm):
    n, d = h.shape
    t = MOVE_TILE
    return pl.pallas_call(
        functools.partial(_combine_kernel, final_norm=final_norm),
        grid=(n // t,),
        in_specs=[pl.BlockSpec((t * TOP_K,), lambda i: (i,), memory_space=pltpu.SMEM),
                  pl.BlockSpec((t * TOP_K,), lambda i: (jnp.minimum(i + 1, n // t - 1),),
                               memory_space=pltpu.SMEM),
                  pl.BlockSpec((t, d), lambda i: (i, 0)),
                  pl.BlockSpec((t, 2 * TOP_K), lambda i: (i, 0)),
                  pl.BlockSpec((1, d), lambda i: (0, 0)),
                  pl.BlockSpec(memory_space=pl.ANY)],
        out_specs=pl.BlockSpec((t, d), lambda i: (i, 0)),
        scratch_shapes=[pltpu.VMEM((2, TOP_K, t * SUBLANES, ys.shape[1]), jnp.float32),
                        pltpu.SemaphoreType.DMA((2,))],
        out_shape=jax.ShapeDtypeStruct((n, d), jnp.float32),
        compiler_params=pltpu.CompilerParams(dimension_semantics=("arbitrary",),
                                             vmem_limit_bytes=VMEM_LIMIT),
        name="combine",
    )(dest, dest, h, eg, g, ys)


def _rope_tables(seq):
    half = ROPE_DIM // 2
    inv = jnp.power(jnp.float32(ROPE_THETA), -jnp.arange(half, dtype=jnp.float32) / half)
    ang = jnp.arange(seq, dtype=jnp.float32)[:, None] * inv[None, :]
    cos, sin = jnp.cos(ang), jnp.sin(ang)
    pad = HEAD_DIM - ROPE_DIM
    one, zero = jnp.ones((seq, pad), jnp.float32), jnp.zeros((seq, pad), jnp.float32)
    zh = jnp.zeros((seq, half), jnp.float32)
    reps = LANES // HEAD_DIM
    c = jnp.tile(jnp.concatenate([cos, cos, one], axis=1), (1, reps))
    a = jnp.tile(jnp.concatenate([-sin, zh, zero], axis=1), (1, reps))
    b = jnp.tile(jnp.concatenate([zh, sin, zero], axis=1), (1, reps))
    return c, a, b


def _layer(h, mem, p, tables, final_g):
    b, s, d = h.shape
    n = b * s
    bf = jnp.bfloat16
    x2 = h.reshape(n, d)
    w_in = p["w_in"]
    o_ik = 3 * ATTN_W + IDXQ_W
    o_u = o_ik + IDX_DIM + N_HEADS_IDX
    wa = w_in[:, :o_ik].astype(bf)
    wb = jnp.pad(w_in[:, o_ik:o_u], ((0, 0), (0, LANES - (o_u - o_ik)))).astype(bf)
    wc = w_in[:, o_u:].astype(bf)
    q, k, vt, iq, iklo, ikhi, iw, u, qm = _in_proj(x2, p["norm_mix_g"][None], wa, wb, wc, *tables, b, s)
    km, vm = _mem_kv(mem, p["mem_norm_g"][None], p["w_mem_kv"].astype(bf))
    b3 = lambda a: a.reshape(b, s, a.shape[-1])
    ya = _dsa(b3(iq), b3(iw), b3(q), b3(iklo), b3(ikhi), b3(k), vt).reshape(n, ATTN_W)

    wbd = jnp.zeros((POOL_W, POOL_W), jnp.float32)
    for g in range(N_POOL_GROUPS):
        lo = g * POOL_GROUP_DIM
        wbd = wbd.at[lo:lo + POOL_GROUP_DIM, lo:lo + POOL_GROUP_DIM].set(p["pool_w_group"][g])
    wr = jnp.pad(p["w_router"], ((0, 0), (0, LANES - N_EXPERTS)))
    wr_hi = wr.astype(bf)
    br = jnp.pad(p["b_router"], (0, LANES - N_EXPERTS), constant_values=NEG_BIG)[None]
    hmid, xn2, eg, ei, counts = _merge(
        x2, ya, u, qm, km, vm, p["norm_mix_g"][None], p["w_gate"].astype(bf), p["b_gate"][None],
        p["w_proj_attn"].astype(bf), p["w_proj_pool"].astype(bf), p["w_proj_mem"].astype(bf),
        wbd.astype(bf), p["pool_scale"][None], p["w_out"].astype(bf), p["norm_ffn_g"][None],
        wr_hi, (wr - wr_hi.astype(jnp.float32)).astype(bf), br, s)

    cnt = counts[0, :N_EXPERTS].astype(jnp.int32)
    padded = ((cnt + ROW_BLOCK - 1) // ROW_BLOCK) * ROW_BLOCK
    pend = jnp.cumsum(padded)
    pstart = (pend - padded).astype(jnp.int32)
    n_rows = n * TOP_K + N_EXPERTS * ROW_BLOCK
    n_blocks = n_rows // ROW_BLOCK
    nused = (pend[-1] // ROW_BLOCK).astype(jnp.int32)
    blk = jnp.minimum(jnp.arange(n_blocks, dtype=jnp.int32), nused - 1) * ROW_BLOCK
    block_e = jnp.sum(blk[:, None] >= pend[None, :], axis=1).astype(jnp.int32)
    block_e = jnp.minimum(block_e, N_EXPERTS - 1)
    chosen = ei[:, :TOP_K, None] == jnp.arange(N_EXPERTS, dtype=jnp.int32)
    dest = (jnp.sum(jnp.where(chosen, pstart, 0), axis=-1) + ei[:, TOP_K:]).reshape(-1)

    xs = _dispatch(pstart, cnt, dest, xn2, n_rows)
    ys = _experts(block_e, nused[None], xs, p["w_up"], p["b_up"][:, None, :],
                  p["w_down"], p["b_down"][:, None, :])
    g = p["norm_ffn_g"][None] if final_g is None else final_g[None]
    return _combine(dest, hmid, eg, g, ys, final_g is not None).reshape(b, s, d)


def kernel(x, mem, norm_mix_g, w_in, w_gate, b_gate, w_proj_attn, w_proj_pool, w_proj_mem,
           pool_w_group, pool_scale, mem_norm_g, w_mem_kv, w_out, norm_ffn_g,
           w_router, b_router, w_up, b_up, w_down, b_down, norm_final_g):
    b, s, d = x.shape
    assert s % IN_TILE == 0 and s % COUNT_CHUNK == 0 and (b * s) % MOVE_TILE == 0
    assert d == LANES * SUBLANES, "the row-tile layout maps one model row onto one (8, 128) tile"
    stacked = dict(norm_mix_g=norm_mix_g, w_in=w_in, w_gate=w_gate, b_gate=b_gate,
                   w_proj_attn=w_proj_attn, w_proj_pool=w_proj_pool, w_proj_mem=w_proj_mem,
                   pool_w_group=pool_w_group, pool_scale=pool_scale, mem_norm_g=mem_norm_g,
                   w_mem_kv=w_mem_kv, w_out=w_out, norm_ffn_g=norm_ffn_g, w_router=w_router,
                   b_router=b_router, w_up=w_up, b_up=b_up, w_down=w_down, b_down=b_down)
    depth = w_in.shape[0]
    tables = _rope_tables(s)
    h = x
    for l in range(depth):
        p = {name: val[l] for name, val in stacked.items()}
        h = _layer(h, mem, p, tables, norm_final_g if l == depth - 1 else None)
    return h
```

```python
import functools

import jax
import jax.numpy as jnp
import numpy as np
from jax import lax
from jax.experimental import pallas as pl
from jax.experimental.pallas import tpu as pltpu

N_HEADS_ATTN = 8
HEAD_DIM = 64
ROPE_DIM = HEAD_DIM // 4
ROPE_THETA = 500000.0
N_HEADS_IDX = 8
IDX_DIM = 64
TOPK_MAX = 256
N_POOL_GROUPS = 4
POOL_GROUP_DIM = 64
POOL_WINDOWS = (2, 4, 8, 16)
POOL_HALO = 16
N_HEADS_MEM = 4
N_BRANCHES = 3
N_EXPERTS = 32
TOP_K = 4
SWIGLU_ALPHA = 1.702
SWIGLU_LIMIT = 7.0
EPS = 1e-6

ATTN_W = N_HEADS_ATTN * HEAD_DIM
POOL_W = N_POOL_GROUPS * POOL_GROUP_DIM
MEM_W = N_HEADS_MEM * HEAD_DIM
IDXQ_W = N_HEADS_IDX * IDX_DIM

LANES = 128
SUBLANES = 8
VMEM_LIMIT = 56 * 1024 * 1024

IN_TILE = 512
Q_BLOCK = 128
KEY_CHUNK = 256
COUNT_CHUNK = 512
COUNT_ROWS = 32
MERGE_TILE = 256
ROW_BLOCK = 256
UP_TILE = 256
MOVE_TILE = 256
ISSUE_UNROLL = 4

VT_ROWS = LANES + 16
VT_ALL = (N_HEADS_ATTN // 2) * VT_ROWS
LOG2_E = 1.4426950408889634

NEG_BIG = -1e30
INT_MIN = -2 ** 31


def _dot(a, b):
    return jnp.dot(a, b, preferred_element_type=jnp.float32)


def _dot_nt(a, b):
    return lax.dot_general(a, b, (((1,), (1,)), ((), ())), preferred_element_type=jnp.float32)


def _rmsnorm(x, g):
    return x * lax.rsqrt(jnp.mean(x * x, axis=-1, keepdims=True) + EPS) * g


def _store_row_tiles(ref, x):
    rows, width = x.shape
    for j in range(width // LANES):
        ref[pl.ds(j, rows, stride=width // LANES), :] = x[:, j * LANES:(j + 1) * LANES]


def _load_row_tiles(ref, rows, width):
    pieces = width // LANES
    return jnp.concatenate([ref[pl.ds(j, rows, stride=pieces), :] for j in range(pieces)], axis=1)


def _row_tile(ref, row):
    return ref.at[pl.ds(pl.multiple_of(row * SUBLANES, SUBLANES), SUBLANES)]


def _rope128(x, c, a, b):
    return x * c + pltpu.roll(x, LANES - ROPE_DIM // 2, 1) * a + pltpu.roll(x, ROPE_DIM // 2, 1) * b


def _in_proj_kernel(x_ref, g_ref, wa_ref, wb_ref, wc_ref, rc_ref, ra_ref, rb_ref,
                    q_ref, k_ref, vt_ref, iq_ref, iklo_ref, ikhi_ref, iw_ref, u_ref, qm_ref):
    xn = _rmsnorm(x_ref[...], g_ref[...]).astype(jnp.bfloat16)
    rc, ra, rb = rc_ref[...], ra_ref[...], rb_ref[...]
    pa = _dot(xn, wa_ref[...])
    for seg, (ref, scale) in enumerate(((q_ref, LOG2_E * HEAD_DIM ** -0.5), (k_ref, None),
                                        (None, None), (iq_ref, IDX_DIM ** -0.5))):
        for c in range(ATTN_W // LANES):
            lo = seg * ATTN_W + c * LANES
            blk = pa[:, lo:lo + LANES]
            if ref is None:
                vt_ref[c * VT_ROWS:c * VT_ROWS + LANES, :] = blk.T.astype(vt_ref.dtype)
                vt_ref[c * VT_ROWS + LANES:(c + 1) * VT_ROWS, :] = jnp.ones(
                    (VT_ROWS - LANES, blk.shape[0]), vt_ref.dtype)
                continue
            blk = _rope128(blk, rc, ra, rb)
            if scale is not None:
                blk = blk * scale
            ref[:, c * LANES:(c + 1) * LANES] = blk.astype(ref.dtype)
    pb = _dot(xn, wb_ref[...])
    lane = lax.broadcasted_iota(jnp.int32, pb.shape, 1)
    ik = jnp.where(lane < IDX_DIM, _rope128(pb, rc, ra, rb), 0.0)
    iklo_ref[...] = ik.astype(iklo_ref.dtype)
    ikhi_ref[...] = pltpu.roll(ik, IDX_DIM, 1).astype(ikhi_ref.dtype)
    iw_ref[...] = pb * (N_HEADS_IDX ** -0.5)
    pc = _dot(xn, wc_ref[...])
    u_ref[...] = pc[:, :POOL_W]
    qm_ref[...] = (pc[:, POOL_W:] * (HEAD_DIM ** -0.5)).astype(qm_ref.dtype)


def _in_proj(x2, g, wa, wb, wc, rc, ra, rb, batch, seq):
    n, d = x2.shape
    t = IN_TILE
    tiles_per_seq = seq // t
    row = lambda i: (i, 0)
    full = lambda i: (0, 0)
    pos = lambda i: (i % tiles_per_seq, 0)
    bf = jnp.bfloat16
    flat = lambda w, dt: (jax.ShapeDtypeStruct((n, w), dt), pl.BlockSpec((t, w), row))
    outs = (flat(ATTN_W, bf), flat(ATTN_W, bf),
            (jax.ShapeDtypeStruct((batch, VT_ALL, seq), bf),
             pl.BlockSpec((None, VT_ALL, t), lambda i: (i // tiles_per_seq, 0, i % tiles_per_seq))),
            flat(IDXQ_W, bf), flat(LANES, bf), flat(LANES, bf), flat(LANES, jnp.float32),
            flat(POOL_W, jnp.float32), flat(MEM_W, bf))
    return pl.pallas_call(
        _in_proj_kernel,
        grid=(n // t,),
        in_specs=[pl.BlockSpec((t, d), row), pl.BlockSpec((1, d), full),
                  pl.BlockSpec(wa.shape, full), pl.BlockSpec(wb.shape, full),
                  pl.BlockSpec(wc.shape, full),
                  pl.BlockSpec((t, LANES), pos), pl.BlockSpec((t, LANES), pos),
                  pl.BlockSpec((t, LANES), pos)],
        out_specs=tuple(o[1] for o in outs),
        out_shape=tuple(o[0] for o in outs),
        compiler_params=pltpu.CompilerParams(dimension_semantics=("arbitrary",),
                                             vmem_limit_bytes=VMEM_LIMIT),
        name="in_proj",
    )(x2, g, wa, wb, wc, rc, ra, rb)


def _mem_kv_kernel(mem_ref, g_ref, w_ref, km_ref, vm_ref):
    mn = _rmsnorm(mem_ref[...], g_ref[...]).astype(jnp.bfloat16)
    kv = _dot(mn, w_ref[...])
    km, vm = kv[:, :MEM_W], kv[:, MEM_W:]
    lane = lax.broadcasted_iota(jnp.int32, km.shape, 1)
    for h in range(N_HEADS_MEM):
        in_head = (lane >= h * HEAD_DIM) & (lane < (h + 1) * HEAD_DIM)
        km_ref[h] = jnp.where(in_head, km, 0.0).astype(km_ref.dtype)
        vm_ref[h] = jnp.where(in_head, vm, 0.0).astype(vm_ref.dtype)


def _mem_kv(mem, g, w):
    b, m, d = mem.shape
    out = jax.ShapeDtypeStruct((b, N_HEADS_MEM, m, MEM_W), jnp.bfloat16)
    return pl.pallas_call(
        _mem_kv_kernel,
        grid=(b,),
        in_specs=[pl.BlockSpec((None, m, d), lambda i: (i, 0, 0)),
                  pl.BlockSpec((1, d), lambda i: (0, 0)),
                  pl.BlockSpec(w.shape, lambda i: (0, 0))],
        out_specs=(pl.BlockSpec((None, N_HEADS_MEM, m, MEM_W), lambda i: (i, 0, 0, 0)),
                   pl.BlockSpec((None, N_HEADS_MEM, m, MEM_W), lambda i: (i, 0, 0, 0))),
        out_shape=(out, out),
        compiler_params=pltpu.CompilerParams(dimension_semantics=("arbitrary",),
                                             vmem_limit_bytes=VMEM_LIMIT),
        name="mem_kv",
    )(mem, g, w)


def _dsa_kernel(iq_ref, iw_ref, q_ref, iklo_ref, ikhi_ref, k_ref, vt_ref, o_ref,
                key_ref, iqt_ref, qt_ref, bias_ref, s_ref, acc_ref, *, n_top, idx_bits):
    qb = pl.program_id(1)
    n_chunks = (qb * Q_BLOCK + Q_BLOCK + KEY_CHUNK - 1) // KEY_CHUNK
    n_count_chunks = (qb * Q_BLOCK + Q_BLOCK + COUNT_CHUNK - 1) // COUNT_CHUNK
    q_pos = qb * Q_BLOCK + lax.broadcasted_iota(jnp.int32, (1, Q_BLOCK), 1)
    row_k = lax.broadcasted_iota(jnp.int32, (KEY_CHUNK, 1), 0)
    row_c = lax.broadcasted_iota(jnp.int32, (COUNT_CHUNK, 1), 0)
    row_d = lax.broadcasted_iota(jnp.int32, (LANES, Q_BLOCK), 0)
    bf = jnp.bfloat16

    def chunk_off(c):
        return pl.multiple_of(c * KEY_CHUNK, KEY_CHUNK)

    wt = iw_ref[...].T
    for j in range(N_HEADS_IDX // 2):
        iqt_ref[j // 2, :, (j % 2) * Q_BLOCK:(j % 2 + 1) * Q_BLOCK] = (
            iq_ref[:, j * LANES:(j + 1) * LANES].astype(jnp.float32).T.astype(bf))
    for j in range(N_HEADS_ATTN // 2):
        pair_t = q_ref[:, j * LANES:(j + 1) * LANES].astype(jnp.float32).T
        qt_ref[j, :, :Q_BLOCK] = jnp.where(row_d < HEAD_DIM, pair_t, 0.0).astype(bf)
        qt_ref[j, :, Q_BLOCK:] = jnp.where(row_d >= HEAD_DIM, pair_t, 0.0).astype(bf)

    def score_chunk(c, carry):
        off = pl.multiple_of(c * COUNT_CHUNK, COUNT_CHUNK)
        acc = jnp.zeros((COUNT_CHUNK, Q_BLOCK), jnp.float32)
        for parity, ik_ref in enumerate((iklo_ref, ikhi_ref)):
            ik = ik_ref[pl.ds(off, COUNT_CHUNK), :]
            for g in range(N_HEADS_IDX // 4):
                dots = _dot(ik, iqt_ref[g])
                for side in range(2):
                    head = 2 * (2 * g + side) + parity
                    w = wt[IDX_DIM + head:IDX_DIM + head + 1, :]
                    acc = acc + w * jnp.maximum(dots[:, side * Q_BLOCK:(side + 1) * Q_BLOCK], 0.0)
        acc = jnp.where(off + row_c <= q_pos, acc, -jnp.inf)
        bits = lax.bitcast_convert_type(acc, jnp.int32)
        key_ref[pl.ds(off, COUNT_CHUNK), :] = bits ^ ((bits >> 31) & 0x7FFFFFFF)
        return carry

    lax.fori_loop(0, n_count_chunks, score_chunk, 0)

    def count(pred):
        def body(c, cnt):
            off = pl.multiple_of(c * COUNT_CHUNK, COUNT_CHUNK)
            hit = jnp.where(pred(key_ref[pl.ds(off, COUNT_CHUNK), :], off + row_c), 1.0, 0.0)
            return cnt + jnp.sum(hit.reshape(COUNT_CHUNK // COUNT_ROWS, COUNT_ROWS, Q_BLOCK), axis=0)
        cnt = lax.fori_loop(0, n_count_chunks, body, jnp.zeros((COUNT_ROWS, Q_BLOCK), jnp.float32))
        return jnp.sum(cnt, axis=0, keepdims=True)

    k_f = jnp.float32(n_top)
    zero = jnp.zeros((1, Q_BLOCK), jnp.int32)
    thr = jnp.where(count(lambda kk, pos: kk >= zero) >= k_f, zero, INT_MIN)

    def thr_bit(i, cur):
        cand = cur + jnp.left_shift(jnp.int32(1), 30 - i)
        return jnp.where(count(lambda kk, pos: kk >= cand) >= k_f, cand, cur)

    thr = lax.fori_loop(0, 31, thr_bit, thr)
    need = k_f - count(lambda kk, pos: kk > thr)
    n_ties = count(lambda kk, pos: kk == thr)

    def tie_search():
        def tie_bit(i, cur):
            cand = cur + jnp.left_shift(jnp.int32(1), idx_bits - 1 - i)
            below = count(lambda kk, pos: (kk == thr) & (pos < cand))
            return jnp.where(below < need, cand, cur)
        return lax.fori_loop(0, idx_bits, tie_bit, zero)

    has_surplus_ties = jnp.max(jnp.where(n_ties > need, 1.0, 0.0)) > 0.0
    tie_pos = lax.cond(has_surplus_ties, tie_search, lambda: zero + (2 ** idx_bits - 1))

    acc_ref[...] = jnp.zeros(acc_ref.shape, jnp.float32)

    n_pairs = N_HEADS_ATTN // 2

    def masked_logits(c, slot):
        off = chunk_off(jnp.minimum(c, n_chunks - 1))
        kk = key_ref[pl.ds(off, KEY_CHUNK), :]
        pos = c * KEY_CHUNK + row_k
        sel = ((kk > thr) | ((kk == thr) & (pos <= tie_pos))) & (pos <= q_pos)
        bias = jnp.where(sel, 0.0, NEG_BIG)
        bias_ref[slot] = jnp.concatenate([bias, bias], axis=1)
        for j in range(n_pairs):
            kp = k_ref[pl.ds(off, KEY_CHUNK), j * LANES:(j + 1) * LANES]
            s_ref[slot, j] = _dot(kp, qt_ref[j]) + bias_ref[slot]

    def softmax_pv(c, slot, ms, ls):
        off = chunk_off(jnp.minimum(c, n_chunks - 1))
        new_ms, new_ls = [], []
        for j in range(n_pairs):
            s = s_ref[slot, j]
            m_new = jnp.maximum(ms[j], jnp.max(s, axis=0, keepdims=True))
            alpha = jnp.exp2(ms[j] - m_new)
            p = jnp.exp2(s - m_new).astype(bf)
            vt = vt_ref[j * VT_ROWS:(j + 1) * VT_ROWS, pl.ds(off, KEY_CHUNK)]
            pv = _dot(vt, p)
            new_ls.append(alpha * ls[j] + pv[LANES:LANES + 1, :])
            new_ms.append(m_new)
            lo, mid, hi = j * LANES, j * LANES + HEAD_DIM, (j + 1) * LANES
            acc_ref[lo:mid, :] = alpha[:, :Q_BLOCK] * acc_ref[lo:mid, :] + pv[:HEAD_DIM, :Q_BLOCK]
            acc_ref[mid:hi, :] = alpha[:, Q_BLOCK:] * acc_ref[mid:hi, :] + pv[HEAD_DIM:LANES, Q_BLOCK:]
        return tuple(new_ms), tuple(new_ls)

    masked_logits(0, 0)

    def attn_two_chunks(i, carry):
        ms, ls = carry
        masked_logits(2 * i + 1, 1)
        ms, ls = softmax_pv(2 * i, 0, ms, ls)
        masked_logits(2 * i + 2, 0)
        return softmax_pv(2 * i + 1, 1, ms, ls)

    init = (tuple(jnp.full((1, 2 * Q_BLOCK), NEG_BIG, jnp.float32) for _ in range(n_pairs)),
            tuple(jnp.zeros((1, 2 * Q_BLOCK), jnp.float32) for _ in range(n_pairs)))
    _, ls = lax.fori_loop(0, (n_chunks + 1) // 2, attn_two_chunks, init)
    for j in range(n_pairs):
        lo, mid, hi = j * LANES, j * LANES + HEAD_DIM, (j + 1) * LANES
        even = acc_ref[lo:mid, :] / ls[j][:, :Q_BLOCK]
        odd = acc_ref[mid:hi, :] / ls[j][:, Q_BLOCK:]
        o_ref[:, lo:hi] = jnp.concatenate([even, odd], axis=0).T.astype(o_ref.dtype)


def _dsa(iq, iw, q, iklo, ikhi, k, vt):
    b, s, _ = q.shape
    n_top = min(TOPK_MAX, s // 4)
    idx_bits = max(1, int(np.ceil(np.log2(s))))
    qblk = lambda w: pl.BlockSpec((None, Q_BLOCK, w), lambda bi, qi: (bi, qi, 0))
    keys = lambda w: pl.BlockSpec((None, s, w), lambda bi, qi: (bi, 0, 0))
    return pl.pallas_call(
        functools.partial(_dsa_kernel, n_top=n_top, idx_bits=idx_bits),
        grid=(b, s // Q_BLOCK),
        in_specs=[qblk(IDXQ_W), qblk(LANES), qblk(ATTN_W), keys(LANES), keys(LANES),
                  keys(ATTN_W), pl.BlockSpec((None, VT_ALL, s), lambda bi, qi: (bi, 0, 0))],
        out_specs=qblk(ATTN_W),
        out_shape=jax.ShapeDtypeStruct((b, s, ATTN_W), jnp.bfloat16),
        scratch_shapes=[pltpu.VMEM((s, Q_BLOCK), jnp.int32),
                        pltpu.VMEM((N_HEADS_IDX // 4, LANES, 2 * Q_BLOCK), jnp.bfloat16),
                        pltpu.VMEM((N_HEADS_ATTN // 2, LANES, 2 * Q_BLOCK), jnp.bfloat16),
                        pltpu.VMEM((2, KEY_CHUNK, 2 * Q_BLOCK), jnp.float32),
                        pltpu.VMEM((2, N_HEADS_ATTN // 2, KEY_CHUNK, 2 * Q_BLOCK), jnp.float32),
                        pltpu.VMEM((ATTN_W, Q_BLOCK), jnp.float32)],
        compiler_params=pltpu.CompilerParams(dimension_semantics=("arbitrary", "arbitrary"),
                                             vmem_limit_bytes=VMEM_LIMIT),
        name="dsa",
    )(iq, iw, q, iklo, ikhi, k, vt)


def _merge_kernel(x_ref, ya_ref, u_ref, uprev_ref, qm_ref, km_ref, vm_ref,
                  gmix_ref, wg_ref, bg_ref, wpa_ref, wpp_ref, wpm_ref, wbd_ref, psc_ref,
                  wo_ref, gffn_ref, wrh_ref, wrl_ref, br_ref,
                  h_ref, xn2_ref, eg_ref, ei_ref, cnt_ref, carry_ref, *, tiles_per_seq):
    i = pl.program_id(0)
    t = x_ref.shape[0]
    d = x_ref.shape[1]
    bf = jnp.bfloat16
    x = x_ref[...]
    xn = _rmsnorm(x, gmix_ref[...]).astype(bf)

    tile_in_seq = i % tiles_per_seq
    u = u_ref[...]
    halo = jnp.where(tile_in_seq == 0, 0.0, uprev_ref[...])
    ext = jnp.concatenate([halo, u], axis=0)
    lane_p = lax.broadcasted_iota(jnp.int32, (t, POOL_W), 1)
    pos1 = (tile_in_seq * t + lax.broadcasted_iota(jnp.int32, (t, 1), 0) + 1).astype(jnp.float32)
    pooled = None
    run, width = ext, 1
    for g, w in enumerate(POOL_WINDOWS):
        while width < w:
            run = run[width:] + run[:-width]
            width *= 2
        start = POOL_HALO + 1 - w
        mean = run[start:start + t] / jnp.minimum(pos1, float(w))
        pooled = mean if pooled is None else jnp.where(lane_p >= g * POOL_GROUP_DIM, mean, pooled)
    pooled = pooled - u
    mixed = _dot(pooled.astype(bf), wbd_ref[...]) * psc_ref[...]
    y_pool = _dot(mixed.astype(bf), wpp_ref[...])

    qm = qm_ref[...]
    probs = []
    for h in range(N_HEADS_MEM):
        s = _dot_nt(qm, km_ref[h])
        p = jnp.exp(s - jnp.max(s, axis=1, keepdims=True))
        probs.append((p / jnp.sum(p, axis=1, keepdims=True)).astype(bf))
    y_mem = _dot(probs[0], vm_ref[0])
    for h in range(1, N_HEADS_MEM):
        y_mem = y_mem + _dot(probs[h], vm_ref[h])
    y_mem = _dot(y_mem.astype(bf), wpm_ref[...])

    y_attn = _dot(ya_ref[...], wpa_ref[...])

    def gate(br):
        z = _dot(xn, wg_ref[:, br * d:(br + 1) * d]) + bg_ref[:, br * d:(br + 1) * d]
        return 1.0 / (1.0 + jnp.exp(-z))

    merged = gate(0) * y_attn + gate(1) * y_pool + gate(2) * y_mem
    h = x + _dot(merged.astype(bf), wo_ref[...])
    h_ref[...] = h
    xn2 = _rmsnorm(h, gffn_ref[...])
    _store_row_tiles(xn2_ref, xn2)

    x_hi = xn2.astype(bf)
    x_lo = (xn2 - x_hi.astype(jnp.float32)).astype(bf)
    logits = (_dot(x_hi, wrh_ref[...]) + (_dot(x_lo, wrh_ref[...]) + _dot(x_hi, wrl_ref[...]))
              + br_ref[...])
    lane_e = lax.broadcasted_iota(jnp.int32, logits.shape, 1).astype(jnp.float32)
    work = logits
    vals, onehots, ids = [], [], []
    for _ in range(TOP_K):
        mx = jnp.max(work, axis=1, keepdims=True)
        idx = jnp.min(jnp.where(work == mx, lane_e, float(LANES)), axis=1, keepdims=True)
        oh = lane_e == idx
        vals.append(mx)
        ids.append(idx)
        onehots.append(oh)
        work = jnp.where(oh, -jnp.inf, work)
    exps = [jnp.exp(v - vals[0]) for v in vals]
    denom = exps[0] + exps[1] + exps[2] + exps[3]

    @pl.when(i == 0)
    def _():
        carry_ref[...] = jnp.zeros(carry_ref.shape, jnp.float32)

    member = jnp.where(onehots[0] | onehots[1] | onehots[2] | onehots[3], 1.0, 0.0)
    r_io = lax.broadcasted_iota(jnp.int32, (t, t), 0)
    c_io = lax.broadcasted_iota(jnp.int32, (t, t), 1)
    earlier = jnp.where(c_io < r_io, 1.0, 0.0).astype(bf)
    before = _dot(earlier, member.astype(bf)) + carry_ref[...]
    carry_ref[...] = carry_ref[...] + jnp.sum(member, axis=0, keepdims=True)
    cnt_ref[...] = carry_ref[...]

    lane8 = lax.broadcasted_iota(jnp.int32, (t, 2 * TOP_K), 1)
    eg = jnp.zeros((t, 2 * TOP_K), jnp.float32)
    ei = jnp.zeros((t, 2 * TOP_K), jnp.float32)
    for j in range(TOP_K):
        rank = jnp.sum(jnp.where(onehots[j], before, 0.0), axis=1, keepdims=True)
        eg = jnp.where(lane8 == j, exps[j] / denom, eg)
        ei = jnp.where(lane8 == j, ids[j], ei)
        ei = jnp.where(lane8 == TOP_K + j, rank, ei)
    eg_ref[...] = eg
    ei_ref[...] = ei.astype(jnp.int32)


def _merge(x2, ya, u, qm, km, vm, gmix, wg, bg, wpa, wpp, wpm, wbd, psc, wo, gffn, wrh, wrl, br, seq):
    n, d = x2.shape
    t = MERGE_TILE
    tiles_per_seq = seq // t
    m = km.shape[2]
    row = lambda i: (i, 0)
    full = lambda i: (0, 0)
    per_batch = lambda i: (i // tiles_per_seq, 0, 0, 0)
    halo_blocks = t // POOL_HALO
    prev = lambda i: (jnp.maximum(i * halo_blocks - 1, 0), 0)
    consts = (gmix, wg, bg, wpa, wpp, wpm, wbd, psc, wo, gffn, wrh, wrl, br)
    pieces = d // LANES
    out_shape = (jax.ShapeDtypeStruct((n, d), jnp.float32),
                 jax.ShapeDtypeStruct((n * pieces, LANES), jnp.float32),
                 jax.ShapeDtypeStruct((n, 2 * TOP_K), jnp.float32),
                 jax.ShapeDtypeStruct((n, 2 * TOP_K), jnp.int32),
                 jax.ShapeDtypeStruct((1, LANES), jnp.float32))
    return pl.pallas_call(
        functools.partial(_merge_kernel, tiles_per_seq=tiles_per_seq),
        grid=(n // t,),
        in_specs=[pl.BlockSpec((t, d), row), pl.BlockSpec((t, ATTN_W), row),
                  pl.BlockSpec((t, POOL_W), row), pl.BlockSpec((POOL_HALO, POOL_W), prev),
                  pl.BlockSpec((t, MEM_W), row),
                  pl.BlockSpec((None, N_HEADS_MEM, m, MEM_W), per_batch),
                  pl.BlockSpec((None, N_HEADS_MEM, m, MEM_W), per_batch)]
                 + [pl.BlockSpec(c.shape, full) for c in consts],
        out_specs=(pl.BlockSpec((t, d), row), pl.BlockSpec((t * pieces, LANES), row),
                   pl.BlockSpec((t, 2 * TOP_K), row), pl.BlockSpec((t, 2 * TOP_K), row),
                   pl.BlockSpec((1, LANES), full)),
        out_shape=out_shape,
        scratch_shapes=[pltpu.VMEM((1, LANES), jnp.float32)],
        compiler_params=pltpu.CompilerParams(dimension_semantics=("arbitrary",),
                                             vmem_limit_bytes=VMEM_LIMIT),
        name="merge",
    )(x2, ya, u, u, qm, km, vm, *consts)


def _row_copy(src_ref, src_row, dst_ref, dst_row, sem):
    return pltpu.make_async_copy(_row_tile(src_ref, src_row), _row_tile(dst_ref, dst_row), sem)


def _dispatch_kernel(pstart_ref, cnt_ref, dest_ref, x_ref, xs_ref, zero_ref, sem):
    t = x_ref.shape[0] // SUBLANES
    block_tiles = ROW_BLOCK * SUBLANES

    @pl.when(pl.program_id(0) == 0)
    def _():
        zero_ref[...] = jnp.zeros(zero_ref.shape, zero_ref.dtype)

        def for_each_pad_row(act):
            def per_expert(e, carry):
                n_pad = (ROW_BLOCK - cnt_ref[e] % ROW_BLOCK) % ROW_BLOCK
                first = pstart_ref[e] + cnt_ref[e]

                def per_row(r, c):
                    act(_row_copy(zero_ref, 0, xs_ref, first + r, sem))
                    return c
                return lax.fori_loop(0, n_pad, per_row, carry)
            lax.fori_loop(0, N_EXPERTS, per_expert, 0)

        def for_each_unused_block(act):
            n_blocks = xs_ref.shape[0] // block_tiles
            first = (pstart_ref[N_EXPERTS - 1] + cnt_ref[N_EXPERTS - 1] + ROW_BLOCK - 1) // ROW_BLOCK

            def per_block(b, c):
                rows = pl.ds(pl.multiple_of(b * block_tiles, block_tiles), block_tiles)
                act(pltpu.make_async_copy(zero_ref, xs_ref.at[rows], sem))
                return c
            lax.fori_loop(first, n_blocks, per_block, 0)

        for act in (lambda cp: cp.start(), lambda cp: cp.wait()):
            for_each_pad_row(act)
            for_each_unused_block(act)

    def issue(tok, carry):
        for j in range(TOP_K):
            _row_copy(x_ref, tok, xs_ref, dest_ref[tok * TOP_K + j], sem).start(priority=j % 2)
        return carry

    lax.fori_loop(0, t, issue, 0, unroll=ISSUE_UNROLL)
    for j in range(TOP_K):
        pltpu.make_async_copy(x_ref, xs_ref.at[pl.ds(0, x_ref.shape[0])], sem).wait()


def _dispatch(pstart, cnt, dest, xn2, n_rows):
    tiles, lanes = xn2.shape
    n = tiles // SUBLANES
    t = MOVE_TILE
    return pl.pallas_call(
        _dispatch_kernel,
        grid_spec=pltpu.PrefetchScalarGridSpec(
            num_scalar_prefetch=2,
            grid=(n // t,),
            in_specs=[pl.BlockSpec((t * TOP_K,), lambda i, ps, ct: (i,), memory_space=pltpu.SMEM),
                      pl.BlockSpec((t * SUBLANES, lanes), lambda i, ps, ct: (i, 0))],
            out_specs=pl.BlockSpec(memory_space=pl.ANY),
            scratch_shapes=[pltpu.VMEM((ROW_BLOCK * SUBLANES, lanes), jnp.float32),
                            pltpu.SemaphoreType.DMA(())]),
        out_shape=jax.ShapeDtypeStruct((n_rows * SUBLANES, lanes), jnp.float32),
        compiler_params=pltpu.CompilerParams(dimension_semantics=("arbitrary",),
                                             vmem_limit_bytes=VMEM_LIMIT),
        name="dispatch",
    )(pstart, cnt, dest, xn2)


def _experts_kernel(be_ref, nused_ref, half_ref, next_ref, xs_ref, wu_hbm, bu_ref, wd_hbm, bd_ref,
                    ys_ref, wu32_ref, wd32_ref, wu16_ref, wd16_ref, sems):
    i = pl.program_id(0)
    d, f = wu16_ref.shape[0], wd16_ref.shape[0]
    half = half_ref[i]

    def weight_copies(e, into):
        return (pltpu.make_async_copy(wu_hbm.at[e], wu32_ref.at[into], sems.at[0, into]),
                pltpu.make_async_copy(wd_hbm.at[e], wd32_ref.at[into], sems.at[1, into]))

    @pl.when(i == 0)
    def _():
        for cp in weight_copies(be_ref[0], 0):
            cp.start()

    @pl.when((i == 0) | (be_ref[i] != be_ref[jnp.maximum(i - 1, 0)]))
    def _():
        for cp in weight_copies(be_ref[i], half):
            cp.wait()
        wu16_ref[...] = wu32_ref[half].astype(wu16_ref.dtype)
        wd16_ref[...] = wd32_ref[half].astype(wd16_ref.dtype)

        @pl.when(next_ref[i] >= 0)
        def _():
            for cp in weight_copies(next_ref[i], 1 - half):
                cp.start()

    @pl.when(i < nused_ref[0])
    def _():
        x = _load_row_tiles(xs_ref, ROW_BLOCK, d).astype(jnp.bfloat16)
        acts = []
        for lo in range(0, f, UP_TILE):
            glu_cols, lin_cols = slice(lo, lo + UP_TILE), slice(f + lo, f + lo + UP_TILE)
            glu = jnp.minimum(_dot(x, wu16_ref[:, glu_cols]) + bu_ref[:, glu_cols], SWIGLU_LIMIT)
            lin = jnp.clip(_dot(x, wu16_ref[:, lin_cols]) + bu_ref[:, lin_cols],
                           -SWIGLU_LIMIT, SWIGLU_LIMIT)
            act = glu * (1.0 / (1.0 + jnp.exp(-SWIGLU_ALPHA * glu))) * (lin + 1.0)
            acts.append(act.astype(jnp.bfloat16))
        act = jnp.concatenate(acts, axis=1)
        _store_row_tiles(ys_ref, _dot(act, wd16_ref[...]) + bd_ref[...])

    @pl.when(i >= nused_ref[0])
    def _():
        ys_ref[...] = jnp.zeros(ys_ref.shape, ys_ref.dtype)


def _experts(block_e, nused, block_half, block_next, xs, wu, bu, wd, bd):
    tiles, lanes = xs.shape
    e, d, f2 = wu.shape
    f = wd.shape[1]
    block = (ROW_BLOCK * SUBLANES, lanes)
    rows = lambda i, be, nu, hf, nx: (i, 0)
    used_rows = lambda i, be, nu, hf, nx: (jnp.minimum(i, nu[0] - 1), 0)
    per_e = lambda i, be, nu, hf, nx: (be[i], 0, 0)
    return pl.pallas_call(
        _experts_kernel,
        grid_spec=pltpu.PrefetchScalarGridSpec(
            num_scalar_prefetch=4,
            grid=(tiles // block[0],),
            in_specs=[pl.BlockSpec(block, used_rows),
                      pl.BlockSpec(memory_space=pl.ANY), pl.BlockSpec((None, 1, f2), per_e),
                      pl.BlockSpec(memory_space=pl.ANY), pl.BlockSpec((None, 1, d), per_e)],
            out_specs=pl.BlockSpec(block, rows),
            scratch_shapes=[pltpu.VMEM((2, d, f2), jnp.float32), pltpu.VMEM((2, f, d), jnp.float32),
                            pltpu.VMEM((d, f2), jnp.bfloat16), pltpu.VMEM((f, d), jnp.bfloat16),
                            pltpu.SemaphoreType.DMA((2, 2))]),
        out_shape=jax.ShapeDtypeStruct(xs.shape, jnp.float32),
        compiler_params=pltpu.CompilerParams(dimension_semantics=("arbitrary",),
                                             vmem_limit_bytes=VMEM_LIMIT),
        name="experts",
    )(block_e, nused, block_half, block_next, xs, wu, bu, wd, bd)


def _combine_kernel(dest_ref, dest_next_ref, h_ref, eg_ref, g_ref, ys_ref, o_ref, buf_ref, sems,
                    *, final_norm):
    i = pl.program_id(0)
    t, d = h_ref.shape
    half = i % 2

    def gather_rows(idx_ref, into):
        def issue(tok, carry):
            for j in range(TOP_K):
                _row_copy(ys_ref, idx_ref[tok * TOP_K + j], buf_ref.at[into, j], tok,
                          sems.at[into]).start(priority=j % 2)
            return carry
        lax.fori_loop(0, t, issue, 0, unroll=ISSUE_UNROLL)

    @pl.when(i == 0)
    def _():
        gather_rows(dest_ref, 0)

    @pl.when(i + 1 < pl.num_programs(0))
    def _():
        gather_rows(dest_next_ref, 1 - half)

    for j in range(TOP_K):
        pltpu.make_async_copy(ys_ref.at[pl.ds(0, t * SUBLANES)], buf_ref.at[half, j],
                              sems.at[half]).wait()
    eg = eg_ref[...]
    h = h_ref[...]
    for j in range(TOP_K):
        h = h + eg[:, j:j + 1] * _load_row_tiles(buf_ref.at[half, j], t, d)
    o_ref[...] = _rmsnorm(h, g_ref[...]) if final_norm else h


def _combine(dest, h, eg, g, ys, final_norm):
    n, d = h.shape
    t = MOVE_TILE
    return pl.pallas_call(
        functools.partial(_combine_kernel, final_norm=final_norm),
        grid=(n // t,),
        in_specs=[pl.BlockSpec((t * TOP_K,), lambda i: (i,), memory_space=pltpu.SMEM),
                  pl.BlockSpec((t * TOP_K,), lambda i: (jnp.minimum(i + 1, n // t - 1),),
                               memory_space=pltpu.SMEM),
                  pl.BlockSpec((t, d), lambda i: (i, 0)),
                  pl.BlockSpec((t, 2 * TOP_K), lambda i: (i, 0)),
                  pl.BlockSpec((1, d), lambda i: (0, 0)),
                  pl.BlockSpec(memory_space=pl.ANY)],
        out_specs=pl.BlockSpec((t, d), lambda i: (i, 0)),
        scratch_shapes=[pltpu.VMEM((2, TOP_K, t * SUBLANES, ys.shape[1]), jnp.float32),
                        pltpu.SemaphoreType.DMA((2,))],
        out_shape=jax.ShapeDtypeStruct((n, d), jnp.float32),
        compiler_params=pltpu.CompilerParams(dimension_semantics=("arbitrary",),
                                             vmem_limit_bytes=VMEM_LIMIT),
        name="combine",
    )(dest, dest, h, eg, g, ys)


def _rope_tables(seq):
    half = ROPE_DIM // 2
    inv = jnp.power(jnp.float32(ROPE_THETA), -jnp.arange(half, dtype=jnp.float32) / half)
    ang = jnp.arange(seq, dtype=jnp.float32)[:, None] * inv[None, :]
    cos, sin = jnp.cos(ang), jnp.sin(ang)
    pad = HEAD_DIM - ROPE_DIM
    one, zero = jnp.ones((seq, pad), jnp.float32), jnp.zeros((seq, pad), jnp.float32)
    zh = jnp.zeros((seq, half), jnp.float32)
    reps = LANES // HEAD_DIM
    c = jnp.tile(jnp.concatenate([cos, cos, one], axis=1), (1, reps))
    a = jnp.tile(jnp.concatenate([-sin, zh, zero], axis=1), (1, reps))
    b = jnp.tile(jnp.concatenate([zh, sin, zero], axis=1), (1, reps))
    return c, a, b


def _layer(h, mem, p, tables, final_g):
    b, s, d = h.shape
    n = b * s
    bf = jnp.bfloat16
    x2 = h.reshape(n, d)
    w_in = p["w_in"]
    o_ik = 3 * ATTN_W + IDXQ_W
    o_u = o_ik + IDX_DIM + N_HEADS_IDX
    wa = w_in[:, :o_ik].astype(bf)
    wb = jnp.pad(w_in[:, o_ik:o_u], ((0, 0), (0, LANES - (o_u - o_ik)))).astype(bf)
    wc = w_in[:, o_u:].astype(bf)
    q, k, vt, iq, iklo, ikhi, iw, u, qm = _in_proj(x2, p["norm_mix_g"][None], wa, wb, wc, *tables, b, s)
    km, vm = _mem_kv(mem, p["mem_norm_g"][None], p["w_mem_kv"].astype(bf))
    b3 = lambda a: a.reshape(b, s, a.shape[-1])
    ya = _dsa(b3(iq), b3(iw), b3(q), b3(iklo), b3(ikhi), b3(k), vt).reshape(n, ATTN_W)

    wbd = jnp.zeros((POOL_W, POOL_W), jnp.float32)
    for g in range(N_POOL_GROUPS):
        lo = g * POOL_GROUP_DIM
        wbd = wbd.at[lo:lo + POOL_GROUP_DIM, lo:lo + POOL_GROUP_DIM].set(p["pool_w_group"][g])
    wr = jnp.pad(p["w_router"], ((0, 0), (0, LANES - N_EXPERTS)))
    wr_hi = wr.astype(bf)
    br = jnp.pad(p["b_router"], (0, LANES - N_EXPERTS), constant_values=NEG_BIG)[None]
    hmid, xn2, eg, ei, counts = _merge(
        x2, ya, u, qm, km, vm, p["norm_mix_g"][None], p["w_gate"].astype(bf), p["b_gate"][None],
        p["w_proj_attn"].astype(bf), p["w_proj_pool"].astype(bf), p["w_proj_mem"].astype(bf),
        wbd.astype(bf), p["pool_scale"][None], p["w_out"].astype(bf), p["norm_ffn_g"][None],
        wr_hi, (wr - wr_hi.astype(jnp.float32)).astype(bf), br, s)

    cnt = counts[0, :N_EXPERTS].astype(jnp.int32)
    padded = ((cnt + ROW_BLOCK - 1) // ROW_BLOCK) * ROW_BLOCK
    pend = jnp.cumsum(padded)
    pstart = (pend - padded).astype(jnp.int32)
    n_rows = n * TOP_K + N_EXPERTS * ROW_BLOCK
    n_blocks = n_rows // ROW_BLOCK
    nused = (pend[-1] // ROW_BLOCK).astype(jnp.int32)
    blk = jnp.minimum(jnp.arange(n_blocks, dtype=jnp.int32), nused - 1) * ROW_BLOCK
    block_e = jnp.sum(blk[:, None] >= pend[None, :], axis=1).astype(jnp.int32)
    block_e = jnp.minimum(block_e, N_EXPERTS - 1)
    has_rows = cnt > 0
    run_index = jnp.cumsum(has_rows) - 1
    ids = jnp.arange(N_EXPERTS, dtype=jnp.int32)
    later = jnp.where(has_rows[None, :] & (ids[None, :] > ids[:, None]), ids[None, :], N_EXPERTS)
    next_used = jnp.min(later, axis=1)
    next_used = jnp.where(next_used < N_EXPERTS, next_used, -1).astype(jnp.int32)
    block_half = (run_index[block_e] % 2).astype(jnp.int32)
    block_next = next_used[block_e]
    chosen =ei[:, :TOP_K, None] == jnp.arange(N_EXPERTS, dtype=jnp.int32)
    dest = (jnp.sum(jnp.where(chosen, pstart, 0), axis=-1) + ei[:, TOP_K:]).reshape(-1)

    xs = _dispatch(pstart, cnt, dest, xn2, n_rows)
    ys = _experts(block_e, nused[None], block_half, block_next, xs, p["w_up"], p["b_up"][:, None, :],
                  p["w_down"], p["b_down"][:, None, :])
    g = p["norm_ffn_g"][None] if final_g is None else final_g[None]
    return _combine(dest, hmid, eg, g, ys, final_g is not None).reshape(b, s, d)


def kernel(x, mem, norm_mix_g, w_in, w_gate, b_gate, w_proj_attn, w_proj_pool, w_proj_mem,
           pool_w_group, pool_scale, mem_norm_g, w_mem_kv, w_out, norm_ffn_g,
           w_router, b_router, w_up, b_up, w_down, b_down, norm_final_g):
    b, s, d = x.shape
    assert s % IN_TILE == 0 and s % COUNT_CHUNK == 0 and (b * s) % MOVE_TILE == 0
    assert d == LANES * SUBLANES, "the row-tile layout maps one model row onto one (8, 128) tile"
    stacked = dict(norm_mix_g=norm_mix_g, w_in=w_in, w_gate=w_gate, b_gate=b_gate,
                   w_proj_attn=w_proj_attn, w_proj_pool=w_proj_pool, w_proj_mem=w_proj_mem,
                   pool_w_group=pool_w_group, pool_scale=pool_scale, mem_norm_g=mem_norm_g,
                   w_mem_kv=w_mem_kv, w_out=w_out, norm_ffn_g=norm_ffn_g, w_router=w_router,
                   b_router=b_router, w_up=w_up, b_up=b_up, w_down=w_down, b_down=b_down)
    depth = w_in.shape[0]
    tables = _rope_tables(s)
    h = x
    for l in range(depth):
        p = {name: val[l] for name, val in stacked.items()}
        h = _layer(h, mem, p, tables, norm_final_g if l == depth - 1 else None)
    return h
```

```python
import functools

import jax
import jax.numpy as jnp
import numpy as np
from jax import lax
from jax.experimental import pallas as pl
from jax.experimental.pallas import tpu as pltpu

N_HEADS_ATTN = 8
HEAD_DIM = 64
ROPE_DIM = HEAD_DIM // 4
ROPE_THETA = 500000.0
N_HEADS_IDX = 8
IDX_DIM = 64
TOPK_MAX = 256
N_POOL_GROUPS = 4
POOL_GROUP_DIM = 64
POOL_WINDOWS = (2, 4, 8, 16)
POOL_HALO = 16
N_HEADS_MEM = 4
N_BRANCHES = 3
N_EXPERTS = 32
TOP_K = 4
SWIGLU_ALPHA = 1.702
SWIGLU_LIMIT = 7.0
EPS = 1e-6

ATTN_W = N_HEADS_ATTN * HEAD_DIM
POOL_W = N_POOL_GROUPS * POOL_GROUP_DIM
MEM_W = N_HEADS_MEM * HEAD_DIM
IDXQ_W = N_HEADS_IDX * IDX_DIM

LANES = 128
SUBLANES = 8
VMEM_LIMIT = 56 * 1024 * 1024

IN_TILE = 512
Q_BLOCK = 128
KEY_CHUNK = 256
COUNT_CHUNK = 512
COUNT_ROWS = 32
MERGE_TILE = 256
ROW_BLOCK = 256
UP_TILE = 256
MOVE_TILE = 256
ISSUE_UNROLL = 4

VT_ROWS = LANES + 16
VT_ALL = (N_HEADS_ATTN // 2) * VT_ROWS
LOG2_E = 1.4426950408889634

NEG_BIG = -1e30
INT_MIN = -2 ** 31


def _dot(a, b):
    return jnp.dot(a, b, preferred_element_type=jnp.float32)


def _dot_nt(a, b):
    return lax.dot_general(a, b, (((1,), (1,)), ((), ())), preferred_element_type=jnp.float32)


def _rmsnorm(x, g):
    return x * lax.rsqrt(jnp.mean(x * x, axis=-1, keepdims=True) + EPS) * g


def _store_row_tiles(ref, x):
    rows, width = x.shape
    for j in range(width // LANES):
        ref[pl.ds(j, rows, stride=width // LANES), :] = x[:, j * LANES:(j + 1) * LANES]


def _load_row_tiles(ref, rows, width):
    pieces = width // LANES
    return jnp.concatenate([ref[pl.ds(j, rows, stride=pieces), :] for j in range(pieces)], axis=1)


def _row_tile(ref, row):
    return ref.at[pl.ds(pl.multiple_of(row * SUBLANES, SUBLANES), SUBLANES)]


def _rope128(x, c, a, b):
    return x * c + pltpu.roll(x, LANES - ROPE_DIM // 2, 1) * a + pltpu.roll(x, ROPE_DIM // 2, 1) * b


def _in_proj_kernel(x_ref, g_ref, wa_ref, wb_ref, wc_ref, rc_ref, ra_ref, rb_ref,
                    q_ref, k_ref, vt_ref, iq_ref, iklo_ref, ikhi_ref, iw_ref, u_ref, qm_ref):
    xn = _rmsnorm(x_ref[...], g_ref[...]).astype(jnp.bfloat16)
    rc, ra, rb = rc_ref[...], ra_ref[...], rb_ref[...]
    pa = _dot(xn, wa_ref[...])
    for seg, (ref, scale) in enumerate(((q_ref, LOG2_E * HEAD_DIM ** -0.5), (k_ref, None),
                                        (None, None), (iq_ref, IDX_DIM ** -0.5))):
        for c in range(ATTN_W // LANES):
            lo = seg * ATTN_W + c * LANES
            blk = pa[:, lo:lo + LANES]
            if ref is None:
                vt_ref[c * VT_ROWS:c * VT_ROWS + LANES, :] = blk.T.astype(vt_ref.dtype)
                vt_ref[c * VT_ROWS + LANES:(c + 1) * VT_ROWS, :] = jnp.ones(
                    (VT_ROWS - LANES, blk.shape[0]), vt_ref.dtype)
                continue
            blk = _rope128(blk, rc, ra, rb)
            if scale is not None:
                blk = blk * scale
            ref[:, c * LANES:(c + 1) * LANES] = blk.astype(ref.dtype)
    pb = _dot(xn, wb_ref[...])
    lane = lax.broadcasted_iota(jnp.int32, pb.shape, 1)
    ik = jnp.where(lane < IDX_DIM, _rope128(pb, rc, ra, rb), 0.0)
    iklo_ref[...] = ik.astype(iklo_ref.dtype)
    ikhi_ref[...] = pltpu.roll(ik, IDX_DIM, 1).astype(ikhi_ref.dtype)
    iw_ref[...] = pb * (N_HEADS_IDX ** -0.5)
    pc = _dot(xn, wc_ref[...])
    u_ref[...] = pc[:, :POOL_W]
    qm_ref[...] = (pc[:, POOL_W:] * (HEAD_DIM ** -0.5)).astype(qm_ref.dtype)


def _in_proj(x2, g, wa, wb, wc, rc, ra, rb, batch, seq):
    n, d = x2.shape
    t = IN_TILE
    tiles_per_seq = seq // t
    row = lambda i: (i, 0)
    full = lambda i: (0, 0)
    pos = lambda i: (i % tiles_per_seq, 0)
    bf = jnp.bfloat16
    flat = lambda w, dt: (jax.ShapeDtypeStruct((n, w), dt), pl.BlockSpec((t, w), row))
    outs = (flat(ATTN_W, bf), flat(ATTN_W, bf),
            (jax.ShapeDtypeStruct((batch, VT_ALL, seq), bf),
             pl.BlockSpec((None, VT_ALL, t), lambda i: (i // tiles_per_seq, 0, i % tiles_per_seq))),
            flat(IDXQ_W, bf), flat(LANES, bf), flat(LANES, bf), flat(LANES, jnp.float32),
            flat(POOL_W, jnp.float32), flat(MEM_W, bf))
    return pl.pallas_call(
        _in_proj_kernel,
        grid=(n // t,),
        in_specs=[pl.BlockSpec((t, d), row), pl.BlockSpec((1, d), full),
                  pl.BlockSpec(wa.shape, full), pl.BlockSpec(wb.shape, full),
                  pl.BlockSpec(wc.shape, full),
                  pl.BlockSpec((t, LANES), pos), pl.BlockSpec((t, LANES), pos),
                  pl.BlockSpec((t, LANES), pos)],
        out_specs=tuple(o[1] for o in outs),
        out_shape=tuple(o[0] for o in outs),
        compiler_params=pltpu.CompilerParams(dimension_semantics=("arbitrary",),
                                             vmem_limit_bytes=VMEM_LIMIT),
        name="in_proj",
    )(x2, g, wa, wb, wc, rc, ra, rb)


def _mem_kv_kernel(mem_ref, g_ref, w_ref, km_ref, vm_ref):
    mn = _rmsnorm(mem_ref[...], g_ref[...]).astype(jnp.bfloat16)
    kv = _dot(mn, w_ref[...])
    km, vm = kv[:, :MEM_W], kv[:, MEM_W:]
    lane = lax.broadcasted_iota(jnp.int32, km.shape, 1)
    for h in range(N_HEADS_MEM):
        in_head = (lane >= h * HEAD_DIM) & (lane < (h + 1) * HEAD_DIM)
        km_ref[h] = jnp.where(in_head, km, 0.0).astype(km_ref.dtype)
        vm_ref[h] = jnp.where(in_head, vm, 0.0).astype(vm_ref.dtype)


def _mem_kv(mem, g, w):
    b, m, d = mem.shape
    out = jax.ShapeDtypeStruct((b, N_HEADS_MEM, m, MEM_W), jnp.bfloat16)
    return pl.pallas_call(
        _mem_kv_kernel,
        grid=(b,),
        in_specs=[pl.BlockSpec((None, m, d), lambda i: (i, 0, 0)),
                  pl.BlockSpec((1, d), lambda i: (0, 0)),
                  pl.BlockSpec(w.shape, lambda i: (0, 0))],
        out_specs=(pl.BlockSpec((None, N_HEADS_MEM, m, MEM_W), lambda i: (i, 0, 0, 0)),
                   pl.BlockSpec((None, N_HEADS_MEM, m, MEM_W), lambda i: (i, 0, 0, 0))),
        out_shape=(out, out),
        compiler_params=pltpu.CompilerParams(dimension_semantics=("arbitrary",),
                                             vmem_limit_bytes=VMEM_LIMIT),
        name="mem_kv",
    )(mem, g, w)


def _dsa_kernel(iq_ref, iw_ref, q_ref, iklo_ref, ikhi_ref, k_ref, vt_ref, o_ref,
                key_ref, iqt_ref, qt_ref, bias_ref, s_ref, acc_ref, *, n_top, idx_bits):
    qb = pl.program_id(1)
    n_chunks = (qb * Q_BLOCK + Q_BLOCK + KEY_CHUNK - 1) // KEY_CHUNK
    n_count_chunks = (qb * Q_BLOCK + Q_BLOCK + COUNT_CHUNK - 1) // COUNT_CHUNK
    q_pos = qb * Q_BLOCK + lax.broadcasted_iota(jnp.int32, (1, Q_BLOCK), 1)
    row_k = lax.broadcasted_iota(jnp.int32, (KEY_CHUNK, 1), 0)
    row_c = lax.broadcasted_iota(jnp.int32, (COUNT_CHUNK, 1), 0)
    row_d = lax.broadcasted_iota(jnp.int32, (LANES, Q_BLOCK), 0)
    bf = jnp.bfloat16

    def chunk_off(c):
        return pl.multiple_of(c * KEY_CHUNK, KEY_CHUNK)

    wt = iw_ref[...].T
    for j in range(N_HEADS_IDX // 2):
        iqt_ref[j // 2, :, (j % 2) * Q_BLOCK:(j % 2 + 1) * Q_BLOCK] = (
            iq_ref[:, j * LANES:(j + 1) * LANES].astype(jnp.float32).T.astype(bf))
    for j in range(N_HEADS_ATTN // 2):
        pair_t = q_ref[:, j * LANES:(j + 1) * LANES].astype(jnp.float32).T
        qt_ref[j, :, :Q_BLOCK] = jnp.where(row_d < HEAD_DIM, pair_t, 0.0).astype(bf)
        qt_ref[j, :, Q_BLOCK:] = jnp.where(row_d >= HEAD_DIM, pair_t, 0.0).astype(bf)

    def score_chunk(c, carry):
        off = pl.multiple_of(c * COUNT_CHUNK, COUNT_CHUNK)
        acc = jnp.zeros((COUNT_CHUNK, Q_BLOCK), jnp.float32)
        for parity, ik_ref in enumerate((iklo_ref, ikhi_ref)):
            ik = ik_ref[pl.ds(off, COUNT_CHUNK), :]
            for g in range(N_HEADS_IDX // 4):
                dots = _dot(ik, iqt_ref[g])
                for side in range(2):
                    head = 2 * (2 * g + side) + parity
                    w = wt[IDX_DIM + head:IDX_DIM + head + 1, :]
                    acc = acc + w * jnp.maximum(dots[:, side * Q_BLOCK:(side + 1) * Q_BLOCK], 0.0)
        acc = jnp.where(off + row_c <= q_pos, acc, -jnp.inf)
        bits = lax.bitcast_convert_type(acc, jnp.int32)
        key_ref[pl.ds(off, COUNT_CHUNK), :] = bits ^ ((bits >> 31) & 0x7FFFFFFF)
        return carry

    lax.fori_loop(0, n_count_chunks, score_chunk, 0)

    def count(pred):
        def body(c, cnt):
            off = pl.multiple_of(c * COUNT_CHUNK, COUNT_CHUNK)
            hit = jnp.where(pred(key_ref[pl.ds(off, COUNT_CHUNK), :], off + row_c), 1.0, 0.0)
            return cnt + jnp.sum(hit.reshape(COUNT_CHUNK // COUNT_ROWS, COUNT_ROWS, Q_BLOCK), axis=0)
        cnt = lax.fori_loop(0, n_count_chunks, body, jnp.zeros((COUNT_ROWS, Q_BLOCK), jnp.float32))
        return jnp.sum(cnt, axis=0, keepdims=True)

    k_f = jnp.float32(n_top)
    zero = jnp.zeros((1, Q_BLOCK), jnp.int32)
    thr = jnp.where(count(lambda kk, pos: kk >= zero) >= k_f, zero, INT_MIN)

    def thr_bit(i, cur):
        cand = cur + jnp.left_shift(jnp.int32(1), 30 - i)
        return jnp.where(count(lambda kk, pos: kk >= cand) >= k_f, cand, cur)

    thr = lax.fori_loop(0, 31, thr_bit, thr)
    need = k_f - count(lambda kk, pos: kk > thr)
    n_ties = count(lambda kk, pos: kk == thr)

    def tie_search():
        def tie_bit(i, cur):
            cand = cur + jnp.left_shift(jnp.int32(1), idx_bits - 1 - i)
            below = count(lambda kk, pos: (kk == thr) & (pos < cand))
            return jnp.where(below < need, cand, cur)
        return lax.fori_loop(0, idx_bits, tie_bit, zero)

    has_surplus_ties = jnp.max(jnp.where(n_ties > need, 1.0, 0.0)) > 0.0
    tie_pos = lax.cond(has_surplus_ties, tie_search, lambda: zero + (2 ** idx_bits - 1))

    acc_ref[...] = jnp.zeros(acc_ref.shape, jnp.float32)

    n_pairs = N_HEADS_ATTN // 2

    def masked_logits(c, slot):
        off = chunk_off(jnp.minimum(c, n_chunks - 1))
        kk = key_ref[pl.ds(off, KEY_CHUNK), :]
        pos = c * KEY_CHUNK + row_k
        sel = ((kk > thr) | ((kk == thr) & (pos <= tie_pos))) & (pos <= q_pos)
        bias = jnp.where(sel, 0.0, NEG_BIG)
        bias_ref[slot] = jnp.concatenate([bias, bias], axis=1)
        for j in range(n_pairs):
            kp = k_ref[pl.ds(off, KEY_CHUNK), j * LANES:(j + 1) * LANES]
            s_ref[slot, j] = _dot(kp, qt_ref[j]) + bias_ref[slot]

    def softmax_pv(c, slot, ms, ls):
        off = chunk_off(jnp.minimum(c, n_chunks - 1))
        new_ms, new_ls = [], []
        for j in range(n_pairs):
            s = s_ref[slot, j]
            m_new = jnp.maximum(ms[j], jnp.max(s, axis=0, keepdims=True))
            alpha = jnp.exp2(ms[j] - m_new)
            p = jnp.exp2(s - m_new).astype(bf)
            vt = vt_ref[j * VT_ROWS:(j + 1) * VT_ROWS, pl.ds(off, KEY_CHUNK)]
            pv = _dot(vt, p)
            new_ls.append(alpha * ls[j] + pv[LANES:LANES + 1, :])
            new_ms.append(m_new)
            lo, mid, hi = j * LANES, j * LANES + HEAD_DIM, (j + 1) * LANES
            acc_ref[lo:mid, :] = alpha[:, :Q_BLOCK] * acc_ref[lo:mid, :] + pv[:HEAD_DIM, :Q_BLOCK]
            acc_ref[mid:hi, :] = alpha[:, Q_BLOCK:] * acc_ref[mid:hi, :] + pv[HEAD_DIM:LANES, Q_BLOCK:]
        return tuple(new_ms), tuple(new_ls)

    masked_logits(0, 0)

    def attn_two_chunks(i, carry):
        ms, ls = carry
        masked_logits(2 * i + 1, 1)
        ms, ls = softmax_pv(2 * i, 0, ms, ls)
        masked_logits(2 * i + 2, 0)
        return softmax_pv(2 * i + 1, 1, ms, ls)

    init = (tuple(jnp.full((1, 2 * Q_BLOCK), NEG_BIG, jnp.float32) for _ in range(n_pairs)),
            tuple(jnp.zeros((1, 2 * Q_BLOCK), jnp.float32) for _ in range(n_pairs)))
    _, ls = lax.fori_loop(0, (n_chunks + 1) // 2, attn_two_chunks, init)
    for j in range(n_pairs):
        lo, mid, hi = j * LANES, j * LANES + HEAD_DIM, (j + 1) * LANES
        even = acc_ref[lo:mid, :] / ls[j][:, :Q_BLOCK]
        odd = acc_ref[mid:hi, :] / ls[j][:, Q_BLOCK:]
        o_ref[:, lo:hi] = jnp.concatenate([even, odd], axis=0).T.astype(o_ref.dtype)


def _dsa(iq, iw, q, iklo, ikhi, k, vt):
    b, s, _ = q.shape
    n_top = min(TOPK_MAX, s // 4)
    idx_bits = max(1, int(np.ceil(np.log2(s))))
    qblk = lambda w: pl.BlockSpec((None, Q_BLOCK, w), lambda bi, qi: (bi, qi, 0))
    keys = lambda w: pl.BlockSpec((None, s, w), lambda bi, qi: (bi, 0, 0))
    return pl.pallas_call(
        functools.partial(_dsa_kernel, n_top=n_top, idx_bits=idx_bits),
        grid=(b, s // Q_BLOCK),
        in_specs=[qblk(IDXQ_W), qblk(LANES), qblk(ATTN_W), keys(LANES), keys(LANES),
                  keys(ATTN_W), pl.BlockSpec((None, VT_ALL, s), lambda bi, qi: (bi, 0, 0))],
        out_specs=qblk(ATTN_W),
        out_shape=jax.ShapeDtypeStruct((b, s, ATTN_W), jnp.bfloat16),
        scratch_shapes=[pltpu.VMEM((s, Q_BLOCK), jnp.int32),
                        pltpu.VMEM((N_HEADS_IDX // 4, LANES, 2 * Q_BLOCK), jnp.bfloat16),
                        pltpu.VMEM((N_HEADS_ATTN // 2, LANES, 2 * Q_BLOCK), jnp.bfloat16),
                        pltpu.VMEM((2, KEY_CHUNK, 2 * Q_BLOCK), jnp.float32),
                        pltpu.VMEM((2, N_HEADS_ATTN // 2, KEY_CHUNK, 2 * Q_BLOCK), jnp.float32),
                        pltpu.VMEM((ATTN_W, Q_BLOCK), jnp.float32)],
        compiler_params=pltpu.CompilerParams(dimension_semantics=("arbitrary", "arbitrary"),
                                             vmem_limit_bytes=VMEM_LIMIT),
        name="dsa",
    )(iq, iw, q, iklo, ikhi, k, vt)


def _merge_kernel(x_ref, ya_ref, u_ref, uprev_ref, qm_ref, km_ref, vm_ref,
                  gmix_ref, wg_ref, bg_ref, wpa_ref, wpp_ref, wpm_ref, wbd_ref, psc_ref,
                  wo_ref, gffn_ref, wrh_ref, wrl_ref, br_ref,
                  h_ref, xn2_ref, eg_ref, ei_ref, cnt_ref, carry_ref, *, tiles_per_seq):
    i = pl.program_id(0)
    t = x_ref.shape[0]
    d = x_ref.shape[1]
    bf = jnp.bfloat16
    x = x_ref[...]
    xn = _rmsnorm(x, gmix_ref[...]).astype(bf)

    tile_in_seq = i % tiles_per_seq
    u = u_ref[...]
    halo = jnp.where(tile_in_seq == 0, 0.0, uprev_ref[...])
    ext = jnp.concatenate([halo, u], axis=0)
    lane_p = lax.broadcasted_iota(jnp.int32, (t, POOL_W), 1)
    pos1 = (tile_in_seq * t + lax.broadcasted_iota(jnp.int32, (t, 1), 0) + 1).astype(jnp.float32)
    pooled = None
    run, width = ext, 1
    for g, w in enumerate(POOL_WINDOWS):
        while width < w:
            run = run[width:] + run[:-width]
            width *= 2
        start = POOL_HALO + 1 - w
        mean = run[start:start + t] / jnp.minimum(pos1, float(w))
        pooled = mean if pooled is None else jnp.where(lane_p >= g * POOL_GROUP_DIM, mean, pooled)
    pooled = pooled - u
    mixed = _dot(pooled.astype(bf), wbd_ref[...]) * psc_ref[...]
    y_pool = _dot(mixed.astype(bf), wpp_ref[...])

    qm = qm_ref[...]
    probs = []
    for h in range(N_HEADS_MEM):
        s = _dot_nt(qm, km_ref[h])
        p = jnp.exp(s - jnp.max(s, axis=1, keepdims=True))
        probs.append((p / jnp.sum(p, axis=1, keepdims=True)).astype(bf))
    y_mem = _dot(probs[0], vm_ref[0])
    for h in range(1, N_HEADS_MEM):
        y_mem = y_mem + _dot(probs[h], vm_ref[h])
    y_mem = _dot(y_mem.astype(bf), wpm_ref[...])

    y_attn = _dot(ya_ref[...], wpa_ref[...])

    def gate(br):
        z = _dot(xn, wg_ref[:, br * d:(br + 1) * d]) + bg_ref[:, br * d:(br + 1) * d]
        return 1.0 / (1.0 + jnp.exp(-z))

    merged = gate(0) * y_attn + gate(1) * y_pool + gate(2) * y_mem
    h = x + _dot(merged.astype(bf), wo_ref[...])
    h_ref[...] = h
    xn2 = _rmsnorm(h, gffn_ref[...])
    _store_row_tiles(xn2_ref, xn2)

    x_hi = xn2.astype(bf)
    x_lo = (xn2 - x_hi.astype(jnp.float32)).astype(bf)
    logits = (_dot(x_hi, wrh_ref[...]) + (_dot(x_lo, wrh_ref[...]) + _dot(x_hi, wrl_ref[...]))
              + br_ref[...])
    lane_e = lax.broadcasted_iota(jnp.int32, logits.shape, 1).astype(jnp.float32)
    work = logits
    vals, onehots, ids = [], [], []
    for _ in range(TOP_K):
        mx = jnp.max(work, axis=1, keepdims=True)
        idx = jnp.min(jnp.where(work == mx, lane_e, float(LANES)), axis=1, keepdims=True)
        oh = lane_e == idx
        vals.append(mx)
        ids.append(idx)
        onehots.append(oh)
        work = jnp.where(oh, -jnp.inf, work)
    exps = [jnp.exp(v - vals[0]) for v in vals]
    denom = exps[0] + exps[1] + exps[2] + exps[3]

    @pl.when(i == 0)
    def _():
        carry_ref[...] = jnp.zeros(carry_ref.shape, jnp.float32)

    member = jnp.where(onehots[0] | onehots[1] | onehots[2] | onehots[3], 1.0, 0.0)
    r_io = lax.broadcasted_iota(jnp.int32, (t, t), 0)
    c_io = lax.broadcasted_iota(jnp.int32, (t, t), 1)
    earlier = jnp.where(c_io < r_io, 1.0, 0.0).astype(bf)
    before = _dot(earlier, member.astype(bf)) + carry_ref[...]
    carry_ref[...] = carry_ref[...] + jnp.sum(member, axis=0, keepdims=True)
    cnt_ref[...] = carry_ref[...]

    lane8 = lax.broadcasted_iota(jnp.int32, (t, 2 * TOP_K), 1)
    eg = jnp.zeros((t, 2 * TOP_K), jnp.float32)
    ei = jnp.zeros((t, 2 * TOP_K), jnp.float32)
    for j in range(TOP_K):
        rank = jnp.sum(jnp.where(onehots[j], before, 0.0), axis=1, keepdims=True)
        eg = jnp.where(lane8 == j, exps[j] / denom, eg)
        ei = jnp.where(lane8 == j, ids[j], ei)
        ei = jnp.where(lane8 == TOP_K + j, rank, ei)
    eg_ref[...] = eg
    ei_ref[...] = ei.astype(jnp.int32)


def _merge(x2, ya, u, qm, km, vm, gmix, wg, bg, wpa, wpp, wpm, wbd, psc, wo, gffn, wrh, wrl, br, seq):
    n, d = x2.shape
    t = MERGE_TILE
    tiles_per_seq = seq // t
    m = km.shape[2]
    row = lambda i: (i, 0)
    full = lambda i: (0, 0)
    per_batch = lambda i: (i // tiles_per_seq, 0, 0, 0)
    halo_blocks = t // POOL_HALO
    prev = lambda i: (jnp.maximum(i * halo_blocks - 1, 0), 0)
    consts = (gmix, wg, bg, wpa, wpp, wpm, wbd, psc, wo, gffn, wrh, wrl, br)
    pieces = d // LANES
    out_shape = (jax.ShapeDtypeStruct((n, d), jnp.float32),
                 jax.ShapeDtypeStruct((n * pieces, LANES), jnp.float32),
                 jax.ShapeDtypeStruct((n, 2 * TOP_K), jnp.float32),
                 jax.ShapeDtypeStruct((n, 2 * TOP_K), jnp.int32),
                 jax.ShapeDtypeStruct((1, LANES), jnp.float32))
    return pl.pallas_call(
        functools.partial(_merge_kernel, tiles_per_seq=tiles_per_seq),
        grid=(n // t,),
        in_specs=[pl.BlockSpec((t, d), row), pl.BlockSpec((t, ATTN_W), row),
                  pl.BlockSpec((t, POOL_W), row), pl.BlockSpec((POOL_HALO, POOL_W), prev),
                  pl.BlockSpec((t, MEM_W), row),
                  pl.BlockSpec((None, N_HEADS_MEM, m, MEM_W), per_batch),
                  pl.BlockSpec((None, N_HEADS_MEM, m, MEM_W), per_batch)]
                 + [pl.BlockSpec(c.shape, full) for c in consts],
        out_specs=(pl.BlockSpec((t, d), row), pl.BlockSpec((t * pieces, LANES), row),
                   pl.BlockSpec((t, 2 * TOP_K), row), pl.BlockSpec((t, 2 * TOP_K), row),
                   pl.BlockSpec((1, LANES), full)),
        out_shape=out_shape,
        scratch_shapes=[pltpu.VMEM((1, LANES), jnp.float32)],
        compiler_params=pltpu.CompilerParams(dimension_semantics=("arbitrary",),
                                             vmem_limit_bytes=VMEM_LIMIT),
        name="merge",
    )(x2, ya, u, u, qm, km, vm, *consts)


def _row_copy(src_ref, src_row, dst_ref, dst_row, sem):
    return pltpu.make_async_copy(_row_tile(src_ref, src_row), _row_tile(dst_ref, dst_row), sem)


def _dispatch_kernel(pstart_ref, cnt_ref, dest_ref, x_ref, xs_ref, zero_ref, sem):
    t = x_ref.shape[0] // SUBLANES
    block_tiles = ROW_BLOCK * SUBLANES

    @pl.when(pl.program_id(0) == 0)
    def _():
        zero_ref[...] = jnp.zeros(zero_ref.shape, zero_ref.dtype)

        def for_each_pad_row(act):
            def per_expert(e, carry):
                n_pad = (ROW_BLOCK - cnt_ref[e] % ROW_BLOCK) % ROW_BLOCK
                first = pstart_ref[e] + cnt_ref[e]

                def per_row(r, c):
                    act(_row_copy(zero_ref, 0, xs_ref, first + r, sem))
                    return c
                return lax.fori_loop(0, n_pad, per_row, carry)
            lax.fori_loop(0, N_EXPERTS, per_expert, 0)

        def for_each_unused_block(act):
            n_blocks = xs_ref.shape[0] // block_tiles
            first = (pstart_ref[N_EXPERTS - 1] + cnt_ref[N_EXPERTS - 1] + ROW_BLOCK - 1) // ROW_BLOCK

            def per_block(b, c):
                rows = pl.ds(pl.multiple_of(b * block_tiles, block_tiles), block_tiles)
                act(pltpu.make_async_copy(zero_ref, xs_ref.at[rows], sem))
                return c
            lax.fori_loop(first, n_blocks, per_block, 0)

        for act in (lambda cp: cp.start(), lambda cp: cp.wait()):
            for_each_pad_row(act)
            for_each_unused_block(act)

    def issue(tok, carry):
        for j in range(TOP_K):
            _row_copy(x_ref, tok, xs_ref, dest_ref[tok * TOP_K + j], sem).start(priority=j % 2)
        return carry

    lax.fori_loop(0, t, issue, 0, unroll=ISSUE_UNROLL)
    for j in range(TOP_K):
        pltpu.make_async_copy(x_ref, xs_ref.at[pl.ds(0, x_ref.shape[0])], sem).wait()


def _dispatch(pstart, cnt, dest, xn2, n_rows):
    tiles, lanes = xn2.shape
    n = tiles // SUBLANES
    t = MOVE_TILE
    return pl.pallas_call(
        _dispatch_kernel,
        grid_spec=pltpu.PrefetchScalarGridSpec(
            num_scalar_prefetch=2,
            grid=(n // t,),
            in_specs=[pl.BlockSpec((t * TOP_K,), lambda i, ps, ct: (i,), memory_space=pltpu.SMEM),
                      pl.BlockSpec((t * SUBLANES, lanes), lambda i, ps, ct: (i, 0))],
            out_specs=pl.BlockSpec(memory_space=pl.ANY),
            scratch_shapes=[pltpu.VMEM((ROW_BLOCK * SUBLANES, lanes), jnp.float32),
                            pltpu.SemaphoreType.DMA(())]),
        out_shape=jax.ShapeDtypeStruct((n_rows * SUBLANES, lanes), jnp.float32),
        compiler_params=pltpu.CompilerParams(dimension_semantics=("arbitrary",),
                                             vmem_limit_bytes=VMEM_LIMIT),
        name="dispatch",
    )(pstart, cnt, dest, xn2)


def _experts_kernel(be_ref, nused_ref, half_ref, next_ref, xs_ref, wu_hbm, bu_ref, wd_hbm, bd_ref,
                    ys_ref, wu32_ref, wd32_ref, wu16_ref, wd16_ref, sems):
    i = pl.program_id(0)
    d, f = wu16_ref.shape[0], wd16_ref.shape[0]
    half = half_ref[i]

    def weight_copies(e, into):
        return (pltpu.make_async_copy(wu_hbm.at[e], wu32_ref.at[into], sems.at[0, into]),
                pltpu.make_async_copy(wd_hbm.at[e], wd32_ref.at[into], sems.at[1, into]))

    @pl.when(i == 0)
    def _():
        for cp in weight_copies(be_ref[0], 0):
            cp.start()

    @pl.when((i == 0) | (be_ref[i] != be_ref[jnp.maximum(i - 1, 0)]))
    def _():
        for cp in weight_copies(be_ref[i], half):
            cp.wait()
        wu16_ref[...] = wu32_ref[half].astype(wu16_ref.dtype)
        wd16_ref[...] = wd32_ref[half].astype(wd16_ref.dtype)

        @pl.when(next_ref[i] >= 0)
        def _():
            for cp in weight_copies(next_ref[i], 1 - half):
                cp.start()

    @pl.when(i < nused_ref[0])
    def _():
        x = _load_row_tiles(xs_ref, ROW_BLOCK, d).astype(jnp.bfloat16)
        acts = []
        for lo in range(0, f, UP_TILE):
            glu_cols, lin_cols = slice(lo, lo + UP_TILE), slice(f + lo, f + lo + UP_TILE)
            glu = jnp.minimum(_dot(x, wu16_ref[:, glu_cols]) + bu_ref[:, glu_cols], SWIGLU_LIMIT)
            lin = jnp.clip(_dot(x, wu16_ref[:, lin_cols]) + bu_ref[:, lin_cols],
                           -SWIGLU_LIMIT, SWIGLU_LIMIT)
            act = glu * (1.0 / (1.0 + jnp.exp(-SWIGLU_ALPHA * glu))) * (lin + 1.0)
            acts.append(act.astype(jnp.bfloat16))
        act = jnp.concatenate(acts, axis=1)
        _store_row_tiles(ys_ref, _dot(act, wd16_ref[...]) + bd_ref[...])

    @pl.when(i >= nused_ref[0])
    def _():
        ys_ref[...] = jnp.zeros(ys_ref.shape, ys_ref.dtype)


def _experts(block_e, nused, block_half, block_next, xs, wu, bu, wd, bd):
    tiles, lanes = xs.shape
    e, d, f2 = wu.shape
    f = wd.shape[1]
    block = (ROW_BLOCK * SUBLANES, lanes)
    rows = lambda i, be, nu, hf, nx: (i, 0)
    used_rows = lambda i, be, nu, hf, nx: (jnp.minimum(i, nu[0] - 1), 0)
    per_e = lambda i, be, nu, hf, nx: (be[i], 0, 0)
    return pl.pallas_call(
        _experts_kernel,
        grid_spec=pltpu.PrefetchScalarGridSpec(
            num_scalar_prefetch=4,
            grid=(tiles // block[0],),
            in_specs=[pl.BlockSpec(block, used_rows),
                      pl.BlockSpec(memory_space=pl.ANY), pl.BlockSpec((None, 1, f2), per_e),
                      pl.BlockSpec(memory_space=pl.ANY), pl.BlockSpec((None, 1, d), per_e)],
            out_specs=pl.BlockSpec(block, rows),
            scratch_shapes=[pltpu.VMEM((2, d, f2), jnp.float32), pltpu.VMEM((2, f, d), jnp.float32),
                            pltpu.VMEM((d, f2), jnp.bfloat16), pltpu.VMEM((f, d), jnp.bfloat16),
                            pltpu.SemaphoreType.DMA((2, 2))]),
        out_shape=jax.ShapeDtypeStruct(xs.shape, jnp.float32),
        compiler_params=pltpu.CompilerParams(dimension_semantics=("arbitrary",),
                                             vmem_limit_bytes=VMEM_LIMIT),
        name="experts",
    )(block_e, nused, block_half, block_next, xs, wu, bu, wd, bd)


def _combine_kernel(dest_ref, dest_next_ref, h_ref, eg_ref, g_ref, ys_ref, o_ref, buf_ref, sems,
                    *, final_norm):
    i = pl.program_id(0)
    t, d = h_ref.shape
    half = i % 2

    def gather_rows(idx_ref, into):
        def issue(tok, carry):
            for j in range(TOP_K):
                _row_copy(ys_ref, idx_ref[tok * TOP_K + j], buf_ref.at[into, j], tok,
                          sems.at[into]).start(priority=j % 2)
            return carry
        lax.fori_loop(0, t, issue, 0, unroll=ISSUE_UNROLL)

    @pl.when(i == 0)
    def _():
        gather_rows(dest_ref, 0)

    @pl.when(i + 1 < pl.num_programs(0))
    def _():
        gather_rows(dest_next_ref, 1 - half)

    for j in range(TOP_K):
        pltpu.make_async_copy(ys_ref.at[pl.ds(0, t * SUBLANES)], buf_ref.at[half, j],
                              sems.at[half]).wait()
    eg = eg_ref[...]
    h = h_ref[...]
    for j in range(TOP_K):
        h = h + eg[:, j:j + 1] * _load_row_tiles(buf_ref.at[half, j], t, d)
    o_ref[...] = _rmsnorm(h, g_ref[...]) if final_norm else h


def _combine(dest, h, eg, g, ys, final_norm):
    n, d = h.shape
    t = MOVE_TILE
    return pl.pallas_call(
        functools.partial(_combine_kernel, final_norm=final_norm),
        grid=(n // t,),
        in_specs=[pl.BlockSpec((t * TOP_K,), lambda i: (i,), memory_space=pltpu.SMEM),
                  pl.BlockSpec((t * TOP_K,), lambda i: (jnp.minimum(i + 1, n // t - 1),),
                               memory_space=pltpu.SMEM),
                  pl.BlockSpec((t, d), lambda i: (i, 0)),
                  pl.BlockSpec((t, 2 * TOP_K), lambda i: (i, 0)),
                  pl.BlockSpec((1, d), lambda i: (0, 0)),
                  pl.BlockSpec(memory_space=pl.ANY)],
        out_specs=pl.BlockSpec((t, d), lambda i: (i, 0)),
        scratch_shapes=[pltpu.VMEM((2, TOP_K, t * SUBLANES, ys.shape[1]), jnp.float32),
                        pltpu.SemaphoreType.DMA((2,))],
        out_shape=jax.ShapeDtypeStruct((n, d), jnp.float32),
        compiler_params=pltpu.CompilerParams(dimension_semantics=("arbitrary",),
                                             vmem_limit_bytes=VMEM_LIMIT),
        name="combine",
    )(dest, dest, h, eg, g, ys)


def _rope_tables(seq):
    half = ROPE_DIM // 2
    inv = jnp.power(jnp.float32(ROPE_THETA), -jnp.arange(half, dtype=jnp.float32) / half)
    ang = jnp.arange(seq, dtype=jnp.float32)[:, None] * inv[None, :]
    cos, sin = jnp.cos(ang), jnp.sin(ang)
    pad = HEAD_DIM - ROPE_DIM
    one, zero = jnp.ones((seq, pad), jnp.float32), jnp.zeros((seq, pad), jnp.float32)
    zh = jnp.zeros((seq, half), jnp.float32)
    reps = LANES // HEAD_DIM
    c = jnp.tile(jnp.concatenate([cos, cos, one], axis=1), (1, reps))
    a = jnp.tile(jnp.concatenate([-sin, zh, zero], axis=1), (1, reps))
    b = jnp.tile(jnp.concatenate([zh, sin, zero], axis=1), (1, reps))
    return c, a, b


def _layer(h, mem, p, tables, final_g):
    b, s, d = h.shape
    n = b * s
    bf = jnp.bfloat16
    x2 = h.reshape(n, d)
    w_in = p["w_in"]
    o_ik = 3 * ATTN_W + IDXQ_W
    o_u = o_ik + IDX_DIM + N_HEADS_IDX
    wa = w_in[:, :o_ik].astype(bf)
    wb = jnp.pad(w_in[:, o_ik:o_u], ((0, 0), (0, LANES - (o_u - o_ik)))).astype(bf)
    wc = w_in[:, o_u:].astype(bf)
    q, k, vt, iq, iklo, ikhi, iw, u, qm = _in_proj(x2, p["norm_mix_g"][None], wa, wb, wc, *tables, b, s)
    km, vm = _mem_kv(mem, p["mem_norm_g"][None], p["w_mem_kv"].astype(bf))
    b3 = lambda a: a.reshape(b, s, a.shape[-1])
    ya = _dsa(b3(iq), b3(iw), b3(q), b3(iklo), b3(ikhi), b3(k), vt).reshape(n, ATTN_W)

    wbd = jnp.zeros((POOL_W, POOL_W), jnp.float32)
    for g in range(N_POOL_GROUPS):
        lo = g * POOL_GROUP_DIM
        wbd = wbd.at[lo:lo + POOL_GROUP_DIM, lo:lo + POOL_GROUP_DIM].set(p["pool_w_group"][g])
    wr = jnp.pad(p["w_router"], ((0, 0), (0, LANES - N_EXPERTS)))
    wr_hi = wr.astype(bf)
    br = jnp.pad(p["b_router"], (0, LANES - N_EXPERTS), constant_values=NEG_BIG)[None]
    hmid, xn2, eg, ei, counts = _merge(
        x2, ya, u, qm, km, vm, p["norm_mix_g"][None], p["w_gate"].astype(bf), p["b_gate"][None],
        p["w_proj_attn"].astype(bf), p["w_proj_pool"].astype(bf), p["w_proj_mem"].astype(bf),
        wbd.astype(bf), p["pool_scale"][None], p["w_out"].astype(bf), p["norm_ffn_g"][None],
        wr_hi, (wr - wr_hi.astype(jnp.float32)).astype(bf), br, s)

    cnt = counts[0, :N_EXPERTS].astype(jnp.int32)
    padded = ((cnt + ROW_BLOCK - 1) // ROW_BLOCK) * ROW_BLOCK
    pend = jnp.cumsum(padded)
    pstart = (pend - padded).astype(jnp.int32)
    n_rows = n * TOP_K + N_EXPERTS * ROW_BLOCK
    n_blocks = n_rows // ROW_BLOCK
    nused = (pend[-1] // ROW_BLOCK).astype(jnp.int32)
    blk = jnp.minimum(jnp.arange(n_blocks, dtype=jnp.int32), nused - 1) * ROW_BLOCK
    block_e = jnp.sum(blk[:, None] >= pend[None, :], axis=1).astype(jnp.int32)
    block_e = jnp.minimum(block_e, N_EXPERTS - 1)
    has_rows = cnt > 0
    run_index = jnp.cumsum(has_rows) - 1
    ids = jnp.arange(N_EXPERTS, dtype=jnp.int32)
    later = jnp.where(has_rows[None, :] & (ids[None, :] > ids[:, None]), ids[None, :], N_EXPERTS)
    next_used = jnp.min(later, axis=1)
    next_used = jnp.where(next_used < N_EXPERTS, next_used, -1).astype(jnp.int32)
    of_block = block_e[:, None] == ids[None, :]
    block_half = jnp.sum(jnp.where(of_block, run_index % 2, 0), axis=1).astype(jnp.int32)
    block_next = jnp.sum(jnp.where(of_block, next_used, 0), axis=1).astype(jnp.int32)
    chosen =ei[:, :TOP_K, None] == jnp.arange(N_EXPERTS, dtype=jnp.int32)
    dest = (jnp.sum(jnp.where(chosen, pstart, 0), axis=-1) + ei[:, TOP_K:]).reshape(-1)

    xs = _dispatch(pstart, cnt, dest, xn2, n_rows)
    ys = _experts(block_e, nused[None], block_half, block_next, xs, p["w_up"], p["b_up"][:, None, :],
                  p["w_down"], p["b_down"][:, None, :])
    g = p["norm_ffn_g"][None] if final_g is None else final_g[None]
    return _combine(dest, hmid, eg, g, ys, final_g is not None).reshape(b, s, d)


def kernel(x, mem, norm_mix_g, w_in, w_gate, b_gate, w_proj_attn, w_proj_pool, w_proj_mem,
           pool_w_group, pool_scale, mem_norm_g, w_mem_kv, w_out, norm_ffn_g,
           w_router, b_router, w_up, b_up, w_down, b_down, norm_final_g):
    b, s, d = x.shape
    assert s % IN_TILE == 0 and s % COUNT_CHUNK == 0 and (b * s) % MOVE_TILE == 0
    assert d == LANES * SUBLANES, "the row-tile layout maps one model row onto one (8, 128) tile"
    stacked = dict(norm_mix_g=norm_mix_g, w_in=w_in, w_gate=w_gate, b_gate=b_gate,
                   w_proj_attn=w_proj_attn, w_proj_pool=w_proj_pool, w_proj_mem=w_proj_mem,
                   pool_w_group=pool_w_group, pool_scale=pool_scale, mem_norm_g=mem_norm_g,
                   w_mem_kv=w_mem_kv, w_out=w_out, norm_ffn_g=norm_ffn_g, w_router=w_router,
                   b_router=b_router, w_up=w_up, b_up=b_up, w_down=w_down, b_down=b_down)
    depth = w_in.shape[0]
    tables = _rope_tables(s)
    h = x
    for l in range(depth):
        p = {name: val[l] for name, val in stacked.items()}
        h = _layer(h, mem, p, tables, norm_final_g if l == depth - 1 else None)
    return h
```

```python
import functools

import jax
import jax.numpy as jnp
import numpy as np
from jax import lax
from jax.experimental import pallas as pl
from jax.experimental.pallas import tpu as pltpu

N_HEADS_ATTN = 8
HEAD_DIM = 64
ROPE_DIM = HEAD_DIM // 4
ROPE_THETA = 500000.0
N_HEADS_IDX = 8
IDX_DIM = 64
TOPK_MAX = 256
N_POOL_GROUPS = 4
POOL_GROUP_DIM = 64
POOL_WINDOWS = (2, 4, 8, 16)
POOL_HALO = 16
N_HEADS_MEM = 4
N_BRANCHES = 3
N_EXPERTS = 32
TOP_K = 4
SWIGLU_ALPHA = 1.702
SWIGLU_LIMIT = 7.0
EPS = 1e-6

ATTN_W = N_HEADS_ATTN * HEAD_DIM
POOL_W = N_POOL_GROUPS * POOL_GROUP_DIM
MEM_W = N_HEADS_MEM * HEAD_DIM
IDXQ_W = N_HEADS_IDX * IDX_DIM

LANES = 128
SUBLANES = 8
VMEM_LIMIT = 56 * 1024 * 1024

IN_TILE = 512
Q_BLOCK = 128
KEY_CHUNK = 256
COUNT_CHUNK = 512
COUNT_ROWS = 32
MERGE_TILE = 512
ROW_BLOCK = 256
UP_TILE = 256
MOVE_TILE = 512
ISSUE_UNROLL = 4

VT_ROWS = LANES + 16
VT_ALL = (N_HEADS_ATTN // 2) * VT_ROWS
LOG2_E = 1.4426950408889634

NEG_BIG = -1e30
INT_MIN = -2 ** 31


def _dot(a, b):
    return jnp.dot(a, b, preferred_element_type=jnp.float32)


def _dot_nt(a, b):
    return lax.dot_general(a, b, (((1,), (1,)), ((), ())), preferred_element_type=jnp.float32)


def _rmsnorm(x, g):
    return x * lax.rsqrt(jnp.mean(x * x, axis=-1, keepdims=True) + EPS) * g


def _store_row_tiles(ref, x):
    rows, width = x.shape
    for j in range(width // LANES):
        ref[pl.ds(j, rows, stride=width // LANES), :] = x[:, j * LANES:(j + 1) * LANES]


def _load_row_tiles(ref, rows, width):
    pieces = width // LANES
    return jnp.concatenate([ref[pl.ds(j, rows, stride=pieces), :] for j in range(pieces)], axis=1)


def _row_tile(ref, row):
    return ref.at[pl.ds(pl.multiple_of(row * SUBLANES, SUBLANES), SUBLANES)]


def _rope128(x, c, a, b):
    return x * c + pltpu.roll(x, LANES - ROPE_DIM // 2, 1) * a + pltpu.roll(x, ROPE_DIM // 2, 1) * b


def _in_proj_kernel(x_ref, g_ref, wa_ref, wb_ref, wc_ref, rc_ref, ra_ref, rb_ref,
                    q_ref, k_ref, vt_ref, iq_ref, iklo_ref, ikhi_ref, iw_ref, u_ref, qm_ref):
    xn = _rmsnorm(x_ref[...], g_ref[...]).astype(jnp.bfloat16)
    rc, ra, rb = rc_ref[...], ra_ref[...], rb_ref[...]
    pa = _dot(xn, wa_ref[...])
    for seg, (ref, scale) in enumerate(((q_ref, LOG2_E * HEAD_DIM ** -0.5), (k_ref, None),
                                        (None, None), (iq_ref, IDX_DIM ** -0.5))):
        for c in range(ATTN_W // LANES):
            lo = seg * ATTN_W + c * LANES
            blk = pa[:, lo:lo + LANES]
            if ref is None:
                vt_ref[c * VT_ROWS:c * VT_ROWS + LANES, :] = blk.T.astype(vt_ref.dtype)
                vt_ref[c * VT_ROWS + LANES:(c + 1) * VT_ROWS, :] = jnp.ones(
                    (VT_ROWS - LANES, blk.shape[0]), vt_ref.dtype)
                continue
            blk = _rope128(blk, rc, ra, rb)
            if scale is not None:
                blk = blk * scale
            ref[:, c * LANES:(c + 1) * LANES] = blk.astype(ref.dtype)
    pb = _dot(xn, wb_ref[...])
    lane = lax.broadcasted_iota(jnp.int32, pb.shape, 1)
    ik = jnp.where(lane < IDX_DIM, _rope128(pb, rc, ra, rb), 0.0)
    iklo_ref[...] = ik.astype(iklo_ref.dtype)
    ikhi_ref[...] = pltpu.roll(ik, IDX_DIM, 1).astype(ikhi_ref.dtype)
    iw_ref[...] = pb * (N_HEADS_IDX ** -0.5)
    pc = _dot(xn, wc_ref[...])
    u_ref[...] = pc[:, :POOL_W]
    qm_ref[...] = (pc[:, POOL_W:] * (HEAD_DIM ** -0.5)).astype(qm_ref.dtype)


def _in_proj(x2, g, wa, wb, wc, rc, ra, rb, batch, seq):
    n, d = x2.shape
    t = IN_TILE
    tiles_per_seq = seq // t
    row = lambda i: (i, 0)
    full = lambda i: (0, 0)
    pos = lambda i: (i % tiles_per_seq, 0)
    bf = jnp.bfloat16
    flat = lambda w, dt: (jax.ShapeDtypeStruct((n, w), dt), pl.BlockSpec((t, w), row))
    outs = (flat(ATTN_W, bf), flat(ATTN_W, bf),
            (jax.ShapeDtypeStruct((batch, VT_ALL, seq), bf),
             pl.BlockSpec((None, VT_ALL, t), lambda i: (i // tiles_per_seq, 0, i % tiles_per_seq))),
            flat(IDXQ_W, bf), flat(LANES, bf), flat(LANES, bf), flat(LANES, jnp.float32),
            flat(POOL_W, jnp.float32), flat(MEM_W, bf))
    return pl.pallas_call(
        _in_proj_kernel,
        grid=(n // t,),
        in_specs=[pl.BlockSpec((t, d), row), pl.BlockSpec((1, d), full),
                  pl.BlockSpec(wa.shape, full), pl.BlockSpec(wb.shape, full),
                  pl.BlockSpec(wc.shape, full),
                  pl.BlockSpec((t, LANES), pos), pl.BlockSpec((t, LANES), pos),
                  pl.BlockSpec((t, LANES), pos)],
        out_specs=tuple(o[1] for o in outs),
        out_shape=tuple(o[0] for o in outs),
        compiler_params=pltpu.CompilerParams(dimension_semantics=("arbitrary",),
                                             vmem_limit_bytes=VMEM_LIMIT),
        name="in_proj",
    )(x2, g, wa, wb, wc, rc, ra, rb)


def _mem_kv_kernel(mem_ref, g_ref, w_ref, km_ref, vm_ref):
    mn = _rmsnorm(mem_ref[...], g_ref[...]).astype(jnp.bfloat16)
    kv = _dot(mn, w_ref[...])
    km, vm = kv[:, :MEM_W], kv[:, MEM_W:]
    lane = lax.broadcasted_iota(jnp.int32, km.shape, 1)
    for h in range(N_HEADS_MEM):
        in_head = (lane >= h * HEAD_DIM) & (lane < (h + 1) * HEAD_DIM)
        km_ref[h] = jnp.where(in_head, km, 0.0).astype(km_ref.dtype)
        vm_ref[h] = jnp.where(in_head, vm, 0.0).astype(vm_ref.dtype)


def _mem_kv(mem, g, w):
    b, m, d = mem.shape
    out = jax.ShapeDtypeStruct((b, N_HEADS_MEM, m, MEM_W), jnp.bfloat16)
    return pl.pallas_call(
        _mem_kv_kernel,
        grid=(b,),
        in_specs=[pl.BlockSpec((None, m, d), lambda i: (i, 0, 0)),
                  pl.BlockSpec((1, d), lambda i: (0, 0)),
                  pl.BlockSpec(w.shape, lambda i: (0, 0))],
        out_specs=(pl.BlockSpec((None, N_HEADS_MEM, m, MEM_W), lambda i: (i, 0, 0, 0)),
                   pl.BlockSpec((None, N_HEADS_MEM, m, MEM_W), lambda i: (i, 0, 0, 0))),
        out_shape=(out, out),
        compiler_params=pltpu.CompilerParams(dimension_semantics=("arbitrary",),
                                             vmem_limit_bytes=VMEM_LIMIT),
        name="mem_kv",
    )(mem, g, w)


def _dsa_kernel(iq_ref, iw_ref, q_ref, iklo_ref, ikhi_ref, k_ref, vt_ref, o_ref,
                key_ref, iqt_ref, qt_ref, bias_ref, s_ref, acc_ref, *, n_top, idx_bits):
    qb = pl.program_id(1)
    n_chunks = (qb * Q_BLOCK + Q_BLOCK + KEY_CHUNK - 1) // KEY_CHUNK
    n_count_chunks = (qb * Q_BLOCK + Q_BLOCK + COUNT_CHUNK - 1) // COUNT_CHUNK
    q_pos = qb * Q_BLOCK + lax.broadcasted_iota(jnp.int32, (1, Q_BLOCK), 1)
    row_k = lax.broadcasted_iota(jnp.int32, (KEY_CHUNK, 1), 0)
    row_c = lax.broadcasted_iota(jnp.int32, (COUNT_CHUNK, 1), 0)
    row_d = lax.broadcasted_iota(jnp.int32, (LANES, Q_BLOCK), 0)
    bf = jnp.bfloat16

    def chunk_off(c):
        return pl.multiple_of(c * KEY_CHUNK, KEY_CHUNK)

    wt = iw_ref[...].T
    for j in range(N_HEADS_IDX // 2):
        iqt_ref[j // 2, :, (j % 2) * Q_BLOCK:(j % 2 + 1) * Q_BLOCK] = (
            iq_ref[:, j * LANES:(j + 1) * LANES].astype(jnp.float32).T.astype(bf))
    for j in range(N_HEADS_ATTN // 2):
        pair_t = q_ref[:, j * LANES:(j + 1) * LANES].astype(jnp.float32).T
        qt_ref[j, :, :Q_BLOCK] = jnp.where(row_d < HEAD_DIM, pair_t, 0.0).astype(bf)
        qt_ref[j, :, Q_BLOCK:] = jnp.where(row_d >= HEAD_DIM, pair_t, 0.0).astype(bf)

    def score_chunk(c, carry):
        off = pl.multiple_of(c * COUNT_CHUNK, COUNT_CHUNK)
        acc = jnp.zeros((COUNT_CHUNK, Q_BLOCK), jnp.float32)
        for parity, ik_ref in enumerate((iklo_ref, ikhi_ref)):
            ik = ik_ref[pl.ds(off, COUNT_CHUNK), :]
            for g in range(N_HEADS_IDX // 4):
                dots = _dot(ik, iqt_ref[g])
                for side in range(2):
                    head = 2 * (2 * g + side) + parity
                    w = wt[IDX_DIM + head:IDX_DIM + head + 1, :]
                    acc = acc + w * jnp.maximum(dots[:, side * Q_BLOCK:(side + 1) * Q_BLOCK], 0.0)
        acc = jnp.where(off + row_c <= q_pos, acc, -jnp.inf)
        bits = lax.bitcast_convert_type(acc, jnp.int32)
        key_ref[pl.ds(off, COUNT_CHUNK), :] = bits ^ ((bits >> 31) & 0x7FFFFFFF)
        return carry

    lax.fori_loop(0, n_count_chunks, score_chunk, 0)

    def count(pred):
        def body(c, cnt):
            off = pl.multiple_of(c * COUNT_CHUNK, COUNT_CHUNK)
            hit = jnp.where(pred(key_ref[pl.ds(off, COUNT_CHUNK), :], off + row_c), 1.0, 0.0)
            return cnt + jnp.sum(hit.reshape(COUNT_CHUNK // COUNT_ROWS, COUNT_ROWS, Q_BLOCK), axis=0)
        cnt = lax.fori_loop(0, n_count_chunks, body, jnp.zeros((COUNT_ROWS, Q_BLOCK), jnp.float32))
        return jnp.sum(cnt, axis=0, keepdims=True)

    k_f = jnp.float32(n_top)
    zero = jnp.zeros((1, Q_BLOCK), jnp.int32)
    thr = jnp.where(count(lambda kk, pos: kk >= zero) >= k_f, zero, INT_MIN)

    def thr_bit(i, cur):
        cand = cur + jnp.left_shift(jnp.int32(1), 30 - i)
        return jnp.where(count(lambda kk, pos: kk >= cand) >= k_f, cand, cur)

    thr = lax.fori_loop(0, 31, thr_bit, thr)
    need = k_f - count(lambda kk, pos: kk > thr)
    n_ties = count(lambda kk, pos: kk == thr)

    def tie_search():
        def tie_bit(i, cur):
            cand = cur + jnp.left_shift(jnp.int32(1), idx_bits - 1 - i)
            below = count(lambda kk, pos: (kk == thr) & (pos < cand))
            return jnp.where(below < need, cand, cur)
        return lax.fori_loop(0, idx_bits, tie_bit, zero)

    has_surplus_ties = jnp.max(jnp.where(n_ties > need, 1.0, 0.0)) > 0.0
    tie_pos = lax.cond(has_surplus_ties, tie_search, lambda: zero + (2 ** idx_bits - 1))

    acc_ref[...] = jnp.zeros(acc_ref.shape, jnp.float32)

    n_pairs = N_HEADS_ATTN // 2

    def masked_logits(c, slot):
        off = chunk_off(jnp.minimum(c, n_chunks - 1))
        kk = key_ref[pl.ds(off, KEY_CHUNK), :]
        pos = c * KEY_CHUNK + row_k
        sel = ((kk > thr) | ((kk == thr) & (pos <= tie_pos))) & (pos <= q_pos)
        bias = jnp.where(sel, 0.0, NEG_BIG)
        bias_ref[slot] = jnp.concatenate([bias, bias], axis=1)
        for j in range(n_pairs):
            kp = k_ref[pl.ds(off, KEY_CHUNK), j * LANES:(j + 1) * LANES]
            s_ref[slot, j] = _dot(kp, qt_ref[j]) + bias_ref[slot]

    def softmax_pv(c, slot, ms, ls):
        off = chunk_off(jnp.minimum(c, n_chunks - 1))
        new_ms, new_ls = [], []
        for j in range(n_pairs):
            s = s_ref[slot, j]
            m_new = jnp.maximum(ms[j], jnp.max(s, axis=0, keepdims=True))
            alpha = jnp.exp2(ms[j] - m_new)
            p = jnp.exp2(s - m_new).astype(bf)
            vt = vt_ref[j * VT_ROWS:(j + 1) * VT_ROWS, pl.ds(off, KEY_CHUNK)]
            pv = _dot(vt, p)
            new_ls.append(alpha * ls[j] + pv[LANES:LANES + 1, :])
            new_ms.append(m_new)
            lo, mid, hi = j * LANES, j * LANES + HEAD_DIM, (j + 1) * LANES
            acc_ref[lo:mid, :] = alpha[:, :Q_BLOCK] * acc_ref[lo:mid, :] + pv[:HEAD_DIM, :Q_BLOCK]
            acc_ref[mid:hi, :] = alpha[:, Q_BLOCK:] * acc_ref[mid:hi, :] + pv[HEAD_DIM:LANES, Q_BLOCK:]
        return tuple(new_ms), tuple(new_ls)

    masked_logits(0, 0)

    def attn_two_chunks(i, carry):
        ms, ls = carry
        masked_logits(2 * i + 1, 1)
        ms, ls = softmax_pv(2 * i, 0, ms, ls)
        masked_logits(2 * i + 2, 0)
        return softmax_pv(2 * i + 1, 1, ms, ls)

    init = (tuple(jnp.full((1, 2 * Q_BLOCK), NEG_BIG, jnp.float32) for _ in range(n_pairs)),
            tuple(jnp.zeros((1, 2 * Q_BLOCK), jnp.float32) for _ in range(n_pairs)))
    _, ls = lax.fori_loop(0, (n_chunks + 1) // 2, attn_two_chunks, init)
    for j in range(n_pairs):
        lo, mid, hi = j * LANES, j * LANES + HEAD_DIM, (j + 1) * LANES
        even = acc_ref[lo:mid, :] / ls[j][:, :Q_BLOCK]
        odd = acc_ref[mid:hi, :] / ls[j][:, Q_BLOCK:]
        o_ref[:, lo:hi] = jnp.concatenate([even, odd], axis=0).T.astype(o_ref.dtype)


def _dsa(iq, iw, q, iklo, ikhi, k, vt):
    b, s, _ = q.shape
    n_top = min(TOPK_MAX, s // 4)
    idx_bits = max(1, int(np.ceil(np.log2(s))))
    qblk = lambda w: pl.BlockSpec((None, Q_BLOCK, w), lambda bi, qi: (bi, qi, 0))
    keys = lambda w: pl.BlockSpec((None, s, w), lambda bi, qi: (bi, 0, 0))
    return pl.pallas_call(
        functools.partial(_dsa_kernel, n_top=n_top, idx_bits=idx_bits),
        grid=(b, s // Q_BLOCK),
        in_specs=[qblk(IDXQ_W), qblk(LANES), qblk(ATTN_W), keys(LANES), keys(LANES),
                  keys(ATTN_W), pl.BlockSpec((None, VT_ALL, s), lambda bi, qi: (bi, 0, 0))],
        out_specs=qblk(ATTN_W),
        out_shape=jax.ShapeDtypeStruct((b, s, ATTN_W), jnp.bfloat16),
        scratch_shapes=[pltpu.VMEM((s, Q_BLOCK), jnp.int32),
                        pltpu.VMEM((N_HEADS_IDX // 4, LANES, 2 * Q_BLOCK), jnp.bfloat16),
                        pltpu.VMEM((N_HEADS_ATTN // 2, LANES, 2 * Q_BLOCK), jnp.bfloat16),
                        pltpu.VMEM((2, KEY_CHUNK, 2 * Q_BLOCK), jnp.float32),
                        pltpu.VMEM((2, N_HEADS_ATTN // 2, KEY_CHUNK, 2 * Q_BLOCK), jnp.float32),
                        pltpu.VMEM((ATTN_W, Q_BLOCK), jnp.float32)],
        compiler_params=pltpu.CompilerParams(dimension_semantics=("arbitrary", "arbitrary"),
                                             vmem_limit_bytes=VMEM_LIMIT),
        name="dsa",
    )(iq, iw, q, iklo, ikhi, k, vt)


def _merge_kernel(x_ref, ya_ref, u_ref, uprev_ref, qm_ref, km_ref, vm_ref,
                  gmix_ref, wg_ref, bg_ref, wpa_ref, wpp_ref, wpm_ref, wbd_ref, psc_ref,
                  wo_ref, gffn_ref, wrh_ref, wrhl_ref, br_ref,
                  h_ref, xn2_ref, eg_ref, ei_ref, cnt_ref, carry_ref, *, tiles_per_seq):
    i = pl.program_id(0)
    t = x_ref.shape[0]
    d = x_ref.shape[1]
    bf = jnp.bfloat16
    x = x_ref[...]
    xn = _rmsnorm(x, gmix_ref[...]).astype(bf)

    tile_in_seq = i % tiles_per_seq
    u = u_ref[...]
    halo = jnp.where(tile_in_seq == 0, 0.0, uprev_ref[...])
    ext = jnp.concatenate([halo, u], axis=0)
    lane_p = lax.broadcasted_iota(jnp.int32, (t, POOL_W), 1)
    pos1 = (tile_in_seq * t + lax.broadcasted_iota(jnp.int32, (t, 1), 0) + 1).astype(jnp.float32)
    pooled = None
    run, width = ext, 1
    for g, w in enumerate(POOL_WINDOWS):
        while width < w:
            run = run[width:] + run[:-width]
            width *= 2
        start = POOL_HALO + 1 - w
        mean = run[start:start + t] / jnp.minimum(pos1, float(w))
        pooled = mean if pooled is None else jnp.where(lane_p >= g * POOL_GROUP_DIM, mean, pooled)
    pooled = pooled - u
    mixed = _dot(pooled.astype(bf), wbd_ref[...]) * psc_ref[...]
    y_pool = _dot(mixed.astype(bf), wpp_ref[...])

    qm = qm_ref[...]
    probs = []
    for h in range(N_HEADS_MEM):
        s = _dot_nt(qm, km_ref[h])
        p = jnp.exp(s - jnp.max(s, axis=1, keepdims=True))
        probs.append((p / jnp.sum(p, axis=1, keepdims=True)).astype(bf))
    y_mem = _dot(probs[0], vm_ref[0])
    for h in range(1, N_HEADS_MEM):
        y_mem = y_mem + _dot(probs[h], vm_ref[h])
    y_mem = _dot(y_mem.astype(bf), wpm_ref[...])

    y_attn = _dot(ya_ref[...], wpa_ref[...])

    def gate(br):
        z = _dot(xn, wg_ref[:, br * d:(br + 1) * d]) + bg_ref[:, br * d:(br + 1) * d]
        return 1.0 / (1.0 + jnp.exp(-z))

    merged = gate(0) * y_attn + gate(1) * y_pool + gate(2) * y_mem
    h = x + _dot(merged.astype(bf), wo_ref[...])
    h_ref[...] = h
    xn2 = _rmsnorm(h, gffn_ref[...])
    _store_row_tiles(xn2_ref, xn2)

    x_hi = xn2.astype(bf)
    x_lo = (xn2 - x_hi.astype(jnp.float32)).astype(bf)
    by_hi = _dot(x_hi, wrhl_ref[...])
    logits = (by_hi[:, :LANES] + (_dot(x_lo, wrh_ref[...]) + by_hi[:, LANES:])) + br_ref[...]
    lane_e = lax.broadcasted_iota(jnp.int32, logits.shape, 1).astype(jnp.float32)
    work = logits
    vals, onehots, ids = [], [], []
    for _ in range(TOP_K):
        mx = jnp.max(work, axis=1, keepdims=True)
        idx = jnp.min(jnp.where(work == mx, lane_e, float(LANES)), axis=1, keepdims=True)
        oh = lane_e == idx
        vals.append(mx)
        ids.append(idx)
        onehots.append(oh)
        work = jnp.where(oh, -jnp.inf, work)
    exps = [jnp.exp(v - vals[0]) for v in vals]
    denom = exps[0] + exps[1] + exps[2] + exps[3]

    @pl.when(i == 0)
    def _():
        carry_ref[...] = jnp.zeros(carry_ref.shape, jnp.float32)

    member = jnp.where(onehots[0] | onehots[1] | onehots[2] | onehots[3], 1.0, 0.0)
    r_io = lax.broadcasted_iota(jnp.int32, (t, t), 0)
    c_io = lax.broadcasted_iota(jnp.int32, (t, t), 1)
    earlier = jnp.where(c_io < r_io, 1.0, 0.0).astype(bf)
    before = _dot(earlier, member.astype(bf)) + carry_ref[...]
    carry_ref[...] = carry_ref[...] + jnp.sum(member, axis=0, keepdims=True)
    cnt_ref[...] = carry_ref[...]

    lane8 = lax.broadcasted_iota(jnp.int32, (t, 2 * TOP_K), 1)
    eg = jnp.zeros((t, 2 * TOP_K), jnp.float32)
    ei = jnp.zeros((t, 2 * TOP_K), jnp.float32)
    for j in range(TOP_K):
        rank = jnp.sum(jnp.where(onehots[j], before, 0.0), axis=1, keepdims=True)
        eg = jnp.where(lane8 == j, exps[j] / denom, eg)
        ei = jnp.where(lane8 == j, ids[j], ei)
        ei = jnp.where(lane8 == TOP_K + j, rank, ei)
    eg_ref[...] = eg
    ei_ref[...] = ei.astype(jnp.int32)


def _merge(x2, ya, u, qm, km, vm, gmix, wg, bg, wpa, wpp, wpm, wbd, psc, wo, gffn, wrh, wrhl, br, seq):
    n, d = x2.shape
    t = MERGE_TILE
    tiles_per_seq = seq // t
    m = km.shape[2]
    row = lambda i: (i, 0)
    full = lambda i: (0, 0)
    per_batch = lambda i: (i // tiles_per_seq, 0, 0, 0)
    halo_blocks = t // POOL_HALO
    prev = lambda i: (jnp.maximum(i * halo_blocks - 1, 0), 0)
    consts = (gmix, wg, bg, wpa, wpp, wpm, wbd, psc, wo, gffn, wrh, wrhl, br)
    pieces = d // LANES
    out_shape = (jax.ShapeDtypeStruct((n, d), jnp.float32),
                 jax.ShapeDtypeStruct((n * pieces, LANES), jnp.float32),
                 jax.ShapeDtypeStruct((n, 2 * TOP_K), jnp.float32),
                 jax.ShapeDtypeStruct((n, 2 * TOP_K), jnp.int32),
                 jax.ShapeDtypeStruct((1, LANES), jnp.float32))
    return pl.pallas_call(
        functools.partial(_merge_kernel, tiles_per_seq=tiles_per_seq),
        grid=(n // t,),
        in_specs=[pl.BlockSpec((t, d), row), pl.BlockSpec((t, ATTN_W), row),
                  pl.BlockSpec((t, POOL_W), row), pl.BlockSpec((POOL_HALO, POOL_W), prev),
                  pl.BlockSpec((t, MEM_W), row),
                  pl.BlockSpec((None, N_HEADS_MEM, m, MEM_W), per_batch),
                  pl.BlockSpec((None, N_HEADS_MEM, m, MEM_W), per_batch)]
                 + [pl.BlockSpec(c.shape, full) for c in consts],
        out_specs=(pl.BlockSpec((t, d), row), pl.BlockSpec((t * pieces, LANES), row),
                   pl.BlockSpec((t, 2 * TOP_K), row), pl.BlockSpec((t, 2 * TOP_K), row),
                   pl.BlockSpec((1, LANES), full)),
        out_shape=out_shape,
        scratch_shapes=[pltpu.VMEM((1, LANES), jnp.float32)],
        compiler_params=pltpu.CompilerParams(dimension_semantics=("arbitrary",),
                                             vmem_limit_bytes=VMEM_LIMIT),
        name="merge",
    )(x2, ya, u, u, qm, km, vm, *consts)


def _row_copy(src_ref, src_row, dst_ref, dst_row, sem):
    return pltpu.make_async_copy(_row_tile(src_ref, src_row), _row_tile(dst_ref, dst_row), sem)


def _dispatch_kernel(pstart_ref, cnt_ref, dest_ref, x_ref, xs_ref, zero_ref, sem):
    t = x_ref.shape[0] // SUBLANES
    block_tiles = ROW_BLOCK * SUBLANES

    @pl.when(pl.program_id(0) == 0)
    def _():
        zero_ref[...] = jnp.zeros(zero_ref.shape, zero_ref.dtype)

        def for_each_pad_row(act):
            def per_expert(e, carry):
                n_pad = (ROW_BLOCK - cnt_ref[e] % ROW_BLOCK) % ROW_BLOCK
                first = pstart_ref[e] + cnt_ref[e]

                def per_row(r, c):
                    act(_row_copy(zero_ref, 0, xs_ref, first + r, sem))
                    return c
                return lax.fori_loop(0, n_pad, per_row, carry)
            lax.fori_loop(0, N_EXPERTS, per_expert, 0)

        def for_each_unused_block(act):
            n_blocks = xs_ref.shape[0] // block_tiles
            first = (pstart_ref[N_EXPERTS - 1] + cnt_ref[N_EXPERTS - 1] + ROW_BLOCK - 1) // ROW_BLOCK

            def per_block(b, c):
                rows = pl.ds(pl.multiple_of(b * block_tiles, block_tiles), block_tiles)
                act(pltpu.make_async_copy(zero_ref, xs_ref.at[rows], sem))
                return c
            lax.fori_loop(first, n_blocks, per_block, 0)

        for act in (lambda cp: cp.start(), lambda cp: cp.wait()):
            for_each_pad_row(act)
            for_each_unused_block(act)

    def issue(tok, carry):
        for j in range(TOP_K):
            _row_copy(x_ref, tok, xs_ref, dest_ref[tok * TOP_K + j], sem).start(priority=j % 2)
        return carry

    lax.fori_loop(0, t, issue, 0, unroll=ISSUE_UNROLL)
    for j in range(TOP_K):
        pltpu.make_async_copy(x_ref, xs_ref.at[pl.ds(0, x_ref.shape[0])], sem).wait()


def _dispatch(pstart, cnt, dest, xn2, n_rows):
    tiles, lanes = xn2.shape
    n = tiles // SUBLANES
    t = MOVE_TILE
    return pl.pallas_call(
        _dispatch_kernel,
        grid_spec=pltpu.PrefetchScalarGridSpec(
            num_scalar_prefetch=2,
            grid=(n // t,),
            in_specs=[pl.BlockSpec((t * TOP_K,), lambda i, ps, ct: (i,), memory_space=pltpu.SMEM),
                      pl.BlockSpec((t * SUBLANES, lanes), lambda i, ps, ct: (i, 0))],
            out_specs=pl.BlockSpec(memory_space=pl.ANY),
            scratch_shapes=[pltpu.VMEM((ROW_BLOCK * SUBLANES, lanes), jnp.float32),
                            pltpu.SemaphoreType.DMA(())]),
        out_shape=jax.ShapeDtypeStruct((n_rows * SUBLANES, lanes), jnp.float32),
        compiler_params=pltpu.CompilerParams(dimension_semantics=("arbitrary",),
                                             vmem_limit_bytes=VMEM_LIMIT),
        name="dispatch",
    )(pstart, cnt, dest, xn2)


def _experts_kernel(be_ref, nused_ref, half_ref, next_ref, xs_ref, wu_hbm, bu_ref, wd_hbm, bd_ref,
                    ys_ref, wu32_ref, wd32_ref, wu16_ref, wd16_ref, sems):
    i = pl.program_id(0)
    d, f = wu16_ref.shape[0], wd16_ref.shape[0]
    half = half_ref[i]

    def weight_copies(e, into):
        return (pltpu.make_async_copy(wu_hbm.at[e], wu32_ref.at[into], sems.at[0, into]),
                pltpu.make_async_copy(wd_hbm.at[e], wd32_ref.at[into], sems.at[1, into]))

    @pl.when(i == 0)
    def _():
        for cp in weight_copies(be_ref[0], 0):
            cp.start()

    @pl.when((i == 0) | (be_ref[i] != be_ref[jnp.maximum(i - 1, 0)]))
    def _():
        for cp in weight_copies(be_ref[i], half):
            cp.wait()
        wu16_ref[...] = wu32_ref[half].astype(wu16_ref.dtype)
        wd16_ref[...] = wd32_ref[half].astype(wd16_ref.dtype)

        @pl.when(next_ref[i] >= 0)
        def _():
            for cp in weight_copies(next_ref[i], 1 - half):
                cp.start()

    @pl.when(i < nused_ref[0])
    def _():
        x = _load_row_tiles(xs_ref, ROW_BLOCK, d).astype(jnp.bfloat16)
        acts = []
        for lo in range(0, f, UP_TILE):
            glu_cols, lin_cols = slice(lo, lo + UP_TILE), slice(f + lo, f + lo + UP_TILE)
            glu = jnp.minimum(_dot(x, wu16_ref[:, glu_cols]) + bu_ref[:, glu_cols], SWIGLU_LIMIT)
            lin = jnp.clip(_dot(x, wu16_ref[:, lin_cols]) + bu_ref[:, lin_cols],
                           -SWIGLU_LIMIT, SWIGLU_LIMIT)
            act = glu * (1.0 / (1.0 + jnp.exp(-SWIGLU_ALPHA * glu))) * (lin + 1.0)
            acts.append(act.astype(jnp.bfloat16))
        act = jnp.concatenate(acts, axis=1)
        _store_row_tiles(ys_ref, _dot(act, wd16_ref[...]) + bd_ref[...])

    @pl.when(i >= nused_ref[0])
    def _():
        ys_ref[...] = jnp.zeros(ys_ref.shape, ys_ref.dtype)


def _experts(block_e, nused, block_half, block_next, xs, wu, bu, wd, bd):
    tiles, lanes = xs.shape
    e, d, f2 = wu.shape
    f = wd.shape[1]
    block = (ROW_BLOCK * SUBLANES, lanes)
    rows = lambda i, be, nu, hf, nx: (i, 0)
    used_rows = lambda i, be, nu, hf, nx: (jnp.minimum(i, nu[0] - 1), 0)
    per_e = lambda i, be, nu, hf, nx: (be[i], 0, 0)
    return pl.pallas_call(
        _experts_kernel,
        grid_spec=pltpu.PrefetchScalarGridSpec(
            num_scalar_prefetch=4,
            grid=(tiles // block[0],),
            in_specs=[pl.BlockSpec(block, used_rows),
                      pl.BlockSpec(memory_space=pl.ANY), pl.BlockSpec((None, 1, f2), per_e),
                      pl.BlockSpec(memory_space=pl.ANY), pl.BlockSpec((None, 1, d), per_e)],
            out_specs=pl.BlockSpec(block, rows),
            scratch_shapes=[pltpu.VMEM((2, d, f2), jnp.float32), pltpu.VMEM((2, f, d), jnp.float32),
                            pltpu.VMEM((d, f2), jnp.bfloat16), pltpu.VMEM((f, d), jnp.bfloat16),
                            pltpu.SemaphoreType.DMA((2, 2))]),
        out_shape=jax.ShapeDtypeStruct(xs.shape, jnp.float32),
        compiler_params=pltpu.CompilerParams(dimension_semantics=("arbitrary",),
                                             vmem_limit_bytes=VMEM_LIMIT),
        name="experts",
    )(block_e, nused, block_half, block_next, xs, wu, bu, wd, bd)


def _combine_kernel(dest_ref, dest_next_ref, h_ref, eg_ref, g_ref, ys_ref, o_ref, buf_ref, sems,
                    *, final_norm):
    i = pl.program_id(0)
    t, d = h_ref.shape
    half = i % 2

    def gather_rows(idx_ref, into):
        def issue(tok, carry):
            for j in range(TOP_K):
                _row_copy(ys_ref, idx_ref[tok * TOP_K + j], buf_ref.at[into, j], tok,
                          sems.at[into]).start(priority=j % 2)
            return carry
        lax.fori_loop(0, t, issue, 0, unroll=ISSUE_UNROLL)

    @pl.when(i == 0)
    def _():
        gather_rows(dest_ref, 0)

    @pl.when(i + 1 < pl.num_programs(0))
    def _():
        gather_rows(dest_next_ref, 1 - half)

    for j in range(TOP_K):
        pltpu.make_async_copy(ys_ref.at[pl.ds(0, t * SUBLANES)], buf_ref.at[half, j],
                              sems.at[half]).wait()
    eg = eg_ref[...]
    h = h_ref[...]
    for j in range(TOP_K):
        h = h + eg[:, j:j + 1] * _load_row_tiles(buf_ref.at[half, j], t, d)
    o_ref[...] = _rmsnorm(h, g_ref[...]) if final_norm else h


def _combine(dest, h, eg, g, ys, final_norm):
    n, d = h.shape
    t = MOVE_TILE
    return pl.pallas_call(
        functools.partial(_combine_kernel, final_norm=final_norm),
        grid=(n // t,),
        in_specs=[pl.BlockSpec((t * TOP_K,), lambda i: (i,), memory_space=pltpu.SMEM),
                  pl.BlockSpec((t * TOP_K,), lambda i: (jnp.minimum(i + 1, n // t - 1),),
                               memory_space=pltpu.SMEM),
                  pl.BlockSpec((t, d), lambda i: (i, 0)),
                  pl.BlockSpec((t, 2 * TOP_K), lambda i: (i, 0)),
                  pl.BlockSpec((1, d), lambda i: (0, 0)),
                  pl.BlockSpec(memory_space=pl.ANY)],
        out_specs=pl.BlockSpec((t, d), lambda i: (i, 0)),
        scratch_shapes=[pltpu.VMEM((2, TOP_K, t * SUBLANES, ys.shape[1]), jnp.float32),
                        pltpu.SemaphoreType.DMA((2,))],
        out_shape=jax.ShapeDtypeStruct((n, d), jnp.float32),
        compiler_params=pltpu.CompilerParams(dimension_semantics=("arbitrary",),
                                             vmem_limit_bytes=VMEM_LIMIT),
        name="combine",
    )(dest, dest, h, eg, g, ys)


def _rope_tables(seq):
    half = ROPE_DIM // 2
    inv = jnp.power(jnp.float32(ROPE_THETA), -jnp.arange(half, dtype=jnp.float32) / half)
    ang = jnp.arange(seq, dtype=jnp.float32)[:, None] * inv[None, :]
    cos, sin = jnp.cos(ang), jnp.sin(ang)
    pad = HEAD_DIM - ROPE_DIM
    one, zero = jnp.ones((seq, pad), jnp.float32), jnp.zeros((seq, pad), jnp.float32)
    zh = jnp.zeros((seq, half), jnp.float32)
    reps = LANES // HEAD_DIM
    c = jnp.tile(jnp.concatenate([cos, cos, one], axis=1), (1, reps))
    a = jnp.tile(jnp.concatenate([-sin, zh, zero], axis=1), (1, reps))
    b = jnp.tile(jnp.concatenate([zh, sin, zero], axis=1), (1, reps))
    return c, a, b


def _layer(h, mem, p, tables, final_g):
    b, s, d = h.shape
    n = b * s
    bf = jnp.bfloat16
    x2 = h.reshape(n, d)
    w_in = p["w_in"]
    o_ik = 3 * ATTN_W + IDXQ_W
    o_u = o_ik + IDX_DIM + N_HEADS_IDX
    wa = w_in[:, :o_ik].astype(bf)
    wb = jnp.pad(w_in[:, o_ik:o_u], ((0, 0), (0, LANES - (o_u - o_ik)))).astype(bf)
    wc = w_in[:, o_u:].astype(bf)
    q, k, vt, iq, iklo, ikhi, iw, u, qm = _in_proj(x2, p["norm_mix_g"][None], wa, wb, wc, *tables, b, s)
    km, vm = _mem_kv(mem, p["mem_norm_g"][None], p["w_mem_kv"].astype(bf))
    b3 = lambda a: a.reshape(b, s, a.shape[-1])
    ya = _dsa(b3(iq), b3(iw), b3(q), b3(iklo), b3(ikhi), b3(k), vt).reshape(n, ATTN_W)

    wbd = jnp.zeros((POOL_W, POOL_W), jnp.float32)
    for g in range(N_POOL_GROUPS):
        lo = g * POOL_GROUP_DIM
        wbd = wbd.at[lo:lo + POOL_GROUP_DIM, lo:lo + POOL_GROUP_DIM].set(p["pool_w_group"][g])
    wr = jnp.pad(p["w_router"], ((0, 0), (0, LANES - N_EXPERTS)))
    wr_hi = wr.astype(bf)
    br = jnp.pad(p["b_router"], (0, LANES - N_EXPERTS), constant_values=NEG_BIG)[None]
    hmid, xn2, eg, ei, counts = _merge(
        x2, ya, u, qm, km, vm, p["norm_mix_g"][None], p["w_gate"].astype(bf), p["b_gate"][None],
        p["w_proj_attn"].astype(bf), p["w_proj_pool"].astype(bf), p["w_proj_mem"].astype(bf),
        wbd.astype(bf), p["pool_scale"][None], p["w_out"].astype(bf), p["norm_ffn_g"][None],
        wr_hi, jnp.concatenate([wr_hi, (wr - wr_hi.astype(jnp.float32)).astype(bf)], axis=1), br, s)

    cnt = counts[0, :N_EXPERTS].astype(jnp.int32)
    padded = ((cnt + ROW_BLOCK - 1) // ROW_BLOCK) * ROW_BLOCK
    pend = jnp.cumsum(padded)
    pstart = (pend - padded).astype(jnp.int32)
    n_rows = n * TOP_K + N_EXPERTS * ROW_BLOCK
    n_blocks = n_rows // ROW_BLOCK
    nused = (pend[-1] // ROW_BLOCK).astype(jnp.int32)
    blk = jnp.minimum(jnp.arange(n_blocks, dtype=jnp.int32), nused - 1) * ROW_BLOCK
    block_e = jnp.sum(blk[:, None] >= pend[None, :], axis=1).astype(jnp.int32)
    block_e = jnp.minimum(block_e, N_EXPERTS - 1)
    has_rows = cnt > 0
    run_index = jnp.cumsum(has_rows) - 1
    ids = jnp.arange(N_EXPERTS, dtype=jnp.int32)
    later = jnp.where(has_rows[None, :] & (ids[None, :] > ids[:, None]), ids[None, :], N_EXPERTS)
    next_used = jnp.min(later, axis=1)
    next_used = jnp.where(next_used < N_EXPERTS, next_used, -1).astype(jnp.int32)
    of_block = block_e[:, None] == ids[None, :]
    block_half = jnp.sum(jnp.where(of_block, run_index % 2, 0), axis=1).astype(jnp.int32)
    block_next = jnp.sum(jnp.where(of_block, next_used, 0), axis=1).astype(jnp.int32)
    chosen =ei[:, :TOP_K, None] == jnp.arange(N_EXPERTS, dtype=jnp.int32)
    dest = (jnp.sum(jnp.where(chosen, pstart, 0), axis=-1) + ei[:, TOP_K:]).reshape(-1)

    xs = _dispatch(pstart, cnt, dest, xn2, n_rows)
    ys = _experts(block_e, nused[None], block_half, block_next, xs, p["w_up"], p["b_up"][:, None, :],
                  p["w_down"], p["b_down"][:, None, :])
    g = p["norm_ffn_g"][None] if final_g is None else final_g[None]
    return _combine(dest, hmid, eg, g, ys, final_g is not None).reshape(b, s, d)


def kernel(x, mem, norm_mix_g, w_in, w_gate, b_gate, w_proj_attn, w_proj_pool, w_proj_mem,
           pool_w_group, pool_scale, mem_norm_g, w_mem_kv, w_out, norm_ffn_g,
           w_router, b_router, w_up, b_up, w_down, b_down, norm_final_g):
    b, s, d = x.shape
    assert s % IN_TILE == 0 and s % COUNT_CHUNK == 0 and (b * s) % MOVE_TILE == 0
    assert d == LANES * SUBLANES, "the row-tile layout maps one model row onto one (8, 128) tile"
    stacked = dict(norm_mix_g=norm_mix_g, w_in=w_in, w_gate=w_gate, b_gate=b_gate,
                   w_proj_attn=w_proj_attn, w_proj_pool=w_proj_pool, w_proj_mem=w_proj_mem,
                   pool_w_group=pool_w_group, pool_scale=pool_scale, mem_norm_g=mem_norm_g,
                   w_mem_kv=w_mem_kv, w_out=w_out, norm_ffn_g=norm_ffn_g, w_router=w_router,
                   b_router=b_router, w_up=w_up, b_up=b_up, w_down=w_down, b_down=b_down)
    depth = w_in.shape[0]
    tables = _rope_tables(s)
    h = x
    for l in range(depth):
        p = {name: val[l] for name, val in stacked.items()}
        h = _layer(h, mem, p, tables, norm_final_g if l == depth - 1 else None)
    return h
```

```python
import functools

import jax
import jax.numpy as jnp
import numpy as np
from jax import lax
from jax.experimental import pallas as pl
from jax.experimental.pallas import tpu as pltpu

N_HEADS_ATTN = 8
HEAD_DIM = 64
ROPE_DIM = HEAD_DIM // 4
ROPE_THETA = 500000.0
N_HEADS_IDX = 8
IDX_DIM = 64
TOPK_MAX = 256
N_POOL_GROUPS = 4
POOL_GROUP_DIM = 64
POOL_WINDOWS = (2, 4, 8, 16)
POOL_HALO = 16
N_HEADS_MEM = 4
N_BRANCHES = 3
N_EXPERTS = 32
TOP_K = 4
SWIGLU_ALPHA = 1.702
SWIGLU_LIMIT = 7.0
EPS = 1e-6

ATTN_W = N_HEADS_ATTN * HEAD_DIM
POOL_W = N_POOL_GROUPS * POOL_GROUP_DIM
MEM_W = N_HEADS_MEM * HEAD_DIM
IDXQ_W = N_HEADS_IDX * IDX_DIM

LANES = 128
SUBLANES = 8
VMEM_LIMIT = 56 * 1024 * 1024

IN_TILE = 512
Q_BLOCK = 128
KEY_CHUNK = 256
COUNT_CHUNK = 512
COUNT_ROWS = 32
MERGE_TILE = 512
ROW_BLOCK = 256
UP_TILE = 256
MOVE_TILE = 512
ISSUE_UNROLL = 4

VT_ROWS = LANES + 16
VT_ALL = (N_HEADS_ATTN // 2) * VT_ROWS
LOG2_E = 1.4426950408889634

NEG_BIG = -1e30
INT_MIN = -2 ** 31


def _dot(a, b):
    return jnp.dot(a, b, preferred_element_type=jnp.float32)


def _dot_nt(a, b):
    return lax.dot_general(a, b, (((1,), (1,)), ((), ())), preferred_element_type=jnp.float32)


def _rmsnorm(x, g):
    return x * lax.rsqrt(jnp.mean(x * x, axis=-1, keepdims=True) + EPS) * g


def _store_row_tiles(ref, x):
    rows, width = x.shape
    for j in range(width // LANES):
        ref[pl.ds(j, rows, stride=width // LANES), :] = x[:, j * LANES:(j + 1) * LANES]


def _load_row_tiles(ref, rows, width):
    pieces = width // LANES
    return jnp.concatenate([ref[pl.ds(j, rows, stride=pieces), :] for j in range(pieces)], axis=1)


def _row_tile(ref, row):
    return ref.at[pl.ds(pl.multiple_of(row * SUBLANES, SUBLANES), SUBLANES)]


def _rope128(x, c, a, b):
    return x * c + pltpu.roll(x, LANES - ROPE_DIM // 2, 1) * a + pltpu.roll(x, ROPE_DIM // 2, 1) * b


def _in_proj_kernel(x_ref, g_ref, wa_ref, wb_ref, wc_ref, rc_ref, ra_ref, rb_ref,
                    q_ref, k_ref, vt_ref, iq_ref, iklo_ref, ikhi_ref, iw_ref, u_ref, qm_ref):
    xn = _rmsnorm(x_ref[...], g_ref[...]).astype(jnp.bfloat16)
    rc, ra, rb = rc_ref[...], ra_ref[...], rb_ref[...]
    pa = _dot(xn, wa_ref[...])
    for seg, (ref, scale) in enumerate(((q_ref, LOG2_E * HEAD_DIM ** -0.5), (k_ref, None),
                                        (None, None), (iq_ref, IDX_DIM ** -0.5))):
        for c in range(ATTN_W // LANES):
            lo = seg * ATTN_W + c * LANES
            blk = pa[:, lo:lo + LANES]
            if ref is None:
                vt_ref[c * VT_ROWS:c * VT_ROWS + LANES, :] = blk.T.astype(vt_ref.dtype)
                vt_ref[c * VT_ROWS + LANES:(c + 1) * VT_ROWS, :] = jnp.ones(
                    (VT_ROWS - LANES, blk.shape[0]), vt_ref.dtype)
                continue
            blk = _rope128(blk, rc, ra, rb)
            if scale is not None:
                blk = blk * scale
            ref[:, c * LANES:(c + 1) * LANES] = blk.astype(ref.dtype)
    pb = _dot(xn, wb_ref[...])
    lane = lax.broadcasted_iota(jnp.int32, pb.shape, 1)
    ik = jnp.where(lane < IDX_DIM, _rope128(pb, rc, ra, rb), 0.0)
    iklo_ref[...] = ik.astype(iklo_ref.dtype)
    ikhi_ref[...] = pltpu.roll(ik, IDX_DIM, 1).astype(ikhi_ref.dtype)
    iw_ref[...] = pb * (N_HEADS_IDX ** -0.5)
    pc = _dot(xn, wc_ref[...])
    u_ref[...] = pc[:, :POOL_W]
    qm_ref[...] = (pc[:, POOL_W:] * (HEAD_DIM ** -0.5)).astype(qm_ref.dtype)


def _in_proj(x2, g, wa, wb, wc, rc, ra, rb, batch, seq):
    n, d = x2.shape
    t = IN_TILE
    tiles_per_seq = seq // t
    row = lambda i: (i, 0)
    full = lambda i: (0, 0)
    pos = lambda i: (i % tiles_per_seq, 0)
    bf = jnp.bfloat16
    flat = lambda w, dt: (jax.ShapeDtypeStruct((n, w), dt), pl.BlockSpec((t, w), row))
    outs = (flat(ATTN_W, bf), flat(ATTN_W, bf),
            (jax.ShapeDtypeStruct((batch, VT_ALL, seq), bf),
             pl.BlockSpec((None, VT_ALL, t), lambda i: (i // tiles_per_seq, 0, i % tiles_per_seq))),
            flat(IDXQ_W, bf), flat(LANES, bf), flat(LANES, bf), flat(LANES, jnp.float32),
            flat(POOL_W, jnp.float32), flat(MEM_W, bf))
    return pl.pallas_call(
        _in_proj_kernel,
        grid=(n // t,),
        in_specs=[pl.BlockSpec((t, d), row), pl.BlockSpec((1, d), full),
                  pl.BlockSpec(wa.shape, full), pl.BlockSpec(wb.shape, full),
                  pl.BlockSpec(wc.shape, full),
                  pl.BlockSpec((t, LANES), pos), pl.BlockSpec((t, LANES), pos),
                  pl.BlockSpec((t, LANES), pos)],
        out_specs=tuple(o[1] for o in outs),
        out_shape=tuple(o[0] for o in outs),
        compiler_params=pltpu.CompilerParams(dimension_semantics=("arbitrary",),
                                             vmem_limit_bytes=VMEM_LIMIT),
        name="in_proj",
    )(x2, g, wa, wb, wc, rc, ra, rb)


def _mem_kv_kernel(mem_ref, g_ref, w_ref, km_ref, vm_ref):
    mn = _rmsnorm(mem_ref[...], g_ref[...]).astype(jnp.bfloat16)
    kv = _dot(mn, w_ref[...])
    km, vm = kv[:, :MEM_W], kv[:, MEM_W:]
    lane = lax.broadcasted_iota(jnp.int32, km.shape, 1)
    for h in range(N_HEADS_MEM):
        in_head = (lane >= h * HEAD_DIM) & (lane < (h + 1) * HEAD_DIM)
        km_ref[h] = jnp.where(in_head, km, 0.0).astype(km_ref.dtype)
        vm_ref[h] = jnp.where(in_head, vm, 0.0).astype(vm_ref.dtype)


def _mem_kv(mem, g, w):
    b, m, d = mem.shape
    out = jax.ShapeDtypeStruct((b, N_HEADS_MEM, m, MEM_W), jnp.bfloat16)
    return pl.pallas_call(
        _mem_kv_kernel,
        grid=(b,),
        in_specs=[pl.BlockSpec((None, m, d), lambda i: (i, 0, 0)),
                  pl.BlockSpec((1, d), lambda i: (0, 0)),
                  pl.BlockSpec(w.shape, lambda i: (0, 0))],
        out_specs=(pl.BlockSpec((None, N_HEADS_MEM, m, MEM_W), lambda i: (i, 0, 0, 0)),
                   pl.BlockSpec((None, N_HEADS_MEM, m, MEM_W), lambda i: (i, 0, 0, 0))),
        out_shape=(out, out),
        compiler_params=pltpu.CompilerParams(dimension_semantics=("arbitrary",),
                                             vmem_limit_bytes=VMEM_LIMIT),
        name="mem_kv",
    )(mem, g, w)


def _dsa_kernel(iq_ref, iw_ref, q_ref, iklo_ref, ikhi_ref, k_ref, vt_ref, o_ref,
                key_ref, iqt_ref, qt_ref, bias_ref, s_ref, acc_ref, *, n_top, idx_bits):
    qb = pl.program_id(1)
    n_chunks = (qb * Q_BLOCK + Q_BLOCK + KEY_CHUNK - 1) // KEY_CHUNK
    n_count_chunks = (qb * Q_BLOCK + Q_BLOCK + COUNT_CHUNK - 1) // COUNT_CHUNK
    q_pos = qb * Q_BLOCK + lax.broadcasted_iota(jnp.int32, (1, Q_BLOCK), 1)
    row_k = lax.broadcasted_iota(jnp.int32, (KEY_CHUNK, 1), 0)
    row_c = lax.broadcasted_iota(jnp.int32, (COUNT_CHUNK, 1), 0)
    row_d = lax.broadcasted_iota(jnp.int32, (LANES, Q_BLOCK), 0)
    bf = jnp.bfloat16

    def chunk_off(c):
        return pl.multiple_of(c * KEY_CHUNK, KEY_CHUNK)

    wt = iw_ref[...].T
    for j in range(N_HEADS_IDX // 2):
        iqt_ref[j // 2, :, (j % 2) * Q_BLOCK:(j % 2 + 1) * Q_BLOCK] = (
            iq_ref[:, j * LANES:(j + 1) * LANES].astype(jnp.float32).T.astype(bf))
    for j in range(N_HEADS_ATTN // 2):
        pair_t = q_ref[:, j * LANES:(j + 1) * LANES].astype(jnp.float32).T
        qt_ref[j, :, :Q_BLOCK] = jnp.where(row_d < HEAD_DIM, pair_t, 0.0).astype(bf)
        qt_ref[j, :, Q_BLOCK:] = jnp.where(row_d >= HEAD_DIM, pair_t, 0.0).astype(bf)

    def score_chunk(c, carry):
        off = pl.multiple_of(c * COUNT_CHUNK, COUNT_CHUNK)
        acc = jnp.zeros((COUNT_CHUNK, Q_BLOCK), jnp.float32)
        for parity, ik_ref in enumerate((iklo_ref, ikhi_ref)):
            ik = ik_ref[pl.ds(off, COUNT_CHUNK), :]
            for g in range(N_HEADS_IDX // 4):
                dots = _dot(ik, iqt_ref[g])
                for side in range(2):
                    head = 2 * (2 * g + side) + parity
                    w = wt[IDX_DIM + head:IDX_DIM + head + 1, :]
                    acc = acc + w * jnp.maximum(dots[:, side * Q_BLOCK:(side + 1) * Q_BLOCK], 0.0)
        acc = jnp.where(off + row_c <= q_pos, acc, -jnp.inf)
        bits = lax.bitcast_convert_type(acc, jnp.int32)
        key_ref[pl.ds(off, COUNT_CHUNK), :] = bits ^ ((bits >> 31) & 0x7FFFFFFF)
        return carry

    lax.fori_loop(0, n_count_chunks, score_chunk, 0)

    def count(pred):
        def body(c, cnt):
            off = pl.multiple_of(c * COUNT_CHUNK, COUNT_CHUNK)
            hit = jnp.where(pred(key_ref[pl.ds(off, COUNT_CHUNK), :], off + row_c), 1.0, 0.0)
            return cnt + jnp.sum(hit.reshape(COUNT_CHUNK // COUNT_ROWS, COUNT_ROWS, Q_BLOCK), axis=0)
        cnt = lax.fori_loop(0, n_count_chunks, body, jnp.zeros((COUNT_ROWS, Q_BLOCK), jnp.float32))
        return jnp.sum(cnt, axis=0, keepdims=True)

    k_f = jnp.float32(n_top)
    zero = jnp.zeros((1, Q_BLOCK), jnp.int32)
    n_nonneg = count(lambda kk, pos: kk >= zero)
    n_stored = (n_count_chunks * COUNT_CHUNK).astype(jnp.float32)
    thr = jnp.where(n_nonneg >= k_f, zero, INT_MIN)
    n_ge = jnp.where(n_nonneg >= k_f, n_nonneg, n_stored)

    def thr_bit(i, carry):
        cur, n_cur = carry
        cand = cur + jnp.left_shift(jnp.int32(1), 30 - i)
        n_cand = count(lambda kk, pos: kk >= cand)
        keep = n_cand >= k_f
        return jnp.where(keep, cand, cur), jnp.where(keep, n_cand, n_cur)

    thr, n_ge = lax.fori_loop(0, 31, thr_bit, (thr, n_ge))

    def tie_search():
        need = k_f - count(lambda kk, pos: kk > thr)

        def tie_bit(i, cur):
            cand = cur + jnp.left_shift(jnp.int32(1), idx_bits - 1 - i)
            below = count(lambda kk, pos: (kk == thr) & (pos < cand))
            return jnp.where(below < need, cand, cur)
        return lax.fori_loop(0, idx_bits, tie_bit, zero)

    has_surplus_ties = jnp.max(n_ge) > k_f
    tie_pos = lax.cond(has_surplus_ties, tie_search, lambda: zero + (2 ** idx_bits - 1))

    acc_ref[...] = jnp.zeros(acc_ref.shape, jnp.float32)

    n_pairs = N_HEADS_ATTN // 2

    def masked_logits(c, slot):
        off = chunk_off(jnp.minimum(c, n_chunks - 1))
        kk = key_ref[pl.ds(off, KEY_CHUNK), :]
        pos = c * KEY_CHUNK + row_k
        sel = ((kk > thr) | ((kk == thr) & (pos <= tie_pos))) & (pos <= q_pos)
        bias = jnp.where(sel, 0.0, NEG_BIG)
        bias_ref[slot] = jnp.concatenate([bias, bias], axis=1)
        for j in range(n_pairs):
            kp = k_ref[pl.ds(off, KEY_CHUNK), j * LANES:(j + 1) * LANES]
            s_ref[slot, j] = _dot(kp, qt_ref[j]) + bias_ref[slot]

    def softmax_pv(c, slot, ms, ls):
        off = chunk_off(jnp.minimum(c, n_chunks - 1))
        new_ms, new_ls = [], []
        for j in range(n_pairs):
            s = s_ref[slot, j]
            m_new = jnp.maximum(ms[j], jnp.max(s, axis=0, keepdims=True))
            alpha = jnp.exp2(ms[j] - m_new)
            p = jnp.exp2(s - m_new).astype(bf)
            vt = vt_ref[j * VT_ROWS:(j + 1) * VT_ROWS, pl.ds(off, KEY_CHUNK)]
            pv = _dot(vt, p)
            new_ls.append(alpha * ls[j] + pv[LANES:LANES + 1, :])
            new_ms.append(m_new)
            lo, mid, hi = j * LANES, j * LANES + HEAD_DIM, (j + 1) * LANES
            acc_ref[lo:mid, :] = alpha[:, :Q_BLOCK] * acc_ref[lo:mid, :] + pv[:HEAD_DIM, :Q_BLOCK]
            acc_ref[mid:hi, :] = alpha[:, Q_BLOCK:] * acc_ref[mid:hi, :] + pv[HEAD_DIM:LANES, Q_BLOCK:]
        return tuple(new_ms), tuple(new_ls)

    masked_logits(0, 0)

    def attn_two_chunks(i, carry):
        ms, ls = carry
        masked_logits(2 * i + 1, 1)
        ms, ls = softmax_pv(2 * i, 0, ms, ls)
        masked_logits(2 * i + 2, 0)
        return softmax_pv(2 * i + 1, 1, ms, ls)

    init = (tuple(jnp.full((1, 2 * Q_BLOCK), NEG_BIG, jnp.float32) for _ in range(n_pairs)),
            tuple(jnp.zeros((1, 2 * Q_BLOCK), jnp.float32) for _ in range(n_pairs)))
    _, ls = lax.fori_loop(0, (n_chunks + 1) // 2, attn_two_chunks, init)
    for j in range(n_pairs):
        lo, mid, hi = j * LANES, j * LANES + HEAD_DIM, (j + 1) * LANES
        even = acc_ref[lo:mid, :] / ls[j][:, :Q_BLOCK]
        odd = acc_ref[mid:hi, :] / ls[j][:, Q_BLOCK:]
        o_ref[:, lo:hi] = jnp.concatenate([even, odd], axis=0).T.astype(o_ref.dtype)


def _dsa(iq, iw, q, iklo, ikhi, k, vt):
    b, s, _ = q.shape
    n_top = min(TOPK_MAX, s // 4)
    idx_bits = max(1, int(np.ceil(np.log2(s))))
    qblk = lambda w: pl.BlockSpec((None, Q_BLOCK, w), lambda bi, qi: (bi, qi, 0))
    keys = lambda w: pl.BlockSpec((None, s, w), lambda bi, qi: (bi, 0, 0))
    return pl.pallas_call(
        functools.partial(_dsa_kernel, n_top=n_top, idx_bits=idx_bits),
        grid=(b, s // Q_BLOCK),
        in_specs=[qblk(IDXQ_W), qblk(LANES), qblk(ATTN_W), keys(LANES), keys(LANES),
                  keys(ATTN_W), pl.BlockSpec((None, VT_ALL, s), lambda bi, qi: (bi, 0, 0))],
        out_specs=qblk(ATTN_W),
        out_shape=jax.ShapeDtypeStruct((b, s, ATTN_W), jnp.bfloat16),
        scratch_shapes=[pltpu.VMEM((s, Q_BLOCK), jnp.int32),
                        pltpu.VMEM((N_HEADS_IDX // 4, LANES, 2 * Q_BLOCK), jnp.bfloat16),
                        pltpu.VMEM((N_HEADS_ATTN // 2, LANES, 2 * Q_BLOCK), jnp.bfloat16),
                        pltpu.VMEM((2, KEY_CHUNK, 2 * Q_BLOCK), jnp.float32),
                        pltpu.VMEM((2, N_HEADS_ATTN // 2, KEY_CHUNK, 2 * Q_BLOCK), jnp.float32),
                        pltpu.VMEM((ATTN_W, Q_BLOCK), jnp.float32)],
        compiler_params=pltpu.CompilerParams(dimension_semantics=("arbitrary", "arbitrary"),
                                             vmem_limit_bytes=VMEM_LIMIT),
        name="dsa",
    )(iq, iw, q, iklo, ikhi, k, vt)


def _merge_kernel(x_ref, ya_ref, u_ref, uprev_ref, qm_ref, km_ref, vm_ref,
                  gmix_ref, wg_ref, bg_ref, wpa_ref, wpp_ref, wpm_ref, wbd_ref, psc_ref,
                  wo_ref, gffn_ref, wrh_ref, wrhl_ref, br_ref,
                  h_ref, xn2_ref, eg_ref, ei_ref, cnt_ref, carry_ref, *, tiles_per_seq):
    i = pl.program_id(0)
    t = x_ref.shape[0]
    d = x_ref.shape[1]
    bf = jnp.bfloat16
    x = x_ref[...]
    xn = _rmsnorm(x, gmix_ref[...]).astype(bf)

    tile_in_seq = i % tiles_per_seq
    u = u_ref[...]
    halo = jnp.where(tile_in_seq == 0, 0.0, uprev_ref[...])
    ext = jnp.concatenate([halo, u], axis=0)
    lane_p = lax.broadcasted_iota(jnp.int32, (t, POOL_W), 1)
    pos1 = (tile_in_seq * t + lax.broadcasted_iota(jnp.int32, (t, 1), 0) + 1).astype(jnp.float32)
    pooled = None
    run, width = ext, 1
    for g, w in enumerate(POOL_WINDOWS):
        while width < w:
            run = run[width:] + run[:-width]
            width *= 2
        start = POOL_HALO + 1 - w
        mean = run[start:start + t] / jnp.minimum(pos1, float(w))
        pooled = mean if pooled is None else jnp.where(lane_p >= g * POOL_GROUP_DIM, mean, pooled)
    pooled = pooled - u
    mixed = _dot(pooled.astype(bf), wbd_ref[...]) * psc_ref[...]
    y_pool = _dot(mixed.astype(bf), wpp_ref[...])

    qm = qm_ref[...]
    probs = []
    for h in range(N_HEADS_MEM):
        s = _dot_nt(qm, km_ref[h])
        p = jnp.exp(s - jnp.max(s, axis=1, keepdims=True))
        probs.append((p / jnp.sum(p, axis=1, keepdims=True)).astype(bf))
    y_mem = _dot(probs[0], vm_ref[0])
    for h in range(1, N_HEADS_MEM):
        y_mem = y_mem + _dot(probs[h], vm_ref[h])
    y_mem = _dot(y_mem.astype(bf), wpm_ref[...])

    y_attn = _dot(ya_ref[...], wpa_ref[...])

    def gate(br):
        z = _dot(xn, wg_ref[:, br * d:(br + 1) * d]) + bg_ref[:, br * d:(br + 1) * d]
        return 1.0 / (1.0 + jnp.exp(-z))

    merged = gate(0) * y_attn + gate(1) * y_pool + gate(2) * y_mem
    h = x + _dot(merged.astype(bf), wo_ref[...])
    h_ref[...] = h
    xn2 = _rmsnorm(h, gffn_ref[...])
    _store_row_tiles(xn2_ref, xn2)

    x_hi = xn2.astype(bf)
    x_lo = (xn2 - x_hi.astype(jnp.float32)).astype(bf)
    by_hi = _dot(x_hi, wrhl_ref[...])
    logits = (by_hi[:, :LANES] + (_dot(x_lo, wrh_ref[...]) + by_hi[:, LANES:])) + br_ref[...]
    lane_e = lax.broadcasted_iota(jnp.int32, logits.shape, 1).astype(jnp.float32)
    work = logits
    vals, onehots, ids = [], [], []
    for _ in range(TOP_K):
        mx = jnp.max(work, axis=1, keepdims=True)
        idx = jnp.min(jnp.where(work == mx, lane_e, float(LANES)), axis=1, keepdims=True)
        oh = lane_e == idx
        vals.append(mx)
        ids.append(idx)
        onehots.append(oh)
        work = jnp.where(oh, -jnp.inf, work)
    exps = [jnp.exp(v - vals[0]) for v in vals]
    denom = exps[0] + exps[1] + exps[2] + exps[3]

    @pl.when(i == 0)
    def _():
        carry_ref[...] = jnp.zeros(carry_ref.shape, jnp.float32)

    member = jnp.where(onehots[0] | onehots[1] | onehots[2] | onehots[3], 1.0, 0.0)
    r_io = lax.broadcasted_iota(jnp.int32, (t, t), 0)
    c_io = lax.broadcasted_iota(jnp.int32, (t, t), 1)
    earlier = jnp.where(c_io < r_io, 1.0, 0.0).astype(bf)
    before = _dot(earlier, member.astype(bf)) + carry_ref[...]
    carry_ref[...] = carry_ref[...] + jnp.sum(member, axis=0, keepdims=True)
    cnt_ref[...] = carry_ref[...]

    lane8 = lax.broadcasted_iota(jnp.int32, (t, 2 * TOP_K), 1)
    eg = jnp.zeros((t, 2 * TOP_K), jnp.float32)
    ei = jnp.zeros((t, 2 * TOP_K), jnp.float32)
    for j in range(TOP_K):
        rank = jnp.sum(jnp.where(onehots[j], before, 0.0), axis=1, keepdims=True)
        eg = jnp.where(lane8 == j, exps[j] / denom, eg)
        ei = jnp.where(lane8 == j, ids[j], ei)
        ei = jnp.where(lane8 == TOP_K + j, rank, ei)
    eg_ref[...] = eg
    ei_ref[...] = ei.astype(jnp.int32)


def _merge(x2, ya, u, qm, km, vm, gmix, wg, bg, wpa, wpp, wpm, wbd, psc, wo, gffn, wrh, wrhl, br, seq):
    n, d = x2.shape
    t = MERGE_TILE
    tiles_per_seq = seq // t
    m = km.shape[2]
    row = lambda i: (i, 0)
    full = lambda i: (0, 0)
    per_batch = lambda i: (i // tiles_per_seq, 0, 0, 0)
    halo_blocks = t // POOL_HALO
    prev = lambda i: (jnp.maximum(i * halo_blocks - 1, 0), 0)
    consts = (gmix, wg, bg, wpa, wpp, wpm, wbd, psc, wo, gffn, wrh, wrhl, br)
    pieces = d // LANES
    out_shape = (jax.ShapeDtypeStruct((n, d), jnp.float32),
                 jax.ShapeDtypeStruct((n * pieces, LANES), jnp.float32),
                 jax.ShapeDtypeStruct((n, 2 * TOP_K), jnp.float32),
                 jax.ShapeDtypeStruct((n, 2 * TOP_K), jnp.int32),
                 jax.ShapeDtypeStruct((1, LANES), jnp.float32))
    return pl.pallas_call(
        functools.partial(_merge_kernel, tiles_per_seq=tiles_per_seq),
        grid=(n // t,),
        in_specs=[pl.BlockSpec((t, d), row), pl.BlockSpec((t, ATTN_W), row),
                  pl.BlockSpec((t, POOL_W), row), pl.BlockSpec((POOL_HALO, POOL_W), prev),
                  pl.BlockSpec((t, MEM_W), row),
                  pl.BlockSpec((None, N_HEADS_MEM, m, MEM_W), per_batch),
                  pl.BlockSpec((None, N_HEADS_MEM, m, MEM_W), per_batch)]
                 + [pl.BlockSpec(c.shape, full) for c in consts],
        out_specs=(pl.BlockSpec((t, d), row), pl.BlockSpec((t * pieces, LANES), row),
                   pl.BlockSpec((t, 2 * TOP_K), row), pl.BlockSpec((t, 2 * TOP_K), row),
                   pl.BlockSpec((1, LANES), full)),
        out_shape=out_shape,
        scratch_shapes=[pltpu.VMEM((1, LANES), jnp.float32)],
        compiler_params=pltpu.CompilerParams(dimension_semantics=("arbitrary",),
                                             vmem_limit_bytes=VMEM_LIMIT),
        name="merge",
    )(x2, ya, u, u, qm, km, vm, *consts)


def _row_copy(src_ref, src_row, dst_ref, dst_row, sem):
    return pltpu.make_async_copy(_row_tile(src_ref, src_row), _row_tile(dst_ref, dst_row), sem)


def _dispatch_kernel(pstart_ref, cnt_ref, dest_ref, x_ref, xs_ref, zero_ref, sem):
    t = x_ref.shape[0] // SUBLANES
    block_tiles = ROW_BLOCK * SUBLANES

    @pl.when(pl.program_id(0) == 0)
    def _():
        zero_ref[...] = jnp.zeros(zero_ref.shape, zero_ref.dtype)

        def for_each_pad_row(act):
            def per_expert(e, carry):
                n_pad = (ROW_BLOCK - cnt_ref[e] % ROW_BLOCK) % ROW_BLOCK
                first = pstart_ref[e] + cnt_ref[e]

                def per_row(r, c):
                    act(_row_copy(zero_ref, 0, xs_ref, first + r, sem))
                    return c
                return lax.fori_loop(0, n_pad, per_row, carry)
            lax.fori_loop(0, N_EXPERTS, per_expert, 0)

        def for_each_unused_block(act):
            n_blocks = xs_ref.shape[0] // block_tiles
            first = (pstart_ref[N_EXPERTS - 1] + cnt_ref[N_EXPERTS - 1] + ROW_BLOCK - 1) // ROW_BLOCK

            def per_block(b, c):
                rows = pl.ds(pl.multiple_of(b * block_tiles, block_tiles), block_tiles)
                act(pltpu.make_async_copy(zero_ref, xs_ref.at[rows], sem))
                return c
            lax.fori_loop(first, n_blocks, per_block, 0)

        for act in (lambda cp: cp.start(), lambda cp: cp.wait()):
            for_each_pad_row(act)
            for_each_unused_block(act)

    def issue(tok, carry):
        for j in range(TOP_K):
            _row_copy(x_ref, tok, xs_ref, dest_ref[tok * TOP_K + j], sem).start(priority=j % 2)
        return carry

    lax.fori_loop(0, t, issue, 0, unroll=ISSUE_UNROLL)
    for j in range(TOP_K):
        pltpu.make_async_copy(x_ref, xs_ref.at[pl.ds(0, x_ref.shape[0])], sem).wait()


def _dispatch(pstart, cnt, dest, xn2, n_rows):
    tiles, lanes = xn2.shape
    n = tiles // SUBLANES
    t = MOVE_TILE
    return pl.pallas_call(
        _dispatch_kernel,
        grid_spec=pltpu.PrefetchScalarGridSpec(
            num_scalar_prefetch=2,
            grid=(n // t,),
            in_specs=[pl.BlockSpec((t * TOP_K,), lambda i, ps, ct: (i,), memory_space=pltpu.SMEM),
                      pl.BlockSpec((t * SUBLANES, lanes), lambda i, ps, ct: (i, 0))],
            out_specs=pl.BlockSpec(memory_space=pl.ANY),
            scratch_shapes=[pltpu.VMEM((ROW_BLOCK * SUBLANES, lanes), jnp.float32),
                            pltpu.SemaphoreType.DMA(())]),
        out_shape=jax.ShapeDtypeStruct((n_rows * SUBLANES, lanes), jnp.float32),
        compiler_params=pltpu.CompilerParams(dimension_semantics=("arbitrary",),
                                             vmem_limit_bytes=VMEM_LIMIT),
        name="dispatch",
    )(pstart, cnt, dest, xn2)


def _experts_kernel(be_ref, nused_ref, half_ref, next_ref, xs_ref, wu_hbm, bu_ref, wd_hbm, bd_ref,
                    ys_ref, wu32_ref, wd32_ref, wu16_ref, wd16_ref, sems):
    i = pl.program_id(0)
    d, f = wu16_ref.shape[0], wd16_ref.shape[0]
    half = half_ref[i]

    def weight_copies(e, into):
        return (pltpu.make_async_copy(wu_hbm.at[e], wu32_ref.at[into], sems.at[0, into]),
                pltpu.make_async_copy(wd_hbm.at[e], wd32_ref.at[into], sems.at[1, into]))

    @pl.when(i == 0)
    def _():
        for cp in weight_copies(be_ref[0], 0):
            cp.start()

    @pl.when((i == 0) | (be_ref[i] != be_ref[jnp.maximum(i - 1, 0)]))
    def _():
        for cp in weight_copies(be_ref[i], half):
            cp.wait()
        wu16_ref[...] = wu32_ref[half].astype(wu16_ref.dtype)
        wd16_ref[...] = wd32_ref[half].astype(wd16_ref.dtype)

        @pl.when(next_ref[i] >= 0)
        def _():
            for cp in weight_copies(next_ref[i], 1 - half):
                cp.start()

    @pl.when(i < nused_ref[0])
    def _():
        x = _load_row_tiles(xs_ref, ROW_BLOCK, d).astype(jnp.bfloat16)
        acts = []
        for lo in range(0, f, UP_TILE):
            glu_cols, lin_cols = slice(lo, lo + UP_TILE), slice(f + lo, f + lo + UP_TILE)
            glu = jnp.minimum(_dot(x, wu16_ref[:, glu_cols]) + bu_ref[:, glu_cols], SWIGLU_LIMIT)
            lin = jnp.clip(_dot(x, wu16_ref[:, lin_cols]) + bu_ref[:, lin_cols],
                           -SWIGLU_LIMIT, SWIGLU_LIMIT)
            act = glu * (1.0 / (1.0 + jnp.exp(-SWIGLU_ALPHA * glu))) * (lin + 1.0)
            acts.append(act.astype(jnp.bfloat16))
        act = jnp.concatenate(acts, axis=1)
        _store_row_tiles(ys_ref, _dot(act, wd16_ref[...]) + bd_ref[...])

    @pl.when(i >= nused_ref[0])
    def _():
        ys_ref[...] = jnp.zeros(ys_ref.shape, ys_ref.dtype)


def _experts(block_e, nused, block_half, block_next, xs, wu, bu, wd, bd):
    tiles, lanes = xs.shape
    e, d, f2 = wu.shape
    f = wd.shape[1]
    block = (ROW_BLOCK * SUBLANES, lanes)
    rows = lambda i, be, nu, hf, nx: (i, 0)
    used_rows = lambda i, be, nu, hf, nx: (jnp.minimum(i, nu[0] - 1), 0)
    per_e = lambda i, be, nu, hf, nx: (be[i], 0, 0)
    return pl.pallas_call(
        _experts_kernel,
        grid_spec=pltpu.PrefetchScalarGridSpec(
            num_scalar_prefetch=4,
            grid=(tiles // block[0],),
            in_specs=[pl.BlockSpec(block, used_rows),
                      pl.BlockSpec(memory_space=pl.ANY), pl.BlockSpec((None, 1, f2), per_e),
                      pl.BlockSpec(memory_space=pl.ANY), pl.BlockSpec((None, 1, d), per_e)],
            out_specs=pl.BlockSpec(block, rows),
            scratch_shapes=[pltpu.VMEM((2, d, f2), jnp.float32), pltpu.VMEM((2, f, d), jnp.float32),
                            pltpu.VMEM((d, f2), jnp.bfloat16), pltpu.VMEM((f, d), jnp.bfloat16),
                            pltpu.SemaphoreType.DMA((2, 2))]),
        out_shape=jax.ShapeDtypeStruct(xs.shape, jnp.float32),
        compiler_params=pltpu.CompilerParams(dimension_semantics=("arbitrary",),
                                             vmem_limit_bytes=VMEM_LIMIT),
        name="experts",
    )(block_e, nused, block_half, block_next, xs, wu, bu, wd, bd)


def _combine_kernel(dest_ref, dest_next_ref, h_ref, eg_ref, g_ref, ys_ref, o_ref, buf_ref, sems,
                    *, final_norm):
    i = pl.program_id(0)
    t, d = h_ref.shape
    half = i % 2

    def gather_rows(idx_ref, into):
        def issue(tok, carry):
            for j in range(TOP_K):
                _row_copy(ys_ref, idx_ref[tok * TOP_K + j], buf_ref.at[into, j], tok,
                          sems.at[into]).start(priority=j % 2)
            return carry
        lax.fori_loop(0, t, issue, 0, unroll=ISSUE_UNROLL)

    @pl.when(i == 0)
    def _():
        gather_rows(dest_ref, 0)

    @pl.when(i + 1 < pl.num_programs(0))
    def _():
        gather_rows(dest_next_ref, 1 - half)

    for j in range(TOP_K):
        pltpu.make_async_copy(ys_ref.at[pl.ds(0, t * SUBLANES)], buf_ref.at[half, j],
                              sems.at[half]).wait()
    eg = eg_ref[...]
    h = h_ref[...]
    for j in range(TOP_K):
        h = h + eg[:, j:j + 1] * _load_row_tiles(buf_ref.at[half, j], t, d)
    o_ref[...] = _rmsnorm(h, g_ref[...]) if final_norm else h


def _combine(dest, h, eg, g, ys, final_norm):
    n, d = h.shape
    t = MOVE_TILE
    return pl.pallas_call(
        functools.partial(_combine_kernel, final_norm=final_norm),
        grid=(n // t,),
        in_specs=[pl.BlockSpec((t * TOP_K,), lambda i: (i,), memory_space=pltpu.SMEM),
                  pl.BlockSpec((t * TOP_K,), lambda i: (jnp.minimum(i + 1, n // t - 1),),
                               memory_space=pltpu.SMEM),
                  pl.BlockSpec((t, d), lambda i: (i, 0)),
                  pl.BlockSpec((t, 2 * TOP_K), lambda i: (i, 0)),
                  pl.BlockSpec((1, d), lambda i: (0, 0)),
                  pl.BlockSpec(memory_space=pl.ANY)],
        out_specs=pl.BlockSpec((t, d), lambda i: (i, 0)),
        scratch_shapes=[pltpu.VMEM((2, TOP_K, t * SUBLANES, ys.shape[1]), jnp.float32),
                        pltpu.SemaphoreType.DMA((2,))],
        out_shape=jax.ShapeDtypeStruct((n, d), jnp.float32),
        compiler_params=pltpu.CompilerParams(dimension_semantics=("arbitrary",),
                                             vmem_limit_bytes=VMEM_LIMIT),
        name="combine",
    )(dest, dest, h, eg, g, ys)


def _rope_tables(seq):
    half = ROPE_DIM // 2
    inv = jnp.power(jnp.float32(ROPE_THETA), -jnp.arange(half, dtype=jnp.float32) / half)
    ang = jnp.arange(seq, dtype=jnp.float32)[:, None] * inv[None, :]
    cos, sin = jnp.cos(ang), jnp.sin(ang)
    pad = HEAD_DIM - ROPE_DIM
    one, zero = jnp.ones((seq, pad), jnp.float32), jnp.zeros((seq, pad), jnp.float32)
    zh = jnp.zeros((seq, half), jnp.float32)
    reps = LANES // HEAD_DIM
    c = jnp.tile(jnp.concatenate([cos, cos, one], axis=1), (1, reps))
    a = jnp.tile(jnp.concatenate([-sin, zh, zero], axis=1), (1, reps))
    b = jnp.tile(jnp.concatenate([zh, sin, zero], axis=1), (1, reps))
    return c, a, b


def _layer(h, mem, p, tables, final_g):
    b, s, d = h.shape
    n = b * s
    bf = jnp.bfloat16
    x2 = h.reshape(n, d)
    w_in = p["w_in"]
    o_ik = 3 * ATTN_W + IDXQ_W
    o_u = o_ik + IDX_DIM + N_HEADS_IDX
    wa = w_in[:, :o_ik].astype(bf)
    wb = jnp.pad(w_in[:, o_ik:o_u], ((0, 0), (0, LANES - (o_u - o_ik)))).astype(bf)
    wc = w_in[:, o_u:].astype(bf)
    q, k, vt, iq, iklo, ikhi, iw, u, qm = _in_proj(x2, p["norm_mix_g"][None], wa, wb, wc, *tables, b, s)
    km, vm = _mem_kv(mem, p["mem_norm_g"][None], p["w_mem_kv"].astype(bf))
    b3 = lambda a: a.reshape(b, s, a.shape[-1])
    ya = _dsa(b3(iq), b3(iw), b3(q), b3(iklo), b3(ikhi), b3(k), vt).reshape(n, ATTN_W)

    wbd = jnp.zeros((POOL_W, POOL_W), jnp.float32)
    for g in range(N_POOL_GROUPS):
        lo = g * POOL_GROUP_DIM
        wbd = wbd.at[lo:lo + POOL_GROUP_DIM, lo:lo + POOL_GROUP_DIM].set(p["pool_w_group"][g])
    wr = jnp.pad(p["w_router"], ((0, 0), (0, LANES - N_EXPERTS)))
    wr_hi = wr.astype(bf)
    br = jnp.pad(p["b_router"], (0, LANES - N_EXPERTS), constant_values=NEG_BIG)[None]
    hmid, xn2, eg, ei, counts = _merge(
        x2, ya, u, qm, km, vm, p["norm_mix_g"][None], p["w_gate"].astype(bf), p["b_gate"][None],
        p["w_proj_attn"].astype(bf), p["w_proj_pool"].astype(bf), p["w_proj_mem"].astype(bf),
        wbd.astype(bf), p["pool_scale"][None], p["w_out"].astype(bf), p["norm_ffn_g"][None],
        wr_hi, jnp.concatenate([wr_hi, (wr - wr_hi.astype(jnp.float32)).astype(bf)], axis=1), br, s)

    cnt = counts[0, :N_EXPERTS].astype(jnp.int32)
    padded = ((cnt + ROW_BLOCK - 1) // ROW_BLOCK) * ROW_BLOCK
    pend = jnp.cumsum(padded)
    pstart = (pend - padded).astype(jnp.int32)
    n_rows = n * TOP_K + N_EXPERTS * ROW_BLOCK
    n_blocks = n_rows // ROW_BLOCK
    nused = (pend[-1] // ROW_BLOCK).astype(jnp.int32)
    blk = jnp.minimum(jnp.arange(n_blocks, dtype=jnp.int32), nused - 1) * ROW_BLOCK
    block_e = jnp.sum(blk[:, None] >= pend[None, :], axis=1).astype(jnp.int32)
    block_e = jnp.minimum(block_e, N_EXPERTS - 1)
    has_rows = cnt > 0
    run_index = jnp.cumsum(has_rows) - 1
    ids = jnp.arange(N_EXPERTS, dtype=jnp.int32)
    later = jnp.where(has_rows[None, :] & (ids[None, :] > ids[:, None]), ids[None, :], N_EXPERTS)
    next_used = jnp.min(later, axis=1)
    next_used = jnp.where(next_used < N_EXPERTS, next_used, -1).astype(jnp.int32)
    of_block = block_e[:, None] == ids[None, :]
    block_half = jnp.sum(jnp.where(of_block, run_index % 2, 0), axis=1).astype(jnp.int32)
    block_next = jnp.sum(jnp.where(of_block, next_used, 0), axis=1).astype(jnp.int32)
    chosen =ei[:, :TOP_K, None] == jnp.arange(N_EXPERTS, dtype=jnp.int32)
    dest = (jnp.sum(jnp.where(chosen, pstart, 0), axis=-1) + ei[:, TOP_K:]).reshape(-1)

    xs = _dispatch(pstart, cnt, dest, xn2, n_rows)
    ys = _experts(block_e, nused[None], block_half, block_next, xs, p["w_up"], p["b_up"][:, None, :],
                  p["w_down"], p["b_down"][:, None, :])
    g = p["norm_ffn_g"][None] if final_g is None else final_g[None]
    return _combine(dest, hmid, eg, g, ys, final_g is not None).reshape(b, s, d)


def kernel(x, mem, norm_mix_g, w_in, w_gate, b_gate, w_proj_attn, w_proj_pool, w_proj_mem,
           pool_w_group, pool_scale, mem_norm_g, w_mem_kv, w_out, norm_ffn_g,
           w_router, b_router, w_up, b_up, w_down, b_down, norm_final_g):
    b, s, d = x.shape
    assert s % IN_TILE == 0 and s % COUNT_CHUNK == 0 and (b * s) % MOVE_TILE == 0
    assert d == LANES * SUBLANES, "the row-tile layout maps one model row onto one (8, 128) tile"
    stacked = dict(norm_mix_g=norm_mix_g, w_in=w_in, w_gate=w_gate, b_gate=b_gate,
                   w_proj_attn=w_proj_attn, w_proj_pool=w_proj_pool, w_proj_mem=w_proj_mem,
                   pool_w_group=pool_w_group, pool_scale=pool_scale, mem_norm_g=mem_norm_g,
                   w_mem_kv=w_mem_kv, w_out=w_out, norm_ffn_g=norm_ffn_g, w_router=w_router,
                   b_router=b_router, w_up=w_up, b_up=b_up, w_down=w_down, b_down=b_down)
    depth = w_in.shape[0]
    tables = _rope_tables(s)
    h = x
    for l in range(depth):
        p = {name: val[l] for name, val in stacked.items()}
        h = _layer(h, mem, p, tables, norm_final_g if l == depth - 1 else None)
    return h
```

```python
import functools

import jax
import jax.numpy as jnp
import numpy as np
from jax import lax
from jax.experimental import pallas as pl
from jax.experimental.pallas import tpu as pltpu

N_HEADS_ATTN = 8
HEAD_DIM = 64
ROPE_DIM = HEAD_DIM // 4
ROPE_THETA = 500000.0
N_HEADS_IDX = 8
IDX_DIM = 64
TOPK_MAX = 256
N_POOL_GROUPS = 4
POOL_GROUP_DIM = 64
POOL_WINDOWS = (2, 4, 8, 16)
POOL_HALO = 16
N_HEADS_MEM = 4
N_BRANCHES = 3
N_EXPERTS = 32
TOP_K = 4
SWIGLU_ALPHA = 1.702
SWIGLU_LIMIT = 7.0
EPS = 1e-6

ATTN_W = N_HEADS_ATTN * HEAD_DIM
POOL_W = N_POOL_GROUPS * POOL_GROUP_DIM
MEM_W = N_HEADS_MEM * HEAD_DIM
IDXQ_W = N_HEADS_IDX * IDX_DIM

LANES = 128
SUBLANES = 8
VMEM_LIMIT = 56 * 1024 * 1024

IN_TILE = 512
Q_BLOCK = 128
KEY_CHUNK = 256
COUNT_CHUNK = 512
COUNT_ROWS = 32
EARLY_EXIT_PASSES = (25, 28)
MERGE_TILE = 512
ROW_BLOCK = 256
UP_TILE = 256
MOVE_TILE = 512
ISSUE_UNROLL = 4

VT_ROWS = LANES + 16
VT_ALL = (N_HEADS_ATTN // 2) * VT_ROWS
LOG2_E = 1.4426950408889634

NEG_BIG = -1e30
INT_MIN = -2 ** 31


def _dot(a, b):
    return jnp.dot(a, b, preferred_element_type=jnp.float32)


def _dot_nt(a, b):
    return lax.dot_general(a, b, (((1,), (1,)), ((), ())), preferred_element_type=jnp.float32)


def _rmsnorm(x, g):
    return x * lax.rsqrt(jnp.mean(x * x, axis=-1, keepdims=True) + EPS) * g


def _store_row_tiles(ref, x):
    rows, width = x.shape
    for j in range(width // LANES):
        ref[pl.ds(j, rows, stride=width // LANES), :] = x[:, j * LANES:(j + 1) * LANES]


def _load_row_tiles(ref, rows, width):
    pieces = width // LANES
    return jnp.concatenate([ref[pl.ds(j, rows, stride=pieces), :] for j in range(pieces)], axis=1)


def _row_tile(ref, row):
    return ref.at[pl.ds(pl.multiple_of(row * SUBLANES, SUBLANES), SUBLANES)]


def _rope128(x, c, a, b):
    return x * c + pltpu.roll(x, LANES - ROPE_DIM // 2, 1) * a + pltpu.roll(x, ROPE_DIM // 2, 1) * b


def _in_proj_kernel(x_ref, g_ref, wa_ref, wb_ref, wc_ref, rc_ref, ra_ref, rb_ref,
                    q_ref, k_ref, vt_ref, iq_ref, iklo_ref, ikhi_ref, iw_ref, u_ref, qm_ref):
    xn = _rmsnorm(x_ref[...], g_ref[...]).astype(jnp.bfloat16)
    rc, ra, rb = rc_ref[...], ra_ref[...], rb_ref[...]
    pa = _dot(xn, wa_ref[...])
    for seg, (ref, scale) in enumerate(((q_ref, LOG2_E * HEAD_DIM ** -0.5), (k_ref, None),
                                        (None, None), (iq_ref, IDX_DIM ** -0.5))):
        for c in range(ATTN_W // LANES):
            lo = seg * ATTN_W + c * LANES
            blk = pa[:, lo:lo + LANES]
            if ref is None:
                vt_ref[c * VT_ROWS:c * VT_ROWS + LANES, :] = blk.T.astype(vt_ref.dtype)
                vt_ref[c * VT_ROWS + LANES:(c + 1) * VT_ROWS, :] = jnp.ones(
                    (VT_ROWS - LANES, blk.shape[0]), vt_ref.dtype)
                continue
            blk = _rope128(blk, rc, ra, rb)
            if scale is not None:
                blk = blk * scale
            ref[:, c * LANES:(c + 1) * LANES] = blk.astype(ref.dtype)
    pb = _dot(xn, wb_ref[...])
    lane = lax.broadcasted_iota(jnp.int32, pb.shape, 1)
    ik = jnp.where(lane < IDX_DIM, _rope128(pb, rc, ra, rb), 0.0)
    iklo_ref[...] = ik.astype(iklo_ref.dtype)
    ikhi_ref[...] = pltpu.roll(ik, IDX_DIM, 1).astype(ikhi_ref.dtype)
    iw_ref[...] = pb * (N_HEADS_IDX ** -0.5)
    pc = _dot(xn, wc_ref[...])
    u_ref[...] = pc[:, :POOL_W]
    qm_ref[...] = (pc[:, POOL_W:] * (HEAD_DIM ** -0.5)).astype(qm_ref.dtype)


def _in_proj(x2, g, wa, wb, wc, rc, ra, rb, batch, seq):
    n, d = x2.shape
    t = IN_TILE
    tiles_per_seq = seq // t
    row = lambda i: (i, 0)
    full = lambda i: (0, 0)
    pos = lambda i: (i % tiles_per_seq, 0)
    bf = jnp.bfloat16
    flat = lambda w, dt: (jax.ShapeDtypeStruct((n, w), dt), pl.BlockSpec((t, w), row))
    outs = (flat(ATTN_W, bf), flat(ATTN_W, bf),
            (jax.ShapeDtypeStruct((batch, VT_ALL, seq), bf),
             pl.BlockSpec((None, VT_ALL, t), lambda i: (i // tiles_per_seq, 0, i % tiles_per_seq))),
            flat(IDXQ_W, bf), flat(LANES, bf), flat(LANES, bf), flat(LANES, jnp.float32),
            flat(POOL_W, jnp.float32), flat(MEM_W, bf))
    return pl.pallas_call(
        _in_proj_kernel,
        grid=(n // t,),
        in_specs=[pl.BlockSpec((t, d), row), pl.BlockSpec((1, d), full),
                  pl.BlockSpec(wa.shape, full), pl.BlockSpec(wb.shape, full),
                  pl.BlockSpec(wc.shape, full),
                  pl.BlockSpec((t, LANES), pos), pl.BlockSpec((t, LANES), pos),
                  pl.BlockSpec((t, LANES), pos)],
        out_specs=tuple(o[1] for o in outs),
        out_shape=tuple(o[0] for o in outs),
        compiler_params=pltpu.CompilerParams(dimension_semantics=("arbitrary",),
                                             vmem_limit_bytes=VMEM_LIMIT),
        name="in_proj",
    )(x2, g, wa, wb, wc, rc, ra, rb)


def _mem_kv_kernel(mem_ref, g_ref, w_ref, km_ref, vm_ref):
    mn = _rmsnorm(mem_ref[...], g_ref[...]).astype(jnp.bfloat16)
    kv = _dot(mn, w_ref[...])
    km, vm = kv[:, :MEM_W], kv[:, MEM_W:]
    lane = lax.broadcasted_iota(jnp.int32, km.shape, 1)
    for h in range(N_HEADS_MEM):
        in_head = (lane >= h * HEAD_DIM) & (lane < (h + 1) * HEAD_DIM)
        km_ref[h] = jnp.where(in_head, km, 0.0).astype(km_ref.dtype)
        vm_ref[h] = jnp.where(in_head, vm, 0.0).astype(vm_ref.dtype)


def _mem_kv(mem, g, w):
    b, m, d = mem.shape
    out = jax.ShapeDtypeStruct((b, N_HEADS_MEM, m, MEM_W), jnp.bfloat16)
    return pl.pallas_call(
        _mem_kv_kernel,
        grid=(b,),
        in_specs=[pl.BlockSpec((None, m, d), lambda i: (i, 0, 0)),
                  pl.BlockSpec((1, d), lambda i: (0, 0)),
                  pl.BlockSpec(w.shape, lambda i: (0, 0))],
        out_specs=(pl.BlockSpec((None, N_HEADS_MEM, m, MEM_W), lambda i: (i, 0, 0, 0)),
                   pl.BlockSpec((None, N_HEADS_MEM, m, MEM_W), lambda i: (i, 0, 0, 0))),
        out_shape=(out, out),
        compiler_params=pltpu.CompilerParams(dimension_semantics=("arbitrary",),
                                             vmem_limit_bytes=VMEM_LIMIT),
        name="mem_kv",
    )(mem, g, w)


def _dsa_kernel(iq_ref, iw_ref, q_ref, iklo_ref, ikhi_ref, k_ref, vt_ref, o_ref,
                key_ref, iqt_ref, qt_ref, bias_ref, s_ref, acc_ref, *, n_top, idx_bits):
    qb = pl.program_id(1)
    n_chunks = (qb * Q_BLOCK + Q_BLOCK + KEY_CHUNK - 1) // KEY_CHUNK
    n_count_chunks = (qb * Q_BLOCK + Q_BLOCK + COUNT_CHUNK - 1) // COUNT_CHUNK
    q_pos = qb * Q_BLOCK + lax.broadcasted_iota(jnp.int32, (1, Q_BLOCK), 1)
    row_k = lax.broadcasted_iota(jnp.int32, (KEY_CHUNK, 1), 0)
    row_c = lax.broadcasted_iota(jnp.int32, (COUNT_CHUNK, 1), 0)
    row_d = lax.broadcasted_iota(jnp.int32, (LANES, Q_BLOCK), 0)
    bf = jnp.bfloat16

    def chunk_off(c):
        return pl.multiple_of(c * KEY_CHUNK, KEY_CHUNK)

    wt = iw_ref[...].T
    for j in range(N_HEADS_IDX // 2):
        iqt_ref[j // 2, :, (j % 2) * Q_BLOCK:(j % 2 + 1) * Q_BLOCK] = (
            iq_ref[:, j * LANES:(j + 1) * LANES].astype(jnp.float32).T.astype(bf))
    for j in range(N_HEADS_ATTN // 2):
        pair_t = q_ref[:, j * LANES:(j + 1) * LANES].astype(jnp.float32).T
        qt_ref[j, :, :Q_BLOCK] = jnp.where(row_d < HEAD_DIM, pair_t, 0.0).astype(bf)
        qt_ref[j, :, Q_BLOCK:] = jnp.where(row_d >= HEAD_DIM, pair_t, 0.0).astype(bf)

    def score_chunk(c, carry):
        off = pl.multiple_of(c * COUNT_CHUNK, COUNT_CHUNK)
        acc = jnp.zeros((COUNT_CHUNK, Q_BLOCK), jnp.float32)
        for parity, ik_ref in enumerate((iklo_ref, ikhi_ref)):
            ik = ik_ref[pl.ds(off, COUNT_CHUNK), :]
            for g in range(N_HEADS_IDX // 4):
                dots = _dot(ik, iqt_ref[g])
                for side in range(2):
                    head = 2 * (2 * g + side) + parity
                    w = wt[IDX_DIM + head:IDX_DIM + head + 1, :]
                    acc = acc + w * jnp.maximum(dots[:, side * Q_BLOCK:(side + 1) * Q_BLOCK], 0.0)
        acc = jnp.where(off + row_c <= q_pos, acc, -jnp.inf)
        bits = lax.bitcast_convert_type(acc, jnp.int32)
        key_ref[pl.ds(off, COUNT_CHUNK), :] = bits ^ ((bits >> 31) & 0x7FFFFFFF)
        return carry

    lax.fori_loop(0, n_count_chunks, score_chunk, 0)

    def count(pred):
        def body(c, cnt):
            off = pl.multiple_of(c * COUNT_CHUNK, COUNT_CHUNK)
            hit = jnp.where(pred(key_ref[pl.ds(off, COUNT_CHUNK), :], off + row_c), 1.0, 0.0)
            return cnt + jnp.sum(hit.reshape(COUNT_CHUNK // COUNT_ROWS, COUNT_ROWS, Q_BLOCK), axis=0)
        cnt = lax.fori_loop(0, n_count_chunks, body, jnp.zeros((COUNT_ROWS, Q_BLOCK), jnp.float32))
        return jnp.sum(cnt, axis=0, keepdims=True)

    k_f = jnp.float32(n_top)
    zero = jnp.zeros((1, Q_BLOCK), jnp.int32)
    n_nonneg = count(lambda kk, pos: kk >= zero)
    n_stored = (n_count_chunks * COUNT_CHUNK).astype(jnp.float32)
    thr = jnp.where(n_nonneg >= k_f, zero, INT_MIN)
    n_ge = jnp.where(n_nonneg >= k_f, n_nonneg, n_stored)

    def thr_bit(i, carry):
        cur, n_cur = carry
        cand = cur + jnp.left_shift(jnp.int32(1), 30 - i)
        n_cand = count(lambda kk, pos: kk >= cand)
        keep = n_cand >= k_f
        return jnp.where(keep, cand, cur), jnp.where(keep, n_cand, n_cur)

    def all_exact(carry):
        return jnp.max(jnp.abs(carry[1] - k_f)) == 0.0

    def search_bits(first, last):
        return lambda carry: lax.fori_loop(first, last, thr_bit, carry)

    carry = search_bits(0, EARLY_EXIT_PASSES[0])((thr, n_ge))
    for first, last in zip(EARLY_EXIT_PASSES, EARLY_EXIT_PASSES[1:] + (31,)):
        carry = lax.cond(all_exact(carry), lambda c: c, search_bits(first, last), carry)
    thr, n_ge = carry

    def tie_search():
        need = k_f - count(lambda kk, pos: kk > thr)

        def tie_bit(i, cur):
            cand = cur + jnp.left_shift(jnp.int32(1), idx_bits - 1 - i)
            below = count(lambda kk, pos: (kk == thr) & (pos < cand))
            return jnp.where(below < need, cand, cur)
        return lax.fori_loop(0, idx_bits, tie_bit, zero)

    has_surplus_ties = jnp.max(n_ge) > k_f
    tie_pos = lax.cond(has_surplus_ties, tie_search, lambda: zero + (2 ** idx_bits - 1))

    acc_ref[...] = jnp.zeros(acc_ref.shape, jnp.float32)

    n_pairs = N_HEADS_ATTN // 2

    def masked_logits(c, slot):
        off = chunk_off(jnp.minimum(c, n_chunks - 1))
        kk = key_ref[pl.ds(off, KEY_CHUNK), :]
        pos = c * KEY_CHUNK + row_k
        sel = ((kk > thr) | ((kk == thr) & (pos <= tie_pos))) & (pos <= q_pos)
        bias = jnp.where(sel, 0.0, NEG_BIG)
        bias_ref[slot] = jnp.concatenate([bias, bias], axis=1)
        for j in range(n_pairs):
            kp = k_ref[pl.ds(off, KEY_CHUNK), j * LANES:(j + 1) * LANES]
            s_ref[slot, j] = _dot(kp, qt_ref[j]) + bias_ref[slot]

    def softmax_pv(c, slot, ms, ls):
        off = chunk_off(jnp.minimum(c, n_chunks - 1))
        new_ms, new_ls = [], []
        for j in range(n_pairs):
            s = s_ref[slot, j]
            m_new = jnp.maximum(ms[j], jnp.max(s, axis=0, keepdims=True))
            alpha = jnp.exp2(ms[j] - m_new)
            p = jnp.exp2(s - m_new).astype(bf)
            vt = vt_ref[j * VT_ROWS:(j + 1) * VT_ROWS, pl.ds(off, KEY_CHUNK)]
            pv = _dot(vt, p)
            new_ls.append(alpha * ls[j] + pv[LANES:LANES + 1, :])
            new_ms.append(m_new)
            lo, mid, hi = j * LANES, j * LANES + HEAD_DIM, (j + 1) * LANES
            acc_ref[lo:mid, :] = alpha[:, :Q_BLOCK] * acc_ref[lo:mid, :] + pv[:HEAD_DIM, :Q_BLOCK]
            acc_ref[mid:hi, :] = alpha[:, Q_BLOCK:] * acc_ref[mid:hi, :] + pv[HEAD_DIM:LANES, Q_BLOCK:]
        return tuple(new_ms), tuple(new_ls)

    masked_logits(0, 0)

    def attn_two_chunks(i, carry):
        ms, ls = carry
        masked_logits(2 * i + 1, 1)
        ms, ls = softmax_pv(2 * i, 0, ms, ls)
        masked_logits(2 * i + 2, 0)
        return softmax_pv(2 * i + 1, 1, ms, ls)

    init = (tuple(jnp.full((1, 2 * Q_BLOCK), NEG_BIG, jnp.float32) for _ in range(n_pairs)),
            tuple(jnp.zeros((1, 2 * Q_BLOCK), jnp.float32) for _ in range(n_pairs)))
    _, ls = lax.fori_loop(0, (n_chunks + 1) // 2, attn_two_chunks, init)
    for j in range(n_pairs):
        lo, mid, hi = j * LANES, j * LANES + HEAD_DIM, (j + 1) * LANES
        even = acc_ref[lo:mid, :] / ls[j][:, :Q_BLOCK]
        odd = acc_ref[mid:hi, :] / ls[j][:, Q_BLOCK:]
        o_ref[:, lo:hi] = jnp.concatenate([even, odd], axis=0).T.astype(o_ref.dtype)


def _dsa(iq, iw, q, iklo, ikhi, k, vt):
    b, s, _ = q.shape
    n_top = min(TOPK_MAX, s // 4)
    idx_bits = max(1, int(np.ceil(np.log2(s))))
    qblk = lambda w: pl.BlockSpec((None, Q_BLOCK, w), lambda bi, qi: (bi, qi, 0))
    keys = lambda w: pl.BlockSpec((None, s, w), lambda bi, qi: (bi, 0, 0))
    return pl.pallas_call(
        functools.partial(_dsa_kernel, n_top=n_top, idx_bits=idx_bits),
        grid=(b, s // Q_BLOCK),
        in_specs=[qblk(IDXQ_W), qblk(LANES), qblk(ATTN_W), keys(LANES), keys(LANES),
                  keys(ATTN_W), pl.BlockSpec((None, VT_ALL, s), lambda bi, qi: (bi, 0, 0))],
        out_specs=qblk(ATTN_W),
        out_shape=jax.ShapeDtypeStruct((b, s, ATTN_W), jnp.bfloat16),
        scratch_shapes=[pltpu.VMEM((s, Q_BLOCK), jnp.int32),
                        pltpu.VMEM((N_HEADS_IDX // 4, LANES, 2 * Q_BLOCK), jnp.bfloat16),
                        pltpu.VMEM((N_HEADS_ATTN // 2, LANES, 2 * Q_BLOCK), jnp.bfloat16),
                        pltpu.VMEM((2, KEY_CHUNK, 2 * Q_BLOCK), jnp.float32),
                        pltpu.VMEM((2, N_HEADS_ATTN // 2, KEY_CHUNK, 2 * Q_BLOCK), jnp.float32),
                        pltpu.VMEM((ATTN_W, Q_BLOCK), jnp.float32)],
        compiler_params=pltpu.CompilerParams(dimension_semantics=("arbitrary", "arbitrary"),
                                             vmem_limit_bytes=VMEM_LIMIT),
        name="dsa",
    )(iq, iw, q, iklo, ikhi, k, vt)


def _merge_kernel(x_ref, ya_ref, u_ref, uprev_ref, qm_ref, km_ref, vm_ref,
                  gmix_ref, wg_ref, bg_ref, wpa_ref, wpp_ref, wpm_ref, wbd_ref, psc_ref,
                  wo_ref, gffn_ref, wrh_ref, wrhl_ref, br_ref,
                  h_ref, xn2_ref, eg_ref, ei_ref, cnt_ref, carry_ref, *, tiles_per_seq):
    i = pl.program_id(0)
    t = x_ref.shape[0]
    d = x_ref.shape[1]
    bf = jnp.bfloat16
    x = x_ref[...]
    xn = _rmsnorm(x, gmix_ref[...]).astype(bf)

    tile_in_seq = i % tiles_per_seq
    u = u_ref[...]
    halo = jnp.where(tile_in_seq == 0, 0.0, uprev_ref[...])
    ext = jnp.concatenate([halo, u], axis=0)
    lane_p = lax.broadcasted_iota(jnp.int32, (t, POOL_W), 1)
    pos1 = (tile_in_seq * t + lax.broadcasted_iota(jnp.int32, (t, 1), 0) + 1).astype(jnp.float32)
    pooled = None
    run, width = ext, 1
    for g, w in enumerate(POOL_WINDOWS):
        while width < w:
            run = run[width:] + run[:-width]
            width *= 2
        start = POOL_HALO + 1 - w
        mean = run[start:start + t] / jnp.minimum(pos1, float(w))
        pooled = mean if pooled is None else jnp.where(lane_p >= g * POOL_GROUP_DIM, mean, pooled)
    pooled = pooled - u
    mixed = _dot(pooled.astype(bf), wbd_ref[...]) * psc_ref[...]
    y_pool = _dot(mixed.astype(bf), wpp_ref[...])

    qm = qm_ref[...]
    probs = []
    for h in range(N_HEADS_MEM):
        s = _dot_nt(qm, km_ref[h])
        p = jnp.exp(s - jnp.max(s, axis=1, keepdims=True))
        probs.append((p / jnp.sum(p, axis=1, keepdims=True)).astype(bf))
    y_mem = _dot(probs[0], vm_ref[0])
    for h in range(1, N_HEADS_MEM):
        y_mem = y_mem + _dot(probs[h], vm_ref[h])
    y_mem = _dot(y_mem.astype(bf), wpm_ref[...])

    y_attn = _dot(ya_ref[...], wpa_ref[...])

    def gate(br):
        z = _dot(xn, wg_ref[:, br * d:(br + 1) * d]) + bg_ref[:, br * d:(br + 1) * d]
        return 1.0 / (1.0 + jnp.exp(-z))

    merged = gate(0) * y_attn + gate(1) * y_pool + gate(2) * y_mem
    h = x + _dot(merged.astype(bf), wo_ref[...])
    h_ref[...] = h
    xn2 = _rmsnorm(h, gffn_ref[...])
    _store_row_tiles(xn2_ref, xn2)

    x_hi = xn2.astype(bf)
    x_lo = (xn2 - x_hi.astype(jnp.float32)).astype(bf)
    by_hi = _dot(x_hi, wrhl_ref[...])
    logits = (by_hi[:, :LANES] + (_dot(x_lo, wrh_ref[...]) + by_hi[:, LANES:])) + br_ref[...]
    lane_e = lax.broadcasted_iota(jnp.int32, logits.shape, 1).astype(jnp.float32)
    work = logits
    vals, onehots, ids = [], [], []
    for _ in range(TOP_K):
        mx = jnp.max(work, axis=1, keepdims=True)
        idx = jnp.min(jnp.where(work == mx, lane_e, float(LANES)), axis=1, keepdims=True)
        oh = lane_e == idx
        vals.append(mx)
        ids.append(idx)
        onehots.append(oh)
        work = jnp.where(oh, -jnp.inf, work)
    exps = [jnp.exp(v - vals[0]) for v in vals]
    denom = exps[0] + exps[1] + exps[2] + exps[3]

    @pl.when(i == 0)
    def _():
        carry_ref[...] = jnp.zeros(carry_ref.shape, jnp.float32)

    member = jnp.where(onehots[0] | onehots[1] | onehots[2] | onehots[3], 1.0, 0.0)
    r_io = lax.broadcasted_iota(jnp.int32, (t, t), 0)
    c_io = lax.broadcasted_iota(jnp.int32, (t, t), 1)
    earlier = jnp.where(c_io < r_io, 1.0, 0.0).astype(bf)
    before = _dot(earlier, member.astype(bf)) + carry_ref[...]
    carry_ref[...] = carry_ref[...] + jnp.sum(member, axis=0, keepdims=True)
    cnt_ref[...] = carry_ref[...]

    lane8 = lax.broadcasted_iota(jnp.int32, (t, 2 * TOP_K), 1)
    eg = jnp.zeros((t, 2 * TOP_K), jnp.float32)
    ei = jnp.zeros((t, 2 * TOP_K), jnp.float32)
    for j in range(TOP_K):
        rank = jnp.sum(jnp.where(onehots[j], before, 0.0), axis=1, keepdims=True)
        eg = jnp.where(lane8 == j, exps[j] / denom, eg)
        ei = jnp.where(lane8 == j, ids[j], ei)
        ei = jnp.where(lane8 == TOP_K + j, rank, ei)
    eg_ref[...] = eg
    ei_ref[...] = ei.astype(jnp.int32)


def _merge(x2, ya, u, qm, km, vm, gmix, wg, bg, wpa, wpp, wpm, wbd, psc, wo, gffn, wrh, wrhl, br, seq):
    n, d = x2.shape
    t = MERGE_TILE
    tiles_per_seq = seq // t
    m = km.shape[2]
    row = lambda i: (i, 0)
    full = lambda i: (0, 0)
    per_batch = lambda i: (i // tiles_per_seq, 0, 0, 0)
    halo_blocks = t // POOL_HALO
    prev = lambda i: (jnp.maximum(i * halo_blocks - 1, 0), 0)
    consts = (gmix, wg, bg, wpa, wpp, wpm, wbd, psc, wo, gffn, wrh, wrhl, br)
    pieces = d // LANES
    out_shape = (jax.ShapeDtypeStruct((n, d), jnp.float32),
                 jax.ShapeDtypeStruct((n * pieces, LANES), jnp.float32),
                 jax.ShapeDtypeStruct((n, 2 * TOP_K), jnp.float32),
                 jax.ShapeDtypeStruct((n, 2 * TOP_K), jnp.int32),
                 jax.ShapeDtypeStruct((1, LANES), jnp.float32))
    return pl.pallas_call(
        functools.partial(_merge_kernel, tiles_per_seq=tiles_per_seq),
        grid=(n // t,),
        in_specs=[pl.BlockSpec((t, d), row), pl.BlockSpec((t, ATTN_W), row),
                  pl.BlockSpec((t, POOL_W), row), pl.BlockSpec((POOL_HALO, POOL_W), prev),
                  pl.BlockSpec((t, MEM_W), row),
                  pl.BlockSpec((None, N_HEADS_MEM, m, MEM_W), per_batch),
                  pl.BlockSpec((None, N_HEADS_MEM, m, MEM_W), per_batch)]
                 + [pl.BlockSpec(c.shape, full) for c in consts],
        out_specs=(pl.BlockSpec((t, d), row), pl.BlockSpec((t * pieces, LANES), row),
                   pl.BlockSpec((t, 2 * TOP_K), row), pl.BlockSpec((t, 2 * TOP_K), row),
                   pl.BlockSpec((1, LANES), full)),
        out_shape=out_shape,
        scratch_shapes=[pltpu.VMEM((1, LANES), jnp.float32)],
        compiler_params=pltpu.CompilerParams(dimension_semantics=("arbitrary",),
                                             vmem_limit_bytes=VMEM_LIMIT),
        name="merge",
    )(x2, ya, u, u, qm, km, vm, *consts)


def _row_copy(src_ref, src_row, dst_ref, dst_row, sem):
    return pltpu.make_async_copy(_row_tile(src_ref, src_row), _row_tile(dst_ref, dst_row), sem)


def _dispatch_kernel(pstart_ref, cnt_ref, dest_ref, x_ref, xs_ref, zero_ref, sem):
    t = x_ref.shape[0] // SUBLANES
    block_tiles = ROW_BLOCK * SUBLANES

    @pl.when(pl.program_id(0) == 0)
    def _():
        zero_ref[...] = jnp.zeros(zero_ref.shape, zero_ref.dtype)

        def for_each_pad_row(act):
            def per_expert(e, carry):
                n_pad = (ROW_BLOCK - cnt_ref[e] % ROW_BLOCK) % ROW_BLOCK
                first = pstart_ref[e] + cnt_ref[e]

                def per_row(r, c):
                    act(_row_copy(zero_ref, 0, xs_ref, first + r, sem))
                    return c
                return lax.fori_loop(0, n_pad, per_row, carry)
            lax.fori_loop(0, N_EXPERTS, per_expert, 0)

        def for_each_unused_block(act):
            n_blocks = xs_ref.shape[0] // block_tiles
            first = (pstart_ref[N_EXPERTS - 1] + cnt_ref[N_EXPERTS - 1] + ROW_BLOCK - 1) // ROW_BLOCK

            def per_block(b, c):
                rows = pl.ds(pl.multiple_of(b * block_tiles, block_tiles), block_tiles)
                act(pltpu.make_async_copy(zero_ref, xs_ref.at[rows], sem))
                return c
            lax.fori_loop(first, n_blocks, per_block, 0)

        for act in (lambda cp: cp.start(), lambda cp: cp.wait()):
            for_each_pad_row(act)
            for_each_unused_block(act)

    def issue(tok, carry):
        for j in range(TOP_K):
            _row_copy(x_ref, tok, xs_ref, dest_ref[tok * TOP_K + j], sem).start(priority=j % 2)
        return carry

    lax.fori_loop(0, t, issue, 0, unroll=ISSUE_UNROLL)
    for j in range(TOP_K):
        pltpu.make_async_copy(x_ref, xs_ref.at[pl.ds(0, x_ref.shape[0])], sem).wait()


def _dispatch(pstart, cnt, dest, xn2, n_rows):
    tiles, lanes = xn2.shape
    n = tiles // SUBLANES
    t = MOVE_TILE
    return pl.pallas_call(
        _dispatch_kernel,
        grid_spec=pltpu.PrefetchScalarGridSpec(
            num_scalar_prefetch=2,
            grid=(n // t,),
            in_specs=[pl.BlockSpec((t * TOP_K,), lambda i, ps, ct: (i,), memory_space=pltpu.SMEM),
                      pl.BlockSpec((t * SUBLANES, lanes), lambda i, ps, ct: (i, 0))],
            out_specs=pl.BlockSpec(memory_space=pl.ANY),
            scratch_shapes=[pltpu.VMEM((ROW_BLOCK * SUBLANES, lanes), jnp.float32),
                            pltpu.SemaphoreType.DMA(())]),
        out_shape=jax.ShapeDtypeStruct((n_rows * SUBLANES, lanes), jnp.float32),
        compiler_params=pltpu.CompilerParams(dimension_semantics=("arbitrary",),
                                             vmem_limit_bytes=VMEM_LIMIT),
        name="dispatch",
    )(pstart, cnt, dest, xn2)


def _experts_kernel(be_ref, nused_ref, half_ref, next_ref, xs_ref, wu_hbm, bu_ref, wd_hbm, bd_ref,
                    ys_ref, wu32_ref, wd32_ref, wu16_ref, wd16_ref, sems):
    i = pl.program_id(0)
    d, f = wu16_ref.shape[0], wd16_ref.shape[0]
    half = half_ref[i]

    def weight_copies(e, into):
        return (pltpu.make_async_copy(wu_hbm.at[e], wu32_ref.at[into], sems.at[0, into]),
                pltpu.make_async_copy(wd_hbm.at[e], wd32_ref.at[into], sems.at[1, into]))

    @pl.when(i == 0)
    def _():
        for cp in weight_copies(be_ref[0], 0):
            cp.start()

    @pl.when((i == 0) | (be_ref[i] != be_ref[jnp.maximum(i - 1, 0)]))
    def _():
        for cp in weight_copies(be_ref[i], half):
            cp.wait()
        wu16_ref[...] = wu32_ref[half].astype(wu16_ref.dtype)
        wd16_ref[...] = wd32_ref[half].astype(wd16_ref.dtype)

        @pl.when(next_ref[i] >= 0)
        def _():
            for cp in weight_copies(next_ref[i], 1 - half):
                cp.start()

    @pl.when(i < nused_ref[0])
    def _():
        x = _load_row_tiles(xs_ref, ROW_BLOCK, d).astype(jnp.bfloat16)
        acts = []
        for lo in range(0, f, UP_TILE):
            glu_cols, lin_cols = slice(lo, lo + UP_TILE), slice(f + lo, f + lo + UP_TILE)
            glu = jnp.minimum(_dot(x, wu16_ref[:, glu_cols]) + bu_ref[:, glu_cols], SWIGLU_LIMIT)
            lin = jnp.clip(_dot(x, wu16_ref[:, lin_cols]) + bu_ref[:, lin_cols],
                           -SWIGLU_LIMIT, SWIGLU_LIMIT)
            act = glu * (1.0 / (1.0 + jnp.exp(-SWIGLU_ALPHA * glu))) * (lin + 1.0)
            acts.append(act.astype(jnp.bfloat16))
        act = jnp.concatenate(acts, axis=1)
        _store_row_tiles(ys_ref, _dot(act, wd16_ref[...]) + bd_ref[...])

    @pl.when(i >= nused_ref[0])
    def _():
        ys_ref[...] = jnp.zeros(ys_ref.shape, ys_ref.dtype)


def _experts(block_e, nused, block_half, block_next, xs, wu, bu, wd, bd):
    tiles, lanes = xs.shape
    e, d, f2 = wu.shape
    f = wd.shape[1]
    block = (ROW_BLOCK * SUBLANES, lanes)
    rows = lambda i, be, nu, hf, nx: (i, 0)
    used_rows = lambda i, be, nu, hf, nx: (jnp.minimum(i, nu[0] - 1), 0)
    per_e = lambda i, be, nu, hf, nx: (be[i], 0, 0)
    return pl.pallas_call(
        _experts_kernel,
        grid_spec=pltpu.PrefetchScalarGridSpec(
            num_scalar_prefetch=4,
            grid=(tiles // block[0],),
            in_specs=[pl.BlockSpec(block, used_rows),
                      pl.BlockSpec(memory_space=pl.ANY), pl.BlockSpec((None, 1, f2), per_e),
                      pl.BlockSpec(memory_space=pl.ANY), pl.BlockSpec((None, 1, d), per_e)],
            out_specs=pl.BlockSpec(block, rows),
            scratch_shapes=[pltpu.VMEM((2, d, f2), jnp.float32), pltpu.VMEM((2, f, d), jnp.float32),
                            pltpu.VMEM((d, f2), jnp.bfloat16), pltpu.VMEM((f, d), jnp.bfloat16),
                            pltpu.SemaphoreType.DMA((2, 2))]),
        out_shape=jax.ShapeDtypeStruct(xs.shape, jnp.float32),
        compiler_params=pltpu.CompilerParams(dimension_semantics=("arbitrary",),
                                             vmem_limit_bytes=VMEM_LIMIT),
        name="experts",
    )(block_e, nused, block_half, block_next, xs, wu, bu, wd, bd)


def _combine_kernel(dest_ref, dest_next_ref, h_ref, eg_ref, g_ref, ys_ref, o_ref, buf_ref, sems,
                    *, final_norm):
    i = pl.program_id(0)
    t, d = h_ref.shape
    half = i % 2

    def gather_rows(idx_ref, into):
        def issue(tok, carry):
            for j in range(TOP_K):
                _row_copy(ys_ref, idx_ref[tok * TOP_K + j], buf_ref.at[into, j], tok,
                          sems.at[into]).start(priority=j % 2)
            return carry
        lax.fori_loop(0, t, issue, 0, unroll=ISSUE_UNROLL)

    @pl.when(i == 0)
    def _():
        gather_rows(dest_ref, 0)

    @pl.when(i + 1 < pl.num_programs(0))
    def _():
        gather_rows(dest_next_ref, 1 - half)

    for j in range(TOP_K):
        pltpu.make_async_copy(ys_ref.at[pl.ds(0, t * SUBLANES)], buf_ref.at[half, j],
                              sems.at[half]).wait()
    eg = eg_ref[...]
    h = h_ref[...]
    for j in range(TOP_K):
        h = h + eg[:, j:j + 1] * _load_row_tiles(buf_ref.at[half, j], t, d)
    o_ref[...] = _rmsnorm(h, g_ref[...]) if final_norm else h


def _combine(dest, h, eg, g, ys, final_norm):
    n, d = h.shape
    t = MOVE_TILE
    return pl.pallas_call(
        functools.partial(_combine_kernel, final_norm=final_norm),
        grid=(n // t,),
        in_specs=[pl.BlockSpec((t * TOP_K,), lambda i: (i,), memory_space=pltpu.SMEM),
                  pl.BlockSpec((t * TOP_K,), lambda i: (jnp.minimum(i + 1, n // t - 1),),
                               memory_space=pltpu.SMEM),
                  pl.BlockSpec((t, d), lambda i: (i, 0)),
                  pl.BlockSpec((t, 2 * TOP_K), lambda i: (i, 0)),
                  pl.BlockSpec((1, d), lambda i: (0, 0)),
                  pl.BlockSpec(memory_space=pl.ANY)],
        out_specs=pl.BlockSpec((t, d), lambda i: (i, 0)),
        scratch_shapes=[pltpu.VMEM((2, TOP_K, t * SUBLANES, ys.shape[1]), jnp.float32),
                        pltpu.SemaphoreType.DMA((2,))],
        out_shape=jax.ShapeDtypeStruct((n, d), jnp.float32),
        compiler_params=pltpu.CompilerParams(dimension_semantics=("arbitrary",),
                                             vmem_limit_bytes=VMEM_LIMIT),
        name="combine",
    )(dest, dest, h, eg, g, ys)


def _rope_tables(seq):
    half = ROPE_DIM // 2
    inv = jnp.power(jnp.float32(ROPE_THETA), -jnp.arange(half, dtype=jnp.float32) / half)
    ang = jnp.arange(seq, dtype=jnp.float32)[:, None] * inv[None, :]
    cos, sin = jnp.cos(ang), jnp.sin(ang)
    pad = HEAD_DIM - ROPE_DIM
    one, zero = jnp.ones((seq, pad), jnp.float32), jnp.zeros((seq, pad), jnp.float32)
    zh = jnp.zeros((seq, half), jnp.float32)
    reps = LANES // HEAD_DIM
    c = jnp.tile(jnp.concatenate([cos, cos, one], axis=1), (1, reps))
    a = jnp.tile(jnp.concatenate([-sin, zh, zero], axis=1), (1, reps))
    b = jnp.tile(jnp.concatenate([zh, sin, zero], axis=1), (1, reps))
    return c, a, b


def _layer(h, mem, p, tables, final_g):
    b, s, d = h.shape
    n = b * s
    bf = jnp.bfloat16
    x2 = h.reshape(n, d)
    w_in = p["w_in"]
    o_ik = 3 * ATTN_W + IDXQ_W
    o_u = o_ik + IDX_DIM + N_HEADS_IDX
    wa = w_in[:, :o_ik].astype(bf)
    wb = jnp.pad(w_in[:, o_ik:o_u], ((0, 0), (0, LANES - (o_u - o_ik)))).astype(bf)
    wc = w_in[:, o_u:].astype(bf)
    q, k, vt, iq, iklo, ikhi, iw, u, qm = _in_proj(x2, p["norm_mix_g"][None], wa, wb, wc, *tables, b, s)
    km, vm = _mem_kv(mem, p["mem_norm_g"][None], p["w_mem_kv"].astype(bf))
    b3 = lambda a: a.reshape(b, s, a.shape[-1])
    ya = _dsa(b3(iq), b3(iw), b3(q), b3(iklo), b3(ikhi), b3(k), vt).reshape(n, ATTN_W)

    wbd = jnp.zeros((POOL_W, POOL_W), jnp.float32)
    for g in range(N_POOL_GROUPS):
        lo = g * POOL_GROUP_DIM
        wbd = wbd.at[lo:lo + POOL_GROUP_DIM, lo:lo + POOL_GROUP_DIM].set(p["pool_w_group"][g])
    wr = jnp.pad(p["w_router"], ((0, 0), (0, LANES - N_EXPERTS)))
    wr_hi = wr.astype(bf)
    br = jnp.pad(p["b_router"], (0, LANES - N_EXPERTS), constant_values=NEG_BIG)[None]
    hmid, xn2, eg, ei, counts = _merge(
        x2, ya, u, qm, km, vm, p["norm_mix_g"][None], p["w_gate"].astype(bf), p["b_gate"][None],
        p["w_proj_attn"].astype(bf), p["w_proj_pool"].astype(bf), p["w_proj_mem"].astype(bf),
        wbd.astype(bf), p["pool_scale"][None], p["w_out"].astype(bf), p["norm_ffn_g"][None],
        wr_hi, jnp.concatenate([wr_hi, (wr - wr_hi.astype(jnp.float32)).astype(bf)], axis=1), br, s)

    cnt = counts[0, :N_EXPERTS].astype(jnp.int32)
    padded = ((cnt + ROW_BLOCK - 1) // ROW_BLOCK) * ROW_BLOCK
    pend = jnp.cumsum(padded)
    pstart = (pend - padded).astype(jnp.int32)
    n_rows = n * TOP_K + N_EXPERTS * ROW_BLOCK
    n_blocks = n_rows // ROW_BLOCK
    nused = (pend[-1] // ROW_BLOCK).astype(jnp.int32)
    blk = jnp.minimum(jnp.arange(n_blocks, dtype=jnp.int32), nused - 1) * ROW_BLOCK
    block_e = jnp.sum(blk[:, None] >= pend[None, :], axis=1).astype(jnp.int32)
    block_e = jnp.minimum(block_e, N_EXPERTS - 1)
    has_rows = cnt > 0
    run_index = jnp.cumsum(has_rows) - 1
    ids = jnp.arange(N_EXPERTS, dtype=jnp.int32)
    later = jnp.where(has_rows[None, :] & (ids[None, :] > ids[:, None]), ids[None, :], N_EXPERTS)
    next_used = jnp.min(later, axis=1)
    next_used = jnp.where(next_used < N_EXPERTS, next_used, -1).astype(jnp.int32)
    of_block = block_e[:, None] == ids[None, :]
    block_half = jnp.sum(jnp.where(of_block, run_index % 2, 0), axis=1).astype(jnp.int32)
    block_next = jnp.sum(jnp.where(of_block, next_used, 0), axis=1).astype(jnp.int32)
    chosen =ei[:, :TOP_K, None] == jnp.arange(N_EXPERTS, dtype=jnp.int32)
    dest = (jnp.sum(jnp.where(chosen, pstart, 0), axis=-1) + ei[:, TOP_K:]).reshape(-1)

    xs = _dispatch(pstart, cnt, dest, xn2, n_rows)
    ys = _experts(block_e, nused[None], block_half, block_next, xs, p["w_up"], p["b_up"][:, None, :],
                  p["w_down"], p["b_down"][:, None, :])
    g = p["norm_ffn_g"][None] if final_g is None else final_g[None]
    return _combine(dest, hmid, eg, g, ys, final_g is not None).reshape(b, s, d)


def kernel(x, mem, norm_mix_g, w_in, w_gate, b_gate, w_proj_attn, w_proj_pool, w_proj_mem,
           pool_w_group, pool_scale, mem_norm_g, w_mem_kv, w_out, norm_ffn_g,
           w_router, b_router, w_up, b_up, w_down, b_down, norm_final_g):
    b, s, d = x.shape
    assert s % IN_TILE == 0 and s % COUNT_CHUNK == 0 and (b * s) % MOVE_TILE == 0
    assert d == LANES * SUBLANES, "the row-tile layout maps one model row onto one (8, 128) tile"
    stacked = dict(norm_mix_g=norm_mix_g, w_in=w_in, w_gate=w_gate, b_gate=b_gate,
                   w_proj_attn=w_proj_attn, w_proj_pool=w_proj_pool, w_proj_mem=w_proj_mem,
                   pool_w_group=pool_w_group, pool_scale=pool_scale, mem_norm_g=mem_norm_g,
                   w_mem_kv=w_mem_kv, w_out=w_out, norm_ffn_g=norm_ffn_g, w_router=w_router,
                   b_router=b_router, w_up=w_up, b_up=b_up, w_down=w_down, b_down=b_down)
    depth = w_in.shape[0]
    tables = _rope_tables(s)
    h = x
    for l in range(depth):
        p = {name: val[l] for name, val in stacked.items()}
        h = _layer(h, mem, p, tables, norm_final_g if l == depth - 1 else None)
    return h
```

```python
import functools

import jax
import jax.numpy as jnp
import numpy as np
from jax import lax
from jax.experimental import pallas as pl
from jax.experimental.pallas import tpu as pltpu

N_HEADS_ATTN = 8
HEAD_DIM = 64
ROPE_DIM = HEAD_DIM // 4
ROPE_THETA = 500000.0
N_HEADS_IDX = 8
IDX_DIM = 64
TOPK_MAX = 256
N_POOL_GROUPS = 4
POOL_GROUP_DIM = 64
POOL_WINDOWS = (2, 4, 8, 16)
POOL_HALO = 16
N_HEADS_MEM = 4
N_BRANCHES = 3
N_EXPERTS = 32
TOP_K = 4
SWIGLU_ALPHA = 1.702
SWIGLU_LIMIT = 7.0
EPS = 1e-6

ATTN_W = N_HEADS_ATTN * HEAD_DIM
POOL_W = N_POOL_GROUPS * POOL_GROUP_DIM
MEM_W = N_HEADS_MEM * HEAD_DIM
IDXQ_W = N_HEADS_IDX * IDX_DIM

LANES = 128
SUBLANES = 8
VMEM_LIMIT = 56 * 1024 * 1024

IN_TILE = 512
Q_BLOCK = 128
KEY_CHUNK = 256
COUNT_CHUNK = 512
COUNT_ROWS = 32
EARLY_EXIT_PASSES = (25, 28)
MERGE_TILE = 512
ROW_BLOCK = 256
UP_TILE = 256
MOVE_TILE = 512
ISSUE_UNROLL = 4

VT_ROWS = LANES + 16
VT_ALL = (N_HEADS_ATTN // 2) * VT_ROWS
LOG2_E = 1.4426950408889634

NEG_BIG = -1e30
INT_MIN = -2 ** 31
NEG_INF_KEY = -0x7F800001


def _dot(a, b):
    return jnp.dot(a, b, preferred_element_type=jnp.float32)


def _dot_nt(a, b):
    return lax.dot_general(a, b, (((1,), (1,)), ((), ())), preferred_element_type=jnp.float32)


def _rmsnorm(x, g):
    return x * lax.rsqrt(jnp.mean(x * x, axis=-1, keepdims=True) + EPS) * g


def _store_row_tiles(ref, x):
    rows, width = x.shape
    for j in range(width // LANES):
        ref[pl.ds(j, rows, stride=width // LANES), :] = x[:, j * LANES:(j + 1) * LANES]


def _load_row_tiles(ref, rows, width):
    pieces = width // LANES
    return jnp.concatenate([ref[pl.ds(j, rows, stride=pieces), :] for j in range(pieces)], axis=1)


def _row_tile(ref, row):
    return ref.at[pl.ds(pl.multiple_of(row * SUBLANES, SUBLANES), SUBLANES)]


def _rope128(x, c, a, b):
    return x * c + pltpu.roll(x, LANES - ROPE_DIM // 2, 1) * a + pltpu.roll(x, ROPE_DIM // 2, 1) * b


def _in_proj_kernel(x_ref, g_ref, wa_ref, wb_ref, wc_ref, rc_ref, ra_ref, rb_ref,
                    q_ref, k_ref, vt_ref, iq_ref, iklo_ref, ikhi_ref, iw_ref, u_ref, qm_ref):
    xn = _rmsnorm(x_ref[...], g_ref[...]).astype(jnp.bfloat16)
    rc, ra, rb = rc_ref[...], ra_ref[...], rb_ref[...]
    pa = _dot(xn, wa_ref[...])
    for seg, (ref, scale) in enumerate(((q_ref, LOG2_E * HEAD_DIM ** -0.5), (k_ref, None),
                                        (None, None), (iq_ref, IDX_DIM ** -0.5))):
        for c in range(ATTN_W // LANES):
            lo = seg * ATTN_W + c * LANES
            blk = pa[:, lo:lo + LANES]
            if ref is None:
                vt_ref[c * VT_ROWS:c * VT_ROWS + LANES, :] = blk.T.astype(vt_ref.dtype)
                vt_ref[c * VT_ROWS + LANES:(c + 1) * VT_ROWS, :] = jnp.ones(
                    (VT_ROWS - LANES, blk.shape[0]), vt_ref.dtype)
                continue
            blk = _rope128(blk, rc, ra, rb)
            if scale is not None:
                blk = blk * scale
            ref[:, c * LANES:(c + 1) * LANES] = blk.astype(ref.dtype)
    pb = _dot(xn, wb_ref[...])
    lane = lax.broadcasted_iota(jnp.int32, pb.shape, 1)
    ik = jnp.where(lane < IDX_DIM, _rope128(pb, rc, ra, rb), 0.0)
    iklo_ref[...] = ik.astype(iklo_ref.dtype)
    ikhi_ref[...] = pltpu.roll(ik, IDX_DIM, 1).astype(ikhi_ref.dtype)
    iw_ref[...] = pb * (N_HEADS_IDX ** -0.5)
    pc = _dot(xn, wc_ref[...])
    u_ref[...] = pc[:, :POOL_W]
    qm_ref[...] = (pc[:, POOL_W:] * (HEAD_DIM ** -0.5)).astype(qm_ref.dtype)


def _in_proj(x2, g, wa, wb, wc, rc, ra, rb, batch, seq):
    n, d = x2.shape
    t = IN_TILE
    tiles_per_seq = seq // t
    row = lambda i: (i, 0)
    full = lambda i: (0, 0)
    pos = lambda i: (i % tiles_per_seq, 0)
    bf = jnp.bfloat16
    flat = lambda w, dt: (jax.ShapeDtypeStruct((n, w), dt), pl.BlockSpec((t, w), row))
    outs = (flat(ATTN_W, bf), flat(ATTN_W, bf),
            (jax.ShapeDtypeStruct((batch, VT_ALL, seq), bf),
             pl.BlockSpec((None, VT_ALL, t), lambda i: (i // tiles_per_seq, 0, i % tiles_per_seq))),
            flat(IDXQ_W, bf), flat(LANES, bf), flat(LANES, bf), flat(LANES, jnp.float32),
            flat(POOL_W, jnp.float32), flat(MEM_W, bf))
    return pl.pallas_call(
        _in_proj_kernel,
        grid=(n // t,),
        in_specs=[pl.BlockSpec((t, d), row), pl.BlockSpec((1, d), full),
                  pl.BlockSpec(wa.shape, full), pl.BlockSpec(wb.shape, full),
                  pl.BlockSpec(wc.shape, full),
                  pl.BlockSpec((t, LANES), pos), pl.BlockSpec((t, LANES), pos),
                  pl.BlockSpec((t, LANES), pos)],
        out_specs=tuple(o[1] for o in outs),
        out_shape=tuple(o[0] for o in outs),
        compiler_params=pltpu.CompilerParams(dimension_semantics=("arbitrary",),
                                             vmem_limit_bytes=VMEM_LIMIT),
        name="in_proj",
    )(x2, g, wa, wb, wc, rc, ra, rb)


def _mem_kv_kernel(mem_ref, g_ref, w_ref, km_ref, vm_ref):
    mn = _rmsnorm(mem_ref[...], g_ref[...]).astype(jnp.bfloat16)
    kv = _dot(mn, w_ref[...])
    km, vm = kv[:, :MEM_W], kv[:, MEM_W:]
    lane = lax.broadcasted_iota(jnp.int32, km.shape, 1)
    for h in range(N_HEADS_MEM):
        in_head = (lane >= h * HEAD_DIM) & (lane < (h + 1) * HEAD_DIM)
        km_ref[h] = jnp.where(in_head, km, 0.0).astype(km_ref.dtype)
        vm_ref[h] = jnp.where(in_head, vm, 0.0).astype(vm_ref.dtype)


def _mem_kv(mem, g, w):
    b, m, d = mem.shape
    out = jax.ShapeDtypeStruct((b, N_HEADS_MEM, m, MEM_W), jnp.bfloat16)
    return pl.pallas_call(
        _mem_kv_kernel,
        grid=(b,),
        in_specs=[pl.BlockSpec((None, m, d), lambda i: (i, 0, 0)),
                  pl.BlockSpec((1, d), lambda i: (0, 0)),
                  pl.BlockSpec(w.shape, lambda i: (0, 0))],
        out_specs=(pl.BlockSpec((None, N_HEADS_MEM, m, MEM_W), lambda i: (i, 0, 0, 0)),
                   pl.BlockSpec((None, N_HEADS_MEM, m, MEM_W), lambda i: (i, 0, 0, 0))),
        out_shape=(out, out),
        compiler_params=pltpu.CompilerParams(dimension_semantics=("arbitrary",),
                                             vmem_limit_bytes=VMEM_LIMIT),
        name="mem_kv",
    )(mem, g, w)


def _dsa_kernel(iq_ref, iw_ref, q_ref, iklo_ref, ikhi_ref, k_ref, vt_ref, o_ref,
                key_ref, iqt_ref, qt_ref, bias_ref, s_ref, acc_ref, *, n_top, idx_bits):
    qb = pl.program_id(1)
    n_chunks = (qb * Q_BLOCK + Q_BLOCK + KEY_CHUNK - 1) // KEY_CHUNK
    n_count_chunks = (qb * Q_BLOCK + Q_BLOCK + COUNT_CHUNK - 1) // COUNT_CHUNK
    q_pos = qb * Q_BLOCK + lax.broadcasted_iota(jnp.int32, (1, Q_BLOCK), 1)
    row_c = lax.broadcasted_iota(jnp.int32, (COUNT_CHUNK, 1), 0)
    row_d = lax.broadcasted_iota(jnp.int32, (LANES, Q_BLOCK), 0)
    bf = jnp.bfloat16

    def chunk_off(c):
        return pl.multiple_of(c * KEY_CHUNK, KEY_CHUNK)

    wt = iw_ref[...].T
    for j in range(N_HEADS_IDX // 2):
        iqt_ref[j // 2, :, (j % 2) * Q_BLOCK:(j % 2 + 1) * Q_BLOCK] = (
            iq_ref[:, j * LANES:(j + 1) * LANES].astype(jnp.float32).T.astype(bf))
    for j in range(N_HEADS_ATTN // 2):
        pair_t = q_ref[:, j * LANES:(j + 1) * LANES].astype(jnp.float32).T
        qt_ref[j, :, :Q_BLOCK] = jnp.where(row_d < HEAD_DIM, pair_t, 0.0).astype(bf)
        qt_ref[j, :, Q_BLOCK:] = jnp.where(row_d >= HEAD_DIM, pair_t, 0.0).astype(bf)

    def score_chunk(c, carry):
        off = pl.multiple_of(c * COUNT_CHUNK, COUNT_CHUNK)
        acc = jnp.zeros((COUNT_CHUNK, Q_BLOCK), jnp.float32)
        for parity, ik_ref in enumerate((iklo_ref, ikhi_ref)):
            ik = ik_ref[pl.ds(off, COUNT_CHUNK), :]
            for g in range(N_HEADS_IDX // 4):
                dots = _dot(ik, iqt_ref[g])
                for side in range(2):
                    head = 2 * (2 * g + side) + parity
                    w = wt[IDX_DIM + head:IDX_DIM + head + 1, :]
                    acc = acc + w * jnp.maximum(dots[:, side * Q_BLOCK:(side + 1) * Q_BLOCK], 0.0)
        acc = jnp.where(off + row_c <= q_pos, acc, -jnp.inf)
        bits = lax.bitcast_convert_type(acc, jnp.int32)
        key_ref[pl.ds(off, COUNT_CHUNK), :] = bits ^ ((bits >> 31) & 0x7FFFFFFF)
        return carry

    lax.fori_loop(0, n_count_chunks, score_chunk, 0)

    def count(pred):
        def body(c, cnt):
            off = pl.multiple_of(c * COUNT_CHUNK, COUNT_CHUNK)
            hit = jnp.where(pred(key_ref[pl.ds(off, COUNT_CHUNK), :], off + row_c), 1.0, 0.0)
            return cnt + jnp.sum(hit.reshape(COUNT_CHUNK // COUNT_ROWS, COUNT_ROWS, Q_BLOCK), axis=0)
        cnt = lax.fori_loop(0, n_count_chunks, body, jnp.zeros((COUNT_ROWS, Q_BLOCK), jnp.float32))
        return jnp.sum(cnt, axis=0, keepdims=True)

    k_f = jnp.float32(n_top)
    zero = jnp.zeros((1, Q_BLOCK), jnp.int32)
    n_nonneg = count(lambda kk, pos: kk >= zero)
    n_stored = (n_count_chunks * COUNT_CHUNK).astype(jnp.float32)
    thr = jnp.where(n_nonneg >= k_f, zero, INT_MIN)
    n_ge = jnp.where(n_nonneg >= k_f, n_nonneg, n_stored)

    def thr_bit(i, carry):
        cur, n_cur = carry
        cand = cur + jnp.left_shift(jnp.int32(1), 30 - i)
        n_cand = count(lambda kk, pos: kk >= cand)
        keep = n_cand >= k_f
        return jnp.where(keep, cand, cur), jnp.where(keep, n_cand, n_cur)

    def all_exact(carry):
        return jnp.max(jnp.abs(carry[1] - k_f)) == 0.0

    def search_bits(first, last):
        return lambda carry: lax.fori_loop(first, last, thr_bit, carry)

    carry = search_bits(0, EARLY_EXIT_PASSES[0])((thr, n_ge))
    for first, last in zip(EARLY_EXIT_PASSES, EARLY_EXIT_PASSES[1:] + (31,)):
        carry = lax.cond(all_exact(carry), lambda c: c, search_bits(first, last), carry)
    thr, n_ge = carry

    @pl.when(jnp.max(n_ge) > k_f)
    def _():
        need = k_f - count(lambda kk, pos: kk > thr)

        def tie_bit(i, cur):
            cand = cur + jnp.left_shift(jnp.int32(1), idx_bits - 1 - i)
            below = count(lambda kk, pos: (kk == thr) & (pos < cand))
            return jnp.where(below < need, cand, cur)
        tie_pos = lax.fori_loop(0, idx_bits, tie_bit, zero)

        def drop_late_ties(c, carry):
            off = pl.multiple_of(c * COUNT_CHUNK, COUNT_CHUNK)
            kk = key_ref[pl.ds(off, COUNT_CHUNK), :]
            late = (kk == thr) & (off + row_c > tie_pos)
            key_ref[pl.ds(off, COUNT_CHUNK), :] = jnp.where(late, INT_MIN, kk)
            return carry
        lax.fori_loop(0, n_count_chunks, drop_late_ties, 0)

    thr = jnp.maximum(thr, NEG_INF_KEY + 1)

    acc_ref[...] = jnp.zeros(acc_ref.shape, jnp.float32)

    n_pairs = N_HEADS_ATTN // 2

    def masked_logits(c, slot):
        off = chunk_off(jnp.minimum(c, n_chunks - 1))
        kk = key_ref[pl.ds(off, KEY_CHUNK), :]
        bias = jnp.where(kk >= thr, jnp.where(c < n_chunks, 0.0, NEG_BIG), NEG_BIG)
        bias_ref[slot] = jnp.concatenate([bias, bias], axis=1)
        for j in range(n_pairs):
            kp = k_ref[pl.ds(off, KEY_CHUNK), j * LANES:(j + 1) * LANES]
            s_ref[slot, j] = _dot(kp, qt_ref[j]) + bias_ref[slot]

    def softmax_pv(c, slot, ms, ls):
        off = chunk_off(jnp.minimum(c, n_chunks - 1))
        new_ms, new_ls = [], []
        for j in range(n_pairs):
            s = s_ref[slot, j]
            m_new = jnp.maximum(ms[j], jnp.max(s, axis=0, keepdims=True))
            alpha = jnp.exp2(ms[j] - m_new)
            p = jnp.exp2(s - m_new).astype(bf)
            vt = vt_ref[j * VT_ROWS:(j + 1) * VT_ROWS, pl.ds(off, KEY_CHUNK)]
            pv = _dot(vt, p)
            new_ls.append(alpha * ls[j] + pv[LANES:LANES + 1, :])
            new_ms.append(m_new)
            lo, mid, hi = j * LANES, j * LANES + HEAD_DIM, (j + 1) * LANES
            acc_ref[lo:mid, :] = alpha[:, :Q_BLOCK] * acc_ref[lo:mid, :] + pv[:HEAD_DIM, :Q_BLOCK]
            acc_ref[mid:hi, :] = alpha[:, Q_BLOCK:] * acc_ref[mid:hi, :] + pv[HEAD_DIM:LANES, Q_BLOCK:]
        return tuple(new_ms), tuple(new_ls)

    masked_logits(0, 0)

    def attn_two_chunks(i, carry):
        ms, ls = carry
        masked_logits(2 * i + 1, 1)
        ms, ls = softmax_pv(2 * i, 0, ms, ls)
        masked_logits(2 * i + 2, 0)
        return softmax_pv(2 * i + 1, 1, ms, ls)

    init = (tuple(jnp.full((1, 2 * Q_BLOCK), NEG_BIG, jnp.float32) for _ in range(n_pairs)),
            tuple(jnp.zeros((1, 2 * Q_BLOCK), jnp.float32) for _ in range(n_pairs)))
    _, ls = lax.fori_loop(0, (n_chunks + 1) // 2, attn_two_chunks, init)
    for j in range(n_pairs):
        lo, mid, hi = j * LANES, j * LANES + HEAD_DIM, (j + 1) * LANES
        even = acc_ref[lo:mid, :] / ls[j][:, :Q_BLOCK]
        odd = acc_ref[mid:hi, :] / ls[j][:, Q_BLOCK:]
        o_ref[:, lo:hi] = jnp.concatenate([even, odd], axis=0).T.astype(o_ref.dtype)


def _dsa(iq, iw, q, iklo, ikhi, k, vt):
    b, s, _ = q.shape
    n_top = min(TOPK_MAX, s // 4)
    idx_bits = max(1, int(np.ceil(np.log2(s))))
    qblk = lambda w: pl.BlockSpec((None, Q_BLOCK, w), lambda bi, qi: (bi, qi, 0))
    keys = lambda w: pl.BlockSpec((None, s, w), lambda bi, qi: (bi, 0, 0))
    return pl.pallas_call(
        functools.partial(_dsa_kernel, n_top=n_top, idx_bits=idx_bits),
        grid=(b, s // Q_BLOCK),
        in_specs=[qblk(IDXQ_W), qblk(LANES), qblk(ATTN_W), keys(LANES), keys(LANES),
                  keys(ATTN_W), pl.BlockSpec((None, VT_ALL, s), lambda bi, qi: (bi, 0, 0))],
        out_specs=qblk(ATTN_W),
        out_shape=jax.ShapeDtypeStruct((b, s, ATTN_W), jnp.bfloat16),
        scratch_shapes=[pltpu.VMEM((s, Q_BLOCK), jnp.int32),
                        pltpu.VMEM((N_HEADS_IDX // 4, LANES, 2 * Q_BLOCK), jnp.bfloat16),
                        pltpu.VMEM((N_HEADS_ATTN // 2, LANES, 2 * Q_BLOCK), jnp.bfloat16),
                        pltpu.VMEM((2, KEY_CHUNK, 2 * Q_BLOCK), jnp.float32),
                        pltpu.VMEM((2, N_HEADS_ATTN // 2, KEY_CHUNK, 2 * Q_BLOCK), jnp.float32),
                        pltpu.VMEM((ATTN_W, Q_BLOCK), jnp.float32)],
        compiler_params=pltpu.CompilerParams(dimension_semantics=("arbitrary", "arbitrary"),
                                             vmem_limit_bytes=VMEM_LIMIT),
        name="dsa",
    )(iq, iw, q, iklo, ikhi, k, vt)


def _merge_kernel(x_ref, ya_ref, u_ref, uprev_ref, qm_ref, km_ref, vm_ref,
                  gmix_ref, wg_ref, bg_ref, wpa_ref, wpp_ref, wpm_ref, wbd_ref, psc_ref,
                  wo_ref, gffn_ref, wrh_ref, wrhl_ref, br_ref,
                  h_ref, xn2_ref, eg_ref, ei_ref, cnt_ref, carry_ref, *, tiles_per_seq):
    i = pl.program_id(0)
    t = x_ref.shape[0]
    d = x_ref.shape[1]
    bf = jnp.bfloat16
    x = x_ref[...]
    xn = _rmsnorm(x, gmix_ref[...]).astype(bf)

    tile_in_seq = i % tiles_per_seq
    u = u_ref[...]
    halo = jnp.where(tile_in_seq == 0, 0.0, uprev_ref[...])
    ext = jnp.concatenate([halo, u], axis=0)
    lane_p = lax.broadcasted_iota(jnp.int32, (t, POOL_W), 1)
    pos1 = (tile_in_seq * t + lax.broadcasted_iota(jnp.int32, (t, 1), 0) + 1).astype(jnp.float32)
    pooled = None
    run, width = ext, 1
    for g, w in enumerate(POOL_WINDOWS):
        while width < w:
            run = run[width:] + run[:-width]
            width *= 2
        start = POOL_HALO + 1 - w
        mean = run[start:start + t] / jnp.minimum(pos1, float(w))
        pooled = mean if pooled is None else jnp.where(lane_p >= g * POOL_GROUP_DIM, mean, pooled)
    pooled = pooled - u
    mixed = _dot(pooled.astype(bf), wbd_ref[...]) * psc_ref[...]
    y_pool = _dot(mixed.astype(bf), wpp_ref[...])

    qm = qm_ref[...]
    probs = []
    for h in range(N_HEADS_MEM):
        s = _dot_nt(qm, km_ref[h])
        p = jnp.exp(s - jnp.max(s, axis=1, keepdims=True))
        probs.append((p / jnp.sum(p, axis=1, keepdims=True)).astype(bf))
    y_mem = _dot(probs[0], vm_ref[0])
    for h in range(1, N_HEADS_MEM):
        y_mem = y_mem + _dot(probs[h], vm_ref[h])
    y_mem = _dot(y_mem.astype(bf), wpm_ref[...])

    y_attn = _dot(ya_ref[...], wpa_ref[...])

    def gate(br):
        z = _dot(xn, wg_ref[:, br * d:(br + 1) * d]) + bg_ref[:, br * d:(br + 1) * d]
        return 1.0 / (1.0 + jnp.exp(-z))

    merged = gate(0) * y_attn + gate(1) * y_pool + gate(2) * y_mem
    h = x + _dot(merged.astype(bf), wo_ref[...])
    h_ref[...] = h
    xn2 = _rmsnorm(h, gffn_ref[...])
    _store_row_tiles(xn2_ref, xn2)

    x_hi = xn2.astype(bf)
    x_lo = (xn2 - x_hi.astype(jnp.float32)).astype(bf)
    by_hi = _dot(x_hi, wrhl_ref[...])
    logits = (by_hi[:, :LANES] + (_dot(x_lo, wrh_ref[...]) + by_hi[:, LANES:])) + br_ref[...]
    lane_e = lax.broadcasted_iota(jnp.int32, logits.shape, 1).astype(jnp.float32)
    work = logits
    vals, onehots, ids = [], [], []
    for _ in range(TOP_K):
        mx = jnp.max(work, axis=1, keepdims=True)
        idx = jnp.min(jnp.where(work == mx, lane_e, float(LANES)), axis=1, keepdims=True)
        oh = lane_e == idx
        vals.append(mx)
        ids.append(idx)
        onehots.append(oh)
        work = jnp.where(oh, -jnp.inf, work)
    exps = [jnp.exp(v - vals[0]) for v in vals]
    denom = exps[0] + exps[1] + exps[2] + exps[3]

    @pl.when(i == 0)
    def _():
        carry_ref[...] = jnp.zeros(carry_ref.shape, jnp.float32)

    member = jnp.where(onehots[0] | onehots[1] | onehots[2] | onehots[3], 1.0, 0.0)
    r_io = lax.broadcasted_iota(jnp.int32, (t, t), 0)
    c_io = lax.broadcasted_iota(jnp.int32, (t, t), 1)
    earlier = jnp.where(c_io < r_io, 1.0, 0.0).astype(bf)
    before = _dot(earlier, member.astype(bf)) + carry_ref[...]
    carry_ref[...] = carry_ref[...] + jnp.sum(member, axis=0, keepdims=True)
    cnt_ref[...] = carry_ref[...]

    lane8 = lax.broadcasted_iota(jnp.int32, (t, 2 * TOP_K), 1)
    eg = jnp.zeros((t, 2 * TOP_K), jnp.float32)
    ei = jnp.zeros((t, 2 * TOP_K), jnp.float32)
    for j in range(TOP_K):
        rank = jnp.sum(jnp.where(onehots[j], before, 0.0), axis=1, keepdims=True)
        eg = jnp.where(lane8 == j, exps[j] / denom, eg)
        ei = jnp.where(lane8 == j, ids[j], ei)
        ei = jnp.where(lane8 == TOP_K + j, rank, ei)
    eg_ref[...] = eg
    ei_ref[...] = ei.astype(jnp.int32)


def _merge(x2, ya, u, qm, km, vm, gmix, wg, bg, wpa, wpp, wpm, wbd, psc, wo, gffn, wrh, wrhl, br, seq):
    n, d = x2.shape
    t = MERGE_TILE
    tiles_per_seq = seq // t
    m = km.shape[2]
    row = lambda i: (i, 0)
    full = lambda i: (0, 0)
    per_batch = lambda i: (i // tiles_per_seq, 0, 0, 0)
    halo_blocks = t // POOL_HALO
    prev = lambda i: (jnp.maximum(i * halo_blocks - 1, 0), 0)
    consts = (gmix, wg, bg, wpa, wpp, wpm, wbd, psc, wo, gffn, wrh, wrhl, br)
    pieces = d // LANES
    out_shape = (jax.ShapeDtypeStruct((n, d), jnp.float32),
                 jax.ShapeDtypeStruct((n * pieces, LANES), jnp.float32),
                 jax.ShapeDtypeStruct((n, 2 * TOP_K), jnp.float32),
                 jax.ShapeDtypeStruct((n, 2 * TOP_K), jnp.int32),
                 jax.ShapeDtypeStruct((1, LANES), jnp.float32))
    return pl.pallas_call(
        functools.partial(_merge_kernel, tiles_per_seq=tiles_per_seq),
        grid=(n // t,),
        in_specs=[pl.BlockSpec((t, d), row), pl.BlockSpec((t, ATTN_W), row),
                  pl.BlockSpec((t, POOL_W), row), pl.BlockSpec((POOL_HALO, POOL_W), prev),
                  pl.BlockSpec((t, MEM_W), row),
                  pl.BlockSpec((None, N_HEADS_MEM, m, MEM_W), per_batch),
                  pl.BlockSpec((None, N_HEADS_MEM, m, MEM_W), per_batch)]
                 + [pl.BlockSpec(c.shape, full) for c in consts],
        out_specs=(pl.BlockSpec((t, d), row), pl.BlockSpec((t * pieces, LANES), row),
                   pl.BlockSpec((t, 2 * TOP_K), row), pl.BlockSpec((t, 2 * TOP_K), row),
                   pl.BlockSpec((1, LANES), full)),
        out_shape=out_shape,
        scratch_shapes=[pltpu.VMEM((1, LANES), jnp.float32)],
        compiler_params=pltpu.CompilerParams(dimension_semantics=("arbitrary",),
                                             vmem_limit_bytes=VMEM_LIMIT),
        name="merge",
    )(x2, ya, u, u, qm, km, vm, *consts)


def _row_copy(src_ref, src_row, dst_ref, dst_row, sem):
    return pltpu.make_async_copy(_row_tile(src_ref, src_row), _row_tile(dst_ref, dst_row), sem)


def _dispatch_kernel(pstart_ref, cnt_ref, dest_ref, x_ref, xs_ref, zero_ref, sem):
    t = x_ref.shape[0] // SUBLANES
    block_tiles = ROW_BLOCK * SUBLANES

    @pl.when(pl.program_id(0) == 0)
    def _():
        zero_ref[...] = jnp.zeros(zero_ref.shape, zero_ref.dtype)

        def for_each_pad_row(act):
            def per_expert(e, carry):
                n_pad = (ROW_BLOCK - cnt_ref[e] % ROW_BLOCK) % ROW_BLOCK
                first = pstart_ref[e] + cnt_ref[e]

                def per_row(r, c):
                    act(_row_copy(zero_ref, 0, xs_ref, first + r, sem))
                    return c
                return lax.fori_loop(0, n_pad, per_row, carry)
            lax.fori_loop(0, N_EXPERTS, per_expert, 0)

        def for_each_unused_block(act):
            n_blocks = xs_ref.shape[0] // block_tiles
            first = (pstart_ref[N_EXPERTS - 1] + cnt_ref[N_EXPERTS - 1] + ROW_BLOCK - 1) // ROW_BLOCK

            def per_block(b, c):
                rows = pl.ds(pl.multiple_of(b * block_tiles, block_tiles), block_tiles)
                act(pltpu.make_async_copy(zero_ref, xs_ref.at[rows], sem))
                return c
            lax.fori_loop(first, n_blocks, per_block, 0)

        for act in (lambda cp: cp.start(), lambda cp: cp.wait()):
            for_each_pad_row(act)
            for_each_unused_block(act)

    def issue(tok, carry):
        for j in range(TOP_K):
            _row_copy(x_ref, tok, xs_ref, dest_ref[tok * TOP_K + j], sem).start(priority=j % 2)
        return carry

    lax.fori_loop(0, t, issue, 0, unroll=ISSUE_UNROLL)
    for j in range(TOP_K):
        pltpu.make_async_copy(x_ref, xs_ref.at[pl.ds(0, x_ref.shape[0])], sem).wait()


def _dispatch(pstart, cnt, dest, xn2, n_rows):
    tiles, lanes = xn2.shape
    n = tiles // SUBLANES
    t = MOVE_TILE
    return pl.pallas_call(
        _dispatch_kernel,
        grid_spec=pltpu.PrefetchScalarGridSpec(
            num_scalar_prefetch=2,
            grid=(n // t,),
            in_specs=[pl.BlockSpec((t * TOP_K,), lambda i, ps, ct: (i,), memory_space=pltpu.SMEM),
                      pl.BlockSpec((t * SUBLANES, lanes), lambda i, ps, ct: (i, 0))],
            out_specs=pl.BlockSpec(memory_space=pl.ANY),
            scratch_shapes=[pltpu.VMEM((ROW_BLOCK * SUBLANES, lanes), jnp.float32),
                            pltpu.SemaphoreType.DMA(())]),
        out_shape=jax.ShapeDtypeStruct((n_rows * SUBLANES, lanes), jnp.float32),
        compiler_params=pltpu.CompilerParams(dimension_semantics=("arbitrary",),
                                             vmem_limit_bytes=VMEM_LIMIT),
        name="dispatch",
    )(pstart, cnt, dest, xn2)


def _experts_kernel(be_ref, nused_ref, half_ref, next_ref, xs_ref, wu_hbm, bu_ref, wd_hbm, bd_ref,
                    ys_ref, wu32_ref, wd32_ref, wu16_ref, wd16_ref, sems):
    i = pl.program_id(0)
    d, f = wu16_ref.shape[0], wd16_ref.shape[0]
    half = half_ref[i]

    def weight_copies(e, into):
        return (pltpu.make_async_copy(wu_hbm.at[e], wu32_ref.at[into], sems.at[0, into]),
                pltpu.make_async_copy(wd_hbm.at[e], wd32_ref.at[into], sems.at[1, into]))

    @pl.when(i == 0)
    def _():
        for cp in weight_copies(be_ref[0], 0):
            cp.start()

    @pl.when((i == 0) | (be_ref[i] != be_ref[jnp.maximum(i - 1, 0)]))
    def _():
        for cp in weight_copies(be_ref[i], half):
            cp.wait()
        wu16_ref[...] = wu32_ref[half].astype(wu16_ref.dtype)
        wd16_ref[...] = wd32_ref[half].astype(wd16_ref.dtype)

        @pl.when(next_ref[i] >= 0)
        def _():
            for cp in weight_copies(next_ref[i], 1 - half):
                cp.start()

    @pl.when(i < nused_ref[0])
    def _():
        x = _load_row_tiles(xs_ref, ROW_BLOCK, d).astype(jnp.bfloat16)
        acts = []
        for lo in range(0, f, UP_TILE):
            glu_cols, lin_cols = slice(lo, lo + UP_TILE), slice(f + lo, f + lo + UP_TILE)
            glu = jnp.minimum(_dot(x, wu16_ref[:, glu_cols]) + bu_ref[:, glu_cols], SWIGLU_LIMIT)
            lin = jnp.clip(_dot(x, wu16_ref[:, lin_cols]) + bu_ref[:, lin_cols],
                           -SWIGLU_LIMIT, SWIGLU_LIMIT)
            act = glu * (1.0 / (1.0 + jnp.exp(-SWIGLU_ALPHA * glu))) * (lin + 1.0)
            acts.append(act.astype(jnp.bfloat16))
        act = jnp.concatenate(acts, axis=1)
        _store_row_tiles(ys_ref, _dot(act, wd16_ref[...]) + bd_ref[...])

    @pl.when(i >= nused_ref[0])
    def _():
        ys_ref[...] = jnp.zeros(ys_ref.shape, ys_ref.dtype)


def _experts(block_e, nused, block_half, block_next, xs, wu, bu, wd, bd):
    tiles, lanes = xs.shape
    e, d, f2 = wu.shape
    f = wd.shape[1]
    block = (ROW_BLOCK * SUBLANES, lanes)
    rows = lambda i, be, nu, hf, nx: (i, 0)
    used_rows = lambda i, be, nu, hf, nx: (jnp.minimum(i, nu[0] - 1), 0)
    per_e = lambda i, be, nu, hf, nx: (be[i], 0, 0)
    return pl.pallas_call(
        _experts_kernel,
        grid_spec=pltpu.PrefetchScalarGridSpec(
            num_scalar_prefetch=4,
            grid=(tiles // block[0],),
            in_specs=[pl.BlockSpec(block, used_rows),
                      pl.BlockSpec(memory_space=pl.ANY), pl.BlockSpec((None, 1, f2), per_e),
                      pl.BlockSpec(memory_space=pl.ANY), pl.BlockSpec((None, 1, d), per_e)],
            out_specs=pl.BlockSpec(block, rows),
            scratch_shapes=[pltpu.VMEM((2, d, f2), jnp.float32), pltpu.VMEM((2, f, d), jnp.float32),
                            pltpu.VMEM((d, f2), jnp.bfloat16), pltpu.VMEM((f, d), jnp.bfloat16),
                            pltpu.SemaphoreType.DMA((2, 2))]),
        out_shape=jax.ShapeDtypeStruct(xs.shape, jnp.float32),
        compiler_params=pltpu.CompilerParams(dimension_semantics=("arbitrary",),
                                             vmem_limit_bytes=VMEM_LIMIT),
        name="experts",
    )(block_e, nused, block_half, block_next, xs, wu, bu, wd, bd)


def _combine_kernel(dest_ref, dest_next_ref, h_ref, eg_ref, g_ref, ys_ref, o_ref, buf_ref, sems,
                    *, final_norm):
    i = pl.program_id(0)
    t, d = h_ref.shape
    half = i % 2

    def gather_rows(idx_ref, into):
        def issue(tok, carry):
            for j in range(TOP_K):
                _row_copy(ys_ref, idx_ref[tok * TOP_K + j], buf_ref.at[into, j], tok,
                          sems.at[into]).start(priority=j % 2)
            return carry
        lax.fori_loop(0, t, issue, 0, unroll=ISSUE_UNROLL)

    @pl.when(i == 0)
    def _():
        gather_rows(dest_ref, 0)

    @pl.when(i + 1 < pl.num_programs(0))
    def _():
        gather_rows(dest_next_ref, 1 - half)

    for j in range(TOP_K):
        pltpu.make_async_copy(ys_ref.at[pl.ds(0, t * SUBLANES)], buf_ref.at[half, j],
                              sems.at[half]).wait()
    eg = eg_ref[...]
    h = h_ref[...]
    for j in range(TOP_K):
        h = h + eg[:, j:j + 1] * _load_row_tiles(buf_ref.at[half, j], t, d)
    o_ref[...] = _rmsnorm(h, g_ref[...]) if final_norm else h


def _combine(dest, h, eg, g, ys, final_norm):
    n, d = h.shape
    t = MOVE_TILE
    return pl.pallas_call(
        functools.partial(_combine_kernel, final_norm=final_norm),
        grid=(n // t,),
        in_specs=[pl.BlockSpec((t * TOP_K,), lambda i: (i,), memory_space=pltpu.SMEM),
                  pl.BlockSpec((t * TOP_K,), lambda i: (jnp.minimum(i + 1, n // t - 1),),
                               memory_space=pltpu.SMEM),
                  pl.BlockSpec((t, d), lambda i: (i, 0)),
                  pl.BlockSpec((t, 2 * TOP_K), lambda i: (i, 0)),
                  pl.BlockSpec((1, d), lambda i: (0, 0)),
                  pl.BlockSpec(memory_space=pl.ANY)],
        out_specs=pl.BlockSpec((t, d), lambda i: (i, 0)),
        scratch_shapes=[pltpu.VMEM((2, TOP_K, t * SUBLANES, ys.shape[1]), jnp.float32),
                        pltpu.SemaphoreType.DMA((2,))],
        out_shape=jax.ShapeDtypeStruct((n, d), jnp.float32),
        compiler_params=pltpu.CompilerParams(dimension_semantics=("arbitrary",),
                                             vmem_limit_bytes=VMEM_LIMIT),
        name="combine",
    )(dest, dest, h, eg, g, ys)


def _rope_tables(seq):
    half = ROPE_DIM // 2
    inv = jnp.power(jnp.float32(ROPE_THETA), -jnp.arange(half, dtype=jnp.float32) / half)
    ang = jnp.arange(seq, dtype=jnp.float32)[:, None] * inv[None, :]
    cos, sin = jnp.cos(ang), jnp.sin(ang)
    pad = HEAD_DIM - ROPE_DIM
    one, zero = jnp.ones((seq, pad), jnp.float32), jnp.zeros((seq, pad), jnp.float32)
    zh = jnp.zeros((seq, half), jnp.float32)
    reps = LANES // HEAD_DIM
    c = jnp.tile(jnp.concatenate([cos, cos, one], axis=1), (1, reps))
    a = jnp.tile(jnp.concatenate([-sin, zh, zero], axis=1), (1, reps))
    b = jnp.tile(jnp.concatenate([zh, sin, zero], axis=1), (1, reps))
    return c, a, b


def _layer(h, mem, p, tables, final_g):
    b, s, d = h.shape
    n = b * s
    bf = jnp.bfloat16
    x2 = h.reshape(n, d)
    w_in = p["w_in"]
    o_ik = 3 * ATTN_W + IDXQ_W
    o_u = o_ik + IDX_DIM + N_HEADS_IDX
    wa = w_in[:, :o_ik].astype(bf)
    wb = jnp.pad(w_in[:, o_ik:o_u], ((0, 0), (0, LANES - (o_u - o_ik)))).astype(bf)
    wc = w_in[:, o_u:].astype(bf)
    q, k, vt, iq, iklo, ikhi, iw, u, qm = _in_proj(x2, p["norm_mix_g"][None], wa, wb, wc, *tables, b, s)
    km, vm = _mem_kv(mem, p["mem_norm_g"][None], p["w_mem_kv"].astype(bf))
    b3 = lambda a: a.reshape(b, s, a.shape[-1])
    ya = _dsa(b3(iq), b3(iw), b3(q), b3(iklo), b3(ikhi), b3(k), vt).reshape(n, ATTN_W)

    wbd = jnp.zeros((POOL_W, POOL_W), jnp.float32)
    for g in range(N_POOL_GROUPS):
        lo = g * POOL_GROUP_DIM
        wbd = wbd.at[lo:lo + POOL_GROUP_DIM, lo:lo + POOL_GROUP_DIM].set(p["pool_w_group"][g])
    wr = jnp.pad(p["w_router"], ((0, 0), (0, LANES - N_EXPERTS)))
    wr_hi = wr.astype(bf)
    br = jnp.pad(p["b_router"], (0, LANES - N_EXPERTS), constant_values=NEG_BIG)[None]
    hmid, xn2, eg, ei, counts = _merge(
        x2, ya, u, qm, km, vm, p["norm_mix_g"][None], p["w_gate"].astype(bf), p["b_gate"][None],
        p["w_proj_attn"].astype(bf), p["w_proj_pool"].astype(bf), p["w_proj_mem"].astype(bf),
        wbd.astype(bf), p["pool_scale"][None], p["w_out"].astype(bf), p["norm_ffn_g"][None],
        wr_hi, jnp.concatenate([wr_hi, (wr - wr_hi.astype(jnp.float32)).astype(bf)], axis=1), br, s)

    cnt = counts[0, :N_EXPERTS].astype(jnp.int32)
    padded = ((cnt + ROW_BLOCK - 1) // ROW_BLOCK) * ROW_BLOCK
    pend = jnp.cumsum(padded)
    pstart = (pend - padded).astype(jnp.int32)
    n_rows = n * TOP_K + N_EXPERTS * ROW_BLOCK
    n_blocks = n_rows // ROW_BLOCK
    nused = (pend[-1] // ROW_BLOCK).astype(jnp.int32)
    blk = jnp.minimum(jnp.arange(n_blocks, dtype=jnp.int32), nused - 1) * ROW_BLOCK
    block_e = jnp.sum(blk[:, None] >= pend[None, :], axis=1).astype(jnp.int32)
    block_e = jnp.minimum(block_e, N_EXPERTS - 1)
    has_rows = cnt > 0
    run_index = jnp.cumsum(has_rows) - 1
    ids = jnp.arange(N_EXPERTS, dtype=jnp.int32)
    later = jnp.where(has_rows[None, :] & (ids[None, :] > ids[:, None]), ids[None, :], N_EXPERTS)
    next_used = jnp.min(later, axis=1)
    next_used = jnp.where(next_used < N_EXPERTS, next_used, -1).astype(jnp.int32)
    of_block = block_e[:, None] == ids[None, :]
    block_half = jnp.sum(jnp.where(of_block, run_index % 2, 0), axis=1).astype(jnp.int32)
    block_next = jnp.sum(jnp.where(of_block, next_used, 0), axis=1).astype(jnp.int32)
    chosen =ei[:, :TOP_K, None] == jnp.arange(N_EXPERTS, dtype=jnp.int32)
    dest = (jnp.sum(jnp.where(chosen, pstart, 0), axis=-1) + ei[:, TOP_K:]).reshape(-1)

    xs = _dispatch(pstart, cnt, dest, xn2, n_rows)
    ys = _experts(block_e, nused[None], block_half, block_next, xs, p["w_up"], p["b_up"][:, None, :],
                  p["w_down"], p["b_down"][:, None, :])
    g = p["norm_ffn_g"][None] if final_g is None else final_g[None]
    return _combine(dest, hmid, eg, g, ys, final_g is not None).reshape(b, s, d)


def kernel(x, mem, norm_mix_g, w_in, w_gate, b_gate, w_proj_attn, w_proj_pool, w_proj_mem,
           pool_w_group, pool_scale, mem_norm_g, w_mem_kv, w_out, norm_ffn_g,
           w_router, b_router, w_up, b_up, w_down, b_down, norm_final_g):
    b, s, d = x.shape
    assert s % IN_TILE == 0 and s % COUNT_CHUNK == 0 and (b * s) % MOVE_TILE == 0
    assert d == LANES * SUBLANES, "the row-tile layout maps one model row onto one (8, 128) tile"
    stacked = dict(norm_mix_g=norm_mix_g, w_in=w_in, w_gate=w_gate, b_gate=b_gate,
                   w_proj_attn=w_proj_attn, w_proj_pool=w_proj_pool, w_proj_mem=w_proj_mem,
                   pool_w_group=pool_w_group, pool_scale=pool_scale, mem_norm_g=mem_norm_g,
                   w_mem_kv=w_mem_kv, w_out=w_out, norm_ffn_g=norm_ffn_g, w_router=w_router,
                   b_router=b_router, w_up=w_up, b_up=b_up, w_down=w_down, b_down=b_down)
    depth = w_in.shape[0]
    tables = _rope_tables(s)
    h = x
    for l in range(depth):
        p = {name: val[l] for name, val in stacked.items()}
        h = _layer(h, mem, p, tables, norm_final_g if l == depth - 1 else None)
    return h
```

```python
import functools

import jax
import jax.numpy as jnp
import numpy as np
from jax import lax
from jax.experimental import pallas as pl
from jax.experimental.pallas import tpu as pltpu

N_HEADS_ATTN = 8
HEAD_DIM = 64
ROPE_DIM = HEAD_DIM // 4
ROPE_THETA = 500000.0
N_HEADS_IDX = 8
IDX_DIM = 64
TOPK_MAX = 256
N_POOL_GROUPS = 4
POOL_GROUP_DIM = 64
POOL_WINDOWS = (2, 4, 8, 16)
POOL_HALO = 16
N_HEADS_MEM = 4
N_BRANCHES = 3
N_EXPERTS = 32
TOP_K = 4
SWIGLU_ALPHA = 1.702
SWIGLU_LIMIT = 7.0
EPS = 1e-6

ATTN_W = N_HEADS_ATTN * HEAD_DIM
POOL_W = N_POOL_GROUPS * POOL_GROUP_DIM
MEM_W = N_HEADS_MEM * HEAD_DIM
IDXQ_W = N_HEADS_IDX * IDX_DIM

LANES = 128
SUBLANES = 8
VMEM_LIMIT = 56 * 1024 * 1024

IN_TILE = 512
Q_BLOCK = 128
KEY_CHUNK = 256
COUNT_CHUNK = 512
COUNT_ROWS = 32
EARLY_EXIT_PASSES = (25, 28)
MERGE_TILE = 512
ROW_BLOCK = 256
UP_TILE = 256
MOVE_TILE = 512
ISSUE_UNROLL = 4

VT_ROWS = LANES + 16
VT_ALL = (N_HEADS_ATTN // 2) * VT_ROWS
LOG2_E = 1.4426950408889634

NEG_BIG = -1e30
INT_MIN = -2 ** 31
NEG_INF_KEY = -0x7F800001


def _dot(a, b):
    return jnp.dot(a, b, preferred_element_type=jnp.float32)


def _dot_nt(a, b):
    return lax.dot_general(a, b, (((1,), (1,)), ((), ())), preferred_element_type=jnp.float32)


def _rmsnorm(x, g):
    return x * lax.rsqrt(jnp.mean(x * x, axis=-1, keepdims=True) + EPS) * g


def _store_row_tiles(ref, x):
    rows, width = x.shape
    for j in range(width // LANES):
        ref[pl.ds(j, rows, stride=width // LANES), :] = x[:, j * LANES:(j + 1) * LANES]


def _load_row_tiles(ref, rows, width):
    pieces = width // LANES
    return jnp.concatenate([ref[pl.ds(j, rows, stride=pieces), :] for j in range(pieces)], axis=1)


def _row_tile(ref, row):
    return ref.at[pl.ds(pl.multiple_of(row * SUBLANES, SUBLANES), SUBLANES)]


def _rope128(x, c, a, b):
    return x * c + pltpu.roll(x, LANES - ROPE_DIM // 2, 1) * a + pltpu.roll(x, ROPE_DIM // 2, 1) * b


def _in_proj_kernel(x_ref, g_ref, wa_ref, wb_ref, wc_ref, rc_ref, ra_ref, rb_ref,
                    q_ref, k_ref, vt_ref, iq_ref, iklo_ref, ikhi_ref, iw_ref, u_ref, qm_ref):
    xn = _rmsnorm(x_ref[...], g_ref[...]).astype(jnp.bfloat16)
    rc, ra, rb = rc_ref[...], ra_ref[...], rb_ref[...]
    pa = _dot(xn, wa_ref[...])
    for seg, (ref, scale) in enumerate(((q_ref, LOG2_E * HEAD_DIM ** -0.5), (k_ref, None),
                                        (None, None), (iq_ref, IDX_DIM ** -0.5))):
        for c in range(ATTN_W // LANES):
            lo = seg * ATTN_W + c * LANES
            blk = pa[:, lo:lo + LANES]
            if ref is None:
                vt_ref[c * VT_ROWS:c * VT_ROWS + LANES, :] = blk.T.astype(vt_ref.dtype)
                vt_ref[c * VT_ROWS + LANES:(c + 1) * VT_ROWS, :] = jnp.ones(
                    (VT_ROWS - LANES, blk.shape[0]), vt_ref.dtype)
                continue
            blk = _rope128(blk, rc, ra, rb)
            if scale is not None:
                blk = blk * scale
            ref[:, c * LANES:(c + 1) * LANES] = blk.astype(ref.dtype)
    pb = _dot(xn, wb_ref[...])
    lane = lax.broadcasted_iota(jnp.int32, pb.shape, 1)
    ik = jnp.where(lane < IDX_DIM, _rope128(pb, rc, ra, rb), 0.0)
    iklo_ref[...] = ik.astype(iklo_ref.dtype)
    ikhi_ref[...] = pltpu.roll(ik, IDX_DIM, 1).astype(ikhi_ref.dtype)
    iw_ref[...] = pb * (N_HEADS_IDX ** -0.5)
    pc = _dot(xn, wc_ref[...])
    u_ref[...] = pc[:, :POOL_W]
    qm_ref[...] = (pc[:, POOL_W:] * (HEAD_DIM ** -0.5)).astype(qm_ref.dtype)


def _in_proj(x2, g, wa, wb, wc, rc, ra, rb, batch, seq):
    n, d = x2.shape
    t = IN_TILE
    tiles_per_seq = seq // t
    row = lambda i: (i, 0)
    full = lambda i: (0, 0)
    pos = lambda i: (i % tiles_per_seq, 0)
    bf = jnp.bfloat16
    flat = lambda w, dt: (jax.ShapeDtypeStruct((n, w), dt), pl.BlockSpec((t, w), row))
    outs = (flat(ATTN_W, bf), flat(ATTN_W, bf),
            (jax.ShapeDtypeStruct((batch, VT_ALL, seq), bf),
             pl.BlockSpec((None, VT_ALL, t), lambda i: (i // tiles_per_seq, 0, i % tiles_per_seq))),
            flat(IDXQ_W, bf), flat(LANES, bf), flat(LANES, bf), flat(LANES, jnp.float32),
            flat(POOL_W, jnp.float32), flat(MEM_W, bf))
    return pl.pallas_call(
        _in_proj_kernel,
        grid=(n // t,),
        in_specs=[pl.BlockSpec((t, d), row), pl.BlockSpec((1, d), full),
                  pl.BlockSpec(wa.shape, full), pl.BlockSpec(wb.shape, full),
                  pl.BlockSpec(wc.shape, full),
                  pl.BlockSpec((t, LANES), pos), pl.BlockSpec((t, LANES), pos),
                  pl.BlockSpec((t, LANES), pos)],
        out_specs=tuple(o[1] for o in outs),
        out_shape=tuple(o[0] for o in outs),
        compiler_params=pltpu.CompilerParams(dimension_semantics=("arbitrary",),
                                             vmem_limit_bytes=VMEM_LIMIT),
        name="in_proj",
    )(x2, g, wa, wb, wc, rc, ra, rb)


def _mem_kv_kernel(mem_ref, g_ref, w_ref, km_ref, vm_ref):
    mn = _rmsnorm(mem_ref[...], g_ref[...]).astype(jnp.bfloat16)
    kv = _dot(mn, w_ref[...])
    km, vm = kv[:, :MEM_W], kv[:, MEM_W:]
    lane = lax.broadcasted_iota(jnp.int32, km.shape, 1)
    for h in range(N_HEADS_MEM):
        in_head = (lane >= h * HEAD_DIM) & (lane < (h + 1) * HEAD_DIM)
        km_ref[h] = jnp.where(in_head, km, 0.0).astype(km_ref.dtype)
        vm_ref[h] = jnp.where(in_head, vm, 0.0).astype(vm_ref.dtype)


def _mem_kv(mem, g, w):
    b, m, d = mem.shape
    out = jax.ShapeDtypeStruct((b, N_HEADS_MEM, m, MEM_W), jnp.bfloat16)
    return pl.pallas_call(
        _mem_kv_kernel,
        grid=(b,),
        in_specs=[pl.BlockSpec((None, m, d), lambda i: (i, 0, 0)),
                  pl.BlockSpec((1, d), lambda i: (0, 0)),
                  pl.BlockSpec(w.shape, lambda i: (0, 0))],
        out_specs=(pl.BlockSpec((None, N_HEADS_MEM, m, MEM_W), lambda i: (i, 0, 0, 0)),
                   pl.BlockSpec((None, N_HEADS_MEM, m, MEM_W), lambda i: (i, 0, 0, 0))),
        out_shape=(out, out),
        compiler_params=pltpu.CompilerParams(dimension_semantics=("arbitrary",),
                                             vmem_limit_bytes=VMEM_LIMIT),
        name="mem_kv",
    )(mem, g, w)


def _dsa_kernel(iq_ref, iw_ref, q_ref, iklo_ref, ikhi_ref, k_ref, vt_ref, o_ref,
                key_ref, iqt_ref, qt_ref, bias_ref, s_ref, acc_ref, *, n_top, idx_bits):
    qb = pl.program_id(1)
    n_chunks = (qb * Q_BLOCK + Q_BLOCK + KEY_CHUNK - 1) // KEY_CHUNK
    n_count_chunks = (qb * Q_BLOCK + Q_BLOCK + COUNT_CHUNK - 1) // COUNT_CHUNK
    q_pos = qb * Q_BLOCK + lax.broadcasted_iota(jnp.int32, (1, Q_BLOCK), 1)
    row_c = lax.broadcasted_iota(jnp.int32, (COUNT_CHUNK, 1), 0)
    row_d = lax.broadcasted_iota(jnp.int32, (LANES, Q_BLOCK), 0)
    bf = jnp.bfloat16

    def chunk_off(c):
        return pl.multiple_of(c * KEY_CHUNK, KEY_CHUNK)

    wt = iw_ref[...].T
    for j in range(N_HEADS_IDX // 2):
        iqt_ref[j // 2, :, (j % 2) * Q_BLOCK:(j % 2 + 1) * Q_BLOCK] = (
            iq_ref[:, j * LANES:(j + 1) * LANES].astype(jnp.float32).T.astype(bf))
    for j in range(N_HEADS_ATTN // 2):
        pair_t = q_ref[:, j * LANES:(j + 1) * LANES].astype(jnp.float32).T
        qt_ref[j, :, :Q_BLOCK] = jnp.where(row_d < HEAD_DIM, pair_t, 0.0).astype(bf)
        qt_ref[j, :, Q_BLOCK:] = jnp.where(row_d >= HEAD_DIM, pair_t, 0.0).astype(bf)

    def score_chunk(c, carry):
        off = pl.multiple_of(c * COUNT_CHUNK, COUNT_CHUNK)
        acc = jnp.zeros((COUNT_CHUNK, Q_BLOCK), jnp.float32)
        for parity, ik_ref in enumerate((iklo_ref, ikhi_ref)):
            ik = ik_ref[pl.ds(off, COUNT_CHUNK), :]
            for g in range(N_HEADS_IDX // 4):
                dots = _dot(ik, iqt_ref[g])
                for side in range(2):
                    head = 2 * (2 * g + side) + parity
                    w = wt[IDX_DIM + head:IDX_DIM + head + 1, :]
                    acc = acc + w * jnp.maximum(dots[:, side * Q_BLOCK:(side + 1) * Q_BLOCK], 0.0)
        acc = jnp.where(off + row_c <= q_pos, acc, -jnp.inf)
        bits = lax.bitcast_convert_type(acc, jnp.int32)
        key_ref[pl.ds(off, COUNT_CHUNK), :] = bits ^ ((bits >> 31) & 0x7FFFFFFF)
        return carry

    lax.fori_loop(0, n_count_chunks, score_chunk, 0)

    def count(pred):
        def body(c, cnt):
            off = pl.multiple_of(c * COUNT_CHUNK, COUNT_CHUNK)
            hit = jnp.where(pred(key_ref[pl.ds(off, COUNT_CHUNK), :], off + row_c), 1.0, 0.0)
            return cnt + jnp.sum(hit.reshape(COUNT_CHUNK // COUNT_ROWS, COUNT_ROWS, Q_BLOCK), axis=0)
        cnt = lax.fori_loop(0, n_count_chunks, body, jnp.zeros((COUNT_ROWS, Q_BLOCK), jnp.float32))
        return jnp.sum(cnt, axis=0, keepdims=True)

    k_f = jnp.float32(n_top)
    zero = jnp.zeros((1, Q_BLOCK), jnp.int32)
    n_nonneg = count(lambda kk, pos: kk >= zero)
    n_stored = (n_count_chunks * COUNT_CHUNK).astype(jnp.float32)
    thr = jnp.where(n_nonneg >= k_f, zero, INT_MIN)
    n_ge = jnp.where(n_nonneg >= k_f, n_nonneg, n_stored)

    def thr_bit(i, carry):
        cur, n_cur = carry
        cand = cur + jnp.left_shift(jnp.int32(1), 30 - i)
        n_cand = count(lambda kk, pos: kk >= cand)
        keep = n_cand >= k_f
        return jnp.where(keep, cand, cur), jnp.where(keep, n_cand, n_cur)

    def all_exact(carry):
        return jnp.max(jnp.abs(carry[1] - k_f)) == 0.0

    def search_bits(first, last):
        return lambda carry: lax.fori_loop(first, last, thr_bit, carry)

    carry = search_bits(0, EARLY_EXIT_PASSES[0])((thr, n_ge))
    for first, last in zip(EARLY_EXIT_PASSES, EARLY_EXIT_PASSES[1:] + (31,)):
        carry = lax.cond(all_exact(carry), lambda c: c, search_bits(first, last), carry)
    thr, n_ge = carry

    @pl.when(jnp.max(n_ge) > k_f)
    def _():
        need = k_f - count(lambda kk, pos: kk > thr)

        def tie_bit(i, cur):
            cand = cur + jnp.left_shift(jnp.int32(1), idx_bits - 1 - i)
            below = count(lambda kk, pos: (kk == thr) & (pos < cand))
            return jnp.where(below < need, cand, cur)
        tie_pos = lax.fori_loop(0, idx_bits, tie_bit, zero)

        def drop_late_ties(c, carry):
            off = pl.multiple_of(c * COUNT_CHUNK, COUNT_CHUNK)
            kk = key_ref[pl.ds(off, COUNT_CHUNK), :]
            late = (kk == thr) & (off + row_c > tie_pos)
            key_ref[pl.ds(off, COUNT_CHUNK), :] = jnp.where(late, INT_MIN, kk)
            return carry
        lax.fori_loop(0, n_count_chunks, drop_late_ties, 0)

    thr = jnp.maximum(thr, NEG_INF_KEY + 1)

    acc_ref[...] = jnp.zeros(acc_ref.shape, jnp.float32)

    n_pairs = N_HEADS_ATTN // 2

    def masked_logits(c, slot):
        off = chunk_off(jnp.minimum(c, n_chunks - 1))
        kk = key_ref[pl.ds(off, KEY_CHUNK), :]
        bias = jnp.where(kk >= thr, jnp.where(c < n_chunks, 0.0, NEG_BIG), NEG_BIG)
        bias_ref[slot] = jnp.concatenate([bias, bias], axis=1)
        for j in range(n_pairs):
            kp = k_ref[pl.ds(off, KEY_CHUNK), j * LANES:(j + 1) * LANES]
            s_ref[slot, j] = _dot(kp, qt_ref[j]) + bias_ref[slot]

    def softmax_pv(c, slot, ms, ls):
        off = chunk_off(jnp.minimum(c, n_chunks - 1))
        new_ms, new_ls = [], []
        for j in range(n_pairs):
            s = s_ref[slot, j]
            m_new = jnp.maximum(ms[j], jnp.max(s, axis=0, keepdims=True))
            alpha = jnp.exp2(ms[j] - m_new)
            p = jnp.exp2(s - m_new).astype(bf)
            vt = vt_ref[j * VT_ROWS:(j + 1) * VT_ROWS, pl.ds(off, KEY_CHUNK)]
            pv = _dot(vt, p)
            new_ls.append(alpha * ls[j] + pv[LANES:LANES + 1, :])
            new_ms.append(m_new)
            lo, mid, hi = j * LANES, j * LANES + HEAD_DIM, (j + 1) * LANES
            acc_ref[lo:mid, :] = alpha[:, :Q_BLOCK] * acc_ref[lo:mid, :] + pv[:HEAD_DIM, :Q_BLOCK]
            acc_ref[mid:hi, :] = alpha[:, Q_BLOCK:] * acc_ref[mid:hi, :] + pv[HEAD_DIM:LANES, Q_BLOCK:]
        return tuple(new_ms), tuple(new_ls)

    masked_logits(0, 0)

    def attn_two_chunks(i, carry):
        ms, ls = carry
        masked_logits(2 * i + 1, 1)
        ms, ls = softmax_pv(2 * i, 0, ms, ls)
        masked_logits(2 * i + 2, 0)
        return softmax_pv(2 * i + 1, 1, ms, ls)

    init = (tuple(jnp.full((1, 2 * Q_BLOCK), NEG_BIG, jnp.float32) for _ in range(n_pairs)),
            tuple(jnp.zeros((1, 2 * Q_BLOCK), jnp.float32) for _ in range(n_pairs)))
    _, ls = lax.fori_loop(0, (n_chunks + 1) // 2, attn_two_chunks, init)
    for j in range(n_pairs):
        lo, mid, hi = j * LANES, j * LANES + HEAD_DIM, (j + 1) * LANES
        even = acc_ref[lo:mid, :] / ls[j][:, :Q_BLOCK]
        odd = acc_ref[mid:hi, :] / ls[j][:, Q_BLOCK:]
        o_ref[:, lo:hi] = jnp.concatenate([even, odd], axis=0).T.astype(o_ref.dtype)


def _dsa(iq, iw, q, iklo, ikhi, k, vt):
    b, s, _ = q.shape
    n_top = min(TOPK_MAX, s // 4)
    idx_bits = max(1, int(np.ceil(np.log2(s))))
    qblk = lambda w: pl.BlockSpec((None, Q_BLOCK, w), lambda bi, qi: (bi, qi, 0))
    keys = lambda w: pl.BlockSpec((None, s, w), lambda bi, qi: (bi, 0, 0))
    return pl.pallas_call(
        functools.partial(_dsa_kernel, n_top=n_top, idx_bits=idx_bits),
        grid=(b, s // Q_BLOCK),
        in_specs=[qblk(IDXQ_W), qblk(LANES), qblk(ATTN_W), keys(LANES), keys(LANES),
                  keys(ATTN_W), pl.BlockSpec((None, VT_ALL, s), lambda bi, qi: (bi, 0, 0))],
        out_specs=qblk(ATTN_W),
        out_shape=jax.ShapeDtypeStruct((b, s, ATTN_W), jnp.bfloat16),
        scratch_shapes=[pltpu.VMEM((s, Q_BLOCK), jnp.int32),
                        pltpu.VMEM((N_HEADS_IDX // 4, LANES, 2 * Q_BLOCK), jnp.bfloat16),
                        pltpu.VMEM((N_HEADS_ATTN // 2, LANES, 2 * Q_BLOCK), jnp.bfloat16),
                        pltpu.VMEM((2, KEY_CHUNK, 2 * Q_BLOCK), jnp.float32),
                        pltpu.VMEM((2, N_HEADS_ATTN // 2, KEY_CHUNK, 2 * Q_BLOCK), jnp.float32),
                        pltpu.VMEM((ATTN_W, Q_BLOCK), jnp.float32)],
        compiler_params=pltpu.CompilerParams(dimension_semantics=("arbitrary", "arbitrary"),
                                             vmem_limit_bytes=VMEM_LIMIT),
        name="dsa",
    )(iq, iw, q, iklo, ikhi, k, vt)


def _merge_kernel(x_ref, ya_ref, u_ref, uprev_ref, qm_ref, km_ref, vm_ref,
                  gmix_ref, wg_ref, bg_ref, wpa_ref, wpp_ref, wpm_ref, wbd_ref, psc_ref,
                  wo_ref, gffn_ref, wrh_ref, wrhl_ref, br_ref,
                  h_ref, xn2_ref, eg_ref, ei_ref, cnt_ref, carry_ref, *, tiles_per_seq):
    i = pl.program_id(0)
    t = x_ref.shape[0]
    d = x_ref.shape[1]
    bf = jnp.bfloat16
    x = x_ref[...]
    xn = _rmsnorm(x, gmix_ref[...]).astype(bf)

    tile_in_seq = i % tiles_per_seq
    u = u_ref[...]
    halo = jnp.where(tile_in_seq == 0, 0.0, uprev_ref[...])
    ext = jnp.concatenate([halo, u], axis=0)
    lane_p = lax.broadcasted_iota(jnp.int32, (t, POOL_W), 1)
    pos1 = (tile_in_seq * t + lax.broadcasted_iota(jnp.int32, (t, 1), 0) + 1).astype(jnp.float32)
    pooled = None
    run, width = ext, 1
    for g, w in enumerate(POOL_WINDOWS):
        while width < w:
            run = run[width:] + run[:-width]
            width *= 2
        start = POOL_HALO + 1 - w
        mean = run[start:start + t] / jnp.minimum(pos1, float(w))
        pooled = mean if pooled is None else jnp.where(lane_p >= g * POOL_GROUP_DIM, mean, pooled)
    pooled = pooled - u
    mixed = _dot(pooled.astype(bf), wbd_ref[...]) * psc_ref[...]
    y_pool = _dot(mixed.astype(bf), wpp_ref[...])

    qm = qm_ref[...]
    probs = []
    for h in range(N_HEADS_MEM):
        s = _dot_nt(qm, km_ref[h])
        p = jnp.exp(s - jnp.max(s, axis=1, keepdims=True))
        probs.append((p / jnp.sum(p, axis=1, keepdims=True)).astype(bf))
    y_mem = _dot(probs[0], vm_ref[0])
    for h in range(1, N_HEADS_MEM):
        y_mem = y_mem + _dot(probs[h], vm_ref[h])
    y_mem = _dot(y_mem.astype(bf), wpm_ref[...])

    y_attn = _dot(ya_ref[...], wpa_ref[...])

    def gate(br):
        z = _dot(xn, wg_ref[:, br * d:(br + 1) * d]) + bg_ref[:, br * d:(br + 1) * d]
        return 1.0 / (1.0 + jnp.exp(-z))

    merged = gate(0) * y_attn + gate(1) * y_pool + gate(2) * y_mem
    h = x + _dot(merged.astype(bf), wo_ref[...])
    h_ref[...] = h
    xn2 = _rmsnorm(h, gffn_ref[...])
    _store_row_tiles(xn2_ref, xn2)

    x_hi = xn2.astype(bf)
    x_lo = (xn2 - x_hi.astype(jnp.float32)).astype(bf)
    by_hi = _dot(x_hi, wrhl_ref[...])
    logits = (by_hi[:, :LANES] + (_dot(x_lo, wrh_ref[...]) + by_hi[:, LANES:])) + br_ref[...]
    lane_e = lax.broadcasted_iota(jnp.int32, logits.shape, 1).astype(jnp.float32)
    work = logits
    vals, onehots, ids = [], [], []
    for _ in range(TOP_K):
        mx = jnp.max(work, axis=1, keepdims=True)
        idx = jnp.min(jnp.where(work == mx, lane_e, float(LANES)), axis=1, keepdims=True)
        oh = lane_e == idx
        vals.append(mx)
        ids.append(idx)
        onehots.append(oh)
        work = jnp.where(oh, -jnp.inf, work)
    exps = [jnp.exp(v - vals[0]) for v in vals]
    denom = exps[0] + exps[1] + exps[2] + exps[3]

    @pl.when(i == 0)
    def _():
        carry_ref[...] = jnp.zeros(carry_ref.shape, jnp.float32)

    member = jnp.where(onehots[0] | onehots[1] | onehots[2] | onehots[3], 1.0, 0.0)
    r_io = lax.broadcasted_iota(jnp.int32, (t, t), 0)
    c_io = lax.broadcasted_iota(jnp.int32, (t, t), 1)
    earlier = jnp.where(c_io < r_io, 1.0, 0.0).astype(bf)
    before = _dot(earlier, member.astype(bf)) + carry_ref[...]
    carry_ref[...] = carry_ref[...] + jnp.sum(member, axis=0, keepdims=True)
    cnt_ref[...] = carry_ref[...]

    lane8 = lax.broadcasted_iota(jnp.int32, (t, 2 * TOP_K), 1)
    eg = jnp.zeros((t, 2 * TOP_K), jnp.float32)
    ei = jnp.zeros((t, 2 * TOP_K), jnp.float32)
    for j in range(TOP_K):
        rank = jnp.sum(jnp.where(onehots[j], before, 0.0), axis=1, keepdims=True)
        eg = jnp.where(lane8 == j, exps[j] / denom, eg)
        ei = jnp.where(lane8 == j, ids[j], ei)
        ei = jnp.where(lane8 == TOP_K + j, rank, ei)
    eg_ref[...] = eg
    ei_ref[...] = ei.astype(jnp.int32)


def _merge(x2, ya, u, qm, km, vm, gmix, wg, bg, wpa, wpp, wpm, wbd, psc, wo, gffn, wrh, wrhl, br, seq):
    n, d = x2.shape
    t = MERGE_TILE
    tiles_per_seq = seq // t
    m = km.shape[2]
    row = lambda i: (i, 0)
    full = lambda i: (0, 0)
    per_batch = lambda i: (i // tiles_per_seq, 0, 0, 0)
    halo_blocks = t // POOL_HALO
    prev = lambda i: (jnp.maximum(i * halo_blocks - 1, 0), 0)
    consts = (gmix, wg, bg, wpa, wpp, wpm, wbd, psc, wo, gffn, wrh, wrhl, br)
    pieces = d // LANES
    out_shape = (jax.ShapeDtypeStruct((n, d), jnp.float32),
                 jax.ShapeDtypeStruct((n * pieces, LANES), jnp.float32),
                 jax.ShapeDtypeStruct((n, 2 * TOP_K), jnp.float32),
                 jax.ShapeDtypeStruct((n, 2 * TOP_K), jnp.int32),
                 jax.ShapeDtypeStruct((1, LANES), jnp.float32))
    return pl.pallas_call(
        functools.partial(_merge_kernel, tiles_per_seq=tiles_per_seq),
        grid=(n // t,),
        in_specs=[pl.BlockSpec((t, d), row), pl.BlockSpec((t, ATTN_W), row),
                  pl.BlockSpec((t, POOL_W), row), pl.BlockSpec((POOL_HALO, POOL_W), prev),
                  pl.BlockSpec((t, MEM_W), row),
                  pl.BlockSpec((None, N_HEADS_MEM, m, MEM_W), per_batch),
                  pl.BlockSpec((None, N_HEADS_MEM, m, MEM_W), per_batch)]
                 + [pl.BlockSpec(c.shape, full) for c in consts],
        out_specs=(pl.BlockSpec((t, d), row), pl.BlockSpec((t * pieces, LANES), row),
                   pl.BlockSpec((t, 2 * TOP_K), row), pl.BlockSpec((t, 2 * TOP_K), row),
                   pl.BlockSpec((1, LANES), full)),
        out_shape=out_shape,
        scratch_shapes=[pltpu.VMEM((1, LANES), jnp.float32)],
        compiler_params=pltpu.CompilerParams(dimension_semantics=("arbitrary",),
                                             vmem_limit_bytes=VMEM_LIMIT),
        name="merge",
    )(x2, ya, u, u, qm, km, vm, *consts)


def _row_copy(src_ref, src_row, dst_ref, dst_row, sem):
    return pltpu.make_async_copy(_row_tile(src_ref, src_row), _row_tile(dst_ref, dst_row), sem)


def _dispatch_kernel(pstart_ref, cnt_ref, dest_ref, x_ref, xs_ref, zero_ref, sem):
    t = x_ref.shape[0] // SUBLANES
    block_tiles = ROW_BLOCK * SUBLANES

    @pl.when(pl.program_id(0) == 0)
    def _():
        zero_ref[...] = jnp.zeros(zero_ref.shape, zero_ref.dtype)

        def for_each_pad_row(act):
            def per_expert(e, carry):
                n_pad = (ROW_BLOCK - cnt_ref[e] % ROW_BLOCK) % ROW_BLOCK
                first = pstart_ref[e] + cnt_ref[e]

                def per_row(r, c):
                    act(_row_copy(zero_ref, 0, xs_ref, first + r, sem))
                    return c
                return lax.fori_loop(0, n_pad, per_row, carry)
            lax.fori_loop(0, N_EXPERTS, per_expert, 0)

        def for_each_unused_block(act):
            n_blocks = xs_ref.shape[0] // block_tiles
            first = (pstart_ref[N_EXPERTS - 1] + cnt_ref[N_EXPERTS - 1] + ROW_BLOCK - 1) // ROW_BLOCK

            def per_block(b, c):
                rows = pl.ds(pl.multiple_of(b * block_tiles, block_tiles), block_tiles)
                act(pltpu.make_async_copy(zero_ref, xs_ref.at[rows], sem))
                return c
            lax.fori_loop(first, n_blocks, per_block, 0)

        for act in (lambda cp: cp.start(), lambda cp: cp.wait()):
            for_each_pad_row(act)
            for_each_unused_block(act)

    def issue(tok, carry):
        for j in range(TOP_K):
            _row_copy(x_ref, tok, xs_ref, dest_ref[tok * TOP_K + j], sem).start(priority=j % 2)
        return carry

    lax.fori_loop(0, t, issue, 0, unroll=ISSUE_UNROLL)
    for j in range(TOP_K):
        pltpu.make_async_copy(x_ref, xs_ref.at[pl.ds(0, x_ref.shape[0])], sem).wait()


def _dispatch(pstart, cnt, dest, xn2, n_rows):
    tiles, lanes = xn2.shape
    n = tiles // SUBLANES
    t = MOVE_TILE
    return pl.pallas_call(
        _dispatch_kernel,
        grid_spec=pltpu.PrefetchScalarGridSpec(
            num_scalar_prefetch=2,
            grid=(n // t,),
            in_specs=[pl.BlockSpec((t * TOP_K,), lambda i, ps, ct: (i,), memory_space=pltpu.SMEM),
                      pl.BlockSpec((t * SUBLANES, lanes), lambda i, ps, ct: (i, 0))],
            out_specs=pl.BlockSpec(memory_space=pl.ANY),
            scratch_shapes=[pltpu.VMEM((ROW_BLOCK * SUBLANES, lanes), jnp.float32),
                            pltpu.SemaphoreType.DMA(())]),
        out_shape=jax.ShapeDtypeStruct((n_rows * SUBLANES, lanes), jnp.float32),
        compiler_params=pltpu.CompilerParams(dimension_semantics=("arbitrary",),
                                             vmem_limit_bytes=VMEM_LIMIT),
        name="dispatch",
    )(pstart, cnt, dest, xn2)


def _experts_kernel(be_ref, nused_ref, half_ref, next_ref, xs_ref, wu_hbm, bu_ref, wd_hbm, bd_ref,
                    ys_ref, wu32_ref, wd32_ref, wu16_ref, wd16_ref, sems):
    i = pl.program_id(0)
    d, f = wu16_ref.shape[0], wd16_ref.shape[0]
    half = half_ref[i]

    def weight_copies(e, into):
        return (pltpu.make_async_copy(wu_hbm.at[e], wu32_ref.at[into], sems.at[0, into]),
                pltpu.make_async_copy(wd_hbm.at[e], wd32_ref.at[into], sems.at[1, into]))

    @pl.when(i == 0)
    def _():
        for cp in weight_copies(be_ref[0], 0):
            cp.start()

    @pl.when((i == 0) | (be_ref[i] != be_ref[jnp.maximum(i - 1, 0)]))
    def _():
        for cp in weight_copies(be_ref[i], half):
            cp.wait()
        wu16_ref[...] = wu32_ref[half].astype(wu16_ref.dtype)
        wd16_ref[...] = wd32_ref[half].astype(wd16_ref.dtype)

        @pl.when(next_ref[i] >= 0)
        def _():
            for cp in weight_copies(next_ref[i], 1 - half):
                cp.start()

    @pl.when(i < nused_ref[0])
    def _():
        x = _load_row_tiles(xs_ref, ROW_BLOCK, d).astype(jnp.bfloat16)
        acts = []
        for lo in range(0, f, UP_TILE):
            glu_cols, lin_cols = slice(lo, lo + UP_TILE), slice(f + lo, f + lo + UP_TILE)
            glu = jnp.minimum(_dot(x, wu16_ref[:, glu_cols]) + bu_ref[:, glu_cols], SWIGLU_LIMIT)
            lin = jnp.clip(_dot(x, wu16_ref[:, lin_cols]) + bu_ref[:, lin_cols],
                           -SWIGLU_LIMIT, SWIGLU_LIMIT)
            act = glu * (1.0 / (1.0 + jnp.exp(-SWIGLU_ALPHA * glu))) * (lin + 1.0)
            acts.append(act.astype(jnp.bfloat16))
        act = jnp.concatenate(acts, axis=1)
        _store_row_tiles(ys_ref, _dot(act, wd16_ref[...]) + bd_ref[...])

    @pl.when(i >= nused_ref[0])
    def _():
        ys_ref[...] = jnp.zeros(ys_ref.shape, ys_ref.dtype)


def _experts(block_e, nused, block_half, block_next, xs, wu, bu, wd, bd):
    tiles, lanes = xs.shape
    e, d, f2 = wu.shape
    f = wd.shape[1]
    block = (ROW_BLOCK * SUBLANES, lanes)
    rows = lambda i, be, nu, hf, nx: (i, 0)
    used_rows = lambda i, be, nu, hf, nx: (jnp.minimum(i, nu[0] - 1), 0)
    per_e = lambda i, be, nu, hf, nx: (be[i], 0, 0)
    return pl.pallas_call(
        _experts_kernel,
        grid_spec=pltpu.PrefetchScalarGridSpec(
            num_scalar_prefetch=4,
            grid=(tiles // block[0],),
            in_specs=[pl.BlockSpec(block, used_rows),
                      pl.BlockSpec(memory_space=pl.ANY), pl.BlockSpec((None, 1, f2), per_e),
                      pl.BlockSpec(memory_space=pl.ANY), pl.BlockSpec((None, 1, d), per_e)],
            out_specs=pl.BlockSpec(block, rows),
            scratch_shapes=[pltpu.VMEM((2, d, f2), jnp.float32), pltpu.VMEM((2, f, d), jnp.float32),
                            pltpu.VMEM((d, f2), jnp.bfloat16), pltpu.VMEM((f, d), jnp.bfloat16),
                            pltpu.SemaphoreType.DMA((2, 2))]),
        out_shape=jax.ShapeDtypeStruct(xs.shape, jnp.float32),
        compiler_params=pltpu.CompilerParams(dimension_semantics=("arbitrary",),
                                             vmem_limit_bytes=VMEM_LIMIT),
        name="experts",
    )(block_e, nused, block_half, block_next, xs, wu, bu, wd, bd)


def _combine_kernel(dest_ref, dest_next_ref, h_ref, eg_ref, g_ref, ys_ref, o_ref, buf0_ref, buf1_ref,
                    sems, *, final_norm):
    i = pl.program_id(0)
    t, d = h_ref.shape
    pieces = d // LANES
    half = i % 2
    bufs = (buf0_ref, buf1_ref)

    def start_rows(idx_ref, into, tok):
        for j in range(TOP_K):
            _row_copy(ys_ref, idx_ref[tok * TOP_K + j], bufs[into].at[j], tok,
                      sems.at[into]).start(priority=j % 2)

    def wait_rows(which):
        for j in range(TOP_K):
            pltpu.make_async_copy(ys_ref.at[pl.ds(0, t * SUBLANES)], bufs[which].at[j],
                                  sems.at[which]).wait()

    @pl.when(i == 0)
    def _():
        def issue(tok, carry):
            start_rows(dest_ref, 0, tok)
            return carry
        lax.fori_loop(0, t, issue, 0, unroll=ISSUE_UNROLL)

    gain = g_ref[...]

    def step(cur, nxt):
        wait_rows(cur)

        def eight_tokens(grp, carry):
            tok0 = pl.multiple_of(grp * SUBLANES, SUBLANES)
            for k in range(SUBLANES):
                start_rows(dest_next_ref, nxt, tok0 + k)
            eg = eg_ref[pl.ds(tok0, SUBLANES), :]
            h = h_ref[pl.ds(tok0, SUBLANES), :]
            cols = []
            for p in range(pieces):
                col = h[:, p * LANES:(p + 1) * LANES]
                for j in range(TOP_K):
                    rows = bufs[cur][j, pl.ds(tok0 * pieces + p, SUBLANES, stride=pieces), :]
                    col = col + eg[:, j:j + 1] * rows
                cols.append(col)
            o_ref[pl.ds(tok0, SUBLANES), :] = jnp.concatenate(cols, axis=1)
            return carry

        lax.fori_loop(0, t // SUBLANES, eight_tokens, 0)
        if final_norm:
            o_ref[...] = _rmsnorm(o_ref[...], gain)

        @pl.when(i == pl.num_programs(0) - 1)
        def _():
            wait_rows(nxt)

    pl.when(half == 0)(lambda: step(0, 1))
    pl.when(half == 1)(lambda: step(1, 0))


def _combine(dest, h, eg, g, ys, final_norm):
    n, d = h.shape
    t = MOVE_TILE
    return pl.pallas_call(
        functools.partial(_combine_kernel, final_norm=final_norm),
        grid=(n // t,),
        in_specs=[pl.BlockSpec((t * TOP_K,), lambda i: (i,), memory_space=pltpu.SMEM),
                  pl.BlockSpec((t * TOP_K,), lambda i: (jnp.minimum(i + 1, n // t - 1),),
                               memory_space=pltpu.SMEM),
                  pl.BlockSpec((t, d), lambda i: (i, 0)),
                  pl.BlockSpec((t, 2 * TOP_K), lambda i: (i, 0)),
                  pl.BlockSpec((1, d), lambda i: (0, 0)),
                  pl.BlockSpec(memory_space=pl.ANY)],
        out_specs=pl.BlockSpec((t, d), lambda i: (i, 0)),
        scratch_shapes=[pltpu.VMEM((TOP_K, t * SUBLANES, ys.shape[1]), jnp.float32),
                        pltpu.VMEM((TOP_K, t * SUBLANES, ys.shape[1]), jnp.float32),
                        pltpu.SemaphoreType.DMA((2,))],
        out_shape=jax.ShapeDtypeStruct((n, d), jnp.float32),
        compiler_params=pltpu.CompilerParams(dimension_semantics=("arbitrary",),
                                             vmem_limit_bytes=VMEM_LIMIT),
        name="combine",
    )(dest, dest, h, eg, g, ys)


def _rope_tables(seq):
    half = ROPE_DIM // 2
    inv = jnp.power(jnp.float32(ROPE_THETA), -jnp.arange(half, dtype=jnp.float32) / half)
    ang = jnp.arange(seq, dtype=jnp.float32)[:, None] * inv[None, :]
    cos, sin = jnp.cos(ang), jnp.sin(ang)
    pad = HEAD_DIM - ROPE_DIM
    one, zero = jnp.ones((seq, pad), jnp.float32), jnp.zeros((seq, pad), jnp.float32)
    zh = jnp.zeros((seq, half), jnp.float32)
    reps = LANES // HEAD_DIM
    c = jnp.tile(jnp.concatenate([cos, cos, one], axis=1), (1, reps))
    a = jnp.tile(jnp.concatenate([-sin, zh, zero], axis=1), (1, reps))
    b = jnp.tile(jnp.concatenate([zh, sin, zero], axis=1), (1, reps))
    return c, a, b


def _layer(h, mem, p, tables, final_g):
    b, s, d = h.shape
    n = b * s
    bf = jnp.bfloat16
    x2 = h.reshape(n, d)
    w_in = p["w_in"]
    o_ik = 3 * ATTN_W + IDXQ_W
    o_u = o_ik + IDX_DIM + N_HEADS_IDX
    wa = w_in[:, :o_ik].astype(bf)
    wb = jnp.pad(w_in[:, o_ik:o_u], ((0, 0), (0, LANES - (o_u - o_ik)))).astype(bf)
    wc = w_in[:, o_u:].astype(bf)
    q, k, vt, iq, iklo, ikhi, iw, u, qm = _in_proj(x2, p["norm_mix_g"][None], wa, wb, wc, *tables, b, s)
    km, vm = _mem_kv(mem, p["mem_norm_g"][None], p["w_mem_kv"].astype(bf))
    b3 = lambda a: a.reshape(b, s, a.shape[-1])
    ya = _dsa(b3(iq), b3(iw), b3(q), b3(iklo), b3(ikhi), b3(k), vt).reshape(n, ATTN_W)

    wbd = jnp.zeros((POOL_W, POOL_W), jnp.float32)
    for g in range(N_POOL_GROUPS):
        lo = g * POOL_GROUP_DIM
        wbd = wbd.at[lo:lo + POOL_GROUP_DIM, lo:lo + POOL_GROUP_DIM].set(p["pool_w_group"][g])
    wr = jnp.pad(p["w_router"], ((0, 0), (0, LANES - N_EXPERTS)))
    wr_hi = wr.astype(bf)
    br = jnp.pad(p["b_router"], (0, LANES - N_EXPERTS), constant_values=NEG_BIG)[None]
    hmid, xn2, eg, ei, counts = _merge(
        x2, ya, u, qm, km, vm, p["norm_mix_g"][None], p["w_gate"].astype(bf), p["b_gate"][None],
        p["w_proj_attn"].astype(bf), p["w_proj_pool"].astype(bf), p["w_proj_mem"].astype(bf),
        wbd.astype(bf), p["pool_scale"][None], p["w_out"].astype(bf), p["norm_ffn_g"][None],
        wr_hi, jnp.concatenate([wr_hi, (wr - wr_hi.astype(jnp.float32)).astype(bf)], axis=1), br, s)

    cnt = counts[0, :N_EXPERTS].astype(jnp.int32)
    padded = ((cnt + ROW_BLOCK - 1) // ROW_BLOCK) * ROW_BLOCK
    pend = jnp.cumsum(padded)
    pstart = (pend - padded).astype(jnp.int32)
    n_rows = n * TOP_K + N_EXPERTS * ROW_BLOCK
    n_blocks = n_rows // ROW_BLOCK
    nused = (pend[-1] // ROW_BLOCK).astype(jnp.int32)
    blk = jnp.minimum(jnp.arange(n_blocks, dtype=jnp.int32), nused - 1) * ROW_BLOCK
    block_e = jnp.sum(blk[:, None] >= pend[None, :], axis=1).astype(jnp.int32)
    block_e = jnp.minimum(block_e, N_EXPERTS - 1)
    has_rows = cnt > 0
    run_index = jnp.cumsum(has_rows) - 1
    ids = jnp.arange(N_EXPERTS, dtype=jnp.int32)
    later = jnp.where(has_rows[None, :] & (ids[None, :] > ids[:, None]), ids[None, :], N_EXPERTS)
    next_used = jnp.min(later, axis=1)
    next_used = jnp.where(next_used < N_EXPERTS, next_used, -1).astype(jnp.int32)
    of_block = block_e[:, None] == ids[None, :]
    block_half = jnp.sum(jnp.where(of_block, run_index % 2, 0), axis=1).astype(jnp.int32)
    block_next = jnp.sum(jnp.where(of_block, next_used, 0), axis=1).astype(jnp.int32)
    chosen =ei[:, :TOP_K, None] == jnp.arange(N_EXPERTS, dtype=jnp.int32)
    dest = (jnp.sum(jnp.where(chosen, pstart, 0), axis=-1) + ei[:, TOP_K:]).reshape(-1)

    xs = _dispatch(pstart, cnt, dest, xn2, n_rows)
    ys = _experts(block_e, nused[None], block_half, block_next, xs, p["w_up"], p["b_up"][:, None, :],
                  p["w_down"], p["b_down"][:, None, :])
    g = p["norm_ffn_g"][None] if final_g is None else final_g[None]
    return _combine(dest, hmid, eg, g, ys, final_g is not None).reshape(b, s, d)


def kernel(x, mem, norm_mix_g, w_in, w_gate, b_gate, w_proj_attn, w_proj_pool, w_proj_mem,
           pool_w_group, pool_scale, mem_norm_g, w_mem_kv, w_out, norm_ffn_g,
           w_router, b_router, w_up, b_up, w_down, b_down, norm_final_g):
    b, s, d = x.shape
    assert s % IN_TILE == 0 and s % MERGE_TILE == 0 and s % COUNT_CHUNK == 0
    assert (b * s) % MOVE_TILE == 0
    assert d == LANES * SUBLANES, "the row-tile layout maps one model row onto one (8, 128) tile"
    stacked = dict(norm_mix_g=norm_mix_g, w_in=w_in, w_gate=w_gate, b_gate=b_gate,
                   w_proj_attn=w_proj_attn, w_proj_pool=w_proj_pool, w_proj_mem=w_proj_mem,
                   pool_w_group=pool_w_group, pool_scale=pool_scale, mem_norm_g=mem_norm_g,
                   w_mem_kv=w_mem_kv, w_out=w_out, norm_ffn_g=norm_ffn_g, w_router=w_router,
                   b_router=b_router, w_up=w_up, b_up=b_up, w_down=w_down, b_down=b_down)
    depth = w_in.shape[0]
    tables = _rope_tables(s)
    h = x
    for l in range(depth):
        p = {name: val[l] for name, val in stacked.items()}
        h = _layer(h, mem, p, tables, norm_final_g if l == depth - 1 else None)
    return h
```

```python
import functools

import jax
import jax.numpy as jnp
import numpy as np
from jax import lax
from jax.experimental import pallas as pl
from jax.experimental.pallas import tpu as pltpu

N_HEADS_ATTN = 8
HEAD_DIM = 64
ROPE_DIM = HEAD_DIM // 4
ROPE_THETA = 500000.0
N_HEADS_IDX = 8
IDX_DIM = 64
TOPK_MAX = 256
N_POOL_GROUPS = 4
POOL_GROUP_DIM = 64
POOL_WINDOWS = (2, 4, 8, 16)
POOL_HALO = 16
N_HEADS_MEM = 4
N_BRANCHES = 3
N_EXPERTS = 32
TOP_K = 4
SWIGLU_ALPHA = 1.702
SWIGLU_LIMIT = 7.0
EPS = 1e-6

ATTN_W = N_HEADS_ATTN * HEAD_DIM
POOL_W = N_POOL_GROUPS * POOL_GROUP_DIM
MEM_W = N_HEADS_MEM * HEAD_DIM
IDXQ_W = N_HEADS_IDX * IDX_DIM

LANES = 128
SUBLANES = 8
VMEM_LIMIT = 56 * 1024 * 1024

IN_TILE = 512
Q_BLOCK = 128
KEY_CHUNK = 256
COUNT_CHUNK = 512
COUNT_ROWS = 32
EARLY_EXIT_PASSES = (25, 28)
MERGE_TILE = 512
ROW_BLOCK = 256
UP_TILE = 256
MOVE_TILE = 512
ISSUE_UNROLL = 4

VT_ROWS = LANES + 16
VT_ALL = (N_HEADS_ATTN // 2) * VT_ROWS
LOG2_E = 1.4426950408889634

NEG_BIG = -1e30
INT_MIN = -2 ** 31
NEG_INF_KEY = -0x7F800001


def _dot(a, b):
    return jnp.dot(a, b, preferred_element_type=jnp.float32)


def _dot_nt(a, b):
    return lax.dot_general(a, b, (((1,), (1,)), ((), ())), preferred_element_type=jnp.float32)


def _rmsnorm(x, g):
    return x * lax.rsqrt(jnp.mean(x * x, axis=-1, keepdims=True) + EPS) * g


def _store_row_tiles(ref, x):
    rows, width = x.shape
    for j in range(width // LANES):
        ref[pl.ds(j, rows, stride=width // LANES), :] = x[:, j * LANES:(j + 1) * LANES]


def _load_row_tiles(ref, rows, width):
    pieces = width // LANES
    return jnp.concatenate([ref[pl.ds(j, rows, stride=pieces), :] for j in range(pieces)], axis=1)


def _row_tile(ref, row):
    return ref.at[pl.ds(pl.multiple_of(row * SUBLANES, SUBLANES), SUBLANES)]


def _rope128(x, c, a, b):
    return x * c + pltpu.roll(x, LANES - ROPE_DIM // 2, 1) * a + pltpu.roll(x, ROPE_DIM // 2, 1) * b


def _in_proj_kernel(x_ref, g_ref, wa_ref, wb_ref, wc_ref, rc_ref, ra_ref, rb_ref,
                    q_ref, k_ref, vt_ref, iq_ref, iklo_ref, ikhi_ref, iw_ref, u_ref, qm_ref):
    xn = _rmsnorm(x_ref[...], g_ref[...]).astype(jnp.bfloat16)
    rc, ra, rb = rc_ref[...], ra_ref[...], rb_ref[...]
    pa = _dot(xn, wa_ref[...])
    for seg, (ref, scale) in enumerate(((q_ref, LOG2_E * HEAD_DIM ** -0.5), (k_ref, None),
                                        (None, None), (iq_ref, IDX_DIM ** -0.5))):
        for c in range(ATTN_W // LANES):
            lo = seg * ATTN_W + c * LANES
            blk = pa[:, lo:lo + LANES]
            if ref is None:
                vt_ref[c * VT_ROWS:c * VT_ROWS + LANES, :] = blk.T.astype(vt_ref.dtype)
                vt_ref[c * VT_ROWS + LANES:(c + 1) * VT_ROWS, :] = jnp.ones(
                    (VT_ROWS - LANES, blk.shape[0]), vt_ref.dtype)
                continue
            blk = _rope128(blk, rc, ra, rb)
            if scale is not None:
                blk = blk * scale
            ref[:, c * LANES:(c + 1) * LANES] = blk.astype(ref.dtype)
    pb = _dot(xn, wb_ref[...])
    lane = lax.broadcasted_iota(jnp.int32, pb.shape, 1)
    ik = jnp.where(lane < IDX_DIM, _rope128(pb, rc, ra, rb), 0.0)
    iklo_ref[...] = ik.astype(iklo_ref.dtype)
    ikhi_ref[...] = pltpu.roll(ik, IDX_DIM, 1).astype(ikhi_ref.dtype)
    iw_ref[...] = pb * (N_HEADS_IDX ** -0.5)
    pc = _dot(xn, wc_ref[...])
    u_ref[...] = pc[:, :POOL_W]
    qm_ref[...] = (pc[:, POOL_W:] * (HEAD_DIM ** -0.5)).astype(qm_ref.dtype)


def _in_proj(x2, g, wa, wb, wc, rc, ra, rb, batch, seq):
    n, d = x2.shape
    t = IN_TILE
    tiles_per_seq = seq // t
    row = lambda i: (i, 0)
    full = lambda i: (0, 0)
    pos = lambda i: (i % tiles_per_seq, 0)
    bf = jnp.bfloat16
    flat = lambda w, dt: (jax.ShapeDtypeStruct((n, w), dt), pl.BlockSpec((t, w), row))
    outs = (flat(ATTN_W, bf), flat(ATTN_W, bf),
            (jax.ShapeDtypeStruct((batch, VT_ALL, seq), bf),
             pl.BlockSpec((None, VT_ALL, t), lambda i: (i // tiles_per_seq, 0, i % tiles_per_seq))),
            flat(IDXQ_W, bf), flat(LANES, bf), flat(LANES, bf), flat(LANES, jnp.float32),
            flat(POOL_W, jnp.float32), flat(MEM_W, bf))
    return pl.pallas_call(
        _in_proj_kernel,
        grid=(n // t,),
        in_specs=[pl.BlockSpec((t, d), row), pl.BlockSpec((1, d), full),
                  pl.BlockSpec(wa.shape, full), pl.BlockSpec(wb.shape, full),
                  pl.BlockSpec(wc.shape, full),
                  pl.BlockSpec((t, LANES), pos), pl.BlockSpec((t, LANES), pos),
                  pl.BlockSpec((t, LANES), pos)],
        out_specs=tuple(o[1] for o in outs),
        out_shape=tuple(o[0] for o in outs),
        compiler_params=pltpu.CompilerParams(dimension_semantics=("arbitrary",),
                                             vmem_limit_bytes=VMEM_LIMIT),
        name="in_proj",
    )(x2, g, wa, wb, wc, rc, ra, rb)


def _mem_kv_kernel(mem_ref, g_ref, w_ref, km_ref, vm_ref):
    mn = _rmsnorm(mem_ref[...], g_ref[...]).astype(jnp.bfloat16)
    kv = _dot(mn, w_ref[...])
    km, vm = kv[:, :MEM_W], kv[:, MEM_W:]
    lane = lax.broadcasted_iota(jnp.int32, km.shape, 1)
    for h in range(N_HEADS_MEM):
        in_head = (lane >= h * HEAD_DIM) & (lane < (h + 1) * HEAD_DIM)
        km_ref[h] = jnp.where(in_head, km, 0.0).astype(km_ref.dtype)
        vm_ref[h] = jnp.where(in_head, vm, 0.0).astype(vm_ref.dtype)


def _mem_kv(mem, g, w):
    b, m, d = mem.shape
    out = jax.ShapeDtypeStruct((b, N_HEADS_MEM, m, MEM_W), jnp.bfloat16)
    return pl.pallas_call(
        _mem_kv_kernel,
        grid=(b,),
        in_specs=[pl.BlockSpec((None, m, d), lambda i: (i, 0, 0)),
                  pl.BlockSpec((1, d), lambda i: (0, 0)),
                  pl.BlockSpec(w.shape, lambda i: (0, 0))],
        out_specs=(pl.BlockSpec((None, N_HEADS_MEM, m, MEM_W), lambda i: (i, 0, 0, 0)),
                   pl.BlockSpec((None, N_HEADS_MEM, m, MEM_W), lambda i: (i, 0, 0, 0))),
        out_shape=(out, out),
        compiler_params=pltpu.CompilerParams(dimension_semantics=("arbitrary",),
                                             vmem_limit_bytes=VMEM_LIMIT),
        name="mem_kv",
    )(mem, g, w)


def _dsa_kernel(iq_ref, iw_ref, q_ref, iklo_ref, ikhi_ref, k_ref, vt_ref, o_ref,
                key_ref, iqt_ref, qt_ref, bias_ref, s_ref, acc_ref, *, n_top, idx_bits):
    qb = pl.program_id(1)
    n_chunks = (qb * Q_BLOCK + Q_BLOCK + KEY_CHUNK - 1) // KEY_CHUNK
    n_count_chunks = (qb * Q_BLOCK + Q_BLOCK + COUNT_CHUNK - 1) // COUNT_CHUNK
    q_pos = qb * Q_BLOCK + lax.broadcasted_iota(jnp.int32, (1, Q_BLOCK), 1)
    row_c = lax.broadcasted_iota(jnp.int32, (COUNT_CHUNK, 1), 0)
    row_d = lax.broadcasted_iota(jnp.int32, (LANES, Q_BLOCK), 0)
    bf = jnp.bfloat16

    def chunk_off(c):
        return pl.multiple_of(c * KEY_CHUNK, KEY_CHUNK)

    wt = iw_ref[...].T
    for j in range(N_HEADS_IDX // 2):
        iqt_ref[j // 2, :, (j % 2) * Q_BLOCK:(j % 2 + 1) * Q_BLOCK] = (
            iq_ref[:, j * LANES:(j + 1) * LANES].astype(jnp.float32).T.astype(bf))
    for j in range(N_HEADS_ATTN // 2):
        pair_t = q_ref[:, j * LANES:(j + 1) * LANES].astype(jnp.float32).T
        qt_ref[j, :, :Q_BLOCK] = jnp.where(row_d < HEAD_DIM, pair_t, 0.0).astype(bf)
        qt_ref[j, :, Q_BLOCK:] = jnp.where(row_d >= HEAD_DIM, pair_t, 0.0).astype(bf)

    def score_chunk(c, carry):
        off = pl.multiple_of(c * COUNT_CHUNK, COUNT_CHUNK)
        acc = jnp.zeros((COUNT_CHUNK, Q_BLOCK), jnp.float32)
        for parity, ik_ref in enumerate((iklo_ref, ikhi_ref)):
            ik = ik_ref[pl.ds(off, COUNT_CHUNK), :]
            for g in range(N_HEADS_IDX // 4):
                dots = _dot(ik, iqt_ref[g])
                for side in range(2):
                    head = 2 * (2 * g + side) + parity
                    w = wt[IDX_DIM + head:IDX_DIM + head + 1, :]
                    acc = acc + w * jnp.maximum(dots[:, side * Q_BLOCK:(side + 1) * Q_BLOCK], 0.0)
        acc = jnp.where(off + row_c <= q_pos, acc, -jnp.inf)
        bits = lax.bitcast_convert_type(acc, jnp.int32)
        key_ref[pl.ds(off, COUNT_CHUNK), :] = bits ^ ((bits >> 31) & 0x7FFFFFFF)
        return carry

    lax.fori_loop(0, n_count_chunks, score_chunk, 0)

    def count(pred):
        def body(c, cnt):
            off = pl.multiple_of(c * COUNT_CHUNK, COUNT_CHUNK)
            hit = jnp.where(pred(key_ref[pl.ds(off, COUNT_CHUNK), :], off + row_c), 1.0, 0.0)
            return cnt + jnp.sum(hit.reshape(COUNT_CHUNK // COUNT_ROWS, COUNT_ROWS, Q_BLOCK), axis=0)
        cnt = lax.fori_loop(0, n_count_chunks, body, jnp.zeros((COUNT_ROWS, Q_BLOCK), jnp.float32))
        return jnp.sum(cnt, axis=0, keepdims=True)

    k_f = jnp.float32(n_top)
    zero = jnp.zeros((1, Q_BLOCK), jnp.int32)
    n_nonneg = count(lambda kk, pos: kk >= zero)
    n_stored = (n_count_chunks * COUNT_CHUNK).astype(jnp.float32)
    thr = jnp.where(n_nonneg >= k_f, zero, INT_MIN)
    n_ge = jnp.where(n_nonneg >= k_f, n_nonneg, n_stored)

    def thr_bit(i, carry):
        cur, n_cur = carry
        cand = cur + jnp.left_shift(jnp.int32(1), 30 - i)
        n_cand = count(lambda kk, pos: kk >= cand)
        keep = n_cand >= k_f
        return jnp.where(keep, cand, cur), jnp.where(keep, n_cand, n_cur)

    def all_exact(carry):
        return jnp.max(jnp.abs(carry[1] - k_f)) == 0.0

    def search_bits(first, last):
        return lambda carry: lax.fori_loop(first, last, thr_bit, carry)

    carry = search_bits(0, EARLY_EXIT_PASSES[0])((thr, n_ge))
    for first, last in zip(EARLY_EXIT_PASSES, EARLY_EXIT_PASSES[1:] + (31,)):
        carry = lax.cond(all_exact(carry), lambda c: c, search_bits(first, last), carry)
    thr, n_ge = carry

    @pl.when(jnp.max(n_ge) > k_f)
    def _():
        need = k_f - count(lambda kk, pos: kk > thr)

        def tie_bit(i, cur):
            cand = cur + jnp.left_shift(jnp.int32(1), idx_bits - 1 - i)
            below = count(lambda kk, pos: (kk == thr) & (pos < cand))
            return jnp.where(below < need, cand, cur)
        tie_pos = lax.fori_loop(0, idx_bits, tie_bit, zero)

        def drop_late_ties(c, carry):
            off = pl.multiple_of(c * COUNT_CHUNK, COUNT_CHUNK)
            kk = key_ref[pl.ds(off, COUNT_CHUNK), :]
            late = (kk == thr) & (off + row_c > tie_pos)
            key_ref[pl.ds(off, COUNT_CHUNK), :] = jnp.where(late, INT_MIN, kk)
            return carry
        lax.fori_loop(0, n_count_chunks, drop_late_ties, 0)

    thr = jnp.maximum(thr, NEG_INF_KEY + 1)

    acc_ref[...] = jnp.zeros(acc_ref.shape, jnp.float32)

    n_pairs = N_HEADS_ATTN // 2

    def masked_logits(c, slot):
        off = chunk_off(jnp.minimum(c, n_chunks - 1))
        kk = key_ref[pl.ds(off, KEY_CHUNK), :]
        bias = jnp.where(kk >= thr, jnp.where(c < n_chunks, 0.0, NEG_BIG), NEG_BIG)
        bias_ref[slot] = jnp.concatenate([bias, bias], axis=1)
        for j in range(n_pairs):
            kp = k_ref[pl.ds(off, KEY_CHUNK), j * LANES:(j + 1) * LANES]
            s_ref[slot, j] = _dot(kp, qt_ref[j]) + bias_ref[slot]

    def softmax_pv(c, slot, ms, ls):
        off = chunk_off(jnp.minimum(c, n_chunks - 1))
        new_ms, new_ls = [], []
        for j in range(n_pairs):
            s = s_ref[slot, j]
            m_new = jnp.maximum(ms[j], jnp.max(s, axis=0, keepdims=True))
            alpha = jnp.exp2(ms[j] - m_new)
            p = jnp.exp2(s - m_new).astype(bf)
            vt = vt_ref[j * VT_ROWS:(j + 1) * VT_ROWS, pl.ds(off, KEY_CHUNK)]
            pv = _dot(vt, p)
            new_ls.append(alpha * ls[j] + pv[LANES:LANES + 1, :])
            new_ms.append(m_new)
            lo, mid, hi = j * LANES, j * LANES + HEAD_DIM, (j + 1) * LANES
            acc_ref[lo:mid, :] = alpha[:, :Q_BLOCK] * acc_ref[lo:mid, :] + pv[:HEAD_DIM, :Q_BLOCK]
            acc_ref[mid:hi, :] = alpha[:, Q_BLOCK:] * acc_ref[mid:hi, :] + pv[HEAD_DIM:LANES, Q_BLOCK:]
        return tuple(new_ms), tuple(new_ls)

    masked_logits(0, 0)

    def attn_two_chunks(i, carry):
        ms, ls = carry
        masked_logits(2 * i + 1, 1)
        ms, ls = softmax_pv(2 * i, 0, ms, ls)
        masked_logits(2 * i + 2, 0)
        return softmax_pv(2 * i + 1, 1, ms, ls)

    init = (tuple(jnp.full((1, 2 * Q_BLOCK), NEG_BIG, jnp.float32) for _ in range(n_pairs)),
            tuple(jnp.zeros((1, 2 * Q_BLOCK), jnp.float32) for _ in range(n_pairs)))
    _, ls = lax.fori_loop(0, (n_chunks + 1) // 2, attn_two_chunks, init)
    for j in range(n_pairs):
        lo, mid, hi = j * LANES, j * LANES + HEAD_DIM, (j + 1) * LANES
        even = acc_ref[lo:mid, :] / ls[j][:, :Q_BLOCK]
        odd = acc_ref[mid:hi, :] / ls[j][:, Q_BLOCK:]
        o_ref[:, lo:hi] = jnp.concatenate([even, odd], axis=0).T.astype(o_ref.dtype)


def _dsa(iq, iw, q, iklo, ikhi, k, vt):
    b, s, _ = q.shape
    n_top = min(TOPK_MAX, s // 4)
    idx_bits = max(1, int(np.ceil(np.log2(s))))
    qblk = lambda w: pl.BlockSpec((None, Q_BLOCK, w), lambda bi, qi: (bi, qi, 0))
    keys = lambda w: pl.BlockSpec((None, s, w), lambda bi, qi: (bi, 0, 0))
    return pl.pallas_call(
        functools.partial(_dsa_kernel, n_top=n_top, idx_bits=idx_bits),
        grid=(b, s // Q_BLOCK),
        in_specs=[qblk(IDXQ_W), qblk(LANES), qblk(ATTN_W), keys(LANES), keys(LANES),
                  keys(ATTN_W), pl.BlockSpec((None, VT_ALL, s), lambda bi, qi: (bi, 0, 0))],
        out_specs=qblk(ATTN_W),
        out_shape=jax.ShapeDtypeStruct((b, s, ATTN_W), jnp.bfloat16),
        scratch_shapes=[pltpu.VMEM((s, Q_BLOCK), jnp.int32),
                        pltpu.VMEM((N_HEADS_IDX // 4, LANES, 2 * Q_BLOCK), jnp.bfloat16),
                        pltpu.VMEM((N_HEADS_ATTN // 2, LANES, 2 * Q_BLOCK), jnp.bfloat16),
                        pltpu.VMEM((2, KEY_CHUNK, 2 * Q_BLOCK), jnp.float32),
                        pltpu.VMEM((2, N_HEADS_ATTN // 2, KEY_CHUNK, 2 * Q_BLOCK), jnp.float32),
                        pltpu.VMEM((ATTN_W, Q_BLOCK), jnp.float32)],
        compiler_params=pltpu.CompilerParams(dimension_semantics=("arbitrary", "arbitrary"),
                                             vmem_limit_bytes=VMEM_LIMIT),
        name="dsa",
    )(iq, iw, q, iklo, ikhi, k, vt)


def _merge_kernel(x_ref, ya_ref, u_ref, uprev_ref, qm_ref, km_ref, vm_ref,
                  gmix_ref, wg_ref, bg_ref, wpa_ref, wpp_ref, wpm_ref, wbd_ref, psc_ref,
                  wo_ref, gffn_ref, wrh_ref, wrhl_ref, br_ref,
                  h_ref, xn2_ref, eg_ref, ei_ref, cnt_ref, carry_ref, *, tiles_per_seq):
    i = pl.program_id(0)
    t = x_ref.shape[0]
    d = x_ref.shape[1]
    bf = jnp.bfloat16
    x = x_ref[...]
    xn = _rmsnorm(x, gmix_ref[...]).astype(bf)

    tile_in_seq = i % tiles_per_seq
    u = u_ref[...]
    halo = jnp.where(tile_in_seq == 0, 0.0, uprev_ref[...])
    ext = jnp.concatenate([halo, u], axis=0)
    lane_p = lax.broadcasted_iota(jnp.int32, (t, POOL_W), 1)
    pos1 = (tile_in_seq * t + lax.broadcasted_iota(jnp.int32, (t, 1), 0) + 1).astype(jnp.float32)
    pooled = None
    run, width = ext, 1
    for g, w in enumerate(POOL_WINDOWS):
        while width < w:
            run = run[width:] + run[:-width]
            width *= 2
        start = POOL_HALO + 1 - w
        mean = run[start:start + t] / jnp.minimum(pos1, float(w))
        pooled = mean if pooled is None else jnp.where(lane_p >= g * POOL_GROUP_DIM, mean, pooled)
    pooled = pooled - u
    mixed = _dot(pooled.astype(bf), wbd_ref[...]) * psc_ref[...]
    y_pool = _dot(mixed.astype(bf), wpp_ref[...])

    qm = qm_ref[...]
    probs = []
    for h in range(N_HEADS_MEM):
        s = _dot_nt(qm, km_ref[h])
        p = jnp.exp(s - jnp.max(s, axis=1, keepdims=True))
        probs.append((p / jnp.sum(p, axis=1, keepdims=True)).astype(bf))
    y_mem = _dot(probs[0], vm_ref[0])
    for h in range(1, N_HEADS_MEM):
        y_mem = y_mem + _dot(probs[h], vm_ref[h])
    y_mem = _dot(y_mem.astype(bf), wpm_ref[...])

    y_attn = _dot(ya_ref[...], wpa_ref[...])

    def gate(br):
        z = _dot(xn, wg_ref[:, br * d:(br + 1) * d]) + bg_ref[:, br * d:(br + 1) * d]
        return 1.0 / (1.0 + jnp.exp(-z))

    merged = gate(0) * y_attn + gate(1) * y_pool + gate(2) * y_mem
    h = x + _dot(merged.astype(bf), wo_ref[...])
    h_ref[...] = h
    xn2 = _rmsnorm(h, gffn_ref[...])
    _store_row_tiles(xn2_ref, xn2)

    x_hi = xn2.astype(bf)
    x_lo = (xn2 - x_hi.astype(jnp.float32)).astype(bf)
    by_hi = _dot(x_hi, wrhl_ref[...])
    logits = (by_hi[:, :LANES] + (_dot(x_lo, wrh_ref[...]) + by_hi[:, LANES:])) + br_ref[...]
    work = logits.T[:N_EXPERTS, :]
    row_e = lax.broadcasted_iota(jnp.int32, work.shape, 0).astype(jnp.float32)
    vals, onehots, ids = [], [], []
    for _ in range(TOP_K):
        mx = jnp.max(work, axis=0, keepdims=True)
        idx = jnp.min(jnp.where(work == mx, row_e, float(N_EXPERTS)), axis=0, keepdims=True)
        oh = row_e == idx
        vals.append(mx)
        ids.append(idx)
        onehots.append(oh)
        work = jnp.where(oh, -jnp.inf, work)
    exps = [jnp.exp(v - vals[0]) for v in vals]
    denom = exps[0] + exps[1] + exps[2] + exps[3]

    @pl.when(i == 0)
    def _():
        carry_ref[...] = jnp.zeros(carry_ref.shape, jnp.float32)

    member = jnp.where(onehots[0] | onehots[1] | onehots[2] | onehots[3], 1.0, 0.0)
    r_io = lax.broadcasted_iota(jnp.int32, (t, t), 0)
    c_io = lax.broadcasted_iota(jnp.int32, (t, t), 1)
    earlier = jnp.where(r_io < c_io, 1.0, 0.0).astype(bf)
    before = _dot(member.astype(bf), earlier) + carry_ref[...]
    carry_ref[...] = carry_ref[...] + jnp.sum(member, axis=1, keepdims=True)
    cnt_ref[...] = carry_ref[...]

    row8 = lax.broadcasted_iota(jnp.int32, (LANES, t), 0)
    eg = jnp.zeros((LANES, t), jnp.float32)
    ei = jnp.zeros((LANES, t), jnp.float32)
    for j in range(TOP_K):
        rank = jnp.sum(jnp.where(onehots[j], before, 0.0), axis=0, keepdims=True)
        eg = jnp.where(row8 == j, exps[j] / denom, eg)
        ei = jnp.where(row8 == j, ids[j], ei)
        ei = jnp.where(row8 == TOP_K + j, rank, ei)
    eg_ref[...] = eg.T[:, :2 * TOP_K]
    ei_ref[...] = ei.T[:, :2 * TOP_K].astype(jnp.int32)


def _merge(x2, ya, u, qm, km, vm, gmix, wg, bg, wpa, wpp, wpm, wbd, psc, wo, gffn, wrh, wrhl, br, seq):
    n, d = x2.shape
    t = MERGE_TILE
    tiles_per_seq = seq // t
    m = km.shape[2]
    row = lambda i: (i, 0)
    full = lambda i: (0, 0)
    per_batch = lambda i: (i // tiles_per_seq, 0, 0, 0)
    halo_blocks = t // POOL_HALO
    prev = lambda i: (jnp.maximum(i * halo_blocks - 1, 0), 0)
    consts = (gmix, wg, bg, wpa, wpp, wpm, wbd, psc, wo, gffn, wrh, wrhl, br)
    pieces = d // LANES
    out_shape = (jax.ShapeDtypeStruct((n, d), jnp.float32),
                 jax.ShapeDtypeStruct((n * pieces, LANES), jnp.float32),
                 jax.ShapeDtypeStruct((n, 2 * TOP_K), jnp.float32),
                 jax.ShapeDtypeStruct((n, 2 * TOP_K), jnp.int32),
                 jax.ShapeDtypeStruct((N_EXPERTS, 1), jnp.float32))
    return pl.pallas_call(
        functools.partial(_merge_kernel, tiles_per_seq=tiles_per_seq),
        grid=(n // t,),
        in_specs=[pl.BlockSpec((t, d), row), pl.BlockSpec((t, ATTN_W), row),
                  pl.BlockSpec((t, POOL_W), row), pl.BlockSpec((POOL_HALO, POOL_W), prev),
                  pl.BlockSpec((t, MEM_W), row),
                  pl.BlockSpec((None, N_HEADS_MEM, m, MEM_W), per_batch),
                  pl.BlockSpec((None, N_HEADS_MEM, m, MEM_W), per_batch)]
                 + [pl.BlockSpec(c.shape, full) for c in consts],
        out_specs=(pl.BlockSpec((t, d), row), pl.BlockSpec((t * pieces, LANES), row),
                   pl.BlockSpec((t, 2 * TOP_K), row), pl.BlockSpec((t, 2 * TOP_K), row),
                   pl.BlockSpec((N_EXPERTS, 1), full)),
        out_shape=out_shape,
        scratch_shapes=[pltpu.VMEM((N_EXPERTS, 1), jnp.float32)],
        compiler_params=pltpu.CompilerParams(dimension_semantics=("arbitrary",),
                                             vmem_limit_bytes=VMEM_LIMIT),
        name="merge",
    )(x2, ya, u, u, qm, km, vm, *consts)


def _row_copy(src_ref, src_row, dst_ref, dst_row, sem):
    return pltpu.make_async_copy(_row_tile(src_ref, src_row), _row_tile(dst_ref, dst_row), sem)


def _dispatch_kernel(pstart_ref, cnt_ref, dest_ref, x_ref, xs_ref, zero_ref, sem):
    t = x_ref.shape[0] // SUBLANES
    block_tiles = ROW_BLOCK * SUBLANES

    @pl.when(pl.program_id(0) == 0)
    def _():
        zero_ref[...] = jnp.zeros(zero_ref.shape, zero_ref.dtype)

        def for_each_pad_row(act):
            def per_expert(e, carry):
                n_pad = (ROW_BLOCK - cnt_ref[e] % ROW_BLOCK) % ROW_BLOCK
                first = pstart_ref[e] + cnt_ref[e]

                def per_row(r, c):
                    act(_row_copy(zero_ref, 0, xs_ref, first + r, sem))
                    return c
                return lax.fori_loop(0, n_pad, per_row, carry)
            lax.fori_loop(0, N_EXPERTS, per_expert, 0)

        def for_each_unused_block(act):
            n_blocks = xs_ref.shape[0] // block_tiles
            first = (pstart_ref[N_EXPERTS - 1] + cnt_ref[N_EXPERTS - 1] + ROW_BLOCK - 1) // ROW_BLOCK

            def per_block(b, c):
                rows = pl.ds(pl.multiple_of(b * block_tiles, block_tiles), block_tiles)
                act(pltpu.make_async_copy(zero_ref, xs_ref.at[rows], sem))
                return c
            lax.fori_loop(first, n_blocks, per_block, 0)

        for act in (lambda cp: cp.start(), lambda cp: cp.wait()):
            for_each_pad_row(act)
            for_each_unused_block(act)

    def issue(tok, carry):
        for j in range(TOP_K):
            _row_copy(x_ref, tok, xs_ref, dest_ref[tok * TOP_K + j], sem).start(priority=j % 2)
        return carry

    lax.fori_loop(0, t, issue, 0, unroll=ISSUE_UNROLL)
    for j in range(TOP_K):
        pltpu.make_async_copy(x_ref, xs_ref.at[pl.ds(0, x_ref.shape[0])], sem).wait()


def _dispatch(pstart, cnt, dest, xn2, n_rows):
    tiles, lanes = xn2.shape
    n = tiles // SUBLANES
    t = MOVE_TILE
    return pl.pallas_call(
        _dispatch_kernel,
        grid_spec=pltpu.PrefetchScalarGridSpec(
            num_scalar_prefetch=2,
            grid=(n // t,),
            in_specs=[pl.BlockSpec((t * TOP_K,), lambda i, ps, ct: (i,), memory_space=pltpu.SMEM),
                      pl.BlockSpec((t * SUBLANES, lanes), lambda i, ps, ct: (i, 0))],
            out_specs=pl.BlockSpec(memory_space=pl.ANY),
            scratch_shapes=[pltpu.VMEM((ROW_BLOCK * SUBLANES, lanes), jnp.float32),
                            pltpu.SemaphoreType.DMA(())]),
        out_shape=jax.ShapeDtypeStruct((n_rows * SUBLANES, lanes), jnp.float32),
        compiler_params=pltpu.CompilerParams(dimension_semantics=("arbitrary",),
                                             vmem_limit_bytes=VMEM_LIMIT),
        name="dispatch",
    )(pstart, cnt, dest, xn2)


def _experts_kernel(be_ref, nused_ref, half_ref, next_ref, xs_ref, wu_hbm, bu_ref, wd_hbm, bd_ref,
                    ys_ref, wu32_ref, wd32_ref, wu16_ref, wd16_ref, sems):
    i = pl.program_id(0)
    d, f = wu16_ref.shape[0], wd16_ref.shape[0]
    half = half_ref[i]

    def weight_copies(e, into):
        return (pltpu.make_async_copy(wu_hbm.at[e], wu32_ref.at[into], sems.at[0, into]),
                pltpu.make_async_copy(wd_hbm.at[e], wd32_ref.at[into], sems.at[1, into]))

    @pl.when(i == 0)
    def _():
        for cp in weight_copies(be_ref[0], 0):
            cp.start()

    @pl.when((i == 0) | (be_ref[i] != be_ref[jnp.maximum(i - 1, 0)]))
    def _():
        for cp in weight_copies(be_ref[i], half):
            cp.wait()
        wu16_ref[...] = wu32_ref[half].astype(wu16_ref.dtype)
        wd16_ref[...] = wd32_ref[half].astype(wd16_ref.dtype)

        @pl.when(next_ref[i] >= 0)
        def _():
            for cp in weight_copies(next_ref[i], 1 - half):
                cp.start()

    @pl.when(i < nused_ref[0])
    def _():
        x = _load_row_tiles(xs_ref, ROW_BLOCK, d).astype(jnp.bfloat16)
        acts = []
        for lo in range(0, f, UP_TILE):
            glu_cols, lin_cols = slice(lo, lo + UP_TILE), slice(f + lo, f + lo + UP_TILE)
            glu = jnp.minimum(_dot(x, wu16_ref[:, glu_cols]) + bu_ref[:, glu_cols], SWIGLU_LIMIT)
            lin = jnp.clip(_dot(x, wu16_ref[:, lin_cols]) + bu_ref[:, lin_cols],
                           -SWIGLU_LIMIT, SWIGLU_LIMIT)
            act = glu * (1.0 / (1.0 + jnp.exp(-SWIGLU_ALPHA * glu))) * (lin + 1.0)
            acts.append(act.astype(jnp.bfloat16))
        act = jnp.concatenate(acts, axis=1)
        _store_row_tiles(ys_ref, _dot(act, wd16_ref[...]) + bd_ref[...])

    @pl.when(i >= nused_ref[0])
    def _():
        ys_ref[...] = jnp.zeros(ys_ref.shape, ys_ref.dtype)


def _experts(block_e, nused, block_half, block_next, xs, wu, bu, wd, bd):
    tiles, lanes = xs.shape
    e, d, f2 = wu.shape
    f = wd.shape[1]
    block = (ROW_BLOCK * SUBLANES, lanes)
    rows = lambda i, be, nu, hf, nx: (i, 0)
    used_rows = lambda i, be, nu, hf, nx: (jnp.minimum(i, nu[0] - 1), 0)
    per_e = lambda i, be, nu, hf, nx: (be[i], 0, 0)
    return pl.pallas_call(
        _experts_kernel,
        grid_spec=pltpu.PrefetchScalarGridSpec(
            num_scalar_prefetch=4,
            grid=(tiles // block[0],),
            in_specs=[pl.BlockSpec(block, used_rows),
                      pl.BlockSpec(memory_space=pl.ANY), pl.BlockSpec((None, 1, f2), per_e),
                      pl.BlockSpec(memory_space=pl.ANY), pl.BlockSpec((None, 1, d), per_e)],
            out_specs=pl.BlockSpec(block, rows),
            scratch_shapes=[pltpu.VMEM((2, d, f2), jnp.float32), pltpu.VMEM((2, f, d), jnp.float32),
                            pltpu.VMEM((d, f2), jnp.bfloat16), pltpu.VMEM((f, d), jnp.bfloat16),
                            pltpu.SemaphoreType.DMA((2, 2))]),
        out_shape=jax.ShapeDtypeStruct(xs.shape, jnp.float32),
        compiler_params=pltpu.CompilerParams(dimension_semantics=("arbitrary",),
                                             vmem_limit_bytes=VMEM_LIMIT),
        name="experts",
    )(block_e, nused, block_half, block_next, xs, wu, bu, wd, bd)


def _combine_kernel(dest_ref, dest_next_ref, h_ref, eg_ref, g_ref, ys_ref, o_ref, buf0_ref, buf1_ref,
                    sems, *, final_norm):
    i = pl.program_id(0)
    t, d = h_ref.shape
    pieces = d // LANES
    half = i % 2
    bufs = (buf0_ref, buf1_ref)

    def start_rows(idx_ref, into, tok):
        for j in range(TOP_K):
            _row_copy(ys_ref, idx_ref[tok * TOP_K + j], bufs[into].at[j], tok,
                      sems.at[into]).start(priority=j % 2)

    def wait_rows(which):
        for j in range(TOP_K):
            pltpu.make_async_copy(ys_ref.at[pl.ds(0, t * SUBLANES)], bufs[which].at[j],
                                  sems.at[which]).wait()

    @pl.when(i == 0)
    def _():
        def issue(tok, carry):
            start_rows(dest_ref, 0, tok)
            return carry
        lax.fori_loop(0, t, issue, 0, unroll=ISSUE_UNROLL)

    gain = g_ref[...]

    def step(cur, nxt):
        wait_rows(cur)

        def eight_tokens(grp, carry):
            tok0 = pl.multiple_of(grp * SUBLANES, SUBLANES)
            for k in range(SUBLANES):
                start_rows(dest_next_ref, nxt, tok0 + k)
            eg = eg_ref[pl.ds(tok0, SUBLANES), :]
            h = h_ref[pl.ds(tok0, SUBLANES), :]
            cols = []
            for p in range(pieces):
                col = h[:, p * LANES:(p + 1) * LANES]
                for j in range(TOP_K):
                    rows = bufs[cur][j, pl.ds(tok0 * pieces + p, SUBLANES, stride=pieces), :]
                    col = col + eg[:, j:j + 1] * rows
                cols.append(col)
            o_ref[pl.ds(tok0, SUBLANES), :] = jnp.concatenate(cols, axis=1)
            return carry

        lax.fori_loop(0, t // SUBLANES, eight_tokens, 0)
        if final_norm:
            o_ref[...] = _rmsnorm(o_ref[...], gain)

        @pl.when(i == pl.num_programs(0) - 1)
        def _():
            wait_rows(nxt)

    pl.when(half == 0)(lambda: step(0, 1))
    pl.when(half == 1)(lambda: step(1, 0))


def _combine(dest, h, eg, g, ys, final_norm):
    n, d = h.shape
    t = MOVE_TILE
    return pl.pallas_call(
        functools.partial(_combine_kernel, final_norm=final_norm),
        grid=(n // t,),
        in_specs=[pl.BlockSpec((t * TOP_K,), lambda i: (i,), memory_space=pltpu.SMEM),
                  pl.BlockSpec((t * TOP_K,), lambda i: (jnp.minimum(i + 1, n // t - 1),),
                               memory_space=pltpu.SMEM),
                  pl.BlockSpec((t, d), lambda i: (i, 0)),
                  pl.BlockSpec((t, 2 * TOP_K), lambda i: (i, 0)),
                  pl.BlockSpec((1, d), lambda i: (0, 0)),
                  pl.BlockSpec(memory_space=pl.ANY)],
        out_specs=pl.BlockSpec((t, d), lambda i: (i, 0)),
        scratch_shapes=[pltpu.VMEM((TOP_K, t * SUBLANES, ys.shape[1]), jnp.float32),
                        pltpu.VMEM((TOP_K, t * SUBLANES, ys.shape[1]), jnp.float32),
                        pltpu.SemaphoreType.DMA((2,))],
        out_shape=jax.ShapeDtypeStruct((n, d), jnp.float32),
        compiler_params=pltpu.CompilerParams(dimension_semantics=("arbitrary",),
                                             vmem_limit_bytes=VMEM_LIMIT),
        name="combine",
    )(dest, dest, h, eg, g, ys)


def _rope_tables(seq):
    half = ROPE_DIM // 2
    inv = jnp.power(jnp.float32(ROPE_THETA), -jnp.arange(half, dtype=jnp.float32) / half)
    ang = jnp.arange(seq, dtype=jnp.float32)[:, None] * inv[None, :]
    cos, sin = jnp.cos(ang), jnp.sin(ang)
    pad = HEAD_DIM - ROPE_DIM
    one, zero = jnp.ones((seq, pad), jnp.float32), jnp.zeros((seq, pad), jnp.float32)
    zh = jnp.zeros((seq, half), jnp.float32)
    reps = LANES // HEAD_DIM
    c = jnp.tile(jnp.concatenate([cos, cos, one], axis=1), (1, reps))
    a = jnp.tile(jnp.concatenate([-sin, zh, zero], axis=1), (1, reps))
    b = jnp.tile(jnp.concatenate([zh, sin, zero], axis=1), (1, reps))
    return c, a, b


def _layer(h, mem, p, tables, final_g):
    b, s, d = h.shape
    n = b * s
    bf = jnp.bfloat16
    x2 = h.reshape(n, d)
    w_in = p["w_in"]
    o_ik = 3 * ATTN_W + IDXQ_W
    o_u = o_ik + IDX_DIM + N_HEADS_IDX
    wa = w_in[:, :o_ik].astype(bf)
    wb = jnp.pad(w_in[:, o_ik:o_u], ((0, 0), (0, LANES - (o_u - o_ik)))).astype(bf)
    wc = w_in[:, o_u:].astype(bf)
    q, k, vt, iq, iklo, ikhi, iw, u, qm = _in_proj(x2, p["norm_mix_g"][None], wa, wb, wc, *tables, b, s)
    km, vm = _mem_kv(mem, p["mem_norm_g"][None], p["w_mem_kv"].astype(bf))
    b3 = lambda a: a.reshape(b, s, a.shape[-1])
    ya = _dsa(b3(iq), b3(iw), b3(q), b3(iklo), b3(ikhi), b3(k), vt).reshape(n, ATTN_W)

    wbd = jnp.zeros((POOL_W, POOL_W), jnp.float32)
    for g in range(N_POOL_GROUPS):
        lo = g * POOL_GROUP_DIM
        wbd = wbd.at[lo:lo + POOL_GROUP_DIM, lo:lo + POOL_GROUP_DIM].set(p["pool_w_group"][g])
    wr = jnp.pad(p["w_router"], ((0, 0), (0, LANES - N_EXPERTS)))
    wr_hi = wr.astype(bf)
    br = jnp.pad(p["b_router"], (0, LANES - N_EXPERTS), constant_values=NEG_BIG)[None]
    hmid, xn2, eg, ei, counts = _merge(
        x2, ya, u, qm, km, vm, p["norm_mix_g"][None], p["w_gate"].astype(bf), p["b_gate"][None],
        p["w_proj_attn"].astype(bf), p["w_proj_pool"].astype(bf), p["w_proj_mem"].astype(bf),
        wbd.astype(bf), p["pool_scale"][None], p["w_out"].astype(bf), p["norm_ffn_g"][None],
        wr_hi, jnp.concatenate([wr_hi, (wr - wr_hi.astype(jnp.float32)).astype(bf)], axis=1), br, s)

    cnt = counts[:, 0].astype(jnp.int32)
    padded = ((cnt + ROW_BLOCK - 1) // ROW_BLOCK) * ROW_BLOCK
    pend = jnp.cumsum(padded)
    pstart = (pend - padded).astype(jnp.int32)
    n_rows = n * TOP_K + N_EXPERTS * ROW_BLOCK
    n_blocks = n_rows // ROW_BLOCK
    nused = (pend[-1] // ROW_BLOCK).astype(jnp.int32)
    blk = jnp.minimum(jnp.arange(n_blocks, dtype=jnp.int32), nused - 1) * ROW_BLOCK
    block_e = jnp.sum(blk[:, None] >= pend[None, :], axis=1).astype(jnp.int32)
    block_e = jnp.minimum(block_e, N_EXPERTS - 1)
    has_rows = cnt > 0
    run_index = jnp.cumsum(has_rows) - 1
    ids = jnp.arange(N_EXPERTS, dtype=jnp.int32)
    later = jnp.where(has_rows[None, :] & (ids[None, :] > ids[:, None]), ids[None, :], N_EXPERTS)
    next_used = jnp.min(later, axis=1)
    next_used = jnp.where(next_used < N_EXPERTS, next_used, -1).astype(jnp.int32)
    of_block = block_e[:, None] == ids[None, :]
    block_half = jnp.sum(jnp.where(of_block, run_index % 2, 0), axis=1).astype(jnp.int32)
    block_next = jnp.sum(jnp.where(of_block, next_used, 0), axis=1).astype(jnp.int32)
    chosen =ei[:, :TOP_K, None] == jnp.arange(N_EXPERTS, dtype=jnp.int32)
    dest = (jnp.sum(jnp.where(chosen, pstart, 0), axis=-1) + ei[:, TOP_K:]).reshape(-1)

    xs = _dispatch(pstart, cnt, dest, xn2, n_rows)
    ys = _experts(block_e, nused[None], block_half, block_next, xs, p["w_up"], p["b_up"][:, None, :],
                  p["w_down"], p["b_down"][:, None, :])
    g = p["norm_ffn_g"][None] if final_g is None else final_g[None]
    return _combine(dest, hmid, eg, g, ys, final_g is not None).reshape(b, s, d)


def kernel(x, mem, norm_mix_g, w_in, w_gate, b_gate, w_proj_attn, w_proj_pool, w_proj_mem,
           pool_w_group, pool_scale, mem_norm_g, w_mem_kv, w_out, norm_ffn_g,
           w_router, b_router, w_up, b_up, w_down, b_down, norm_final_g):
    b, s, d = x.shape
    assert s % IN_TILE == 0 and s % MERGE_TILE == 0 and s % COUNT_CHUNK == 0
    assert (b * s) % MOVE_TILE == 0
    assert d == LANES * SUBLANES, "the row-tile layout maps one model row onto one (8, 128) tile"
    stacked = dict(norm_mix_g=norm_mix_g, w_in=w_in, w_gate=w_gate, b_gate=b_gate,
                   w_proj_attn=w_proj_attn, w_proj_pool=w_proj_pool, w_proj_mem=w_proj_mem,
                   pool_w_group=pool_w_group, pool_scale=pool_scale, mem_norm_g=mem_norm_g,
                   w_mem_kv=w_mem_kv, w_out=w_out, norm_ffn_g=norm_ffn_g, w_router=w_router,
                   b_router=b_router, w_up=w_up, b_up=b_up, w_down=w_down, b_down=b_down)
    depth = w_in.shape[0]
    tables = _rope_tables(s)
    h = x
    for l in range(depth):
        p = {name: val[l] for name, val in stacked.items()}
        h = _layer(h, mem, p, tables, norm_final_g if l == depth - 1 else None)
    return h
```

```python
import functools

import jax
import jax.numpy as jnp
import numpy as np
from jax import lax
from jax.experimental import pallas as pl
from jax.experimental.pallas import tpu as pltpu

N_HEADS_ATTN = 8
HEAD_DIM = 64
ROPE_DIM = HEAD_DIM // 4
ROPE_THETA = 500000.0
N_HEADS_IDX = 8
IDX_DIM = 64
TOPK_MAX = 256
N_POOL_GROUPS = 4
POOL_GROUP_DIM = 64
POOL_WINDOWS = (2, 4, 8, 16)
POOL_HALO = 16
N_HEADS_MEM = 4
N_BRANCHES = 3
N_EXPERTS = 32
TOP_K = 4
SWIGLU_ALPHA = 1.702
SWIGLU_LIMIT = 7.0
EPS = 1e-6

ATTN_W = N_HEADS_ATTN * HEAD_DIM
POOL_W = N_POOL_GROUPS * POOL_GROUP_DIM
MEM_W = N_HEADS_MEM * HEAD_DIM
IDXQ_W = N_HEADS_IDX * IDX_DIM

LANES = 128
SUBLANES = 8
VMEM_LIMIT = 56 * 1024 * 1024

IN_TILE = 512
Q_BLOCK = 128
KEY_CHUNK = 256
COUNT_CHUNK = 512
COUNT_ROWS = 32
EARLY_EXIT_PASSES = (25, 28)
MERGE_TILE = 512
ROW_BLOCK = 256
UP_TILE = 256
MOVE_TILE = 512
ISSUE_UNROLL = 4

VT_ROWS = LANES + 16
VT_ALL = (N_HEADS_ATTN // 2) * VT_ROWS
LOG2_E = 1.4426950408889634

NEG_BIG = -1e30
INT_MIN = -2 ** 31
NEG_INF_KEY = -0x7F800001


def _dot(a, b):
    return jnp.dot(a, b, preferred_element_type=jnp.float32)


def _dot_nt(a, b):
    return lax.dot_general(a, b, (((1,), (1,)), ((), ())), preferred_element_type=jnp.float32)


def _rmsnorm(x, g):
    return x * lax.rsqrt(jnp.mean(x * x, axis=-1, keepdims=True) + EPS) * g


def _store_row_tiles(ref, x):
    rows, width = x.shape
    for j in range(width // LANES):
        ref[pl.ds(j, rows, stride=width // LANES), :] = x[:, j * LANES:(j + 1) * LANES]


def _load_row_tiles(ref, rows, width):
    pieces = width // LANES
    return jnp.concatenate([ref[pl.ds(j, rows, stride=pieces), :] for j in range(pieces)], axis=1)


def _row_tile(ref, row):
    return ref.at[pl.ds(pl.multiple_of(row * SUBLANES, SUBLANES), SUBLANES)]


def _rope128(x, c, a, b):
    return x * c + pltpu.roll(x, LANES - ROPE_DIM // 2, 1) * a + pltpu.roll(x, ROPE_DIM // 2, 1) * b


def _in_proj_kernel(x_ref, g_ref, wa_ref, wb_ref, wc_ref, rc_ref, ra_ref, rb_ref,
                    q_ref, k_ref, vt_ref, iq_ref, iklo_ref, ikhi_ref, iw_ref, u_ref, qm_ref):
    xn = _rmsnorm(x_ref[...], g_ref[...]).astype(jnp.bfloat16)
    rc, ra, rb = rc_ref[...], ra_ref[...], rb_ref[...]
    pa = _dot(xn, wa_ref[...])
    for seg, (ref, scale) in enumerate(((q_ref, LOG2_E * HEAD_DIM ** -0.5), (k_ref, None),
                                        (None, None), (iq_ref, IDX_DIM ** -0.5))):
        for c in range(ATTN_W // LANES):
            lo = seg * ATTN_W + c * LANES
            blk = pa[:, lo:lo + LANES]
            if ref is None:
                vt_ref[c * VT_ROWS:c * VT_ROWS + LANES, :] = blk.T.astype(vt_ref.dtype)
                vt_ref[c * VT_ROWS + LANES:(c + 1) * VT_ROWS, :] = jnp.ones(
                    (VT_ROWS - LANES, blk.shape[0]), vt_ref.dtype)
                continue
            blk = _rope128(blk, rc, ra, rb)
            if scale is not None:
                blk = blk * scale
            ref[:, c * LANES:(c + 1) * LANES] = blk.astype(ref.dtype)
    pb = _dot(xn, wb_ref[...])
    lane = lax.broadcasted_iota(jnp.int32, pb.shape, 1)
    ik = jnp.where(lane < IDX_DIM, _rope128(pb, rc, ra, rb), 0.0)
    iklo_ref[...] = ik.astype(iklo_ref.dtype)
    ikhi_ref[...] = pltpu.roll(ik, IDX_DIM, 1).astype(ikhi_ref.dtype)
    iw_ref[...] = pb * (N_HEADS_IDX ** -0.5)
    pc = _dot(xn, wc_ref[...])
    u_ref[...] = pc[:, :POOL_W]
    qm_ref[...] = (pc[:, POOL_W:] * (HEAD_DIM ** -0.5)).astype(qm_ref.dtype)


def _in_proj(x2, g, wa, wb, wc, rc, ra, rb, batch, seq):
    n, d = x2.shape
    t = IN_TILE
    tiles_per_seq = seq // t
    row = lambda i: (i, 0)
    full = lambda i: (0, 0)
    pos = lambda i: (i % tiles_per_seq, 0)
    bf = jnp.bfloat16
    flat = lambda w, dt: (jax.ShapeDtypeStruct((n, w), dt), pl.BlockSpec((t, w), row))
    outs = (flat(ATTN_W, bf), flat(ATTN_W, bf),
            (jax.ShapeDtypeStruct((batch, VT_ALL, seq), bf),
             pl.BlockSpec((None, VT_ALL, t), lambda i: (i // tiles_per_seq, 0, i % tiles_per_seq))),
            flat(IDXQ_W, bf), flat(LANES, bf), flat(LANES, bf), flat(LANES, jnp.float32),
            flat(POOL_W, jnp.float32), flat(MEM_W, bf))
    return pl.pallas_call(
        _in_proj_kernel,
        grid=(n // t,),
        in_specs=[pl.BlockSpec((t, d), row), pl.BlockSpec((1, d), full),
                  pl.BlockSpec(wa.shape, full), pl.BlockSpec(wb.shape, full),
                  pl.BlockSpec(wc.shape, full),
                  pl.BlockSpec((t, LANES), pos), pl.BlockSpec((t, LANES), pos),
                  pl.BlockSpec((t, LANES), pos)],
        out_specs=tuple(o[1] for o in outs),
        out_shape=tuple(o[0] for o in outs),
        compiler_params=pltpu.CompilerParams(dimension_semantics=("arbitrary",),
                                             vmem_limit_bytes=VMEM_LIMIT),
        name="in_proj",
    )(x2, g, wa, wb, wc, rc, ra, rb)


def _mem_kv_kernel(mem_ref, g_ref, w_ref, km_ref, vm_ref):
    mn = _rmsnorm(mem_ref[...], g_ref[...]).astype(jnp.bfloat16)
    kv = _dot(mn, w_ref[...])
    km, vm = kv[:, :MEM_W], kv[:, MEM_W:]
    lane = lax.broadcasted_iota(jnp.int32, km.shape, 1)
    for h in range(N_HEADS_MEM):
        in_head = (lane >= h * HEAD_DIM) & (lane < (h + 1) * HEAD_DIM)
        km_ref[h] = jnp.where(in_head, km, 0.0).astype(km_ref.dtype)
        vm_ref[h] = jnp.where(in_head, vm, 0.0).astype(vm_ref.dtype)


def _mem_kv(mem, g, w):
    b, m, d = mem.shape
    out = jax.ShapeDtypeStruct((b, N_HEADS_MEM, m, MEM_W), jnp.bfloat16)
    return pl.pallas_call(
        _mem_kv_kernel,
        grid=(b,),
        in_specs=[pl.BlockSpec((None, m, d), lambda i: (i, 0, 0)),
                  pl.BlockSpec((1, d), lambda i: (0, 0)),
                  pl.BlockSpec(w.shape, lambda i: (0, 0))],
        out_specs=(pl.BlockSpec((None, N_HEADS_MEM, m, MEM_W), lambda i: (i, 0, 0, 0)),
                   pl.BlockSpec((None, N_HEADS_MEM, m, MEM_W), lambda i: (i, 0, 0, 0))),
        out_shape=(out, out),
        compiler_params=pltpu.CompilerParams(dimension_semantics=("arbitrary",),
                                             vmem_limit_bytes=VMEM_LIMIT),
        name="mem_kv",
    )(mem, g, w)


def _dsa_kernel(iq_ref, iw_ref, q_ref, iklo_ref, ikhi_ref, k_ref, vt_ref, o_ref,
                key_ref, iqt_ref, qt_ref, bias_ref, s_ref, acc_ref, *, n_top, idx_bits):
    qb = pl.program_id(1)
    n_chunks = (qb * Q_BLOCK + Q_BLOCK + KEY_CHUNK - 1) // KEY_CHUNK
    n_count_chunks = (qb * Q_BLOCK + Q_BLOCK + COUNT_CHUNK - 1) // COUNT_CHUNK
    q_pos = qb * Q_BLOCK + lax.broadcasted_iota(jnp.int32, (1, Q_BLOCK), 1)
    row_c = lax.broadcasted_iota(jnp.int32, (COUNT_CHUNK, 1), 0)
    row_d = lax.broadcasted_iota(jnp.int32, (LANES, Q_BLOCK), 0)
    bf = jnp.bfloat16

    def chunk_off(c):
        return pl.multiple_of(c * KEY_CHUNK, KEY_CHUNK)

    wt = iw_ref[...].T
    for j in range(N_HEADS_IDX // 2):
        iqt_ref[j // 2, :, (j % 2) * Q_BLOCK:(j % 2 + 1) * Q_BLOCK] = (
            iq_ref[:, j * LANES:(j + 1) * LANES].astype(jnp.float32).T.astype(bf))
    for j in range(N_HEADS_ATTN // 2):
        pair_t = q_ref[:, j * LANES:(j + 1) * LANES].astype(jnp.float32).T
        qt_ref[j, :, :Q_BLOCK] = jnp.where(row_d < HEAD_DIM, pair_t, 0.0).astype(bf)
        qt_ref[j, :, Q_BLOCK:] = jnp.where(row_d >= HEAD_DIM, pair_t, 0.0).astype(bf)

    def score_chunk(c, carry):
        off = pl.multiple_of(c * COUNT_CHUNK, COUNT_CHUNK)
        acc = jnp.zeros((COUNT_CHUNK, Q_BLOCK), jnp.float32)
        for parity, ik_ref in enumerate((iklo_ref, ikhi_ref)):
            ik = ik_ref[pl.ds(off, COUNT_CHUNK), :]
            for g in range(N_HEADS_IDX // 4):
                dots = _dot(ik, iqt_ref[g])
                for side in range(2):
                    head = 2 * (2 * g + side) + parity
                    w = wt[IDX_DIM + head:IDX_DIM + head + 1, :]
                    acc = acc + w * jnp.maximum(dots[:, side * Q_BLOCK:(side + 1) * Q_BLOCK], 0.0)
        acc = jnp.where(off + row_c <= q_pos, acc, -jnp.inf)
        bits = lax.bitcast_convert_type(acc, jnp.int32)
        key_ref[pl.ds(off, COUNT_CHUNK), :] = bits ^ ((bits >> 31) & 0x7FFFFFFF)
        return carry

    lax.fori_loop(0, n_count_chunks, score_chunk, 0)

    def count(pred):
        def body(c, cnt):
            off = pl.multiple_of(c * COUNT_CHUNK, COUNT_CHUNK)
            hit = jnp.where(pred(key_ref[pl.ds(off, COUNT_CHUNK), :], off + row_c), 1.0, 0.0)
            return cnt + jnp.sum(hit.reshape(COUNT_CHUNK // COUNT_ROWS, COUNT_ROWS, Q_BLOCK), axis=0)
        cnt = lax.fori_loop(0, n_count_chunks, body, jnp.zeros((COUNT_ROWS, Q_BLOCK), jnp.float32))
        return jnp.sum(cnt, axis=0, keepdims=True)

    k_f = jnp.float32(n_top)
    zero = jnp.zeros((1, Q_BLOCK), jnp.int32)
    n_nonneg = count(lambda kk, pos: kk >= zero)
    n_stored = (n_count_chunks * COUNT_CHUNK).astype(jnp.float32)
    thr = jnp.where(n_nonneg >= k_f, zero, INT_MIN)
    n_ge = jnp.where(n_nonneg >= k_f, n_nonneg, n_stored)

    def thr_bit(i, carry):
        cur, n_cur = carry
        cand = cur + jnp.left_shift(jnp.int32(1), 30 - i)
        n_cand = count(lambda kk, pos: kk >= cand)
        keep = n_cand >= k_f
        return jnp.where(keep, cand, cur), jnp.where(keep, n_cand, n_cur)

    def all_exact(carry):
        return jnp.max(jnp.abs(carry[1] - k_f)) == 0.0

    def search_bits(first, last):
        return lambda carry: lax.fori_loop(first, last, thr_bit, carry)

    carry = search_bits(0, EARLY_EXIT_PASSES[0])((thr, n_ge))
    for first, last in zip(EARLY_EXIT_PASSES, EARLY_EXIT_PASSES[1:] + (31,)):
        carry = lax.cond(all_exact(carry), lambda c: c, search_bits(first, last), carry)
    thr, n_ge = carry

    @pl.when(jnp.max(n_ge) > k_f)
    def _():
        need = k_f - count(lambda kk, pos: kk > thr)

        def tie_bit(i, cur):
            cand = cur + jnp.left_shift(jnp.int32(1), idx_bits - 1 - i)
            below = count(lambda kk, pos: (kk == thr) & (pos < cand))
            return jnp.where(below < need, cand, cur)
        tie_pos = lax.fori_loop(0, idx_bits, tie_bit, zero)

        def drop_late_ties(c, carry):
            off = pl.multiple_of(c * COUNT_CHUNK, COUNT_CHUNK)
            kk = key_ref[pl.ds(off, COUNT_CHUNK), :]
            late = (kk == thr) & (off + row_c > tie_pos)
            key_ref[pl.ds(off, COUNT_CHUNK), :] = jnp.where(late, INT_MIN, kk)
            return carry
        lax.fori_loop(0, n_count_chunks, drop_late_ties, 0)

    thr = jnp.maximum(thr, NEG_INF_KEY + 1)

    acc_ref[...] = jnp.zeros(acc_ref.shape, jnp.float32)

    n_pairs = N_HEADS_ATTN // 2

    def masked_logits(c, slot):
        off = chunk_off(jnp.minimum(c, n_chunks - 1))
        kk = key_ref[pl.ds(off, KEY_CHUNK), :]
        bias = jnp.where(kk >= thr, jnp.where(c < n_chunks, 0.0, NEG_BIG), NEG_BIG)
        bias_ref[slot] = jnp.concatenate([bias, bias], axis=1)
        for j in range(n_pairs):
            kp = k_ref[pl.ds(off, KEY_CHUNK), j * LANES:(j + 1) * LANES]
            s_ref[slot, j] = _dot(kp, qt_ref[j]) + bias_ref[slot]

    def softmax_pv(c, slot, ms, ls):
        off = chunk_off(jnp.minimum(c, n_chunks - 1))
        new_ms, new_ls = [], []
        for j in range(n_pairs):
            s = s_ref[slot, j]
            m_new = jnp.maximum(ms[j], jnp.max(s, axis=0, keepdims=True))
            alpha = jnp.exp2(ms[j] - m_new)
            p = jnp.exp2(s - m_new).astype(bf)
            vt = vt_ref[j * VT_ROWS:(j + 1) * VT_ROWS, pl.ds(off, KEY_CHUNK)]
            pv = _dot(vt, p)
            new_ls.append(alpha * ls[j] + pv[LANES:LANES + 1, :])
            new_ms.append(m_new)
            lo, mid, hi = j * LANES, j * LANES + HEAD_DIM, (j + 1) * LANES
            acc_ref[lo:mid, :] = alpha[:, :Q_BLOCK] * acc_ref[lo:mid, :] + pv[:HEAD_DIM, :Q_BLOCK]
            acc_ref[mid:hi, :] = alpha[:, Q_BLOCK:] * acc_ref[mid:hi, :] + pv[HEAD_DIM:LANES, Q_BLOCK:]
        return tuple(new_ms), tuple(new_ls)

    masked_logits(0, 0)

    def attn_two_chunks(i, carry):
        ms, ls = carry
        masked_logits(2 * i + 1, 1)
        ms, ls = softmax_pv(2 * i, 0, ms, ls)
        masked_logits(2 * i + 2, 0)
        return softmax_pv(2 * i + 1, 1, ms, ls)

    init = (tuple(jnp.full((1, 2 * Q_BLOCK), NEG_BIG, jnp.float32) for _ in range(n_pairs)),
            tuple(jnp.zeros((1, 2 * Q_BLOCK), jnp.float32) for _ in range(n_pairs)))
    _, ls = lax.fori_loop(0, (n_chunks + 1) // 2, attn_two_chunks, init)
    for j in range(n_pairs):
        lo, mid, hi = j * LANES, j * LANES + HEAD_DIM, (j + 1) * LANES
        even = acc_ref[lo:mid, :] / ls[j][:, :Q_BLOCK]
        odd = acc_ref[mid:hi, :] / ls[j][:, Q_BLOCK:]
        o_ref[:, lo:hi] = jnp.concatenate([even, odd], axis=0).T.astype(o_ref.dtype)


def _dsa(iq, iw, q, iklo, ikhi, k, vt):
    b, s, _ = q.shape
    n_top = min(TOPK_MAX, s // 4)
    idx_bits = max(1, int(np.ceil(np.log2(s))))
    qblk = lambda w: pl.BlockSpec((None, Q_BLOCK, w), lambda bi, qi: (bi, qi, 0))
    keys = lambda w: pl.BlockSpec((None, s, w), lambda bi, qi: (bi, 0, 0))
    return pl.pallas_call(
        functools.partial(_dsa_kernel, n_top=n_top, idx_bits=idx_bits),
        grid=(b, s // Q_BLOCK),
        in_specs=[qblk(IDXQ_W), qblk(LANES), qblk(ATTN_W), keys(LANES), keys(LANES),
                  keys(ATTN_W), pl.BlockSpec((None, VT_ALL, s), lambda bi, qi: (bi, 0, 0))],
        out_specs=qblk(ATTN_W),
        out_shape=jax.ShapeDtypeStruct((b, s, ATTN_W), jnp.bfloat16),
        scratch_shapes=[pltpu.VMEM((s, Q_BLOCK), jnp.int32),
                        pltpu.VMEM((N_HEADS_IDX // 4, LANES, 2 * Q_BLOCK), jnp.bfloat16),
                        pltpu.VMEM((N_HEADS_ATTN // 2, LANES, 2 * Q_BLOCK), jnp.bfloat16),
                        pltpu.VMEM((2, KEY_CHUNK, 2 * Q_BLOCK), jnp.float32),
                        pltpu.VMEM((2, N_HEADS_ATTN // 2, KEY_CHUNK, 2 * Q_BLOCK), jnp.float32),
                        pltpu.VMEM((ATTN_W, Q_BLOCK), jnp.float32)],
        compiler_params=pltpu.CompilerParams(dimension_semantics=("arbitrary", "arbitrary"),
                                             vmem_limit_bytes=VMEM_LIMIT),
        name="dsa",
    )(iq, iw, q, iklo, ikhi, k, vt)


def _merge_kernel(x_ref, ya_ref, u_ref, uprev_ref, qm_ref, km_ref, vm_ref,
                  gmix_ref, wg_ref, bg_ref, wpa_ref, wpp_ref, wpm_ref, wbd_ref, psc_ref,
                  wo_ref, gffn_ref, wrh_ref, wrhl_ref, br_ref,
                  h_ref, xn2_ref, eg_ref, ei_ref, cnt_ref, carry_ref, *, tiles_per_seq):
    i = pl.program_id(0)
    t = x_ref.shape[0]
    d = x_ref.shape[1]
    bf = jnp.bfloat16
    x = x_ref[...]
    xn = _rmsnorm(x, gmix_ref[...]).astype(bf)

    tile_in_seq = i % tiles_per_seq
    u = u_ref[...]
    halo = jnp.where(tile_in_seq == 0, 0.0, uprev_ref[...])
    ext = jnp.concatenate([halo, u], axis=0)
    lane_p = lax.broadcasted_iota(jnp.int32, (t, POOL_W), 1)
    pos1 = (tile_in_seq * t + lax.broadcasted_iota(jnp.int32, (t, 1), 0) + 1).astype(jnp.float32)
    pooled = None
    run, width = ext, 1
    for g, w in enumerate(POOL_WINDOWS):
        while width < w:
            run = run[width:] + run[:-width]
            width *= 2
        start = POOL_HALO + 1 - w
        mean = run[start:start + t] / jnp.minimum(pos1, float(w))
        pooled = mean if pooled is None else jnp.where(lane_p >= g * POOL_GROUP_DIM, mean, pooled)
    pooled = pooled - u
    mixed = _dot(pooled.astype(bf), wbd_ref[...]) * psc_ref[...]
    y_pool = _dot(mixed.astype(bf), wpp_ref[...])

    qm = qm_ref[...]
    probs = []
    for h in range(N_HEADS_MEM):
        s = _dot_nt(qm, km_ref[h])
        p = jnp.exp(s - jnp.max(s, axis=1, keepdims=True))
        probs.append((p / jnp.sum(p, axis=1, keepdims=True)).astype(bf))
    y_mem = _dot(probs[0], vm_ref[0])
    for h in range(1, N_HEADS_MEM):
        y_mem = y_mem + _dot(probs[h], vm_ref[h])
    y_mem = _dot(y_mem.astype(bf), wpm_ref[...])

    y_attn = _dot(ya_ref[...], wpa_ref[...])

    def gate(br):
        z = _dot(xn, wg_ref[:, br * d:(br + 1) * d]) + bg_ref[:, br * d:(br + 1) * d]
        return 1.0 / (1.0 + jnp.exp(-z))

    merged = gate(0) * y_attn + gate(1) * y_pool + gate(2) * y_mem
    h = x + _dot(merged.astype(bf), wo_ref[...])
    h_ref[...] = h
    xn2 = _rmsnorm(h, gffn_ref[...])
    _store_row_tiles(xn2_ref, xn2)

    x_hi = xn2.astype(bf)
    x_lo = (xn2 - x_hi.astype(jnp.float32)).astype(bf)
    by_hi = _dot(x_hi, wrhl_ref[...])
    logits = (by_hi[:, :LANES] + (_dot(x_lo, wrh_ref[...]) + by_hi[:, LANES:])) + br_ref[...]
    work = logits.T[:N_EXPERTS, :]
    row_e = lax.broadcasted_iota(jnp.int32, work.shape, 0).astype(jnp.float32)
    vals, onehots, ids = [], [], []
    for _ in range(TOP_K):
        mx = jnp.max(work, axis=0, keepdims=True)
        idx = jnp.min(jnp.where(work == mx, row_e, float(N_EXPERTS)), axis=0, keepdims=True)
        oh = row_e == idx
        vals.append(mx)
        ids.append(idx)
        onehots.append(oh)
        work = jnp.where(oh, -jnp.inf, work)
    exps = [jnp.exp(v - vals[0]) for v in vals]
    denom = exps[0] + exps[1] + exps[2] + exps[3]

    @pl.when(i == 0)
    def _():
        carry_ref[...] = jnp.zeros(carry_ref.shape, jnp.float32)

    member = jnp.where(onehots[0] | onehots[1] | onehots[2] | onehots[3], 1.0, 0.0)
    r_io = lax.broadcasted_iota(jnp.int32, (t, t), 0)
    c_io = lax.broadcasted_iota(jnp.int32, (t, t), 1)
    earlier = jnp.where(r_io < c_io, 1.0, 0.0).astype(bf)
    before = _dot(member.astype(bf), earlier) + carry_ref[...]
    carry_ref[...] = carry_ref[...] + jnp.sum(member, axis=1, keepdims=True)
    cnt_ref[...] = carry_ref[...]

    row8 = lax.broadcasted_iota(jnp.int32, (LANES, t), 0)
    eg = jnp.zeros((LANES, t), jnp.float32)
    ei = jnp.zeros((LANES, t), jnp.float32)
    for j in range(TOP_K):
        rank = jnp.sum(jnp.where(onehots[j], before, 0.0), axis=0, keepdims=True)
        eg = jnp.where(row8 == j, exps[j] / denom, eg)
        ei = jnp.where(row8 == j, ids[j], ei)
        ei = jnp.where(row8 == TOP_K + j, rank, ei)
    eg_ref[...] = eg.T[:, :2 * TOP_K]
    ei_ref[...] = ei.T[:, :2 * TOP_K].astype(jnp.int32)


def _merge(x2, ya, u, qm, km, vm, gmix, wg, bg, wpa, wpp, wpm, wbd, psc, wo, gffn, wrh, wrhl, br, seq):
    n, d = x2.shape
    t = MERGE_TILE
    tiles_per_seq = seq // t
    m = km.shape[2]
    row = lambda i: (i, 0)
    full = lambda i: (0, 0)
    per_batch = lambda i: (i // tiles_per_seq, 0, 0, 0)
    halo_blocks = t // POOL_HALO
    prev = lambda i: (jnp.maximum(i * halo_blocks - 1, 0), 0)
    consts = (gmix, wg, bg, wpa, wpp, wpm, wbd, psc, wo, gffn, wrh, wrhl, br)
    pieces = d // LANES
    out_shape = (jax.ShapeDtypeStruct((n, d), jnp.float32),
                 jax.ShapeDtypeStruct((n * pieces, LANES), jnp.float32),
                 jax.ShapeDtypeStruct((n, 2 * TOP_K), jnp.float32),
                 jax.ShapeDtypeStruct((n, 2 * TOP_K), jnp.int32),
                 jax.ShapeDtypeStruct((N_EXPERTS, 1), jnp.float32))
    return pl.pallas_call(
        functools.partial(_merge_kernel, tiles_per_seq=tiles_per_seq),
        grid=(n // t,),
        in_specs=[pl.BlockSpec((t, d), row), pl.BlockSpec((t, ATTN_W), row),
                  pl.BlockSpec((t, POOL_W), row), pl.BlockSpec((POOL_HALO, POOL_W), prev),
                  pl.BlockSpec((t, MEM_W), row),
                  pl.BlockSpec((None, N_HEADS_MEM, m, MEM_W), per_batch),
                  pl.BlockSpec((None, N_HEADS_MEM, m, MEM_W), per_batch)]
                 + [pl.BlockSpec(c.shape, full) for c in consts],
        out_specs=(pl.BlockSpec((t, d), row), pl.BlockSpec((t * pieces, LANES), row),
                   pl.BlockSpec((t, 2 * TOP_K), row), pl.BlockSpec((t, 2 * TOP_K), row),
                   pl.BlockSpec((N_EXPERTS, 1), full)),
        out_shape=out_shape,
        scratch_shapes=[pltpu.VMEM((N_EXPERTS, 1), jnp.float32)],
        compiler_params=pltpu.CompilerParams(dimension_semantics=("arbitrary",),
                                             vmem_limit_bytes=VMEM_LIMIT),
        name="merge",
    )(x2, ya, u, u, qm, km, vm, *consts)


def _row_copy(src_ref, src_row, dst_ref, dst_row, sem):
    return pltpu.make_async_copy(_row_tile(src_ref, src_row), _row_tile(dst_ref, dst_row), sem)


def _dispatch_kernel(pstart_ref, cnt_ref, dest_ref, x_ref, xs_ref, zero_ref, sem):
    t = x_ref.shape[0] // SUBLANES
    block_tiles = ROW_BLOCK * SUBLANES

    @pl.when(pl.program_id(0) == 0)
    def _():
        zero_ref[...] = jnp.zeros(zero_ref.shape, zero_ref.dtype)

        def for_each_pad_row(act):
            def per_expert(e, carry):
                n_pad = (ROW_BLOCK - cnt_ref[e] % ROW_BLOCK) % ROW_BLOCK
                row = pstart_ref[e] + cnt_ref[e]
                size = ROW_BLOCK // 2
                while size >= 1:
                    has = (n_pad & size) != 0
                    span = size * SUBLANES
                    dst = xs_ref.at[pl.ds(pl.multiple_of(row * SUBLANES, SUBLANES), span)]
                    pl.when(has)(functools.partial(
                        act, pltpu.make_async_copy(zero_ref.at[pl.ds(0, span)], dst, sem)))
                    row = row + jnp.where(has, size, 0)
                    size //= 2
                return carry
            lax.fori_loop(0, N_EXPERTS, per_expert, 0)

        def for_each_unused_block(act):
            n_blocks = xs_ref.shape[0] // block_tiles
            first = (pstart_ref[N_EXPERTS - 1] + cnt_ref[N_EXPERTS - 1] + ROW_BLOCK - 1) // ROW_BLOCK

            def per_block(b, c):
                rows = pl.ds(pl.multiple_of(b * block_tiles, block_tiles), block_tiles)
                act(pltpu.make_async_copy(zero_ref, xs_ref.at[rows], sem))
                return c
            lax.fori_loop(first, n_blocks, per_block, 0)

        for act in (lambda cp: cp.start(), lambda cp: cp.wait()):
            for_each_pad_row(act)
            for_each_unused_block(act)

    def issue(tok, carry):
        for j in range(TOP_K):
            _row_copy(x_ref, tok, xs_ref, dest_ref[tok * TOP_K + j], sem).start(priority=j % 2)
        return carry

    lax.fori_loop(0, t, issue, 0, unroll=ISSUE_UNROLL)
    for j in range(TOP_K):
        pltpu.make_async_copy(x_ref, xs_ref.at[pl.ds(0, x_ref.shape[0])], sem).wait()


def _dispatch(pstart, cnt, dest, xn2, n_rows):
    tiles, lanes = xn2.shape
    n = tiles // SUBLANES
    t = MOVE_TILE
    return pl.pallas_call(
        _dispatch_kernel,
        grid_spec=pltpu.PrefetchScalarGridSpec(
            num_scalar_prefetch=2,
            grid=(n // t,),
            in_specs=[pl.BlockSpec((t * TOP_K,), lambda i, ps, ct: (i,), memory_space=pltpu.SMEM),
                      pl.BlockSpec((t * SUBLANES, lanes), lambda i, ps, ct: (i, 0))],
            out_specs=pl.BlockSpec(memory_space=pl.ANY),
            scratch_shapes=[pltpu.VMEM((ROW_BLOCK * SUBLANES, lanes), jnp.float32),
                            pltpu.SemaphoreType.DMA(())]),
        out_shape=jax.ShapeDtypeStruct((n_rows * SUBLANES, lanes), jnp.float32),
        compiler_params=pltpu.CompilerParams(dimension_semantics=("arbitrary",),
                                             vmem_limit_bytes=VMEM_LIMIT),
        name="dispatch",
    )(pstart, cnt, dest, xn2)


def _experts_kernel(be_ref, nused_ref, half_ref, next_ref, xs_ref, wu_hbm, bu_ref, wd_hbm, bd_ref,
                    ys_ref, wu32_ref, wd32_ref, wu16_ref, wd16_ref, sems):
    i = pl.program_id(0)
    d, f = wu16_ref.shape[0], wd16_ref.shape[0]
    half = half_ref[i]

    def weight_copies(e, into):
        return (pltpu.make_async_copy(wu_hbm.at[e], wu32_ref.at[into], sems.at[0, into]),
                pltpu.make_async_copy(wd_hbm.at[e], wd32_ref.at[into], sems.at[1, into]))

    @pl.when(i == 0)
    def _():
        for cp in weight_copies(be_ref[0], 0):
            cp.start()

    @pl.when((i == 0) | (be_ref[i] != be_ref[jnp.maximum(i - 1, 0)]))
    def _():
        for cp in weight_copies(be_ref[i], half):
            cp.wait()
        wu16_ref[...] = wu32_ref[half].astype(wu16_ref.dtype)
        wd16_ref[...] = wd32_ref[half].astype(wd16_ref.dtype)

        @pl.when(next_ref[i] >= 0)
        def _():
            for cp in weight_copies(next_ref[i], 1 - half):
                cp.start()

    @pl.when(i < nused_ref[0])
    def _():
        x = _load_row_tiles(xs_ref, ROW_BLOCK, d).astype(jnp.bfloat16)
        acts = []
        for lo in range(0, f, UP_TILE):
            glu_cols, lin_cols = slice(lo, lo + UP_TILE), slice(f + lo, f + lo + UP_TILE)
            glu = jnp.minimum(_dot(x, wu16_ref[:, glu_cols]) + bu_ref[:, glu_cols], SWIGLU_LIMIT)
            lin = jnp.clip(_dot(x, wu16_ref[:, lin_cols]) + bu_ref[:, lin_cols],
                           -SWIGLU_LIMIT, SWIGLU_LIMIT)
            act = glu * (1.0 / (1.0 + jnp.exp(-SWIGLU_ALPHA * glu))) * (lin + 1.0)
            acts.append(act.astype(jnp.bfloat16))
        act = jnp.concatenate(acts, axis=1)
        _store_row_tiles(ys_ref, _dot(act, wd16_ref[...]) + bd_ref[...])

    @pl.when(i >= nused_ref[0])
    def _():
        ys_ref[...] = jnp.zeros(ys_ref.shape, ys_ref.dtype)


def _experts(block_e, nused, block_half, block_next, xs, wu, bu, wd, bd):
    tiles, lanes = xs.shape
    e, d, f2 = wu.shape
    f = wd.shape[1]
    block = (ROW_BLOCK * SUBLANES, lanes)
    rows = lambda i, be, nu, hf, nx: (i, 0)
    used_rows = lambda i, be, nu, hf, nx: (jnp.minimum(i, nu[0] - 1), 0)
    per_e = lambda i, be, nu, hf, nx: (be[i], 0, 0)
    return pl.pallas_call(
        _experts_kernel,
        grid_spec=pltpu.PrefetchScalarGridSpec(
            num_scalar_prefetch=4,
            grid=(tiles // block[0],),
            in_specs=[pl.BlockSpec(block, used_rows),
                      pl.BlockSpec(memory_space=pl.ANY), pl.BlockSpec((None, 1, f2), per_e),
                      pl.BlockSpec(memory_space=pl.ANY), pl.BlockSpec((None, 1, d), per_e)],
            out_specs=pl.BlockSpec(block, rows),
            scratch_shapes=[pltpu.VMEM((2, d, f2), jnp.float32), pltpu.VMEM((2, f, d), jnp.float32),
                            pltpu.VMEM((d, f2), jnp.bfloat16), pltpu.VMEM((f, d), jnp.bfloat16),
                            pltpu.SemaphoreType.DMA((2, 2))]),
        out_shape=jax.ShapeDtypeStruct(xs.shape, jnp.float32),
        compiler_params=pltpu.CompilerParams(dimension_semantics=("arbitrary",),
                                             vmem_limit_bytes=VMEM_LIMIT),
        name="experts",
    )(block_e, nused, block_half, block_next, xs, wu, bu, wd, bd)


def _combine_kernel(dest_ref, dest_next_ref, h_ref, eg_ref, g_ref, ys_ref, o_ref, buf0_ref, buf1_ref,
                    sems, *, final_norm):
    i = pl.program_id(0)
    t, d = h_ref.shape
    pieces = d // LANES
    half = i % 2
    bufs = (buf0_ref, buf1_ref)

    def start_rows(idx_ref, into, tok):
        for j in range(TOP_K):
            _row_copy(ys_ref, idx_ref[tok * TOP_K + j], bufs[into].at[j], tok,
                      sems.at[into]).start(priority=j % 2)

    def wait_rows(which):
        for j in range(TOP_K):
            pltpu.make_async_copy(ys_ref.at[pl.ds(0, t * SUBLANES)], bufs[which].at[j],
                                  sems.at[which]).wait()

    @pl.when(i == 0)
    def _():
        def issue(tok, carry):
            start_rows(dest_ref, 0, tok)
            return carry
        lax.fori_loop(0, t, issue, 0, unroll=ISSUE_UNROLL)

    gain = g_ref[...]

    def step(cur, nxt):
        wait_rows(cur)

        def eight_tokens(grp, carry):
            tok0 = pl.multiple_of(grp * SUBLANES, SUBLANES)
            for k in range(SUBLANES):
                start_rows(dest_next_ref, nxt, tok0 + k)
            eg = eg_ref[pl.ds(tok0, SUBLANES), :]
            h = h_ref[pl.ds(tok0, SUBLANES), :]
            cols = []
            for p in range(pieces):
                col = h[:, p * LANES:(p + 1) * LANES]
                for j in range(TOP_K):
                    rows = bufs[cur][j, pl.ds(tok0 * pieces + p, SUBLANES, stride=pieces), :]
                    col = col + eg[:, j:j + 1] * rows
                cols.append(col)
            o_ref[pl.ds(tok0, SUBLANES), :] = jnp.concatenate(cols, axis=1)
            return carry

        lax.fori_loop(0, t // SUBLANES, eight_tokens, 0)
        if final_norm:
            o_ref[...] = _rmsnorm(o_ref[...], gain)

        @pl.when(i == pl.num_programs(0) - 1)
        def _():
            wait_rows(nxt)

    pl.when(half == 0)(lambda: step(0, 1))
    pl.when(half == 1)(lambda: step(1, 0))


def _combine(dest, h, eg, g, ys, final_norm):
    n, d = h.shape
    t = MOVE_TILE
    return pl.pallas_call(
        functools.partial(_combine_kernel, final_norm=final_norm),
        grid=(n // t,),
        in_specs=[pl.BlockSpec((t * TOP_K,), lambda i: (i,), memory_space=pltpu.SMEM),
                  pl.BlockSpec((t * TOP_K,), lambda i: (jnp.minimum(i + 1, n // t - 1),),
                               memory_space=pltpu.SMEM),
                  pl.BlockSpec((t, d), lambda i: (i, 0)),
                  pl.BlockSpec((t, 2 * TOP_K), lambda i: (i, 0)),
                  pl.BlockSpec((1, d), lambda i: (0, 0)),
                  pl.BlockSpec(memory_space=pl.ANY)],
        out_specs=pl.BlockSpec((t, d), lambda i: (i, 0)),
        scratch_shapes=[pltpu.VMEM((TOP_K, t * SUBLANES, ys.shape[1]), jnp.float32),
                        pltpu.VMEM((TOP_K, t * SUBLANES, ys.shape[1]), jnp.float32),
                        pltpu.SemaphoreType.DMA((2,))],
        out_shape=jax.ShapeDtypeStruct((n, d), jnp.float32),
        compiler_params=pltpu.CompilerParams(dimension_semantics=("arbitrary",),
                                             vmem_limit_bytes=VMEM_LIMIT),
        name="combine",
    )(dest, dest, h, eg, g, ys)


def _rope_tables(seq):
    half = ROPE_DIM // 2
    inv = jnp.power(jnp.float32(ROPE_THETA), -jnp.arange(half, dtype=jnp.float32) / half)
    ang = jnp.arange(seq, dtype=jnp.float32)[:, None] * inv[None, :]
    cos, sin = jnp.cos(ang), jnp.sin(ang)
    pad = HEAD_DIM - ROPE_DIM
    one, zero = jnp.ones((seq, pad), jnp.float32), jnp.zeros((seq, pad), jnp.float32)
    zh = jnp.zeros((seq, half), jnp.float32)
    reps = LANES // HEAD_DIM
    c = jnp.tile(jnp.concatenate([cos, cos, one], axis=1), (1, reps))
    a = jnp.tile(jnp.concatenate([-sin, zh, zero], axis=1), (1, reps))
    b = jnp.tile(jnp.concatenate([zh, sin, zero], axis=1), (1, reps))
    return c, a, b


def _layer(h, mem, p, tables, final_g):
    b, s, d = h.shape
    n = b * s
    bf = jnp.bfloat16
    x2 = h.reshape(n, d)
    w_in = p["w_in"]
    o_ik = 3 * ATTN_W + IDXQ_W
    o_u = o_ik + IDX_DIM + N_HEADS_IDX
    wa = w_in[:, :o_ik].astype(bf)
    wb = jnp.pad(w_in[:, o_ik:o_u], ((0, 0), (0, LANES - (o_u - o_ik)))).astype(bf)
    wc = w_in[:, o_u:].astype(bf)
    q, k, vt, iq, iklo, ikhi, iw, u, qm = _in_proj(x2, p["norm_mix_g"][None], wa, wb, wc, *tables, b, s)
    km, vm = _mem_kv(mem, p["mem_norm_g"][None], p["w_mem_kv"].astype(bf))
    b3 = lambda a: a.reshape(b, s, a.shape[-1])
    ya = _dsa(b3(iq), b3(iw), b3(q), b3(iklo), b3(ikhi), b3(k), vt).reshape(n, ATTN_W)

    wbd = jnp.zeros((POOL_W, POOL_W), jnp.float32)
    for g in range(N_POOL_GROUPS):
        lo = g * POOL_GROUP_DIM
        wbd = wbd.at[lo:lo + POOL_GROUP_DIM, lo:lo + POOL_GROUP_DIM].set(p["pool_w_group"][g])
    wr = jnp.pad(p["w_router"], ((0, 0), (0, LANES - N_EXPERTS)))
    wr_hi = wr.astype(bf)
    br = jnp.pad(p["b_router"], (0, LANES - N_EXPERTS), constant_values=NEG_BIG)[None]
    hmid, xn2, eg, ei, counts = _merge(
        x2, ya, u, qm, km, vm, p["norm_mix_g"][None], p["w_gate"].astype(bf), p["b_gate"][None],
        p["w_proj_attn"].astype(bf), p["w_proj_pool"].astype(bf), p["w_proj_mem"].astype(bf),
        wbd.astype(bf), p["pool_scale"][None], p["w_out"].astype(bf), p["norm_ffn_g"][None],
        wr_hi, jnp.concatenate([wr_hi, (wr - wr_hi.astype(jnp.float32)).astype(bf)], axis=1), br, s)

    cnt = counts[:, 0].astype(jnp.int32)
    padded = ((cnt + ROW_BLOCK - 1) // ROW_BLOCK) * ROW_BLOCK
    pend = jnp.cumsum(padded)
    pstart = (pend - padded).astype(jnp.int32)
    n_rows = n * TOP_K + N_EXPERTS * ROW_BLOCK
    n_blocks = n_rows // ROW_BLOCK
    nused = (pend[-1] // ROW_BLOCK).astype(jnp.int32)
    blk = jnp.minimum(jnp.arange(n_blocks, dtype=jnp.int32), nused - 1) * ROW_BLOCK
    block_e = jnp.sum(blk[:, None] >= pend[None, :], axis=1).astype(jnp.int32)
    block_e = jnp.minimum(block_e, N_EXPERTS - 1)
    has_rows = cnt > 0
    run_index = jnp.cumsum(has_rows) - 1
    ids = jnp.arange(N_EXPERTS, dtype=jnp.int32)
    later = jnp.where(has_rows[None, :] & (ids[None, :] > ids[:, None]), ids[None, :], N_EXPERTS)
    next_used = jnp.min(later, axis=1)
    next_used = jnp.where(next_used < N_EXPERTS, next_used, -1).astype(jnp.int32)
    of_block = block_e[:, None] == ids[None, :]
    block_half = jnp.sum(jnp.where(of_block, run_index % 2, 0), axis=1).astype(jnp.int32)
    block_next = jnp.sum(jnp.where(of_block, next_used, 0), axis=1).astype(jnp.int32)
    chosen =ei[:, :TOP_K, None] == jnp.arange(N_EXPERTS, dtype=jnp.int32)
    dest = (jnp.sum(jnp.where(chosen, pstart, 0), axis=-1) + ei[:, TOP_K:]).reshape(-1)

    xs = _dispatch(pstart, cnt, dest, xn2, n_rows)
    ys = _experts(block_e, nused[None], block_half, block_next, xs, p["w_up"], p["b_up"][:, None, :],
                  p["w_down"], p["b_down"][:, None, :])
    g = p["norm_ffn_g"][None] if final_g is None else final_g[None]
    return _combine(dest, hmid, eg, g, ys, final_g is not None).reshape(b, s, d)


def kernel(x, mem, norm_mix_g, w_in, w_gate, b_gate, w_proj_attn, w_proj_pool, w_proj_mem,
           pool_w_group, pool_scale, mem_norm_g, w_mem_kv, w_out, norm_ffn_g,
           w_router, b_router, w_up, b_up, w_down, b_down, norm_final_g):
    b, s, d = x.shape
    assert s % IN_TILE == 0 and s % MERGE_TILE == 0 and s % COUNT_CHUNK == 0
    assert (b * s) % MOVE_TILE == 0
    assert d == LANES * SUBLANES, "the row-tile layout maps one model row onto one (8, 128) tile"
    stacked = dict(norm_mix_g=norm_mix_g, w_in=w_in, w_gate=w_gate, b_gate=b_gate,
                   w_proj_attn=w_proj_attn, w_proj_pool=w_proj_pool, w_proj_mem=w_proj_mem,
                   pool_w_group=pool_w_group, pool_scale=pool_scale, mem_norm_g=mem_norm_g,
                   w_mem_kv=w_mem_kv, w_out=w_out, norm_ffn_g=norm_ffn_g, w_router=w_router,
                   b_router=b_router, w_up=w_up, b_up=b_up, w_down=w_down, b_down=b_down)
    depth = w_in.shape[0]
    tables = _rope_tables(s)
    h = x
    for l in range(depth):
        p = {name: val[l] for name, val in stacked.items()}
        h = _layer(h, mem, p, tables, norm_final_g if l == depth - 1 else None)
    return h
```

```python
import functools

import jax
import jax.numpy as jnp
import numpy as np
from jax import lax
from jax.experimental import pallas as pl
from jax.experimental.pallas import tpu as pltpu

N_HEADS_ATTN = 8
HEAD_DIM = 64
ROPE_DIM = HEAD_DIM // 4
ROPE_THETA = 500000.0
N_HEADS_IDX = 8
IDX_DIM = 64
TOPK_MAX = 256
N_POOL_GROUPS = 4
POOL_GROUP_DIM = 64
POOL_WINDOWS = (2, 4, 8, 16)
POOL_HALO = 16
N_HEADS_MEM = 4
N_BRANCHES = 3
N_EXPERTS = 32
TOP_K = 4
SWIGLU_ALPHA = 1.702
SWIGLU_LIMIT = 7.0
EPS = 1e-6

ATTN_W = N_HEADS_ATTN * HEAD_DIM
POOL_W = N_POOL_GROUPS * POOL_GROUP_DIM
MEM_W = N_HEADS_MEM * HEAD_DIM
IDXQ_W = N_HEADS_IDX * IDX_DIM

LANES = 128
SUBLANES = 8
VMEM_LIMIT = 56 * 1024 * 1024

IN_TILE = 512
Q_BLOCK = 128
KEY_CHUNK = 256
COUNT_CHUNK = 512
COUNT_ROWS = 32
EARLY_EXIT_PASSES = (25, 28)
MERGE_TILE = 512
ROW_BLOCK = 256
UP_TILE = 256
MOVE_TILE = 512
ISSUE_UNROLL = 4

VT_ROWS = LANES + 16
VT_ALL = (N_HEADS_ATTN // 2) * VT_ROWS
LOG2_E = 1.4426950408889634

NEG_BIG = -1e30
INT_MIN = -2 ** 31
NEG_INF_KEY = -0x7F800001


def _dot(a, b):
    return jnp.dot(a, b, preferred_element_type=jnp.float32)


def _dot_nt(a, b):
    return lax.dot_general(a, b, (((1,), (1,)), ((), ())), preferred_element_type=jnp.float32)


def _rmsnorm(x, g):
    return x * lax.rsqrt(jnp.mean(x * x, axis=-1, keepdims=True) + EPS) * g


def _store_row_tiles(ref, x):
    rows, width = x.shape
    for j in range(width // LANES):
        ref[pl.ds(j, rows, stride=width // LANES), :] = x[:, j * LANES:(j + 1) * LANES]


def _load_row_tiles(ref, rows, width):
    pieces = width // LANES
    return jnp.concatenate([ref[pl.ds(j, rows, stride=pieces), :] for j in range(pieces)], axis=1)


def _row_tile(ref, row):
    return ref.at[pl.ds(pl.multiple_of(row * SUBLANES, SUBLANES), SUBLANES)]


def _rope128(x, c, a, b):
    return x * c + pltpu.roll(x, LANES - ROPE_DIM // 2, 1) * a + pltpu.roll(x, ROPE_DIM // 2, 1) * b


def _in_proj_kernel(x_ref, g_ref, wa_ref, wb_ref, wc_ref, rc_ref, ra_ref, rb_ref,
                    q_ref, k_ref, vt_ref, iq_ref, iklo_ref, ikhi_ref, iw_ref, u_ref, qm_ref):
    xn = _rmsnorm(x_ref[...], g_ref[...]).astype(jnp.bfloat16)
    rc, ra, rb = rc_ref[...], ra_ref[...], rb_ref[...]
    pa = _dot(xn, wa_ref[...])
    for seg, (ref, scale) in enumerate(((q_ref, LOG2_E * HEAD_DIM ** -0.5), (k_ref, None),
                                        (None, None), (iq_ref, IDX_DIM ** -0.5))):
        for c in range(ATTN_W // LANES):
            lo = seg * ATTN_W + c * LANES
            blk = pa[:, lo:lo + LANES]
            if ref is None:
                vt_ref[c * VT_ROWS:c * VT_ROWS + LANES, :] = blk.T.astype(vt_ref.dtype)
                vt_ref[c * VT_ROWS + LANES:(c + 1) * VT_ROWS, :] = jnp.ones(
                    (VT_ROWS - LANES, blk.shape[0]), vt_ref.dtype)
                continue
            blk = _rope128(blk, rc, ra, rb)
            if scale is not None:
                blk = blk * scale
            ref[:, c * LANES:(c + 1) * LANES] = blk.astype(ref.dtype)
    pb = _dot(xn, wb_ref[...])
    lane = lax.broadcasted_iota(jnp.int32, pb.shape, 1)
    ik = jnp.where(lane < IDX_DIM, _rope128(pb, rc, ra, rb), 0.0)
    iklo_ref[...] = ik.astype(iklo_ref.dtype)
    ikhi_ref[...] = pltpu.roll(ik, IDX_DIM, 1).astype(ikhi_ref.dtype)
    iw_ref[...] = pb * (N_HEADS_IDX ** -0.5)
    pc = _dot(xn, wc_ref[...])
    u_ref[...] = pc[:, :POOL_W]
    qm_ref[...] = (pc[:, POOL_W:] * (HEAD_DIM ** -0.5)).astype(qm_ref.dtype)


def _in_proj(x2, g, wa, wb, wc, rc, ra, rb, batch, seq):
    n, d = x2.shape
    t = IN_TILE
    tiles_per_seq = seq // t
    row = lambda i: (i, 0)
    full = lambda i: (0, 0)
    pos = lambda i: (i % tiles_per_seq, 0)
    bf = jnp.bfloat16
    flat = lambda w, dt: (jax.ShapeDtypeStruct((n, w), dt), pl.BlockSpec((t, w), row))
    outs = (flat(ATTN_W, bf), flat(ATTN_W, bf),
            (jax.ShapeDtypeStruct((batch, VT_ALL, seq), bf),
             pl.BlockSpec((None, VT_ALL, t), lambda i: (i // tiles_per_seq, 0, i % tiles_per_seq))),
            flat(IDXQ_W, bf), flat(LANES, bf), flat(LANES, bf), flat(LANES, jnp.float32),
            flat(POOL_W, jnp.float32), flat(MEM_W, bf))
    return pl.pallas_call(
        _in_proj_kernel,
        grid=(n // t,),
        in_specs=[pl.BlockSpec((t, d), row), pl.BlockSpec((1, d), full),
                  pl.BlockSpec(wa.shape, full), pl.BlockSpec(wb.shape, full),
                  pl.BlockSpec(wc.shape, full),
                  pl.BlockSpec((t, LANES), pos), pl.BlockSpec((t, LANES), pos),
                  pl.BlockSpec((t, LANES), pos)],
        out_specs=tuple(o[1] for o in outs),
        out_shape=tuple(o[0] for o in outs),
        compiler_params=pltpu.CompilerParams(dimension_semantics=("arbitrary",),
                                             vmem_limit_bytes=VMEM_LIMIT),
        name="in_proj",
    )(x2, g, wa, wb, wc, rc, ra, rb)


def _mem_kv_kernel(mem_ref, g_ref, w_ref, km_ref, vm_ref):
    mn = _rmsnorm(mem_ref[...], g_ref[...]).astype(jnp.bfloat16)
    kv = _dot(mn, w_ref[...])
    km, vm = kv[:, :MEM_W], kv[:, MEM_W:]
    lane = lax.broadcasted_iota(jnp.int32, km.shape, 1)
    for h in range(N_HEADS_MEM):
        in_head = (lane >= h * HEAD_DIM) & (lane < (h + 1) * HEAD_DIM)
        km_ref[h] = jnp.where(in_head, km, 0.0).astype(km_ref.dtype)
        vm_ref[h] = jnp.where(in_head, vm, 0.0).astype(vm_ref.dtype)


def _mem_kv(mem, g, w):
    b, m, d = mem.shape
    out = jax.ShapeDtypeStruct((b, N_HEADS_MEM, m, MEM_W), jnp.bfloat16)
    return pl.pallas_call(
        _mem_kv_kernel,
        grid=(b,),
        in_specs=[pl.BlockSpec((None, m, d), lambda i: (i, 0, 0)),
                  pl.BlockSpec((1, d), lambda i: (0, 0)),
                  pl.BlockSpec(w.shape, lambda i: (0, 0))],
        out_specs=(pl.BlockSpec((None, N_HEADS_MEM, m, MEM_W), lambda i: (i, 0, 0, 0)),
                   pl.BlockSpec((None, N_HEADS_MEM, m, MEM_W), lambda i: (i, 0, 0, 0))),
        out_shape=(out, out),
        compiler_params=pltpu.CompilerParams(dimension_semantics=("arbitrary",),
                                             vmem_limit_bytes=VMEM_LIMIT),
        name="mem_kv",
    )(mem, g, w)


def _dsa_kernel(iq_ref, iw_ref, q_ref, iklo_ref, ikhi_ref, k_ref, vt_ref, o_ref,
                key_ref, iqt_ref, qt_ref, bias_ref, s_ref, acc_ref, *, n_top, idx_bits):
    qb = pl.program_id(1)
    n_chunks = (qb * Q_BLOCK + Q_BLOCK + KEY_CHUNK - 1) // KEY_CHUNK
    n_count_chunks = (qb * Q_BLOCK + Q_BLOCK + COUNT_CHUNK - 1) // COUNT_CHUNK
    q_pos = qb * Q_BLOCK + lax.broadcasted_iota(jnp.int32, (1, Q_BLOCK), 1)
    row_c = lax.broadcasted_iota(jnp.int32, (COUNT_CHUNK, 1), 0)
    row_d = lax.broadcasted_iota(jnp.int32, (LANES, Q_BLOCK), 0)
    bf = jnp.bfloat16

    def chunk_off(c):
        return pl.multiple_of(c * KEY_CHUNK, KEY_CHUNK)

    wt = iw_ref[...].T
    for j in range(N_HEADS_IDX // 2):
        iqt_ref[j // 2, :, (j % 2) * Q_BLOCK:(j % 2 + 1) * Q_BLOCK] = (
            iq_ref[:, j * LANES:(j + 1) * LANES].astype(jnp.float32).T.astype(bf))
    for j in range(N_HEADS_ATTN // 2):
        pair_t = q_ref[:, j * LANES:(j + 1) * LANES].astype(jnp.float32).T
        qt_ref[j, :, :Q_BLOCK] = jnp.where(row_d < HEAD_DIM, pair_t, 0.0).astype(bf)
        qt_ref[j, :, Q_BLOCK:] = jnp.where(row_d >= HEAD_DIM, pair_t, 0.0).astype(bf)

    def score_chunk(c, carry):
        off = pl.multiple_of(c * COUNT_CHUNK, COUNT_CHUNK)
        acc = jnp.zeros((COUNT_CHUNK, Q_BLOCK), jnp.float32)
        for parity, ik_ref in enumerate((iklo_ref, ikhi_ref)):
            ik = ik_ref[pl.ds(off, COUNT_CHUNK), :]
            for g in range(N_HEADS_IDX // 4):
                dots = _dot(ik, iqt_ref[g])
                for side in range(2):
                    head = 2 * (2 * g + side) + parity
                    w = wt[IDX_DIM + head:IDX_DIM + head + 1, :]
                    acc = acc + w * jnp.maximum(dots[:, side * Q_BLOCK:(side + 1) * Q_BLOCK], 0.0)
        acc = jnp.where(off + row_c <= q_pos, acc, -jnp.inf)
        bits = lax.bitcast_convert_type(acc, jnp.int32)
        key_ref[pl.ds(off, COUNT_CHUNK), :] = bits ^ ((bits >> 31) & 0x7FFFFFFF)
        return carry

    lax.fori_loop(0, n_count_chunks, score_chunk, 0)

    def count(pred):
        def body(c, cnt):
            off = pl.multiple_of(c * COUNT_CHUNK, COUNT_CHUNK)
            hit = jnp.where(pred(key_ref[pl.ds(off, COUNT_CHUNK), :], off + row_c), 1.0, 0.0)
            return cnt + jnp.sum(hit.reshape(COUNT_CHUNK // COUNT_ROWS, COUNT_ROWS, Q_BLOCK), axis=0)
        cnt = lax.fori_loop(0, n_count_chunks, body, jnp.zeros((COUNT_ROWS, Q_BLOCK), jnp.float32))
        return jnp.sum(cnt, axis=0, keepdims=True)

    k_f = jnp.float32(n_top)
    zero = jnp.zeros((1, Q_BLOCK), jnp.int32)
    n_nonneg = count(lambda kk, pos: kk >= zero)
    n_stored = (n_count_chunks * COUNT_CHUNK).astype(jnp.float32)
    thr = jnp.where(n_nonneg >= k_f, zero, INT_MIN)
    n_ge = jnp.where(n_nonneg >= k_f, n_nonneg, n_stored)

    def thr_bit(i, carry):
        cur, n_cur = carry
        cand = cur + jnp.left_shift(jnp.int32(1), 30 - i)
        n_cand = count(lambda kk, pos: kk >= cand)
        keep = n_cand >= k_f
        return jnp.where(keep, cand, cur), jnp.where(keep, n_cand, n_cur)

    def all_exact(carry):
        return jnp.max(jnp.abs(carry[1] - k_f)) == 0.0

    def search_bits(first, last):
        return lambda carry: lax.fori_loop(first, last, thr_bit, carry)

    carry = search_bits(0, EARLY_EXIT_PASSES[0])((thr, n_ge))
    for first, last in zip(EARLY_EXIT_PASSES, EARLY_EXIT_PASSES[1:] + (31,)):
        carry = lax.cond(all_exact(carry), lambda c: c, search_bits(first, last), carry)
    thr, n_ge = carry

    @pl.when(jnp.max(n_ge) > k_f)
    def _():
        need = k_f - count(lambda kk, pos: kk > thr)

        def tie_bit(i, cur):
            cand = cur + jnp.left_shift(jnp.int32(1), idx_bits - 1 - i)
            below = count(lambda kk, pos: (kk == thr) & (pos < cand))
            return jnp.where(below < need, cand, cur)
        tie_pos = lax.fori_loop(0, idx_bits, tie_bit, zero)

        def drop_late_ties(c, carry):
            off = pl.multiple_of(c * COUNT_CHUNK, COUNT_CHUNK)
            kk = key_ref[pl.ds(off, COUNT_CHUNK), :]
            late = (kk == thr) & (off + row_c > tie_pos)
            key_ref[pl.ds(off, COUNT_CHUNK), :] = jnp.where(late, INT_MIN, kk)
            return carry
        lax.fori_loop(0, n_count_chunks, drop_late_ties, 0)

    thr = jnp.maximum(thr, NEG_INF_KEY + 1)

    acc_ref[...] = jnp.zeros(acc_ref.shape, jnp.float32)

    n_pairs = N_HEADS_ATTN // 2

    def masked_logits(c, slot):
        off = chunk_off(jnp.minimum(c, n_chunks - 1))
        kk = key_ref[pl.ds(off, KEY_CHUNK), :]
        bias = jnp.where(kk >= thr, jnp.where(c < n_chunks, 0.0, NEG_BIG), NEG_BIG)
        bias_ref[slot] = jnp.concatenate([bias, bias], axis=1)
        for j in range(n_pairs):
            kp = k_ref[pl.ds(off, KEY_CHUNK), j * LANES:(j + 1) * LANES]
            s_ref[slot, j] = _dot(kp, qt_ref[j]) + bias_ref[slot]

    def softmax_pv(c, slot, ms, ls):
        off = chunk_off(jnp.minimum(c, n_chunks - 1))
        new_ms, new_ls = [], []
        for j in range(n_pairs):
            s = s_ref[slot, j]
            m_new = jnp.maximum(ms[j], jnp.max(s, axis=0, keepdims=True))
            alpha = jnp.exp2(ms[j] - m_new)
            p = jnp.exp2(s - m_new).astype(bf)
            vt = vt_ref[j * VT_ROWS:(j + 1) * VT_ROWS, pl.ds(off, KEY_CHUNK)]
            pv = _dot(vt, p)
            new_ls.append(alpha * ls[j] + pv[LANES:LANES + 1, :])
            new_ms.append(m_new)
            lo, mid, hi = j * LANES, j * LANES + HEAD_DIM, (j + 1) * LANES
            acc_ref[lo:mid, :] = alpha[:, :Q_BLOCK] * acc_ref[lo:mid, :] + pv[:HEAD_DIM, :Q_BLOCK]
            acc_ref[mid:hi, :] = alpha[:, Q_BLOCK:] * acc_ref[mid:hi, :] + pv[HEAD_DIM:LANES, Q_BLOCK:]
        return tuple(new_ms), tuple(new_ls)

    masked_logits(0, 0)

    def attn_two_chunks(i, carry):
        ms, ls = carry
        masked_logits(2 * i + 1, 1)
        ms, ls = softmax_pv(2 * i, 0, ms, ls)
        masked_logits(2 * i + 2, 0)
        return softmax_pv(2 * i + 1, 1, ms, ls)

    init = (tuple(jnp.full((1, 2 * Q_BLOCK), NEG_BIG, jnp.float32) for _ in range(n_pairs)),
            tuple(jnp.zeros((1, 2 * Q_BLOCK), jnp.float32) for _ in range(n_pairs)))
    _, ls = lax.fori_loop(0, (n_chunks + 1) // 2, attn_two_chunks, init)
    for j in range(n_pairs):
        lo, mid, hi = j * LANES, j * LANES + HEAD_DIM, (j + 1) * LANES
        even = acc_ref[lo:mid, :] / ls[j][:, :Q_BLOCK]
        odd = acc_ref[mid:hi, :] / ls[j][:, Q_BLOCK:]
        o_ref[:, lo:hi] = jnp.concatenate([even, odd], axis=0).T.astype(o_ref.dtype)


def _dsa(iq, iw, q, iklo, ikhi, k, vt):
    b, s, _ = q.shape
    n_top = min(TOPK_MAX, s // 4)
    idx_bits = max(1, int(np.ceil(np.log2(s))))
    qblk = lambda w: pl.BlockSpec((None, Q_BLOCK, w), lambda bi, qi: (bi, qi, 0))
    keys = lambda w: pl.BlockSpec((None, s, w), lambda bi, qi: (bi, 0, 0))
    return pl.pallas_call(
        functools.partial(_dsa_kernel, n_top=n_top, idx_bits=idx_bits),
        grid=(b, s // Q_BLOCK),
        in_specs=[qblk(IDXQ_W), qblk(LANES), qblk(ATTN_W), keys(LANES), keys(LANES),
                  keys(ATTN_W), pl.BlockSpec((None, VT_ALL, s), lambda bi, qi: (bi, 0, 0))],
        out_specs=qblk(ATTN_W),
        out_shape=jax.ShapeDtypeStruct((b, s, ATTN_W), jnp.bfloat16),
        scratch_shapes=[pltpu.VMEM((s, Q_BLOCK), jnp.int32),
                        pltpu.VMEM((N_HEADS_IDX // 4, LANES, 2 * Q_BLOCK), jnp.bfloat16),
                        pltpu.VMEM((N_HEADS_ATTN // 2, LANES, 2 * Q_BLOCK), jnp.bfloat16),
                        pltpu.VMEM((2, KEY_CHUNK, 2 * Q_BLOCK), jnp.float32),
                        pltpu.VMEM((2, N_HEADS_ATTN // 2, KEY_CHUNK, 2 * Q_BLOCK), jnp.float32),
                        pltpu.VMEM((ATTN_W, Q_BLOCK), jnp.float32)],
        compiler_params=pltpu.CompilerParams(dimension_semantics=("arbitrary", "arbitrary"),
                                             vmem_limit_bytes=VMEM_LIMIT),
        name="dsa",
    )(iq, iw, q, iklo, ikhi, k, vt)


def _merge_kernel(x_ref, ya_ref, u_ref, uprev_ref, qm_ref, km_ref, vm_ref,
                  gmix_ref, wg_ref, bg_ref, wpa_ref, wpp_ref, wpm_ref, wbd_ref, psc_ref,
                  wo_ref, gffn_ref, wrh_ref, wrhl_ref, br_ref,
                  h_ref, xn2_ref, eg_ref, ei_ref, cnt_ref, carry_ref, *, tiles_per_seq):
    i = pl.program_id(0)
    t = x_ref.shape[0]
    d = x_ref.shape[1]
    bf = jnp.bfloat16
    x = x_ref[...]
    xn = _rmsnorm(x, gmix_ref[...]).astype(bf)

    tile_in_seq = i % tiles_per_seq
    u = u_ref[...]
    halo = jnp.where(tile_in_seq == 0, 0.0, uprev_ref[...])
    ext = jnp.concatenate([halo, u], axis=0)
    lane_p = lax.broadcasted_iota(jnp.int32, (t, POOL_W), 1)
    pos1 = (tile_in_seq * t + lax.broadcasted_iota(jnp.int32, (t, 1), 0) + 1).astype(jnp.float32)
    pooled = None
    run, width = ext, 1
    for g, w in enumerate(POOL_WINDOWS):
        while width < w:
            run = run[width:] + run[:-width]
            width *= 2
        start = POOL_HALO + 1 - w
        mean = run[start:start + t] / jnp.minimum(pos1, float(w))
        pooled = mean if pooled is None else jnp.where(lane_p >= g * POOL_GROUP_DIM, mean, pooled)
    pooled = pooled - u
    mixed = _dot(pooled.astype(bf), wbd_ref[...]) * psc_ref[...]
    mixed = mixed.astype(bf)

    qm = qm_ref[...]
    probs = []
    for h in range(N_HEADS_MEM):
        s = _dot_nt(qm, km_ref[h])
        p = jnp.exp(s - jnp.max(s, axis=1, keepdims=True))
        probs.append((p / jnp.sum(p, axis=1, keepdims=True)).astype(bf))
    y_mem = _dot(probs[0], vm_ref[0])
    for h in range(1, N_HEADS_MEM):
        y_mem = y_mem + _dot(probs[h], vm_ref[h])
    y_mem = y_mem.astype(bf)
    ya = ya_ref[...]

    tiles = []
    for lo in range(0, d, UP_TILE):
        cols = slice(lo, lo + UP_TILE)
        merged = None
        for br, (y, w_ref) in enumerate(((ya, wpa_ref), (mixed, wpp_ref), (y_mem, wpm_ref))):
            gcols = slice(br * d + lo, br * d + lo + UP_TILE)
            gate = 1.0 / (1.0 + jnp.exp(-(_dot(xn, wg_ref[:, gcols]) + bg_ref[:, gcols])))
            term = gate * _dot(y, w_ref[:, cols])
            merged = term if merged is None else merged + term
        tiles.append(merged.astype(bf))
    h = x + _dot(jnp.concatenate(tiles, axis=1), wo_ref[...])
    h_ref[...] = h
    xn2 = _rmsnorm(h, gffn_ref[...])
    _store_row_tiles(xn2_ref, xn2)

    x_hi = xn2.astype(bf)
    x_lo = (xn2 - x_hi.astype(jnp.float32)).astype(bf)
    by_hi = _dot(x_hi, wrhl_ref[...])
    logits = (by_hi[:, :LANES] + (_dot(x_lo, wrh_ref[...]) + by_hi[:, LANES:])) + br_ref[...]
    work = logits.T[:N_EXPERTS, :]
    row_e = lax.broadcasted_iota(jnp.int32, work.shape, 0).astype(jnp.float32)
    vals, onehots, ids = [], [], []
    for _ in range(TOP_K):
        mx = jnp.max(work, axis=0, keepdims=True)
        idx = jnp.min(jnp.where(work == mx, row_e, float(N_EXPERTS)), axis=0, keepdims=True)
        oh = row_e == idx
        vals.append(mx)
        ids.append(idx)
        onehots.append(oh)
        work = jnp.where(oh, -jnp.inf, work)
    exps = [jnp.exp(v - vals[0]) for v in vals]
    denom = exps[0] + exps[1] + exps[2] + exps[3]

    @pl.when(i == 0)
    def _():
        carry_ref[...] = jnp.zeros(carry_ref.shape, jnp.float32)

    member = jnp.where(onehots[0] | onehots[1] | onehots[2] | onehots[3], 1.0, 0.0)
    r_io = lax.broadcasted_iota(jnp.int32, (t, t), 0)
    c_io = lax.broadcasted_iota(jnp.int32, (t, t), 1)
    earlier = jnp.where(r_io < c_io, 1.0, 0.0).astype(bf)
    before = _dot(member.astype(bf), earlier) + carry_ref[...]
    carry_ref[...] = carry_ref[...] + jnp.sum(member, axis=1, keepdims=True)
    cnt_ref[...] = carry_ref[...]

    row8 = lax.broadcasted_iota(jnp.int32, (LANES, t), 0)
    eg = jnp.zeros((LANES, t), jnp.float32)
    ei = jnp.zeros((LANES, t), jnp.float32)
    for j in range(TOP_K):
        rank = jnp.sum(jnp.where(onehots[j], before, 0.0), axis=0, keepdims=True)
        eg = jnp.where(row8 == j, exps[j] / denom, eg)
        ei = jnp.where(row8 == j, ids[j], ei)
        ei = jnp.where(row8 == TOP_K + j, rank, ei)
    eg_ref[...] = eg.T[:, :2 * TOP_K]
    ei_ref[...] = ei.T[:, :2 * TOP_K].astype(jnp.int32)


def _merge(x2, ya, u, qm, km, vm, gmix, wg, bg, wpa, wpp, wpm, wbd, psc, wo, gffn, wrh, wrhl, br, seq):
    n, d = x2.shape
    t = MERGE_TILE
    tiles_per_seq = seq // t
    m = km.shape[2]
    row = lambda i: (i, 0)
    full = lambda i: (0, 0)
    per_batch = lambda i: (i // tiles_per_seq, 0, 0, 0)
    halo_blocks = t // POOL_HALO
    prev = lambda i: (jnp.maximum(i * halo_blocks - 1, 0), 0)
    consts = (gmix, wg, bg, wpa, wpp, wpm, wbd, psc, wo, gffn, wrh, wrhl, br)
    pieces = d // LANES
    out_shape = (jax.ShapeDtypeStruct((n, d), jnp.float32),
                 jax.ShapeDtypeStruct((n * pieces, LANES), jnp.float32),
                 jax.ShapeDtypeStruct((n, 2 * TOP_K), jnp.float32),
                 jax.ShapeDtypeStruct((n, 2 * TOP_K), jnp.int32),
                 jax.ShapeDtypeStruct((N_EXPERTS, 1), jnp.float32))
    return pl.pallas_call(
        functools.partial(_merge_kernel, tiles_per_seq=tiles_per_seq),
        grid=(n // t,),
        in_specs=[pl.BlockSpec((t, d), row), pl.BlockSpec((t, ATTN_W), row),
                  pl.BlockSpec((t, POOL_W), row), pl.BlockSpec((POOL_HALO, POOL_W), prev),
                  pl.BlockSpec((t, MEM_W), row),
                  pl.BlockSpec((None, N_HEADS_MEM, m, MEM_W), per_batch),
                  pl.BlockSpec((None, N_HEADS_MEM, m, MEM_W), per_batch)]
                 + [pl.BlockSpec(c.shape, full) for c in consts],
        out_specs=(pl.BlockSpec((t, d), row), pl.BlockSpec((t * pieces, LANES), row),
                   pl.BlockSpec((t, 2 * TOP_K), row), pl.BlockSpec((t, 2 * TOP_K), row),
                   pl.BlockSpec((N_EXPERTS, 1), full)),
        out_shape=out_shape,
        scratch_shapes=[pltpu.VMEM((N_EXPERTS, 1), jnp.float32)],
        compiler_params=pltpu.CompilerParams(dimension_semantics=("arbitrary",),
                                             vmem_limit_bytes=VMEM_LIMIT),
        name="merge",
    )(x2, ya, u, u, qm, km, vm, *consts)


def _row_copy(src_ref, src_row, dst_ref, dst_row, sem):
    return pltpu.make_async_copy(_row_tile(src_ref, src_row), _row_tile(dst_ref, dst_row), sem)


def _dispatch_kernel(pstart_ref, cnt_ref, dest_ref, x_ref, xs_ref, zero_ref, sem):
    t = x_ref.shape[0] // SUBLANES
    block_tiles = ROW_BLOCK * SUBLANES

    @pl.when(pl.program_id(0) == 0)
    def _():
        zero_ref[...] = jnp.zeros(zero_ref.shape, zero_ref.dtype)

        def for_each_pad_row(act):
            def per_expert(e, carry):
                n_pad = (ROW_BLOCK - cnt_ref[e] % ROW_BLOCK) % ROW_BLOCK
                row = pstart_ref[e] + cnt_ref[e]
                size = ROW_BLOCK // 2
                while size >= 1:
                    has = (n_pad & size) != 0
                    span = size * SUBLANES
                    dst = xs_ref.at[pl.ds(pl.multiple_of(row * SUBLANES, SUBLANES), span)]
                    pl.when(has)(functools.partial(
                        act, pltpu.make_async_copy(zero_ref.at[pl.ds(0, span)], dst, sem)))
                    row = row + jnp.where(has, size, 0)
                    size //= 2
                return carry
            lax.fori_loop(0, N_EXPERTS, per_expert, 0)

        def for_each_unused_block(act):
            n_blocks = xs_ref.shape[0] // block_tiles
            first = (pstart_ref[N_EXPERTS - 1] + cnt_ref[N_EXPERTS - 1] + ROW_BLOCK - 1) // ROW_BLOCK

            def per_block(b, c):
                rows = pl.ds(pl.multiple_of(b * block_tiles, block_tiles), block_tiles)
                act(pltpu.make_async_copy(zero_ref, xs_ref.at[rows], sem))
                return c
            lax.fori_loop(first, n_blocks, per_block, 0)

        for act in (lambda cp: cp.start(), lambda cp: cp.wait()):
            for_each_pad_row(act)
            for_each_unused_block(act)

    def issue(tok, carry):
        for j in range(TOP_K):
            _row_copy(x_ref, tok, xs_ref, dest_ref[tok * TOP_K + j], sem).start(priority=j % 2)
        return carry

    lax.fori_loop(0, t, issue, 0, unroll=ISSUE_UNROLL)
    for j in range(TOP_K):
        pltpu.make_async_copy(x_ref, xs_ref.at[pl.ds(0, x_ref.shape[0])], sem).wait()


def _dispatch(pstart, cnt, dest, xn2, n_rows):
    tiles, lanes = xn2.shape
    n = tiles // SUBLANES
    t = MOVE_TILE
    return pl.pallas_call(
        _dispatch_kernel,
        grid_spec=pltpu.PrefetchScalarGridSpec(
            num_scalar_prefetch=2,
            grid=(n // t,),
            in_specs=[pl.BlockSpec((t * TOP_K,), lambda i, ps, ct: (i,), memory_space=pltpu.SMEM),
                      pl.BlockSpec((t * SUBLANES, lanes), lambda i, ps, ct: (i, 0))],
            out_specs=pl.BlockSpec(memory_space=pl.ANY),
            scratch_shapes=[pltpu.VMEM((ROW_BLOCK * SUBLANES, lanes), jnp.float32),
                            pltpu.SemaphoreType.DMA(())]),
        out_shape=jax.ShapeDtypeStruct((n_rows * SUBLANES, lanes), jnp.float32),
        compiler_params=pltpu.CompilerParams(dimension_semantics=("arbitrary",),
                                             vmem_limit_bytes=VMEM_LIMIT),
        name="dispatch",
    )(pstart, cnt, dest, xn2)


def _experts_kernel(be_ref, nused_ref, half_ref, next_ref, xs_ref, wu_hbm, bu_ref, wd_hbm, bd_ref,
                    ys_ref, wu32_ref, wd32_ref, wu16_ref, wd16_ref, sems):
    i = pl.program_id(0)
    d, f = wu16_ref.shape[0], wd16_ref.shape[0]
    half = half_ref[i]

    def weight_copies(e, into):
        return (pltpu.make_async_copy(wu_hbm.at[e], wu32_ref.at[into], sems.at[0, into]),
                pltpu.make_async_copy(wd_hbm.at[e], wd32_ref.at[into], sems.at[1, into]))

    @pl.when(i == 0)
    def _():
        for cp in weight_copies(be_ref[0], 0):
            cp.start()

    @pl.when((i == 0) | (be_ref[i] != be_ref[jnp.maximum(i - 1, 0)]))
    def _():
        for cp in weight_copies(be_ref[i], half):
            cp.wait()
        wu16_ref[...] = wu32_ref[half].astype(wu16_ref.dtype)
        wd16_ref[...] = wd32_ref[half].astype(wd16_ref.dtype)

        @pl.when(next_ref[i] >= 0)
        def _():
            for cp in weight_copies(next_ref[i], 1 - half):
                cp.start()

    @pl.when(i < nused_ref[0])
    def _():
        x = _load_row_tiles(xs_ref, ROW_BLOCK, d).astype(jnp.bfloat16)
        acts = []
        for lo in range(0, f, UP_TILE):
            glu_cols, lin_cols = slice(lo, lo + UP_TILE), slice(f + lo, f + lo + UP_TILE)
            glu = jnp.minimum(_dot(x, wu16_ref[:, glu_cols]) + bu_ref[:, glu_cols], SWIGLU_LIMIT)
            lin = jnp.clip(_dot(x, wu16_ref[:, lin_cols]) + bu_ref[:, lin_cols],
                           -SWIGLU_LIMIT, SWIGLU_LIMIT)
            act = glu * (1.0 / (1.0 + jnp.exp(-SWIGLU_ALPHA * glu))) * (lin + 1.0)
            acts.append(act.astype(jnp.bfloat16))
        act = jnp.concatenate(acts, axis=1)
        _store_row_tiles(ys_ref, _dot(act, wd16_ref[...]) + bd_ref[...])

    @pl.when(i >= nused_ref[0])
    def _():
        ys_ref[...] = jnp.zeros(ys_ref.shape, ys_ref.dtype)


def _experts(block_e, nused, block_half, block_next, xs, wu, bu, wd, bd):
    tiles, lanes = xs.shape
    e, d, f2 = wu.shape
    f = wd.shape[1]
    block = (ROW_BLOCK * SUBLANES, lanes)
    rows = lambda i, be, nu, hf, nx: (i, 0)
    used_rows = lambda i, be, nu, hf, nx: (jnp.minimum(i, nu[0] - 1), 0)
    per_e = lambda i, be, nu, hf, nx: (be[i], 0, 0)
    return pl.pallas_call(
        _experts_kernel,
        grid_spec=pltpu.PrefetchScalarGridSpec(
            num_scalar_prefetch=4,
            grid=(tiles // block[0],),
            in_specs=[pl.BlockSpec(block, used_rows),
                      pl.BlockSpec(memory_space=pl.ANY), pl.BlockSpec((None, 1, f2), per_e),
                      pl.BlockSpec(memory_space=pl.ANY), pl.BlockSpec((None, 1, d), per_e)],
            out_specs=pl.BlockSpec(block, rows),
            scratch_shapes=[pltpu.VMEM((2, d, f2), jnp.float32), pltpu.VMEM((2, f, d), jnp.float32),
                            pltpu.VMEM((d, f2), jnp.bfloat16), pltpu.VMEM((f, d), jnp.bfloat16),
                            pltpu.SemaphoreType.DMA((2, 2))]),
        out_shape=jax.ShapeDtypeStruct(xs.shape, jnp.float32),
        compiler_params=pltpu.CompilerParams(dimension_semantics=("arbitrary",),
                                             vmem_limit_bytes=VMEM_LIMIT),
        name="experts",
    )(block_e, nused, block_half, block_next, xs, wu, bu, wd, bd)


def _combine_kernel(dest_ref, dest_next_ref, h_ref, eg_ref, g_ref, ys_ref, o_ref, buf0_ref, buf1_ref,
                    sems, *, final_norm):
    i = pl.program_id(0)
    t, d = h_ref.shape
    pieces = d // LANES
    half = i % 2
    bufs = (buf0_ref, buf1_ref)

    def start_rows(idx_ref, into, tok):
        for j in range(TOP_K):
            _row_copy(ys_ref, idx_ref[tok * TOP_K + j], bufs[into].at[j], tok,
                      sems.at[into]).start(priority=j % 2)

    def wait_rows(which):
        for j in range(TOP_K):
            pltpu.make_async_copy(ys_ref.at[pl.ds(0, t * SUBLANES)], bufs[which].at[j],
                                  sems.at[which]).wait()

    @pl.when(i == 0)
    def _():
        def issue(tok, carry):
            start_rows(dest_ref, 0, tok)
            return carry
        lax.fori_loop(0, t, issue, 0, unroll=ISSUE_UNROLL)

    gain = g_ref[...]

    def step(cur, nxt):
        wait_rows(cur)

        def eight_tokens(grp, carry):
            tok0 = pl.multiple_of(grp * SUBLANES, SUBLANES)
            for k in range(SUBLANES):
                start_rows(dest_next_ref, nxt, tok0 + k)
            eg = eg_ref[pl.ds(tok0, SUBLANES), :]
            h = h_ref[pl.ds(tok0, SUBLANES), :]
            cols = []
            for p in range(pieces):
                col = h[:, p * LANES:(p + 1) * LANES]
                for j in range(TOP_K):
                    rows = bufs[cur][j, pl.ds(tok0 * pieces + p, SUBLANES, stride=pieces), :]
                    col = col + eg[:, j:j + 1] * rows
                cols.append(col)
            o_ref[pl.ds(tok0, SUBLANES), :] = jnp.concatenate(cols, axis=1)
            return carry

        lax.fori_loop(0, t // SUBLANES, eight_tokens, 0)
        if final_norm:
            o_ref[...] = _rmsnorm(o_ref[...], gain)

        @pl.when(i == pl.num_programs(0) - 1)
        def _():
            wait_rows(nxt)

    pl.when(half == 0)(lambda: step(0, 1))
    pl.when(half == 1)(lambda: step(1, 0))


def _combine(dest, h, eg, g, ys, final_norm):
    n, d = h.shape
    t = MOVE_TILE
    return pl.pallas_call(
        functools.partial(_combine_kernel, final_norm=final_norm),
        grid=(n // t,),
        in_specs=[pl.BlockSpec((t * TOP_K,), lambda i: (i,), memory_space=pltpu.SMEM),
                  pl.BlockSpec((t * TOP_K,), lambda i: (jnp.minimum(i + 1, n // t - 1),),
                               memory_space=pltpu.SMEM),
                  pl.BlockSpec((t, d), lambda i: (i, 0)),
                  pl.BlockSpec((t, 2 * TOP_K), lambda i: (i, 0)),
                  pl.BlockSpec((1, d), lambda i: (0, 0)),
                  pl.BlockSpec(memory_space=pl.ANY)],
        out_specs=pl.BlockSpec((t, d), lambda i: (i, 0)),
        scratch_shapes=[pltpu.VMEM((TOP_K, t * SUBLANES, ys.shape[1]), jnp.float32),
                        pltpu.VMEM((TOP_K, t * SUBLANES, ys.shape[1]), jnp.float32),
                        pltpu.SemaphoreType.DMA((2,))],
        out_shape=jax.ShapeDtypeStruct((n, d), jnp.float32),
        compiler_params=pltpu.CompilerParams(dimension_semantics=("arbitrary",),
                                             vmem_limit_bytes=VMEM_LIMIT),
        name="combine",
    )(dest, dest, h, eg, g, ys)


def _rope_tables(seq):
    half = ROPE_DIM // 2
    inv = jnp.power(jnp.float32(ROPE_THETA), -jnp.arange(half, dtype=jnp.float32) / half)
    ang = jnp.arange(seq, dtype=jnp.float32)[:, None] * inv[None, :]
    cos, sin = jnp.cos(ang), jnp.sin(ang)
    pad = HEAD_DIM - ROPE_DIM
    one, zero = jnp.ones((seq, pad), jnp.float32), jnp.zeros((seq, pad), jnp.float32)
    zh = jnp.zeros((seq, half), jnp.float32)
    reps = LANES // HEAD_DIM
    c = jnp.tile(jnp.concatenate([cos, cos, one], axis=1), (1, reps))
    a = jnp.tile(jnp.concatenate([-sin, zh, zero], axis=1), (1, reps))
    b = jnp.tile(jnp.concatenate([zh, sin, zero], axis=1), (1, reps))
    return c, a, b


def _layer(h, mem, p, tables, final_g):
    b, s, d = h.shape
    n = b * s
    bf = jnp.bfloat16
    x2 = h.reshape(n, d)
    w_in = p["w_in"]
    o_ik = 3 * ATTN_W + IDXQ_W
    o_u = o_ik + IDX_DIM + N_HEADS_IDX
    wa = w_in[:, :o_ik].astype(bf)
    wb = jnp.pad(w_in[:, o_ik:o_u], ((0, 0), (0, LANES - (o_u - o_ik)))).astype(bf)
    wc = w_in[:, o_u:].astype(bf)
    q, k, vt, iq, iklo, ikhi, iw, u, qm = _in_proj(x2, p["norm_mix_g"][None], wa, wb, wc, *tables, b, s)
    km, vm = _mem_kv(mem, p["mem_norm_g"][None], p["w_mem_kv"].astype(bf))
    b3 = lambda a: a.reshape(b, s, a.shape[-1])
    ya = _dsa(b3(iq), b3(iw), b3(q), b3(iklo), b3(ikhi), b3(k), vt).reshape(n, ATTN_W)

    wbd = jnp.zeros((POOL_W, POOL_W), jnp.float32)
    for g in range(N_POOL_GROUPS):
        lo = g * POOL_GROUP_DIM
        wbd = wbd.at[lo:lo + POOL_GROUP_DIM, lo:lo + POOL_GROUP_DIM].set(p["pool_w_group"][g])
    wr = jnp.pad(p["w_router"], ((0, 0), (0, LANES - N_EXPERTS)))
    wr_hi = wr.astype(bf)
    br = jnp.pad(p["b_router"], (0, LANES - N_EXPERTS), constant_values=NEG_BIG)[None]
    hmid, xn2, eg, ei, counts = _merge(
        x2, ya, u, qm, km, vm, p["norm_mix_g"][None], p["w_gate"].astype(bf), p["b_gate"][None],
        p["w_proj_attn"].astype(bf), p["w_proj_pool"].astype(bf), p["w_proj_mem"].astype(bf),
        wbd.astype(bf), p["pool_scale"][None], p["w_out"].astype(bf), p["norm_ffn_g"][None],
        wr_hi, jnp.concatenate([wr_hi, (wr - wr_hi.astype(jnp.float32)).astype(bf)], axis=1), br, s)

    cnt = counts[:, 0].astype(jnp.int32)
    padded = ((cnt + ROW_BLOCK - 1) // ROW_BLOCK) * ROW_BLOCK
    pend = jnp.cumsum(padded)
    pstart = (pend - padded).astype(jnp.int32)
    n_rows = n * TOP_K + N_EXPERTS * ROW_BLOCK
    n_blocks = n_rows // ROW_BLOCK
    nused = (pend[-1] // ROW_BLOCK).astype(jnp.int32)
    blk = jnp.minimum(jnp.arange(n_blocks, dtype=jnp.int32), nused - 1) * ROW_BLOCK
    block_e = jnp.sum(blk[:, None] >= pend[None, :], axis=1).astype(jnp.int32)
    block_e = jnp.minimum(block_e, N_EXPERTS - 1)
    has_rows = cnt > 0
    run_index = jnp.cumsum(has_rows) - 1
    ids = jnp.arange(N_EXPERTS, dtype=jnp.int32)
    later = jnp.where(has_rows[None, :] & (ids[None, :] > ids[:, None]), ids[None, :], N_EXPERTS)
    next_used = jnp.min(later, axis=1)
    next_used = jnp.where(next_used < N_EXPERTS, next_used, -1).astype(jnp.int32)
    of_block = block_e[:, None] == ids[None, :]
    block_half = jnp.sum(jnp.where(of_block, run_index % 2, 0), axis=1).astype(jnp.int32)
    block_next = jnp.sum(jnp.where(of_block, next_used, 0), axis=1).astype(jnp.int32)
    chosen =ei[:, :TOP_K, None] == jnp.arange(N_EXPERTS, dtype=jnp.int32)
    dest = (jnp.sum(jnp.where(chosen, pstart, 0), axis=-1) + ei[:, TOP_K:]).reshape(-1)

    xs = _dispatch(pstart, cnt, dest, xn2, n_rows)
    ys = _experts(block_e, nused[None], block_half, block_next, xs, p["w_up"], p["b_up"][:, None, :],
                  p["w_down"], p["b_down"][:, None, :])
    g = p["norm_ffn_g"][None] if final_g is None else final_g[None]
    return _combine(dest, hmid, eg, g, ys, final_g is not None).reshape(b, s, d)


def kernel(x, mem, norm_mix_g, w_in, w_gate, b_gate, w_proj_attn, w_proj_pool, w_proj_mem,
           pool_w_group, pool_scale, mem_norm_g, w_mem_kv, w_out, norm_ffn_g,
           w_router, b_router, w_up, b_up, w_down, b_down, norm_final_g):
    b, s, d = x.shape
    assert s % IN_TILE == 0 and s % MERGE_TILE == 0 and s % COUNT_CHUNK == 0
    assert (b * s) % MOVE_TILE == 0
    assert d == LANES * SUBLANES, "the row-tile layout maps one model row onto one (8, 128) tile"
    stacked = dict(norm_mix_g=norm_mix_g, w_in=w_in, w_gate=w_gate, b_gate=b_gate,
                   w_proj_attn=w_proj_attn, w_proj_pool=w_proj_pool, w_proj_mem=w_proj_mem,
                   pool_w_group=pool_w_group, pool_scale=pool_scale, mem_norm_g=mem_norm_g,
                   w_mem_kv=w_mem_kv, w_out=w_out, norm_ffn_g=norm_ffn_g, w_router=w_router,
                   b_router=b_router, w_up=w_up, b_up=b_up, w_down=w_down, b_down=b_down)
    depth = w_in.shape[0]
    tables = _rope_tables(s)
    h = x
    for l in range(depth):
        p = {name: val[l] for name, val in stacked.items()}
        h = _layer(h, mem, p, tables, norm_final_g if l == depth - 1 else None)
    return h
```
